```python
import math
import jax, jax.numpy as jnp
from jax import lax
import numpy as np

D_MODEL = 1024
BATCH = 4
SEQ = 4096
DEPTH = 1

MIX_WIDTH = D_MODEL
DIFF_WIDTH = MIX_WIDTH // 2
MLA_WIDTH = MIX_WIDTH - DIFF_WIDTH
DIFF_HEADS = 4
DIFF_HEAD_DIM = DIFF_WIDTH // DIFF_HEADS // 2
DIFF_V_DIM = 2 * DIFF_HEAD_DIM
MLA_HEADS = 4
MLA_V_DIM = MLA_WIDTH // MLA_HEADS
MLA_NOPE_DIM = 128
MLA_ROPE_DIM = 64
MLA_QK_DIM = MLA_NOPE_DIM + MLA_ROPE_DIM
MLA_Q_RANK = 256
MLA_KV_RANK = 128
ROPE_THETA = 10000.0

DIFF_Q_COLS = DIFF_HEADS * 2 * DIFF_HEAD_DIM
DIFF_K_COLS = DIFF_HEADS * 2 * DIFF_HEAD_DIM
DIFF_V_COLS = DIFF_HEADS * DIFF_V_DIM
IN_SPLITS = (DIFF_Q_COLS, DIFF_K_COLS, DIFF_V_COLS, MLA_Q_RANK, MLA_KV_RANK, MLA_ROPE_DIM)
IN_WIDTH = sum(IN_SPLITS)
IN_OFFSETS = tuple(int(o) for o in np.cumsum(IN_SPLITS)[:-1])

N_EXPERTS = 32
TOP_K = 4
D_FF_EXPERT = D_MODEL
SWIGLU_LIMIT = 7.0
SWIGLU_ALPHA = 1.702
MOE_BLOCK = 128

Q_BLOCK = 128
LN_EPS = 1e-5
SUBLN_EPS = 1e-5
MLA_RMS_EPS = 1e-6
DN_ALPHA = (2.0 * DEPTH) ** 0.25
DN_BETA = (8.0 * DEPTH) ** -0.25

kernel_name = "hybrid_diffattn_mla_moe_deepnorm"


def lambda_init_fn(layer_idx):
    return 0.8 - 0.6 * math.exp(-0.3 * layer_idx)


def layer_norm(x, g, b):
    xf = x.astype(jnp.float32)
    mu = jnp.mean(xf, axis=-1, keepdims=True)
    var = jnp.mean(jnp.square(xf - mu), axis=-1, keepdims=True)
    y = (xf - mu) * lax.rsqrt(var + LN_EPS) * g.astype(jnp.float32) + b.astype(jnp.float32)
    return y.astype(x.dtype)


def rms_norm(x, g, eps):
    xf = x.astype(jnp.float32)
    y = xf * lax.rsqrt(jnp.mean(jnp.square(xf), axis=-1, keepdims=True) + eps) * g.astype(jnp.float32)
    return y.astype(x.dtype)


def rope(x, positions):
    d = x.shape[-1]
    inv_freq = 1.0 / (ROPE_THETA ** (jnp.arange(0, d, 2, dtype=jnp.float32) / d))
    ang = positions.astype(jnp.float32)[..., None] * inv_freq
    cos = jnp.cos(ang)[:, :, None, :]
    sin = jnp.sin(ang)[:, :, None, :]
    xf = x.astype(jnp.float32)
    x1, x2 = xf[..., : d // 2], xf[..., d // 2:]
    return jnp.concatenate([x1 * cos - x2 * sin, x2 * cos + x1 * sin], axis=-1).astype(x.dtype)


def causal_mask(block_idx, seq):
    qpos = block_idx * Q_BLOCK + jnp.arange(Q_BLOCK)
    return jnp.arange(seq)[None, :] <= qpos[:, None]


def diff_attention(q, k, v, lam):
    B, S, H, _, dh = q.shape
    nb = S // Q_BLOCK
    scale = dh ** -0.5
    qb = q.reshape(B, nb, Q_BLOCK, H, 2, dh).transpose(1, 0, 2, 3, 4, 5)

    def block(args):
        qi, i = args
        s = jnp.einsum('bqhcd,bkhcd->bhcqk', qi, k, preferred_element_type=jnp.float32) * scale
        s = jnp.where(causal_mask(i, S), s, -jnp.inf)
        p = jax.nn.softmax(s, axis=-1)
        w = p[:, :, 0] - lam * p[:, :, 1]
        return jnp.einsum('bhqk,bkhd->bqhd', w.astype(v.dtype), v)

    out = lax.map(block, (qb, jnp.arange(nb)))
    return out.transpose(1, 0, 2, 3, 4).reshape(B, S, H, v.shape[-1])


def causal_attention(q, k, v, scale):
    B, S, H, dk = q.shape
    nb = S // Q_BLOCK
    qb = q.reshape(B, nb, Q_BLOCK, H, dk).transpose(1, 0, 2, 3, 4)

    def block(args):
        qi, i = args
        s = jnp.einsum('bqhd,bkhd->bhqk', qi, k, preferred_element_type=jnp.float32) * scale
        p = jax.nn.softmax(jnp.where(causal_mask(i, S), s, -jnp.inf), axis=-1)
        return jnp.einsum('bhqk,bkhd->bqhd', p.astype(v.dtype), v)

    out = lax.map(block, (qb, jnp.arange(nb)))
    return out.transpose(1, 0, 2, 3, 4).reshape(B, S, H, v.shape[-1])


def moe_ffn(x, w_router, b_router, w_gate_up, b_gate_up, w_down, b_down):
    B, S, D = x.shape
    T = B * S
    xt = x.reshape(T, D)
    logits = (xt @ w_router + b_router).astype(jnp.float32)
    top_logit, top_idx = lax.top_k(logits, TOP_K)
    gates = jax.nn.softmax(top_logit, axis=-1)

    A = T * TOP_K
    flat_e = top_idx.reshape(A)
    flat_tok = jnp.repeat(jnp.arange(T, dtype=jnp.int32), TOP_K)
    flat_g = gates.reshape(A)
    order = jnp.argsort(flat_e, stable=True)
    sorted_e = flat_e[order]
    counts = jnp.bincount(flat_e, length=N_EXPERTS)
    padded = (counts + MOE_BLOCK - 1) // MOE_BLOCK * MOE_BLOCK
    start = jnp.cumsum(counts) - counts
    pad_end = jnp.cumsum(padded)
    pad_start = pad_end - padded
    dest = pad_start[sorted_e] + jnp.arange(A) - start[sorted_e]
    n_blocks = -(-A // MOE_BLOCK) + N_EXPERTS
    cap = n_blocks * MOE_BLOCK
    slot_tok = jnp.zeros((cap,), jnp.int32).at[dest].set(flat_tok[order])
    slot_g = jnp.zeros((cap,), jnp.float32).at[dest].set(flat_g[order])
    block_e = jnp.minimum(
        jnp.searchsorted(pad_end, jnp.arange(n_blocks) * MOE_BLOCK, side='right'), N_EXPERTS - 1)

    def expert_block(args):
        tok, g, e = args
        xb = xt[tok]
        h = xb @ w_gate_up[e] + b_gate_up[e]
        gate = jnp.minimum(h[:, :D_FF_EXPERT], SWIGLU_LIMIT)
        up = jnp.clip(h[:, D_FF_EXPERT:], -SWIGLU_LIMIT, SWIGLU_LIMIT)
        act = (up + 1.0) * (gate * jax.nn.sigmoid(gate * SWIGLU_ALPHA))
        y = act @ w_down[e] + b_down[e]
        return y * g[:, None].astype(y.dtype)

    ys = lax.map(expert_block, (slot_tok.reshape(n_blocks, MOE_BLOCK),
                                slot_g.reshape(n_blocks, MOE_BLOCK), block_e))
    out = jnp.zeros((T, D), x.dtype).at[slot_tok].add(ys.reshape(cap, D).astype(x.dtype))
    return out.reshape(B, S, D)


def setup_inputs(seed: int = 0) -> dict:
    key = jax.random.key(seed)
    ks = jax.random.split(key, 24)
    f32 = jnp.float32
    L, D, E, F = DEPTH, D_MODEL, N_EXPERTS, D_FF_EXPERT

    def nrm(k, shape, scale):
        return jax.random.normal(k, shape, f32) * scale

    return {
        "x": nrm(ks[0], (BATCH, SEQ, D), 1.0),
        "positions": jnp.broadcast_to(jnp.arange(SEQ, dtype=jnp.int32), (BATCH, SEQ)),
        "w_in": nrm(ks[1], (L, D, IN_WIDTH), D ** -0.5),
        "lambda_q1": nrm(ks[2], (L, DIFF_HEAD_DIM), 0.1),
        "lambda_k1": nrm(ks[3], (L, DIFF_HEAD_DIM), 0.1),
        "lambda_q2": nrm(ks[4], (L, DIFF_HEAD_DIM), 0.1),
        "lambda_k2": nrm(ks[5], (L, DIFF_HEAD_DIM), 0.1),
        "subln_g": 1.0 + nrm(ks[6], (L, DIFF_V_DIM), 0.02),
        "mla_q_norm_g": 1.0 + nrm(ks[7], (L, MLA_Q_RANK), 0.02),
        "w_uq": nrm(ks[8], (L, MLA_Q_RANK, MLA_HEADS * MLA_QK_DIM), MLA_Q_RANK ** -0.5),
        "mla_kv_norm_g": 1.0 + nrm(ks[9], (L, MLA_KV_RANK), 0.02),
        "w_ukv": nrm(ks[10], (L, MLA_KV_RANK, MLA_HEADS * (MLA_NOPE_DIM + MLA_V_DIM)), MLA_KV_RANK ** -0.5),
        "w_o": nrm(ks[11], (L, MIX_WIDTH, D), MIX_WIDTH ** -0.5 * DN_BETA),
        "ln1_g": 1.0 + nrm(ks[12], (L, D), 0.02),
        "ln1_b": nrm(ks[13], (L, D), 0.02),
        "w_router": nrm(ks[14], (L, D, E), D ** -0.5),
        "b_router": nrm(ks[15], (L, E), 0.01),
        "w_gate_up": nrm(ks[16], (L, E, D, 2 * F), D ** -0.5),
        "b_gate_up": nrm(ks[17], (L, E, 2 * F), 0.01),
        "w_down": nrm(ks[18], (L, E, F, D), F ** -0.5 * DN_BETA),
        "b_down": nrm(ks[19], (L, E, D), 0.01),
        "ln2_g": 1.0 + nrm(ks[20], (L, D), 0.02),
        "ln2_b": nrm(ks[21], (L, D), 0.02),
    }


def reference(x, positions, w_in, lambda_q1, lambda_k1, lambda_q2, lambda_k2, subln_g,
              mla_q_norm_g, w_uq, mla_kv_norm_g, w_ukv, w_o, ln1_g, ln1_b,
              w_router, b_router, w_gate_up, b_gate_up, w_down, b_down, ln2_g, ln2_b):
    B, S, D = x.shape
    for l in range(DEPTH):
        lam_init = lambda_init_fn(l)
        h = x @ w_in[l]
        dq, dk, dv, c_q, c_kv, k_r = jnp.split(h, IN_OFFSETS, axis=-1)

        dq = rope(dq.reshape(B, S, 2 * DIFF_HEADS, DIFF_HEAD_DIM), positions)
        dk = rope(dk.reshape(B, S, 2 * DIFF_HEADS, DIFF_HEAD_DIM), positions)
        dq = dq.reshape(B, S, DIFF_HEADS, 2, DIFF_HEAD_DIM)
        dk = dk.reshape(B, S, DIFF_HEADS, 2, DIFF_HEAD_DIM)
        dv = dv.reshape(B, S, DIFF_HEADS, DIFF_V_DIM)
        lam = (jnp.exp(jnp.sum(lambda_q1[l].astype(jnp.float32) * lambda_k1[l].astype(jnp.float32)))
               - jnp.exp(jnp.sum(lambda_q2[l].astype(jnp.float32) * lambda_k2[l].astype(jnp.float32)))
               + lam_init)
        o_a = diff_attention(dq, dk, dv, lam)
        o_a = rms_norm(o_a, subln_g[l], SUBLN_EPS) * (1.0 - lam_init)

        c_q = rms_norm(c_q, mla_q_norm_g[l], MLA_RMS_EPS)
        q = (c_q @ w_uq[l]).reshape(B, S, MLA_HEADS, MLA_QK_DIM)
        q = jnp.concatenate([q[..., :MLA_NOPE_DIM], rope(q[..., MLA_NOPE_DIM:], positions)], axis=-1)
        c_kv = rms_norm(c_kv, mla_kv_norm_g[l], MLA_RMS_EPS)
        kv = (c_kv @ w_ukv[l]).reshape(B, S, MLA_HEADS, MLA_NOPE_DIM + MLA_V_DIM)
        k_nope, v_m = kv[..., :MLA_NOPE_DIM], kv[..., MLA_NOPE_DIM:]
        k_pe = jnp.broadcast_to(rope(k_r[:, :, None, :], positions), (B, S, MLA_HEADS, MLA_ROPE_DIM))
        k = jnp.concatenate([k_nope, k_pe], axis=-1)
        o_b = causal_attention(q, k, v_m, MLA_QK_DIM ** -0.5)

        mixed = jnp.concatenate([o_a.reshape(B, S, DIFF_WIDTH), o_b.reshape(B, S, MLA_WIDTH)], axis=-1) @ w_o[l]
        x = layer_norm(DN_ALPHA * x + mixed, ln1_g[l], ln1_b[l])

        y = moe_ffn(x, w_router[l], b_router[l], w_gate_up[l], b_gate_up[l], w_down[l], b_down[l])
        x = layer_norm(DN_ALPHA * x + y, ln2_g[l], ln2_b[l])
    return x
```

```python
import functools
import math

import numpy as np
import jax
import jax.numpy as jnp
from jax import lax
from jax.experimental import pallas as pl
from jax.experimental.pallas import tpu as pltpu

D_MODEL = 1024
DIFF_HEADS = 4
DIFF_HEAD_DIM = 64
DIFF_V_DIM = 128
MLA_HEADS = 4
MLA_V_DIM = 128
MLA_NOPE_DIM = 128
MLA_ROPE_DIM = 64
MLA_QK_DIM = MLA_NOPE_DIM + MLA_ROPE_DIM
MLA_Q_RANK = 256
MLA_KV_RANK = 128
ROPE_THETA = 10000.0
N_EXPERTS = 32
TOP_K = 4
D_FF = 1024
SWIGLU_LIMIT = 7.0
SWIGLU_ALPHA = 1.702
LN_EPS = 1e-5
SUBLN_EPS = 1e-5
MLA_RMS_EPS = 1e-6
DEPTH = 1
DN_ALPHA = (2.0 * DEPTH) ** 0.25
LAMBDA_INIT = 0.8 - 0.6 * math.exp(-0.3 * 0)

LANES = 128
MLA_QK_PAD = 2 * LANES
VMEM_LIMIT = 56 * 1024 * 1024

PROJ_TM = 512
ATTN_TQ = 512
ATTN_TK = 512
POST_TM = 512
RANK_TM = 512
MOE_BM = 256
COMB_TM = 256

NEG_BIG = -1e30
F32 = jnp.float32
BF16 = jnp.bfloat16


def _dot(a, b):
    return jnp.dot(a, b, preferred_element_type=F32)


def _dot_nt(a, b):
    return lax.dot_general(a, b, (((1,), (1,)), ((), ())), preferred_element_type=F32)


def _rope128(blk, cos, sin):
    return blk * cos + pltpu.roll(blk, 64, axis=1) * sin


def _proj_kernel(x_ref, cos_ref, sin_ref, wdq_ref, wdk_ref, wdv_ref, wcq_ref, wckv_ref, wkr_ref,
                 gq_ref, gkv_ref, wuq_ref, wuk_ref, wuv_ref,
                 dq_ref, dk_ref, dv_ref, mq_ref, mk_ref, mv_ref):
    xb = x_ref[...].astype(BF16)
    cos = cos_ref[...]
    sin = sin_ref[...]

    dq = _dot(xb, wdq_ref[...])
    dk = _dot(xb, wdk_ref[...])
    for h in range(DIFF_HEADS):
        sl = slice(h * LANES, (h + 1) * LANES)
        dq_ref[:, sl] = (_rope128(dq[:, sl], cos, sin) * (DIFF_HEAD_DIM ** -0.5)).astype(BF16)
        dk_ref[:, sl] = _rope128(dk[:, sl], cos, sin).astype(BF16)
    dv_ref[...] = _dot(xb, wdv_ref[...]).astype(BF16)

    cq = _dot(xb, wcq_ref[...])
    cq = cq * lax.rsqrt(jnp.mean(cq * cq, axis=-1, keepdims=True) + MLA_RMS_EPS) * gq_ref[...]
    q = _dot(cq.astype(BF16), wuq_ref[...])
    scale = MLA_QK_DIM ** -0.5
    for h in range(MLA_HEADS):
        nope = slice(h * MLA_QK_PAD, h * MLA_QK_PAD + LANES)
        ropes = slice(h * MLA_QK_PAD + LANES, (h + 1) * MLA_QK_PAD)
        mq_ref[:, nope] = (q[:, nope] * scale).astype(BF16)
        mq_ref[:, ropes] = (_rope128(q[:, ropes], cos, sin) * scale).astype(BF16)

    ckv = _dot(xb, wckv_ref[...])
    ckv = ckv * lax.rsqrt(jnp.mean(ckv * ckv, axis=-1, keepdims=True) + MLA_RMS_EPS) * gkv_ref[...]
    ckvb = ckv.astype(BF16)
    k_nope = _dot(ckvb, wuk_ref[...])
    mv_ref[...] = _dot(ckvb, wuv_ref[...]).astype(BF16)
    k_pe = _rope128(_dot(xb, wkr_ref[...]), cos, sin).astype(BF16)
    for h in range(MLA_HEADS):
        mk_ref[:, h * MLA_QK_PAD:h * MLA_QK_PAD + LANES] = k_nope[:, h * LANES:(h + 1) * LANES].astype(BF16)
        mk_ref[:, h * MLA_QK_PAD + LANES:(h + 1) * MLA_QK_PAD] = k_pe


def _proj_call(x2, cos_t, sin_t, w):
    T = x2.shape[0]
    tm = PROJ_TM
    row = lambda i: (i, 0)
    full = lambda i: (0, 0)
    weights = [w["dq"], w["dk"], w["dv"], w["cq"], w["ckv"], w["kr"], w["gq"], w["gkv"], w["uq"], w["uk"], w["uv"]]
    in_specs = [pl.BlockSpec((tm, D_MODEL), row), pl.BlockSpec((tm, LANES), row), pl.BlockSpec((tm, LANES), row)]
    in_specs += [pl.BlockSpec(a.shape, full) for a in weights]
    widths = [512, 512, 512, MLA_HEADS * MLA_QK_PAD, MLA_HEADS * MLA_QK_PAD, 512]
    return pl.pallas_call(
        _proj_kernel,
        grid=(T // tm,),
        in_specs=in_specs,
        out_specs=[pl.BlockSpec((tm, n), row) for n in widths],
        out_shape=[jax.ShapeDtypeStruct((T, n), BF16) for n in widths],
        compiler_params=pltpu.CompilerParams(dimension_semantics=("arbitrary",), vmem_limit_bytes=VMEM_LIMIT),
        name="proj",
    )(x2, cos_t, sin_t, *weights)


def _flash_body(qs, k_ref, v_ref, acc_ref, m_ref, l_ref, *, tq, tk):
    i = pl.program_id(2)
    n_sm = len(qs)
    m_ref[...] = jnp.full(m_ref.shape, -jnp.inf, F32)
    l_ref[...] = jnp.zeros(l_ref.shape, F32)
    acc_ref[...] = jnp.zeros(acc_ref.shape, F32)

    def step(j, masked):
        start = pl.multiple_of(j * tk, tk)
        k = k_ref[pl.ds(start, tk), :]
        v = v_ref[pl.ds(start, tk), :]
        for c in range(n_sm):
            s = _dot_nt(qs[c], k)
            if masked:
                row = i * tq + lax.broadcasted_iota(jnp.int32, (tq, tk), 0)
                col = j * tk + lax.broadcasted_iota(jnp.int32, (tq, tk), 1)
                s = jnp.where(col <= row, s, -jnp.inf)
            m_prev = m_ref[c]
            m_new = jnp.maximum(m_prev, jnp.max(s, axis=-1, keepdims=True))
            alpha = jnp.exp(m_prev - m_new)
            p = jnp.exp(s - m_new)
            l_ref[c] = alpha * l_ref[c] + jnp.sum(p, axis=-1, keepdims=True)
            acc_ref[c] = alpha * acc_ref[c] + _dot(p.astype(BF16), v)
            m_ref[c] = m_new

    n_diag = tq // tk
    n_full = i * n_diag

    def full_step(j, carry):
        step(j, False)
        return carry

    lax.fori_loop(0, n_full, full_step, 0)
    for d in range(n_diag):
        step(n_full + d, True)


def _diff_attn_kernel(lq1_ref, lk1_ref, lq2_ref, lk2_ref, g_ref, q_ref, k_ref, v_ref, o_ref,
                      acc_ref, m_ref, l_ref, *, tq, tk):
    q = q_ref[...]
    lane = lax.broadcasted_iota(jnp.int32, q.shape, 1)
    first = (lane % 64) < 32
    zero = jnp.zeros_like(q)
    qs = (jnp.where(first, q, zero), jnp.where(first, zero, q))
    _flash_body(qs, k_ref, v_ref, acc_ref, m_ref, l_ref, tq=tq, tk=tk)

    lam = (jnp.exp(jnp.sum(lq1_ref[...] * lk1_ref[...], axis=-1, keepdims=True))
           - jnp.exp(jnp.sum(lq2_ref[...] * lk2_ref[...], axis=-1, keepdims=True)) + LAMBDA_INIT)
    o = acc_ref[0] / l_ref[0] - lam * (acc_ref[1] / l_ref[1])
    o = o * lax.rsqrt(jnp.mean(o * o, axis=-1, keepdims=True) + SUBLN_EPS) * g_ref[...]
    o_ref[...] = (o * (1.0 - LAMBDA_INIT)).astype(o_ref.dtype)


def _mla_attn_kernel(q_ref, k_ref, v_ref, o_ref, acc_ref, m_ref, l_ref, *, tq, tk):
    _flash_body((q_ref[...],), k_ref, v_ref, acc_ref, m_ref, l_ref, tq=tq, tk=tk)
    o_ref[...] = (acc_ref[0] / l_ref[0]).astype(o_ref.dtype)


def _attn_call(kernel, n_sm, extra, q, k, v, heads, dk, name):
    B, S, _ = q.shape
    tq, tk = ATTN_TQ, ATTN_TK
    dv = LANES
    in_specs = [pl.BlockSpec(a.shape, lambda b, h, i: (0, 0)) for a in extra]
    in_specs += [
        pl.BlockSpec((None, tq, dk), lambda b, h, i: (b, i, h)),
        pl.BlockSpec((None, S, dk), lambda b, h, i: (b, 0, h)),
        pl.BlockSpec((None, S, dv), lambda b, h, i: (b, 0, h)),
    ]
    return pl.pallas_call(
        functools.partial(kernel, tq=tq, tk=tk),
        grid=(B, heads, S // tq),
        in_specs=in_specs,
        out_specs=pl.BlockSpec((None, tq, dv), lambda b, h, i: (b, i, h)),
        out_shape=jax.ShapeDtypeStruct((B, S, heads * dv), BF16),
        scratch_shapes=[
            pltpu.VMEM((n_sm, tq, dv), F32),
            pltpu.VMEM((n_sm, tq, 1), F32),
            pltpu.VMEM((n_sm, tq, 1), F32),
        ],
        compiler_params=pltpu.CompilerParams(
            dimension_semantics=("arbitrary", "arbitrary", "arbitrary"), vmem_limit_bytes=VMEM_LIMIT),
        name=name,
    )(*extra, q, k, v)


def _layer_norm(y, g, b):
    mu = jnp.mean(y, axis=-1, keepdims=True)
    d = y - mu
    var = jnp.mean(d * d, axis=-1, keepdims=True)
    return d * lax.rsqrt(var + LN_EPS) * g + b


def _pack_bf16_pairs(x):
    half = x.shape[1] // 2
    bits = pltpu.bitcast(x.astype(BF16).astype(F32), jnp.uint32)
    return (bits[:, :half] >> 16) | (bits[:, half:] & jnp.uint32(0xFFFF0000))


def _unpack_bf16_pairs(p):
    lo = pltpu.bitcast(p << 16, F32).astype(BF16)
    hi = pltpu.bitcast(p & jnp.uint32(0xFFFF0000), F32).astype(BF16)
    return lo, hi


def _post_kernel(oa_ref, ob_ref, x_ref, wo_ref, g_ref, b_ref, wr_ref, br_ref,
                 x1_ref, xp_ref, idx_ref, gate_ref):
    half = oa_ref.shape[1]
    mixed = _dot(oa_ref[...], wo_ref[:half, :]) + _dot(ob_ref[...], wo_ref[half:, :])
    x1 = _layer_norm(DN_ALPHA * x_ref[...] + mixed, g_ref[...], b_ref[...])
    x1_ref[...] = x1
    xp_ref[...] = _pack_bf16_pairs(x1)

    logits = jnp.dot(x1, wr_ref[...], preferred_element_type=F32, precision=lax.Precision.HIGHEST) + br_ref[...]
    lane = lax.broadcasted_iota(jnp.int32, logits.shape, 1).astype(F32)
    work = logits
    vals, idxs = [], []
    for _ in range(TOP_K):
        m = jnp.max(work, axis=-1, keepdims=True)
        idx = jnp.min(jnp.where(work == m, lane, float(LANES)), axis=-1, keepdims=True)
        vals.append(m)
        idxs.append(idx)
        work = jnp.where(lane == idx, NEG_BIG, work)
    es = [jnp.exp(v - vals[0]) for v in vals]
    den = es[0] + es[1] + es[2] + es[3]
    idx_out = jnp.zeros(logits.shape, F32)
    gate_out = jnp.zeros(logits.shape, F32)
    for k in range(TOP_K):
        idx_out = jnp.where(lane == float(k), idxs[k], idx_out)
        gate_out = jnp.where(lane == float(k), es[k] / den, gate_out)
    idx_ref[...] = idx_out.astype(jnp.int32)
    gate_ref[...] = gate_out


def _post_call(oa, ob, x2, wo, g, b, wr, br):
    T = x2.shape[0]
    tm = POST_TM
    row = lambda i: (i, 0)
    full = lambda i: (0, 0)
    return pl.pallas_call(
        _post_kernel,
        grid=(T // tm,),
        in_specs=[
            pl.BlockSpec((tm, oa.shape[1]), row), pl.BlockSpec((tm, ob.shape[1]), row),
            pl.BlockSpec((tm, D_MODEL), row), pl.BlockSpec(wo.shape, full),
            pl.BlockSpec(g.shape, full), pl.BlockSpec(b.shape, full),
            pl.BlockSpec(wr.shape, full), pl.BlockSpec(br.shape, full),
        ],
        out_specs=[
            pl.BlockSpec((tm, D_MODEL), row), pl.BlockSpec((tm, D_MODEL // 2), row),
            pl.BlockSpec((tm, LANES), row), pl.BlockSpec((tm, LANES), row),
        ],
        out_shape=[
            jax.ShapeDtypeStruct((T, D_MODEL), F32), jax.ShapeDtypeStruct((T, D_MODEL // 2), jnp.uint32),
            jax.ShapeDtypeStruct((T, LANES), jnp.int32), jax.ShapeDtypeStruct((T, LANES), F32),
        ],
        compiler_params=pltpu.CompilerParams(dimension_semantics=("arbitrary",), vmem_limit_bytes=VMEM_LIMIT),
        name="post",
    )(oa, ob, x2, wo, g, b, wr, br)


def _rank_kernel(e_ref, rank_ref, count_ref, run_ref, *, tm):
    i = pl.program_id(0)

    @pl.when(i == 0)
    def _():
        run_ref[...] = jnp.zeros(run_ref.shape, F32)

    e = e_ref[...]
    lane = lax.broadcasted_iota(jnp.int32, (tm, LANES), 1)
    onehot = (lane == e).astype(BF16)
    r = lax.broadcasted_iota(jnp.int32, (tm, tm), 0)
    c = lax.broadcasted_iota(jnp.int32, (tm, tm), 1)
    tri = (c < r).astype(BF16)
    before = _dot(tri, onehot) + run_ref[...]
    rank = jnp.sum(jnp.where(lane == e, before, 0.0), axis=-1, keepdims=True)
    rank_ref[...] = rank.astype(jnp.int32)
    run_ref[...] = run_ref[...] + jnp.sum(onehot.astype(F32), axis=0, keepdims=True)
    count_ref[...] = run_ref[...].astype(jnp.int32)


def _rank_call(flat_e):
    A = flat_e.shape[0]
    tm = RANK_TM
    rank, counts = pl.pallas_call(
        functools.partial(_rank_kernel, tm=tm),
        grid=(A // tm,),
        in_specs=[pl.BlockSpec((tm, 1), lambda i: (i, 0))],
        out_specs=[pl.BlockSpec((tm, 1), lambda i: (i, 0)), pl.BlockSpec((1, LANES), lambda i: (0, 0))],
        out_shape=[jax.ShapeDtypeStruct((A, 1), jnp.int32), jax.ShapeDtypeStruct((1, LANES), jnp.int32)],
        scratch_shapes=[pltpu.VMEM((1, LANES), F32)],
        compiler_params=pltpu.CompilerParams(dimension_semantics=("arbitrary",)),
        name="rank",
    )(flat_e.reshape(A, 1))
    return rank.reshape(A), counts[0, :N_EXPERTS]


def _gather_copy(src_hbm, row, dst, dst_row, sem):
    return pltpu.make_async_copy(src_hbm.at[pl.ds(row, 1), :], dst.at[pl.ds(dst_row, 1), :], sem)


def _experts_kernel(nused_ref, be_ref, tok_ref, xp_hbm, wgu_ref, bgu_ref, wd_ref, bd_ref, y_ref,
                    xs_ref, wgu_b, wd_b, sem, *, bm):
    j = pl.program_id(0)
    n_used = nused_ref[0]

    def issue(blk, slot):
        def body(r, carry):
            _gather_copy(xp_hbm, tok_ref[blk * bm + r], xs_ref.at[slot], r, sem.at[slot]).start()
            return carry
        lax.fori_loop(0, bm, body, 0, unroll=8)

    def wait(slot):
        pltpu.make_async_copy(xp_hbm.at[pl.ds(0, bm), :], xs_ref.at[slot], sem.at[slot]).wait()

    @pl.when(j == 0)
    def _():
        issue(0, 0)

    @pl.when(j + 1 < n_used)
    def _():
        issue(j + 1, (j + 1) % 2)

    new_expert = jnp.logical_or(j == 0, be_ref[j] != be_ref[jnp.maximum(j - 1, 0)])

    @pl.when(jnp.logical_and(j < n_used, new_expert))
    def _():
        wgu_b[...] = wgu_ref[...].astype(BF16)
        wd_b[...] = wd_ref[...].astype(BF16)

    @pl.when(j < n_used)
    def _():
        slot = j % 2
        wait(slot)
        lo, hi = _unpack_bf16_pairs(xs_ref[slot])
        half = D_MODEL // 2
        h = _dot(lo, wgu_b[:half, :]) + _dot(hi, wgu_b[half:, :]) + bgu_ref[...]
        gate = jnp.minimum(h[:, :D_FF], SWIGLU_LIMIT)
        up = jnp.clip(h[:, D_FF:], -SWIGLU_LIMIT, SWIGLU_LIMIT)
        act = (up + 1.0) * (gate * jax.nn.sigmoid(gate * SWIGLU_ALPHA))
        y_ref[...] = _dot(act.astype(BF16), wd_b[...]) + bd_ref[...]

    @pl.when(j >= n_used)
    def _():
        y_ref[...] = jnp.zeros(y_ref.shape, F32)


def _experts_call(n_used, block_e, slot_tok, xp, wgu, bgu, wd, bd, n_blocks):
    bm = MOE_BM
    E = N_EXPERTS
    grid_spec = pltpu.PrefetchScalarGridSpec(
        num_scalar_prefetch=3,
        grid=(n_blocks,),
        in_specs=[
            pl.BlockSpec(memory_space=pl.ANY),
            pl.BlockSpec((None, D_MODEL, 2 * D_FF), lambda j, nu, be, tok: (be[j], 0, 0)),
            pl.BlockSpec((None, 1, 2 * D_FF), lambda j, nu, be, tok: (be[j], 0, 0)),
            pl.BlockSpec((None, D_FF, D_MODEL), lambda j, nu, be, tok: (be[j], 0, 0)),
            pl.BlockSpec((None, 1, D_MODEL), lambda j, nu, be, tok: (be[j], 0, 0)),
        ],
        out_specs=pl.BlockSpec((bm, D_MODEL), lambda j, nu, be, tok: (j, 0)),
        scratch_shapes=[
            pltpu.VMEM((2, bm, D_MODEL // 2), jnp.uint32),
            pltpu.VMEM((D_MODEL, 2 * D_FF), BF16),
            pltpu.VMEM((D_FF, D_MODEL), BF16),
            pltpu.SemaphoreType.DMA((2,)),
        ],
    )
    return pl.pallas_call(
        functools.partial(_experts_kernel, bm=bm),
        grid_spec=grid_spec,
        out_shape=jax.ShapeDtypeStruct((n_blocks * bm, D_MODEL), F32),
        compiler_params=pltpu.CompilerParams(dimension_semantics=("arbitrary",), vmem_limit_bytes=VMEM_LIMIT),
        name="experts",
    )(n_used, block_e, slot_tok, xp, wgu, bgu.reshape(E, 1, 2 * D_FF), wd, bd.reshape(E, 1, D_MODEL))


def _combine_kernel(pos_ref, y_hbm, x1_ref, gate_ref, g_ref, b_ref, o_ref, ys_ref, sem, *, tm):
    i = pl.program_id(0)
    n = pl.num_programs(0)

    def issue(tile, slot):
        def body(t, carry):
            for k in range(TOP_K):
                row = pos_ref[(tile * tm + t) * TOP_K + k]
                _gather_copy(y_hbm, row, ys_ref.at[slot, k], t, sem.at[slot]).start()
            return carry
        lax.fori_loop(0, tm, body, 0, unroll=4)

    def wait(slot):
        for k in range(TOP_K):
            pltpu.make_async_copy(y_hbm.at[pl.ds(0, tm), :], ys_ref.at[slot, k], sem.at[slot]).wait()

    @pl.when(i == 0)
    def _():
        issue(0, 0)

    @pl.when(i + 1 < n)
    def _():
        issue(i + 1, (i + 1) % 2)

    slot = i % 2
    wait(slot)
    gates = gate_ref[...]
    y = gates[:, 0:1] * ys_ref[slot, 0]
    for k in range(1, TOP_K):
        y = y + gates[:, k:k + 1] * ys_ref[slot, k]
    o_ref[...] = _layer_norm(DN_ALPHA * x1_ref[...] + y, g_ref[...], b_ref[...])


def _combine_call(pos, y, x1, gates, g, b):
    T = x1.shape[0]
    tm = COMB_TM
    grid_spec = pltpu.PrefetchScalarGridSpec(
        num_scalar_prefetch=1,
        grid=(T // tm,),
        in_specs=[
            pl.BlockSpec(memory_space=pl.ANY),
            pl.BlockSpec((tm, D_MODEL), lambda i, pos: (i, 0)),
            pl.BlockSpec((tm, LANES), lambda i, pos: (i, 0)),
            pl.BlockSpec(g.shape, lambda i, pos: (0, 0)),
            pl.BlockSpec(b.shape, lambda i, pos: (0, 0)),
        ],
        out_specs=pl.BlockSpec((tm, D_MODEL), lambda i, pos: (i, 0)),
        scratch_shapes=[
            pltpu.VMEM((2, TOP_K, tm, D_MODEL), F32),
            pltpu.SemaphoreType.DMA((2,)),
        ],
    )
    return pl.pallas_call(
        functools.partial(_combine_kernel, tm=tm),
        grid_spec=grid_spec,
        out_shape=jax.ShapeDtypeStruct((T, D_MODEL), F32),
        compiler_params=pltpu.CompilerParams(dimension_semantics=("arbitrary",), vmem_limit_bytes=VMEM_LIMIT),
        name="combine",
    )(pos, y, x1, gates, g, b)


def _rope_lane_order(n_sub):
    half = DIFF_HEAD_DIM // 2
    per = LANES // 2 // n_sub
    assert per == half or n_sub == 1
    cols = []
    for part in range(2):
        for sub in range(n_sub):
            cols.extend(sub * 64 + part * half + d for d in range(half))
    return np.asarray(cols)


def _prep_weights(w_in, mla_q_norm_g, w_uq, mla_kv_norm_g, w_ukv):
    o_dq, o_dk, o_dv, o_cq, o_ckv, o_kr = 0, 512, 1024, 1536, 1792, 1920
    head_order = _rope_lane_order(2)
    diff_cols = np.concatenate([h * LANES + head_order for h in range(DIFF_HEADS)])
    w = {}
    w["dq"] = w_in[:, o_dq + diff_cols].astype(BF16)
    w["dk"] = w_in[:, o_dk + diff_cols].astype(BF16)
    w["dv"] = w_in[:, o_dv:o_cq].astype(BF16)
    w["cq"] = w_in[:, o_cq:o_ckv].astype(BF16)
    w["ckv"] = w_in[:, o_ckv:o_kr].astype(BF16)

    def spread_rope(cols64):
        z = jnp.zeros((cols64.shape[0], 32), cols64.dtype)
        return jnp.concatenate([cols64[:, :32], z, cols64[:, 32:], z], axis=1)

    w["kr"] = spread_rope(w_in[:, o_kr:o_kr + MLA_ROPE_DIM]).astype(BF16)
    uq = []
    for h in range(MLA_HEADS):
        base = h * MLA_QK_DIM
        uq.append(w_uq[:, base:base + MLA_NOPE_DIM])
        uq.append(spread_rope(w_uq[:, base + MLA_NOPE_DIM:base + MLA_QK_DIM]))
    w["uq"] = jnp.concatenate(uq, axis=1).astype(BF16)
    per = MLA_NOPE_DIM + MLA_V_DIM
    w["uk"] = jnp.concatenate([w_ukv[:, h * per:h * per + MLA_NOPE_DIM] for h in range(MLA_HEADS)], axis=1).astype(BF16)
    w["uv"] = jnp.concatenate([w_ukv[:, h * per + MLA_NOPE_DIM:(h + 1) * per] for h in range(MLA_HEADS)], axis=1).astype(BF16)
    w["gq"] = mla_q_norm_g.reshape(1, MLA_Q_RANK)
    w["gkv"] = mla_kv_norm_g.reshape(1, MLA_KV_RANK)
    return w


def _rope_tables(positions):
    half = MLA_ROPE_DIM // 2
    inv_freq = 1.0 / (ROPE_THETA ** (jnp.arange(0, MLA_ROPE_DIM, 2, dtype=F32) / MLA_ROPE_DIM))
    ang = positions.reshape(-1).astype(F32)[:, None] * inv_freq
    ang = jnp.tile(ang, (1, LANES // half))
    sign = jnp.where(jnp.arange(LANES) < LANES // 2, -1.0, 1.0).astype(F32)
    return jnp.cos(ang), jnp.sin(ang) * sign


def _route(top_idx, bm, n_blocks):
    T = top_idx.shape[0]
    A = T * TOP_K
    flat_e = top_idx.reshape(A)
    rank, counts = _rank_call(flat_e)
    padded = (counts + bm - 1) // bm * bm
    pad_end = jnp.cumsum(padded)
    pad_start = pad_end - padded
    pos = pad_start[flat_e] + rank
    slot_tok = jnp.zeros((n_blocks * bm,), jnp.int32).at[pos].set(jnp.arange(A, dtype=jnp.int32) // TOP_K)
    block_e = jnp.minimum(
        jnp.searchsorted(pad_end, jnp.arange(n_blocks, dtype=jnp.int32) * bm, side="right"), N_EXPERTS - 1)
    n_used = (pad_end[-1] // bm).astype(jnp.int32).reshape(1)
    return pos.astype(jnp.int32), slot_tok, block_e.astype(jnp.int32), n_used


def kernel(x, positions, w_in, lambda_q1, lambda_k1, lambda_q2, lambda_k2, subln_g, mla_q_norm_g, w_uq,
           mla_kv_norm_g, w_ukv, w_o, ln1_g, ln1_b, w_router, b_router, w_gate_up, b_gate_up, w_down, b_down,
           ln2_g, ln2_b):
    B, S, D = x.shape
    T = B * S
    l = 0
    x2 = x.reshape(T, D)
    cos_t, sin_t = _rope_tables(positions)
    w = _prep_weights(w_in[l], mla_q_norm_g[l], w_uq[l], mla_kv_norm_g[l], w_ukv[l])

    dq, dk, dv, mq, mk, mv = _proj_call(x2, cos_t, sin_t, w)
    r3 = lambda a: a.reshape(B, S, a.shape[1])
    lam_vecs = [v[l].reshape(1, DIFF_HEAD_DIM) for v in (lambda_q1, lambda_k1, lambda_q2, lambda_k2)]
    o_a = _attn_call(_diff_attn_kernel, 2, lam_vecs + [subln_g[l].reshape(1, DIFF_V_DIM)],
                     r3(dq), r3(dk), r3(dv), DIFF_HEADS, LANES, "diff_attn")
    o_b = _attn_call(_mla_attn_kernel, 1, [], r3(mq), r3(mk), r3(mv), MLA_HEADS, MLA_QK_PAD, "mla_attn")

    wr = jnp.pad(w_router[l], ((0, 0), (0, LANES - N_EXPERTS)))
    br = jnp.pad(b_router[l], (0, LANES - N_EXPERTS), constant_values=NEG_BIG).reshape(1, LANES)
    x1, xp, idx, gates = _post_call(
        o_a.reshape(T, -1), o_b.reshape(T, -1), x2, w_o[l].astype(BF16),
        ln1_g[l].reshape(1, D), ln1_b[l].reshape(1, D), wr, br)

    bm = MOE_BM
    n_blocks = T * TOP_K // bm + N_EXPERTS
    pos, slot_tok, block_e, n_used = _route(idx[:, :TOP_K], bm, n_blocks)
    y = _experts_call(n_used, block_e, slot_tok, xp, w_gate_up[l], b_gate_up[l], w_down[l], b_down[l], n_blocks)
    out = _combine_call(pos, y, x1, gates, ln2_g[l].reshape(1, D), ln2_b[l].reshape(1, D))
    return out.reshape(B, S, D)
```

```python
import functools
import math

import numpy as np
import jax
import jax.numpy as jnp
from jax import lax
from jax.experimental import pallas as pl
from jax.experimental.pallas import tpu as pltpu

D_MODEL = 1024
DIFF_HEADS = 4
DIFF_HEAD_DIM = 64
DIFF_V_DIM = 128
MLA_HEADS = 4
MLA_V_DIM = 128
MLA_NOPE_DIM = 128
MLA_ROPE_DIM = 64
MLA_QK_DIM = MLA_NOPE_DIM + MLA_ROPE_DIM
MLA_Q_RANK = 256
MLA_KV_RANK = 128
ROPE_THETA = 10000.0
N_EXPERTS = 32
TOP_K = 4
D_FF = 1024
SWIGLU_LIMIT = 7.0
SWIGLU_ALPHA = 1.702
LN_EPS = 1e-5
SUBLN_EPS = 1e-5
MLA_RMS_EPS = 1e-6
DEPTH = 1
DN_ALPHA = (2.0 * DEPTH) ** 0.25
LAMBDA_INIT = 0.8 - 0.6 * math.exp(-0.3 * 0)

LANES = 128
MLA_QK_PAD = 2 * LANES
VMEM_LIMIT = 56 * 1024 * 1024

ATTN_TILE = 512
POST_TM = 512
RANK_TM = 512
MOE_BM = 256
COMB_TM = 256

NEG_BIG = -1e30
LOG2E = math.log2(math.e)
F32 = jnp.float32
BF16 = jnp.bfloat16


def _dot(a, b):
    return jnp.dot(a, b, preferred_element_type=F32)


def _dot_nt(a, b):
    return lax.dot_general(a, b, (((1,), (1,)), ((), ())), preferred_element_type=F32)


def _rope128(blk, cos, sin):
    return blk * cos + pltpu.roll(blk, 64, axis=1) * sin


def _rope128_t(blk, cos, sin):
    half = LANES // 2
    rolled = jnp.concatenate([blk[half:], blk[:half]], axis=0)
    return blk * cos + rolled * sin


def _rms_rows(t, g, eps):
    return t * lax.rsqrt(jnp.mean(t * t, axis=-1, keepdims=True) + eps) * g


def _rms_cols(t, g, eps):
    return t * lax.rsqrt(jnp.mean(t * t, axis=0, keepdims=True) + eps) * g


def _proj_kernel(x_ref, cos_ref, sin_ref, cosT_ref, sinT_ref,
                 wdqT_ref, wdk_ref, wdvT_ref, wcqT_ref, wckv_ref, wckvT_ref, wkr_ref,
                 gq_ref, gkv_ref, gkvc_ref, wuqT_ref, wuk_ref, wuvT_ref,
                 dqT_ref, dk_ref, dvT_ref, mqT_ref, mk_ref, mvT_ref):
    xb = x_ref[...].astype(BF16)
    cos, sin = cos_ref[...], sin_ref[...]
    cosT, sinT = cosT_ref[...], sinT_ref[...]

    dq_scale = DIFF_HEAD_DIM ** -0.5 * LOG2E
    mq_scale = MLA_QK_DIM ** -0.5 * LOG2E

    dqT = _dot_nt(wdqT_ref[...], xb)
    for h in range(DIFF_HEADS):
        sl = slice(h * LANES, (h + 1) * LANES)
        dqT_ref[sl, :] = (_rope128_t(dqT[sl], cosT, sinT) * dq_scale).astype(BF16)
    dvT_ref[...] = _dot_nt(wdvT_ref[...], xb).astype(BF16)
    dk = _dot(xb, wdk_ref[...])
    for h in range(DIFF_HEADS):
        sl = slice(h * LANES, (h + 1) * LANES)
        dk_ref[:, sl] = _rope128(dk[:, sl], cos, sin).astype(BF16)

    cqT = _rms_cols(_dot_nt(wcqT_ref[...], xb), gq_ref[...], MLA_RMS_EPS)
    qT = _dot(wuqT_ref[...], cqT.astype(BF16))
    for h in range(MLA_HEADS):
        nope = slice(h * MLA_QK_PAD, h * MLA_QK_PAD + LANES)
        ropes = slice(h * MLA_QK_PAD + LANES, (h + 1) * MLA_QK_PAD)
        mqT_ref[nope, :] = (qT[nope] * mq_scale).astype(BF16)
        mqT_ref[ropes, :] = (_rope128_t(qT[ropes], cosT, sinT) * mq_scale).astype(BF16)

    ckvT = _rms_cols(_dot_nt(wckvT_ref[...], xb), gkvc_ref[...], MLA_RMS_EPS)
    mvT_ref[...] = _dot(wuvT_ref[...], ckvT.astype(BF16)).astype(BF16)
    ckv = _rms_rows(_dot(xb, wckv_ref[...]), gkv_ref[...], MLA_RMS_EPS)
    k_nope = _dot(ckv.astype(BF16), wuk_ref[...])
    k_pe = _rope128(_dot(xb, wkr_ref[...]), cos, sin).astype(BF16)
    for h in range(MLA_HEADS):
        mk_ref[:, h * MLA_QK_PAD:h * MLA_QK_PAD + LANES] = k_nope[:, h * LANES:(h + 1) * LANES].astype(BF16)
        mk_ref[:, h * MLA_QK_PAD + LANES:(h + 1) * MLA_QK_PAD] = k_pe


def _proj_call(x3, tabs, w):
    B, S, D = x3.shape
    tm = ATTN_TILE
    nt = S // tm
    cos_t, sin_t, cosT, sinT = tabs
    weights = [w["dqT"], w["dk"], w["dvT"], w["cqT"], w["ckv"], w["ckvT"], w["kr"],
               w["gq"], w["gkv"], w["gkvc"], w["uqT"], w["uk"], w["uvT"]]
    tok = lambda b, i: (b, i, 0)
    feat = lambda b, i: (b, i, 0, 0)
    in_specs = [pl.BlockSpec((None, tm, D), tok),
                pl.BlockSpec((None, tm, LANES), tok), pl.BlockSpec((None, tm, LANES), tok),
                pl.BlockSpec((None, LANES, tm), lambda b, i: (b, 0, i)),
                pl.BlockSpec((None, LANES, tm), lambda b, i: (b, 0, i))]
    in_specs += [pl.BlockSpec(a.shape, lambda b, i: (0, 0)) for a in weights]
    mq_w = MLA_HEADS * MLA_QK_PAD
    out_specs = [pl.BlockSpec((None, None, 512, tm), feat), pl.BlockSpec((None, tm, 512), tok),
                 pl.BlockSpec((None, None, 512, tm), feat), pl.BlockSpec((None, None, mq_w, tm), feat),
                 pl.BlockSpec((None, tm, mq_w), tok), pl.BlockSpec((None, None, 512, tm), feat)]
    out_shape = [jax.ShapeDtypeStruct((B, nt, 512, tm), BF16), jax.ShapeDtypeStruct((B, S, 512), BF16),
                 jax.ShapeDtypeStruct((B, nt, 512, tm), BF16), jax.ShapeDtypeStruct((B, nt, mq_w, tm), BF16),
                 jax.ShapeDtypeStruct((B, S, mq_w), BF16), jax.ShapeDtypeStruct((B, nt, 512, tm), BF16)]
    return pl.pallas_call(
        _proj_kernel,
        grid=(B, nt),
        in_specs=in_specs,
        out_specs=out_specs,
        out_shape=out_shape,
        compiler_params=pltpu.CompilerParams(dimension_semantics=("arbitrary", "arbitrary"),
                                             vmem_limit_bytes=VMEM_LIMIT),
        name="proj",
    )(x3, cos_t, sin_t, cosT, sinT, *weights)


def _flash_body(qTs, k_ref, vT_ref, acc_ref, m_ref, l_ref, *, t):
    i = pl.program_id(2)
    n_sm = len(qTs)
    m_ref[...] = jnp.full(m_ref.shape, -jnp.inf, F32)
    l_ref[...] = jnp.zeros(l_ref.shape, F32)
    acc_ref[...] = jnp.zeros(acc_ref.shape, F32)

    def step(j, masked):
        start = pl.multiple_of(j * t, t)
        k = k_ref[pl.ds(start, t), :]
        vT = vT_ref[j]
        for c in range(n_sm):
            s = _dot(k, qTs[c])
            if masked:
                key = lax.broadcasted_iota(jnp.int32, (t, t), 0)
                qry = lax.broadcasted_iota(jnp.int32, (t, t), 1)
                s = jnp.where(key <= qry, s, -jnp.inf)
            m_prev = m_ref[c]
            m_new = jnp.maximum(m_prev, jnp.max(s, axis=0, keepdims=True))
            alpha = jnp.exp2(m_prev - m_new)
            p = jnp.exp2(s - m_new)
            l_ref[c] = alpha * l_ref[c] + jnp.sum(p, axis=0, keepdims=True)
            acc_ref[c] = alpha * acc_ref[c] + _dot(vT, p.astype(BF16))
            m_ref[c] = m_new

    def full_step(j, carry):
        step(j, False)
        return carry

    lax.fori_loop(0, i, full_step, 0)
    step(i, True)


def _diff_attn_kernel(lq1_ref, lk1_ref, lq2_ref, lk2_ref, g_ref, qT_ref, k_ref, vT_ref, o_ref,
                      acc_ref, m_ref, l_ref, *, t):
    qT = qT_ref[...]
    row = lax.broadcasted_iota(jnp.int32, qT.shape, 0)
    first = (row % 64) < 32
    zero = jnp.zeros_like(qT)
    qTs = (jnp.where(first, qT, zero), jnp.where(first, zero, qT))
    _flash_body(qTs, k_ref, vT_ref, acc_ref, m_ref, l_ref, t=t)

    lam = (jnp.exp(jnp.sum(lq1_ref[...] * lk1_ref[...], axis=-1, keepdims=True))
           - jnp.exp(jnp.sum(lq2_ref[...] * lk2_ref[...], axis=-1, keepdims=True)) + LAMBDA_INIT)
    oT = acc_ref[0] / l_ref[0] - lam * (acc_ref[1] / l_ref[1])
    oT = _rms_cols(oT, g_ref[...], SUBLN_EPS) * (1.0 - LAMBDA_INIT)
    o_ref[...] = oT.T.astype(o_ref.dtype)


def _mla_attn_kernel(qT_ref, k_ref, vT_ref, o_ref, acc_ref, m_ref, l_ref, *, t):
    _flash_body((qT_ref[...],), k_ref, vT_ref, acc_ref, m_ref, l_ref, t=t)
    o_ref[...] = (acc_ref[0] / l_ref[0]).T.astype(o_ref.dtype)


def _attn_call(kernel, n_sm, extra, qT, k, vT, heads, dk, name):
    B, nt, _, t = qT.shape
    S = nt * t
    dv = LANES
    in_specs = [pl.BlockSpec(a.shape, lambda b, h, i: (0, 0)) for a in extra]
    in_specs += [
        pl.BlockSpec((None, None, dk, t), lambda b, h, i: (b, i, h, 0)),
        pl.BlockSpec((None, S, dk), lambda b, h, i: (b, 0, h)),
        pl.BlockSpec((None, nt, dv, t), lambda b, h, i: (b, 0, h, 0)),
    ]
    return pl.pallas_call(
        functools.partial(kernel, t=t),
        grid=(B, heads, nt),
        in_specs=in_specs,
        out_specs=pl.BlockSpec((None, t, dv), lambda b, h, i: (b, i, h)),
        out_shape=jax.ShapeDtypeStruct((B, S, heads * dv), BF16),
        scratch_shapes=[
            pltpu.VMEM((n_sm, dv, t), F32),
            pltpu.VMEM((n_sm, 1, t), F32),
            pltpu.VMEM((n_sm, 1, t), F32),
        ],
        compiler_params=pltpu.CompilerParams(
            dimension_semantics=("arbitrary", "arbitrary", "arbitrary"), vmem_limit_bytes=VMEM_LIMIT),
        name=name,
    )(*extra, qT, k, vT)


def _layer_norm(y, g, b):
    mu = jnp.mean(y, axis=-1, keepdims=True)
    d = y - mu
    var = jnp.mean(d * d, axis=-1, keepdims=True)
    return d * lax.rsqrt(var + LN_EPS) * g + b


def _pack_bf16_pairs(x):
    half = x.shape[1] // 2
    bits = pltpu.bitcast(x.astype(BF16).astype(F32), jnp.uint32)
    return (bits[:, :half] >> 16) | (bits[:, half:] & jnp.uint32(0xFFFF0000))


def _unpack_bf16_pairs(p):
    lo = pltpu.bitcast(p << 16, F32).astype(BF16)
    hi = pltpu.bitcast(p & jnp.uint32(0xFFFF0000), F32).astype(BF16)
    return lo, hi


def _post_kernel(oa_ref, ob_ref, x_ref, wo_ref, g_ref, b_ref, wr_ref, br_ref,
                 x1_ref, xp_ref, idx_ref, gate_ref):
    half = oa_ref.shape[1]
    mixed = _dot(oa_ref[...], wo_ref[:half, :]) + _dot(ob_ref[...], wo_ref[half:, :])
    x1 = _layer_norm(DN_ALPHA * x_ref[...] + mixed, g_ref[...], b_ref[...])
    x1_ref[...] = x1
    xp_ref[...] = _pack_bf16_pairs(x1)

    logits = jnp.dot(x1, wr_ref[...], preferred_element_type=F32, precision=lax.Precision.HIGHEST) + br_ref[...]
    lane = lax.broadcasted_iota(jnp.int32, logits.shape, 1).astype(F32)
    work = logits
    vals, idxs = [], []
    for _ in range(TOP_K):
        m = jnp.max(work, axis=-1, keepdims=True)
        idx = jnp.min(jnp.where(work == m, lane, float(LANES)), axis=-1, keepdims=True)
        vals.append(m)
        idxs.append(idx)
        work = jnp.where(lane == idx, NEG_BIG, work)
    es = [jnp.exp(v - vals[0]) for v in vals]
    den = es[0] + es[1] + es[2] + es[3]
    idx_out = jnp.zeros(logits.shape, F32)
    gate_out = jnp.zeros(logits.shape, F32)
    for k in range(TOP_K):
        idx_out = jnp.where(lane == float(k), idxs[k], idx_out)
        gate_out = jnp.where(lane == float(k), es[k] / den, gate_out)
    idx_ref[...] = idx_out.astype(jnp.int32)
    gate_ref[...] = gate_out


def _post_call(oa, ob, x2, wo, g, b, wr, br):
    T = x2.shape[0]
    tm = POST_TM
    row = lambda i: (i, 0)
    full = lambda i: (0, 0)
    return pl.pallas_call(
        _post_kernel,
        grid=(T // tm,),
        in_specs=[
            pl.BlockSpec((tm, oa.shape[1]), row), pl.BlockSpec((tm, ob.shape[1]), row),
            pl.BlockSpec((tm, D_MODEL), row), pl.BlockSpec(wo.shape, full),
            pl.BlockSpec(g.shape, full), pl.BlockSpec(b.shape, full),
            pl.BlockSpec(wr.shape, full), pl.BlockSpec(br.shape, full),
        ],
        out_specs=[
            pl.BlockSpec((tm, D_MODEL), row), pl.BlockSpec((tm, D_MODEL // 2), row),
            pl.BlockSpec((tm, LANES), row), pl.BlockSpec((tm, LANES), row),
        ],
        out_shape=[
            jax.ShapeDtypeStruct((T, D_MODEL), F32), jax.ShapeDtypeStruct((T, D_MODEL // 2), jnp.uint32),
            jax.ShapeDtypeStruct((T, LANES), jnp.int32), jax.ShapeDtypeStruct((T, LANES), F32),
        ],
        compiler_params=pltpu.CompilerParams(dimension_semantics=("arbitrary",), vmem_limit_bytes=VMEM_LIMIT),
        name="post",
    )(oa, ob, x2, wo, g, b, wr, br)


def _rank_kernel(e_ref, rank_ref, count_ref, run_ref, *, tm):
    i = pl.program_id(0)

    @pl.when(i == 0)
    def _():
        run_ref[...] = jnp.zeros(run_ref.shape, F32)

    e = e_ref[...]
    lane = lax.broadcasted_iota(jnp.int32, (tm, LANES), 1)
    onehot = (lane == e).astype(BF16)
    r = lax.broadcasted_iota(jnp.int32, (tm, tm), 0)
    c = lax.broadcasted_iota(jnp.int32, (tm, tm), 1)
    tri = (c < r).astype(BF16)
    before = _dot(tri, onehot) + run_ref[...]
    rank = jnp.sum(jnp.where(lane == e, before, 0.0), axis=-1, keepdims=True)
    rank_ref[...] = rank.astype(jnp.int32)
    run_ref[...] = run_ref[...] + jnp.sum(onehot.astype(F32), axis=0, keepdims=True)
    count_ref[...] = run_ref[...].astype(jnp.int32)


def _rank_call(flat_e):
    A = flat_e.shape[0]
    tm = RANK_TM
    rank, counts = pl.pallas_call(
        functools.partial(_rank_kernel, tm=tm),
        grid=(A // tm,),
        in_specs=[pl.BlockSpec((tm, 1), lambda i: (i, 0))],
        out_specs=[pl.BlockSpec((tm, 1), lambda i: (i, 0)), pl.BlockSpec((1, LANES), lambda i: (0, 0))],
        out_shape=[jax.ShapeDtypeStruct((A, 1), jnp.int32), jax.ShapeDtypeStruct((1, LANES), jnp.int32)],
        scratch_shapes=[pltpu.VMEM((1, LANES), F32)],
        compiler_params=pltpu.CompilerParams(dimension_semantics=("arbitrary",)),
        name="rank",
    )(flat_e.reshape(A, 1))
    return rank.reshape(A), counts[0, :N_EXPERTS]


def _gather_copy(src_hbm, row, dst, dst_row, sem):
    return pltpu.make_async_copy(src_hbm.at[pl.ds(row, 1), :], dst.at[pl.ds(dst_row, 1), :], sem)


def _experts_kernel(nused_ref, be_ref, tok_ref, xp_hbm, wgu_ref, bgu_ref, wd_ref, bd_ref, y_ref,
                    xs_ref, wgu_b, wd_b, sem, *, bm):
    j = pl.program_id(0)
    n_used = nused_ref[0]

    def issue(blk, slot):
        def body(r, carry):
            _gather_copy(xp_hbm, tok_ref[blk * bm + r], xs_ref.at[slot], r, sem.at[slot]).start()
            return carry
        lax.fori_loop(0, bm, body, 0, unroll=8)

    def wait(slot):
        pltpu.make_async_copy(xp_hbm.at[pl.ds(0, bm), :], xs_ref.at[slot], sem.at[slot]).wait()

    @pl.when(j == 0)
    def _():
        issue(0, 0)

    @pl.when(j + 1 < n_used)
    def _():
        issue(j + 1, (j + 1) % 2)

    new_expert = jnp.logical_or(j == 0, be_ref[j] != be_ref[jnp.maximum(j - 1, 0)])

    @pl.when(jnp.logical_and(j < n_used, new_expert))
    def _():
        wgu_b[...] = wgu_ref[...].astype(BF16)
        wd_b[...] = wd_ref[...].astype(BF16)

    @pl.when(j < n_used)
    def _():
        slot = j % 2
        wait(slot)
        lo, hi = _unpack_bf16_pairs(xs_ref[slot])
        half = D_MODEL // 2
        h = _dot(lo, wgu_b[:half, :]) + _dot(hi, wgu_b[half:, :]) + bgu_ref[...]
        gate = jnp.minimum(h[:, :D_FF], SWIGLU_LIMIT)
        up = jnp.clip(h[:, D_FF:], -SWIGLU_LIMIT, SWIGLU_LIMIT)
        act = (up + 1.0) * (gate * jax.nn.sigmoid(gate * SWIGLU_ALPHA))
        y_ref[...] = _dot(act.astype(BF16), wd_b[...]) + bd_ref[...]

    @pl.when(j >= n_used)
    def _():
        y_ref[...] = jnp.zeros(y_ref.shape, F32)


def _experts_call(n_used, block_e, slot_tok, xp, wgu, bgu, wd, bd, n_blocks):
    bm = MOE_BM
    E = N_EXPERTS
    grid_spec = pltpu.PrefetchScalarGridSpec(
        num_scalar_prefetch=3,
        grid=(n_blocks,),
        in_specs=[
            pl.BlockSpec(memory_space=pl.ANY),
            pl.BlockSpec((None, D_MODEL, 2 * D_FF), lambda j, nu, be, tok: (be[j], 0, 0)),
            pl.BlockSpec((None, 1, 2 * D_FF), lambda j, nu, be, tok: (be[j], 0, 0)),
            pl.BlockSpec((None, D_FF, D_MODEL), lambda j, nu, be, tok: (be[j], 0, 0)),
            pl.BlockSpec((None, 1, D_MODEL), lambda j, nu, be, tok: (be[j], 0, 0)),
        ],
        out_specs=pl.BlockSpec((bm, D_MODEL), lambda j, nu, be, tok: (j, 0)),
        scratch_shapes=[
            pltpu.VMEM((2, bm, D_MODEL // 2), jnp.uint32),
            pltpu.VMEM((D_MODEL, 2 * D_FF), BF16),
            pltpu.VMEM((D_FF, D_MODEL), BF16),
            pltpu.SemaphoreType.DMA((2,)),
        ],
    )
    return pl.pallas_call(
        functools.partial(_experts_kernel, bm=bm),
        grid_spec=grid_spec,
        out_shape=jax.ShapeDtypeStruct((n_blocks * bm, D_MODEL), F32),
        compiler_params=pltpu.CompilerParams(dimension_semantics=("arbitrary",), vmem_limit_bytes=VMEM_LIMIT),
        name="experts",
    )(n_used, block_e, slot_tok, xp, wgu, bgu.reshape(E, 1, 2 * D_FF), wd, bd.reshape(E, 1, D_MODEL))


def _combine_kernel(pos_ref, y_hbm, x1_ref, gate_ref, g_ref, b_ref, o_ref, ys_ref, sem, *, tm):
    i = pl.program_id(0)
    n = pl.num_programs(0)

    def issue(tile, slot):
        def body(t, carry):
            for k in range(TOP_K):
                row = pos_ref[(tile * tm + t) * TOP_K + k]
                _gather_copy(y_hbm, row, ys_ref.at[slot, k], t, sem.at[slot]).start()
            return carry
        lax.fori_loop(0, tm, body, 0, unroll=4)

    def wait(slot):
        for k in range(TOP_K):
            pltpu.make_async_copy(y_hbm.at[pl.ds(0, tm), :], ys_ref.at[slot, k], sem.at[slot]).wait()

    @pl.when(i == 0)
    def _():
        issue(0, 0)

    @pl.when(i + 1 < n)
    def _():
        issue(i + 1, (i + 1) % 2)

    slot = i % 2
    wait(slot)
    gates = gate_ref[...]
    y = gates[:, 0:1] * ys_ref[slot, 0]
    for k in range(1, TOP_K):
        y = y + gates[:, k:k + 1] * ys_ref[slot, k]
    o_ref[...] = _layer_norm(DN_ALPHA * x1_ref[...] + y, g_ref[...], b_ref[...])


def _combine_call(pos, y, x1, gates, g, b):
    T = x1.shape[0]
    tm = COMB_TM
    grid_spec = pltpu.PrefetchScalarGridSpec(
        num_scalar_prefetch=1,
        grid=(T // tm,),
        in_specs=[
            pl.BlockSpec(memory_space=pl.ANY),
            pl.BlockSpec((tm, D_MODEL), lambda i, pos: (i, 0)),
            pl.BlockSpec((tm, LANES), lambda i, pos: (i, 0)),
            pl.BlockSpec(g.shape, lambda i, pos: (0, 0)),
            pl.BlockSpec(b.shape, lambda i, pos: (0, 0)),
        ],
        out_specs=pl.BlockSpec((tm, D_MODEL), lambda i, pos: (i, 0)),
        scratch_shapes=[
            pltpu.VMEM((2, TOP_K, tm, D_MODEL), F32),
            pltpu.SemaphoreType.DMA((2,)),
        ],
    )
    return pl.pallas_call(
        functools.partial(_combine_kernel, tm=tm),
        grid_spec=grid_spec,
        out_shape=jax.ShapeDtypeStruct((T, D_MODEL), F32),
        compiler_params=pltpu.CompilerParams(dimension_semantics=("arbitrary",), vmem_limit_bytes=VMEM_LIMIT),
        name="combine",
    )(pos, y, x1, gates, g, b)


def _rope_lane_order(n_sub):
    half = DIFF_HEAD_DIM // 2
    per = LANES // 2 // n_sub
    assert per == half or n_sub == 1
    cols = []
    for part in range(2):
        for sub in range(n_sub):
            cols.extend(sub * 64 + part * half + d for d in range(half))
    return np.asarray(cols)


def _prep_weights(w_in, mla_q_norm_g, w_uq, mla_kv_norm_g, w_ukv):
    o_dq, o_dk, o_dv, o_cq, o_ckv, o_kr = 0, 512, 1024, 1536, 1792, 1920
    head_order = _rope_lane_order(2)
    diff_cols = np.concatenate([h * LANES + head_order for h in range(DIFF_HEADS)])
    w = {}
    w["dqT"] = w_in[:, o_dq + diff_cols].T.astype(BF16)
    w["dk"] = w_in[:, o_dk + diff_cols].astype(BF16)
    w["dvT"] = w_in[:, o_dv:o_cq].T.astype(BF16)
    w["cqT"] = w_in[:, o_cq:o_ckv].T.astype(BF16)
    w["ckv"] = w_in[:, o_ckv:o_kr].astype(BF16)
    w["ckvT"] = w["ckv"].T

    def spread_rope(cols64):
        z = jnp.zeros((cols64.shape[0], 32), cols64.dtype)
        return jnp.concatenate([cols64[:, :32], z, cols64[:, 32:], z], axis=1)

    w["kr"] = spread_rope(w_in[:, o_kr:o_kr + MLA_ROPE_DIM]).astype(BF16)
    uq = []
    for h in range(MLA_HEADS):
        base = h * MLA_QK_DIM
        uq.append(w_uq[:, base:base + MLA_NOPE_DIM])
        uq.append(spread_rope(w_uq[:, base + MLA_NOPE_DIM:base + MLA_QK_DIM]))
    w["uqT"] = jnp.concatenate(uq, axis=1).T.astype(BF16)
    per = MLA_NOPE_DIM + MLA_V_DIM
    w["uk"] = jnp.concatenate([w_ukv[:, h * per:h * per + MLA_NOPE_DIM] for h in range(MLA_HEADS)], axis=1).astype(BF16)
    w["uvT"] = jnp.concatenate(
        [w_ukv[:, h * per + MLA_NOPE_DIM:(h + 1) * per] for h in range(MLA_HEADS)], axis=1).T.astype(BF16)
    w["gq"] = mla_q_norm_g.reshape(MLA_Q_RANK, 1)
    w["gkv"] = mla_kv_norm_g.reshape(1, MLA_KV_RANK)
    w["gkvc"] = mla_kv_norm_g.reshape(MLA_KV_RANK, 1)
    return w


def _rope_tables(positions):
    half = MLA_ROPE_DIM // 2
    inv_freq = 1.0 / (ROPE_THETA ** (jnp.arange(0, MLA_ROPE_DIM, 2, dtype=F32) / MLA_ROPE_DIM))
    ang = positions.astype(F32)[..., None] * inv_freq
    ang = jnp.tile(ang, (1, 1, LANES // half))
    sign = jnp.where(jnp.arange(LANES) < LANES // 2, -1.0, 1.0).astype(F32)
    cos_t, sin_t = jnp.cos(ang), jnp.sin(ang) * sign
    return cos_t, sin_t, cos_t.transpose(0, 2, 1), sin_t.transpose(0, 2, 1)


def _route(top_idx, bm, n_blocks):
    T = top_idx.shape[0]
    A = T * TOP_K
    flat_e = top_idx.reshape(A)
    rank, counts = _rank_call(flat_e)
    padded = (counts + bm - 1) // bm * bm
    pad_end = jnp.cumsum(padded)
    pad_start = pad_end - padded
    pos = pad_start[flat_e] + rank
    slot_tok = jnp.zeros((n_blocks * bm,), jnp.int32).at[pos].set(jnp.arange(A, dtype=jnp.int32) // TOP_K)
    block_e = jnp.minimum(
        jnp.searchsorted(pad_end, jnp.arange(n_blocks, dtype=jnp.int32) * bm, side="right"), N_EXPERTS - 1)
    n_used = (pad_end[-1] // bm).astype(jnp.int32).reshape(1)
    return pos.astype(jnp.int32), slot_tok, block_e.astype(jnp.int32), n_used


def kernel(x, positions, w_in, lambda_q1, lambda_k1, lambda_q2, lambda_k2, subln_g, mla_q_norm_g, w_uq,
           mla_kv_norm_g, w_ukv, w_o, ln1_g, ln1_b, w_router, b_router, w_gate_up, b_gate_up, w_down, b_down,
           ln2_g, ln2_b):
    B, S, D = x.shape
    T = B * S
    l = 0
    x2 = x.reshape(T, D)
    w = _prep_weights(w_in[l], mla_q_norm_g[l], w_uq[l], mla_kv_norm_g[l], w_ukv[l])

    dqT, dk, dvT, mqT, mk, mvT = _proj_call(x, _rope_tables(positions), w)
    lam_vecs = [v[l].reshape(1, DIFF_HEAD_DIM) for v in (lambda_q1, lambda_k1, lambda_q2, lambda_k2)]
    o_a = _attn_call(_diff_attn_kernel, 2, lam_vecs + [subln_g[l].reshape(DIFF_V_DIM, 1)],
                     dqT, dk, dvT, DIFF_HEADS, LANES, "diff_attn")
    o_b = _attn_call(_mla_attn_kernel, 1, [], mqT, mk, mvT, MLA_HEADS, MLA_QK_PAD, "mla_attn")

    wr = jnp.pad(w_router[l], ((0, 0), (0, LANES - N_EXPERTS)))
    br = jnp.pad(b_router[l], (0, LANES - N_EXPERTS), constant_values=NEG_BIG).reshape(1, LANES)
    x1, xp, idx, gates = _post_call(
        o_a.reshape(T, -1), o_b.reshape(T, -1), x2, w_o[l].astype(BF16),
        ln1_g[l].reshape(1, D), ln1_b[l].reshape(1, D), wr, br)

    bm = MOE_BM
    n_blocks = T * TOP_K // bm + N_EXPERTS
    pos, slot_tok, block_e, n_used = _route(idx[:, :TOP_K], bm, n_blocks)
    y = _experts_call(n_used, block_e, slot_tok, xp, w_gate_up[l], b_gate_up[l], w_down[l], b_down[l], n_blocks)
    out = _combine_call(pos, y, x1, gates, ln2_g[l].reshape(1, D), ln2_b[l].reshape(1, D))
    return out.reshape(B, S, D)
```

```python
import functools
import math

import numpy as np
import jax
import jax.numpy as jnp
from jax import lax
from jax.experimental import pallas as pl
from jax.experimental.pallas import tpu as pltpu

D_MODEL = 1024
DIFF_HEADS = 4
DIFF_HEAD_DIM = 64
DIFF_V_DIM = 128
MLA_HEADS = 4
MLA_V_DIM = 128
MLA_NOPE_DIM = 128
MLA_ROPE_DIM = 64
MLA_QK_DIM = MLA_NOPE_DIM + MLA_ROPE_DIM
MLA_Q_RANK = 256
MLA_KV_RANK = 128
ROPE_THETA = 10000.0
N_EXPERTS = 32
TOP_K = 4
D_FF = 1024
SWIGLU_LIMIT = 7.0
SWIGLU_ALPHA = 1.702
LN_EPS = 1e-5
SUBLN_EPS = 1e-5
MLA_RMS_EPS = 1e-6
DEPTH = 1
DN_ALPHA = (2.0 * DEPTH) ** 0.25
LAMBDA_INIT = 0.8 - 0.6 * math.exp(-0.3 * 0)

LANES = 128
MLA_QK_PAD = 2 * LANES
VMEM_LIMIT = 56 * 1024 * 1024

ATTN_TILE = 512
POST_TM = 512
MOE_BM = 256
COMB_TM = 256

NEG_BIG = -1e30
LOG2E = math.log2(math.e)
F32 = jnp.float32
BF16 = jnp.bfloat16


def _dot(a, b):
    return jnp.dot(a, b, preferred_element_type=F32)


def _dot_nt(a, b):
    return lax.dot_general(a, b, (((1,), (1,)), ((), ())), preferred_element_type=F32)


def _rope128(blk, cos, sin):
    return blk * cos + pltpu.roll(blk, 64, axis=1) * sin


def _rope128_t(blk, cos, sin):
    half = LANES // 2
    rolled = jnp.concatenate([blk[half:], blk[:half]], axis=0)
    return blk * cos + rolled * sin


def _rms_rows(t, g, eps):
    return t * lax.rsqrt(jnp.mean(t * t, axis=-1, keepdims=True) + eps) * g


def _rms_cols(t, g, eps):
    return t * lax.rsqrt(jnp.mean(t * t, axis=0, keepdims=True) + eps) * g


def _proj_kernel(x_ref, cos_ref, sin_ref, cosT_ref, sinT_ref,
                 wdqT_ref, wdk_ref, wdvT_ref, wcqT_ref, wckv_ref, wckvT_ref, wkr_ref,
                 gq_ref, gkv_ref, gkvc_ref, wuqT_ref, wuk_ref, wuvT_ref,
                 dqT_ref, dk_ref, dvT_ref, mqT_ref, mk_ref, mvT_ref):
    xb = x_ref[...].astype(BF16)
    cos, sin = cos_ref[...], sin_ref[...]
    cosT, sinT = cosT_ref[...], sinT_ref[...]

    dq_scale = DIFF_HEAD_DIM ** -0.5 * LOG2E
    mq_scale = MLA_QK_DIM ** -0.5 * LOG2E

    dqT = _dot_nt(wdqT_ref[...], xb)
    for h in range(DIFF_HEADS):
        sl = slice(h * LANES, (h + 1) * LANES)
        dqT_ref[sl, :] = (_rope128_t(dqT[sl], cosT, sinT) * dq_scale).astype(BF16)
    dvT_ref[...] = _dot_nt(wdvT_ref[...], xb).astype(BF16)
    dk = _dot(xb, wdk_ref[...])
    for h in range(DIFF_HEADS):
        sl = slice(h * LANES, (h + 1) * LANES)
        dk_ref[:, sl] = _rope128(dk[:, sl], cos, sin).astype(BF16)

    cqT = _rms_cols(_dot_nt(wcqT_ref[...], xb), gq_ref[...], MLA_RMS_EPS)
    qT = _dot(wuqT_ref[...], cqT.astype(BF16))
    for h in range(MLA_HEADS):
        nope = slice(h * MLA_QK_PAD, h * MLA_QK_PAD + LANES)
        ropes = slice(h * MLA_QK_PAD + LANES, (h + 1) * MLA_QK_PAD)
        mqT_ref[nope, :] = (qT[nope] * mq_scale).astype(BF16)
        mqT_ref[ropes, :] = (_rope128_t(qT[ropes], cosT, sinT) * mq_scale).astype(BF16)

    ckvT = _rms_cols(_dot_nt(wckvT_ref[...], xb), gkvc_ref[...], MLA_RMS_EPS)
    mvT_ref[...] = _dot(wuvT_ref[...], ckvT.astype(BF16)).astype(BF16)
    ckv = _rms_rows(_dot(xb, wckv_ref[...]), gkv_ref[...], MLA_RMS_EPS)
    k_nope = _dot(ckv.astype(BF16), wuk_ref[...])
    k_pe = _rope128(_dot(xb, wkr_ref[...]), cos, sin).astype(BF16)
    for h in range(MLA_HEADS):
        mk_ref[:, h * MLA_QK_PAD:h * MLA_QK_PAD + LANES] = k_nope[:, h * LANES:(h + 1) * LANES].astype(BF16)
        mk_ref[:, h * MLA_QK_PAD + LANES:(h + 1) * MLA_QK_PAD] = k_pe


def _proj_call(x3, tabs, w):
    B, S, D = x3.shape
    tm = ATTN_TILE
    nt = S // tm
    cos_t, sin_t, cosT, sinT = tabs
    weights = [w["dqT"], w["dk"], w["dvT"], w["cqT"], w["ckv"], w["ckvT"], w["kr"],
               w["gq"], w["gkv"], w["gkvc"], w["uqT"], w["uk"], w["uvT"]]
    tok = lambda b, i: (b, i, 0)
    feat = lambda b, i: (b, i, 0, 0)
    in_specs = [pl.BlockSpec((None, tm, D), tok),
                pl.BlockSpec((None, tm, LANES), tok), pl.BlockSpec((None, tm, LANES), tok),
                pl.BlockSpec((None, LANES, tm), lambda b, i: (b, 0, i)),
                pl.BlockSpec((None, LANES, tm), lambda b, i: (b, 0, i))]
    in_specs += [pl.BlockSpec(a.shape, lambda b, i: (0, 0)) for a in weights]
    mq_w = MLA_HEADS * MLA_QK_PAD
    out_specs = [pl.BlockSpec((None, None, 512, tm), feat), pl.BlockSpec((None, tm, 512), tok),
                 pl.BlockSpec((None, None, 512, tm), feat), pl.BlockSpec((None, None, mq_w, tm), feat),
                 pl.BlockSpec((None, tm, mq_w), tok), pl.BlockSpec((None, None, 512, tm), feat)]
    out_shape = [jax.ShapeDtypeStruct((B, nt, 512, tm), BF16), jax.ShapeDtypeStruct((B, S, 512), BF16),
                 jax.ShapeDtypeStruct((B, nt, 512, tm), BF16), jax.ShapeDtypeStruct((B, nt, mq_w, tm), BF16),
                 jax.ShapeDtypeStruct((B, S, mq_w), BF16), jax.ShapeDtypeStruct((B, nt, 512, tm), BF16)]
    return pl.pallas_call(
        _proj_kernel,
        grid=(B, nt),
        in_specs=in_specs,
        out_specs=out_specs,
        out_shape=out_shape,
        compiler_params=pltpu.CompilerParams(dimension_semantics=("arbitrary", "arbitrary"),
                                             vmem_limit_bytes=VMEM_LIMIT),
        name="proj",
    )(x3, cos_t, sin_t, cosT, sinT, *weights)


def _flash_body(qTs, k_ref, vT_ref, acc_ref, m_ref, l_ref, *, t):
    i = pl.program_id(2)
    n_sm = len(qTs)
    m_ref[...] = jnp.full(m_ref.shape, -jnp.inf, F32)
    l_ref[...] = jnp.zeros(l_ref.shape, F32)
    acc_ref[...] = jnp.zeros(acc_ref.shape, F32)

    def step(j, masked):
        start = pl.multiple_of(j * t, t)
        k = k_ref[pl.ds(start, t), :]
        vT = vT_ref[j]
        for c in range(n_sm):
            s = _dot(k, qTs[c])
            if masked:
                key = lax.broadcasted_iota(jnp.int32, (t, t), 0)
                qry = lax.broadcasted_iota(jnp.int32, (t, t), 1)
                s = jnp.where(key <= qry, s, -jnp.inf)
            m_prev = m_ref[c]
            m_new = jnp.maximum(m_prev, jnp.max(s, axis=0, keepdims=True))
            alpha = jnp.exp2(m_prev - m_new)
            p = jnp.exp2(s - m_new)
            l_ref[c] = alpha * l_ref[c] + jnp.sum(p, axis=0, keepdims=True)
            acc_ref[c] = alpha * acc_ref[c] + _dot(vT, p.astype(BF16))
            m_ref[c] = m_new

    def full_step(j, carry):
        step(j, False)
        return carry

    lax.fori_loop(0, i, full_step, 0)
    step(i, True)


def _diff_attn_kernel(lq1_ref, lk1_ref, lq2_ref, lk2_ref, g_ref, qT_ref, k_ref, vT_ref, o_ref,
                      acc_ref, m_ref, l_ref, *, t):
    qT = qT_ref[...]
    row = lax.broadcasted_iota(jnp.int32, qT.shape, 0)
    first = (row % 64) < 32
    zero = jnp.zeros_like(qT)
    qTs = (jnp.where(first, qT, zero), jnp.where(first, zero, qT))
    _flash_body(qTs, k_ref, vT_ref, acc_ref, m_ref, l_ref, t=t)

    lam = (jnp.exp(jnp.sum(lq1_ref[...] * lk1_ref[...], axis=-1, keepdims=True))
           - jnp.exp(jnp.sum(lq2_ref[...] * lk2_ref[...], axis=-1, keepdims=True)) + LAMBDA_INIT)
    oT = acc_ref[0] / l_ref[0] - lam * (acc_ref[1] / l_ref[1])
    oT = _rms_cols(oT, g_ref[...], SUBLN_EPS) * (1.0 - LAMBDA_INIT)
    o_ref[...] = oT.T.astype(o_ref.dtype)


def _mla_attn_kernel(qT_ref, k_ref, vT_ref, o_ref, acc_ref, m_ref, l_ref, *, t):
    _flash_body((qT_ref[...],), k_ref, vT_ref, acc_ref, m_ref, l_ref, t=t)
    o_ref[...] = (acc_ref[0] / l_ref[0]).T.astype(o_ref.dtype)


def _attn_call(kernel, n_sm, extra, qT, k, vT, heads, dk, name):
    B, nt, _, t = qT.shape
    S = nt * t
    dv = LANES
    in_specs = [pl.BlockSpec(a.shape, lambda b, h, i: (0, 0)) for a in extra]
    in_specs += [
        pl.BlockSpec((None, None, dk, t), lambda b, h, i: (b, i, h, 0)),
        pl.BlockSpec((None, S, dk), lambda b, h, i: (b, 0, h)),
        pl.BlockSpec((None, nt, dv, t), lambda b, h, i: (b, 0, h, 0)),
    ]
    return pl.pallas_call(
        functools.partial(kernel, t=t),
        grid=(B, heads, nt),
        in_specs=in_specs,
        out_specs=pl.BlockSpec((None, t, dv), lambda b, h, i: (b, i, h)),
        out_shape=jax.ShapeDtypeStruct((B, S, heads * dv), BF16),
        scratch_shapes=[
            pltpu.VMEM((n_sm, dv, t), F32),
            pltpu.VMEM((n_sm, 1, t), F32),
            pltpu.VMEM((n_sm, 1, t), F32),
        ],
        compiler_params=pltpu.CompilerParams(
            dimension_semantics=("arbitrary", "arbitrary", "arbitrary"), vmem_limit_bytes=VMEM_LIMIT),
        name=name,
    )(*extra, qT, k, vT)


def _layer_norm(y, g, b):
    mu = jnp.mean(y, axis=-1, keepdims=True)
    d = y - mu
    var = jnp.mean(d * d, axis=-1, keepdims=True)
    return d * lax.rsqrt(var + LN_EPS) * g + b


def _post_kernel(oa_ref, ob_ref, x_ref, wo_ref, g_ref, b_ref, wr_ref, br_ref,
                 x1_ref, idx_ref, gate_ref):
    half = oa_ref.shape[1]
    mixed = _dot(oa_ref[...], wo_ref[:half, :]) + _dot(ob_ref[...], wo_ref[half:, :])
    x1 = _layer_norm(DN_ALPHA * x_ref[...] + mixed, g_ref[...], b_ref[...])
    x1_ref[...] = x1

    logits = jnp.dot(x1, wr_ref[...], preferred_element_type=F32, precision=lax.Precision.HIGHEST) + br_ref[...]
    lane = lax.broadcasted_iota(jnp.int32, logits.shape, 1).astype(F32)
    work = logits
    vals, idxs = [], []
    for _ in range(TOP_K):
        m = jnp.max(work, axis=-1, keepdims=True)
        idx = jnp.min(jnp.where(work == m, lane, float(LANES)), axis=-1, keepdims=True)
        vals.append(m)
        idxs.append(idx)
        work = jnp.where(lane == idx, NEG_BIG, work)
    es = [jnp.exp(v - vals[0]) for v in vals]
    den = es[0] + es[1] + es[2] + es[3]
    idx_out = jnp.zeros(logits.shape, F32)
    gate_out = jnp.zeros(logits.shape, F32)
    for k in range(TOP_K):
        idx_out = jnp.where(lane == float(k), idxs[k], idx_out)
        gate_out = jnp.where(lane == float(k), es[k] / den, gate_out)
    idx_ref[...] = idx_out.astype(jnp.int32)
    gate_ref[...] = gate_out


def _post_call(oa, ob, x2, wo, g, b, wr, br):
    T = x2.shape[0]
    tm = POST_TM
    row = lambda i: (i, 0)
    full = lambda i: (0, 0)
    return pl.pallas_call(
        _post_kernel,
        grid=(T // tm,),
        in_specs=[
            pl.BlockSpec((tm, oa.shape[1]), row), pl.BlockSpec((tm, ob.shape[1]), row),
            pl.BlockSpec((tm, D_MODEL), row), pl.BlockSpec(wo.shape, full),
            pl.BlockSpec(g.shape, full), pl.BlockSpec(b.shape, full),
            pl.BlockSpec(wr.shape, full), pl.BlockSpec(br.shape, full),
        ],
        out_specs=[
            pl.BlockSpec((tm, D_MODEL), row), pl.BlockSpec((tm, LANES), row), pl.BlockSpec((tm, LANES), row),
        ],
        out_shape=[
            jax.ShapeDtypeStruct((T, D_MODEL), F32),
            jax.ShapeDtypeStruct((T, LANES), jnp.int32), jax.ShapeDtypeStruct((T, LANES), F32),
        ],
        compiler_params=pltpu.CompilerParams(dimension_semantics=("arbitrary",), vmem_limit_bytes=VMEM_LIMIT),
        name="post",
    )(oa, ob, x2, wo, g, b, wr, br)


def _route_kernel(e_ref, pos_ref, be_ref, nu_ref, *, bm):
    e = e_ref[...]
    R = e.shape[0]
    r_i = lax.broadcasted_iota(jnp.int32, (LANES, LANES), 0)
    c_i = lax.broadcasted_iota(jnp.int32, (LANES, LANES), 1)
    lane_incl = (r_i <= c_i).astype(BF16)
    rr = lax.broadcasted_iota(jnp.int32, (R, R), 0)
    rc = lax.broadcasted_iota(jnp.int32, (R, R), 1)
    rows_before = (rc < rr).astype(BF16)
    blk = (lax.broadcasted_iota(jnp.int32, be_ref.shape, 0) * LANES
           + lax.broadcasted_iota(jnp.int32, be_ref.shape, 1)).astype(F32) * float(bm)

    pos = jnp.zeros(e.shape, F32)
    block_e = jnp.zeros(be_ref.shape, F32)
    start = jnp.zeros((1, 1), F32)
    for ex in range(N_EXPERTS):
        hit = e == ex
        m = jnp.where(hit, 1.0, 0.0).astype(BF16)
        incl = _dot(m, lane_incl)
        row_tot = jnp.broadcast_to(incl[:, LANES - 1:LANES], e.shape)
        before_rows = _dot(rows_before, row_tot.astype(BF16))
        rank = incl - 1.0 + before_rows
        pos = jnp.where(hit, start + rank, pos)
        count = before_rows[R - 1:R, 0:1] + incl[R - 1:R, LANES - 1:LANES]
        start = start + jnp.floor((count + (bm - 1.0)) * (1.0 / bm)) * bm
        block_e = block_e + jnp.where(blk >= start, 1.0, 0.0)
    pos_ref[...] = pos.astype(jnp.int32)
    be_ref[...] = jnp.minimum(block_e, N_EXPERTS - 1.0).astype(jnp.int32)
    nu_ref[...] = jnp.broadcast_to(start * (1.0 / bm), nu_ref.shape).astype(jnp.int32)


def _route_call(flat_e, bm, n_blocks):
    A = flat_e.shape[0]
    R = A // LANES
    be_rows = 8 * pl.cdiv(pl.cdiv(n_blocks, LANES), 8)
    full = lambda i: (0, 0)
    pos, be, nu = pl.pallas_call(
        functools.partial(_route_kernel, bm=bm),
        grid=(1,),
        in_specs=[pl.BlockSpec((R, LANES), full)],
        out_specs=[pl.BlockSpec((R, LANES), full), pl.BlockSpec((be_rows, LANES), full),
                   pl.BlockSpec((8, LANES), full)],
        out_shape=[jax.ShapeDtypeStruct((R, LANES), jnp.int32), jax.ShapeDtypeStruct((be_rows, LANES), jnp.int32),
                   jax.ShapeDtypeStruct((8, LANES), jnp.int32)],
        compiler_params=pltpu.CompilerParams(dimension_semantics=("arbitrary",), vmem_limit_bytes=VMEM_LIMIT),
        name="route",
    )(flat_e.reshape(R, LANES))
    return pos.reshape(A), be.reshape(-1)[:n_blocks], nu[0, :1]


def _invert_kernel(pos_ref, zeros_hbm, slot_ref, sem, *, n):
    init = pltpu.make_async_copy(zeros_hbm, slot_ref, sem)
    init.start()
    init.wait()

    def body(t, carry):
        for k in range(TOP_K):
            slot_ref[pos_ref[t * TOP_K + k]] = t
        return carry

    lax.fori_loop(0, n // TOP_K, body, 0, unroll=8)


def _invert_call(pos, cap):
    A = pos.shape[0]
    return pl.pallas_call(
        functools.partial(_invert_kernel, n=A),
        grid_spec=pltpu.PrefetchScalarGridSpec(
            num_scalar_prefetch=1,
            grid=(1,),
            in_specs=[pl.BlockSpec(memory_space=pl.ANY)],
            out_specs=pl.BlockSpec(memory_space=pltpu.SMEM),
            scratch_shapes=[pltpu.SemaphoreType.DMA],
        ),
        out_shape=jax.ShapeDtypeStruct((cap,), jnp.int32),
        compiler_params=pltpu.CompilerParams(dimension_semantics=("arbitrary",)),
        name="invert",
    )(pos, jnp.zeros((cap,), jnp.int32))


def _gather_copy(src_hbm, row, dst, dst_row, sem):
    return pltpu.make_async_copy(src_hbm.at[pl.ds(row, 1), :], dst.at[pl.ds(dst_row, 1), :], sem)


def _token_copy(src_hbm, row, dst, dst_row, sem):
    return pltpu.make_async_copy(src_hbm.at[row], dst.at[pl.ds(dst_row, 1), :], sem)


def _experts_kernel(nused_ref, be_ref, tok_ref, xp_hbm, wgu_ref, bgu_ref, wd_ref, bd_ref, y_ref,
                    xs_ref, wgu_b, wd_b, sem, *, bm):
    j = pl.program_id(0)
    n_used = nused_ref[0]

    def issue(blk, slot):
        def body(r, carry):
            _token_copy(xp_hbm, tok_ref[blk * bm + r], xs_ref.at[slot], r, sem.at[slot]).start()
            return carry
        lax.fori_loop(0, bm, body, 0, unroll=8)

    def wait(slot):
        pltpu.make_async_copy(xs_ref.at[slot], xs_ref.at[slot], sem.at[slot]).wait()

    @pl.when(j == 0)
    def _():
        issue(0, 0)

    new_expert = jnp.logical_or(j == 0, be_ref[j] != be_ref[jnp.maximum(j - 1, 0)])

    @pl.when(jnp.logical_and(j < n_used, new_expert))
    def _():
        wgu_b[...] = wgu_ref[...].astype(BF16)
        wd_b[...] = wd_ref[...].astype(BF16)

    @pl.when(j < n_used)
    def _():
        slot = j % 2
        wait(slot)
        nxt = jnp.minimum(j + 1, n_used - 1)
        for r in range(bm):
            _token_copy(xp_hbm, tok_ref[nxt * bm + r], xs_ref.at[1 - slot], r, sem.at[1 - slot]).start()
        h = _dot(xs_ref[slot].astype(BF16), wgu_b[...]) + bgu_ref[...]
        gate = jnp.minimum(h[:, :D_FF], SWIGLU_LIMIT)
        up = jnp.clip(h[:, D_FF:], -SWIGLU_LIMIT, SWIGLU_LIMIT)
        act = (up + 1.0) * (gate * jax.nn.sigmoid(gate * SWIGLU_ALPHA))
        y_ref[...] = _dot(act.astype(BF16), wd_b[...]) + bd_ref[...]

    @pl.when(j == n_used - 1)
    def _():
        wait(1 - j % 2)

    @pl.when(j >= n_used)
    def _():
        y_ref[...] = jnp.zeros(y_ref.shape, F32)


def _experts_call(n_used, block_e, slot_tok, xp, wgu, bgu, wd, bd, n_blocks):
    bm = MOE_BM
    E = N_EXPERTS
    grid_spec = pltpu.PrefetchScalarGridSpec(
        num_scalar_prefetch=3,
        grid=(n_blocks,),
        in_specs=[
            pl.BlockSpec(memory_space=pl.ANY),
            pl.BlockSpec((None, D_MODEL, 2 * D_FF), lambda j, nu, be, tok: (be[j], 0, 0)),
            pl.BlockSpec((None, 1, 2 * D_FF), lambda j, nu, be, tok: (be[j], 0, 0)),
            pl.BlockSpec((None, D_FF, D_MODEL), lambda j, nu, be, tok: (be[j], 0, 0)),
            pl.BlockSpec((None, 1, D_MODEL), lambda j, nu, be, tok: (be[j], 0, 0)),
        ],
        out_specs=pl.BlockSpec((bm, D_MODEL), lambda j, nu, be, tok: (j, 0)),
        scratch_shapes=[
            pltpu.VMEM((2, bm, D_MODEL), F32),
            pltpu.VMEM((D_MODEL, 2 * D_FF), BF16),
            pltpu.VMEM((D_FF, D_MODEL), BF16),
            pltpu.SemaphoreType.DMA((2,)),
        ],
    )
    return pl.pallas_call(
        functools.partial(_experts_kernel, bm=bm),
        grid_spec=grid_spec,
        out_shape=jax.ShapeDtypeStruct((n_blocks * bm, D_MODEL), F32),
        compiler_params=pltpu.CompilerParams(dimension_semantics=("arbitrary",), vmem_limit_bytes=VMEM_LIMIT),
        name="experts",
    )(n_used, block_e, slot_tok, xp.reshape(xp.shape[0], 1, xp.shape[1]),
      wgu, bgu.reshape(E, 1, 2 * D_FF), wd, bd.reshape(E, 1, D_MODEL))


def _combine_kernel(pos_ref, y_hbm, x1_ref, gate_ref, g_ref, b_ref, o_ref, ys_ref, sem, *, tm):
    i = pl.program_id(0)
    n = pl.num_programs(0)

    def issue(tile, slot):
        def body(t, carry):
            for k in range(TOP_K):
                row = pos_ref[(tile * tm + t) * TOP_K + k]
                _gather_copy(y_hbm, row, ys_ref.at[slot, k], t, sem.at[slot]).start()
            return carry
        lax.fori_loop(0, tm, body, 0, unroll=4)

    def wait(slot):
        for k in range(TOP_K):
            pltpu.make_async_copy(y_hbm.at[pl.ds(0, tm), :], ys_ref.at[slot, k], sem.at[slot]).wait()

    @pl.when(i == 0)
    def _():
        issue(0, 0)

    @pl.when(i + 1 < n)
    def _():
        issue(i + 1, (i + 1) % 2)

    slot = i % 2
    wait(slot)
    gates = gate_ref[...]
    y = gates[:, 0:1] * ys_ref[slot, 0]
    for k in range(1, TOP_K):
        y = y + gates[:, k:k + 1] * ys_ref[slot, k]
    o_ref[...] = _layer_norm(DN_ALPHA * x1_ref[...] + y, g_ref[...], b_ref[...])


def _combine_call(pos, y, x1, gates, g, b):
    T = x1.shape[0]
    tm = COMB_TM
    grid_spec = pltpu.PrefetchScalarGridSpec(
        num_scalar_prefetch=1,
        grid=(T // tm,),
        in_specs=[
            pl.BlockSpec(memory_space=pl.ANY),
            pl.BlockSpec((tm, D_MODEL), lambda i, pos: (i, 0)),
            pl.BlockSpec((tm, LANES), lambda i, pos: (i, 0)),
            pl.BlockSpec(g.shape, lambda i, pos: (0, 0)),
            pl.BlockSpec(b.shape, lambda i, pos: (0, 0)),
        ],
        out_specs=pl.BlockSpec((tm, D_MODEL), lambda i, pos: (i, 0)),
        scratch_shapes=[
            pltpu.VMEM((2, TOP_K, tm, D_MODEL), F32),
            pltpu.SemaphoreType.DMA((2,)),
        ],
    )
    return pl.pallas_call(
        functools.partial(_combine_kernel, tm=tm),
        grid_spec=grid_spec,
        out_shape=jax.ShapeDtypeStruct((T, D_MODEL), F32),
        compiler_params=pltpu.CompilerParams(dimension_semantics=("arbitrary",), vmem_limit_bytes=VMEM_LIMIT),
        name="combine",
    )(pos, y, x1, gates, g, b)


def _rope_lane_order(n_sub):
    half = DIFF_HEAD_DIM // 2
    per = LANES // 2 // n_sub
    assert per == half or n_sub == 1
    cols = []
    for part in range(2):
        for sub in range(n_sub):
            cols.extend(sub * 64 + part * half + d for d in range(half))
    return np.asarray(cols)


def _prep_weights(w_in, mla_q_norm_g, w_uq, mla_kv_norm_g, w_ukv):
    o_dq, o_dk, o_dv, o_cq, o_ckv, o_kr = 0, 512, 1024, 1536, 1792, 1920
    head_order = _rope_lane_order(2)
    diff_cols = np.concatenate([h * LANES + head_order for h in range(DIFF_HEADS)])
    w = {}
    w["dqT"] = w_in[:, o_dq + diff_cols].T.astype(BF16)
    w["dk"] = w_in[:, o_dk + diff_cols].astype(BF16)
    w["dvT"] = w_in[:, o_dv:o_cq].T.astype(BF16)
    w["cqT"] = w_in[:, o_cq:o_ckv].T.astype(BF16)
    w["ckv"] = w_in[:, o_ckv:o_kr].astype(BF16)
    w["ckvT"] = w["ckv"].T

    def spread_rope(cols64):
        z = jnp.zeros((cols64.shape[0], 32), cols64.dtype)
        return jnp.concatenate([cols64[:, :32], z, cols64[:, 32:], z], axis=1)

    w["kr"] = spread_rope(w_in[:, o_kr:o_kr + MLA_ROPE_DIM]).astype(BF16)
    uq = []
    for h in range(MLA_HEADS):
        base = h * MLA_QK_DIM
        uq.append(w_uq[:, base:base + MLA_NOPE_DIM])
        uq.append(spread_rope(w_uq[:, base + MLA_NOPE_DIM:base + MLA_QK_DIM]))
    w["uqT"] = jnp.concatenate(uq, axis=1).T.astype(BF16)
    per = MLA_NOPE_DIM + MLA_V_DIM
    w["uk"] = jnp.concatenate([w_ukv[:, h * per:h * per + MLA_NOPE_DIM] for h in range(MLA_HEADS)], axis=1).astype(BF16)
    w["uvT"] = jnp.concatenate(
        [w_ukv[:, h * per + MLA_NOPE_DIM:(h + 1) * per] for h in range(MLA_HEADS)], axis=1).T.astype(BF16)
    w["gq"] = mla_q_norm_g.reshape(MLA_Q_RANK, 1)
    w["gkv"] = mla_kv_norm_g.reshape(1, MLA_KV_RANK)
    w["gkvc"] = mla_kv_norm_g.reshape(MLA_KV_RANK, 1)
    return w


def _rope_tables(positions):
    half = MLA_ROPE_DIM // 2
    inv_freq = 1.0 / (ROPE_THETA ** (jnp.arange(0, MLA_ROPE_DIM, 2, dtype=F32) / MLA_ROPE_DIM))
    ang = positions.astype(F32)[..., None] * inv_freq
    ang = jnp.tile(ang, (1, 1, LANES // half))
    sign = jnp.where(jnp.arange(LANES) < LANES // 2, -1.0, 1.0).astype(F32)
    cos_t, sin_t = jnp.cos(ang), jnp.sin(ang) * sign
    return cos_t, sin_t, cos_t.transpose(0, 2, 1), sin_t.transpose(0, 2, 1)


def _route(top_idx, bm, n_blocks):
    T = top_idx.shape[0]
    A = T * TOP_K
    pos, block_e, n_used = _route_call(top_idx.reshape(A), bm, n_blocks)
    slot_tok = _invert_call(pos, n_blocks * bm)
    return pos, slot_tok, block_e, n_used


def kernel(x, positions, w_in, lambda_q1, lambda_k1, lambda_q2, lambda_k2, subln_g, mla_q_norm_g, w_uq,
           mla_kv_norm_g, w_ukv, w_o, ln1_g, ln1_b, w_router, b_router, w_gate_up, b_gate_up, w_down, b_down,
           ln2_g, ln2_b):
    B, S, D = x.shape
    T = B * S
    l = 0
    x2 = x.reshape(T, D)
    w = _prep_weights(w_in[l], mla_q_norm_g[l], w_uq[l], mla_kv_norm_g[l], w_ukv[l])

    dqT, dk, dvT, mqT, mk, mvT = _proj_call(x, _rope_tables(positions), w)
    lam_vecs = [v[l].reshape(1, DIFF_HEAD_DIM) for v in (lambda_q1, lambda_k1, lambda_q2, lambda_k2)]
    o_a = _attn_call(_diff_attn_kernel, 2, lam_vecs + [subln_g[l].reshape(DIFF_V_DIM, 1)],
                     dqT, dk, dvT, DIFF_HEADS, LANES, "diff_attn")
    o_b = _attn_call(_mla_attn_kernel, 1, [], mqT, mk, mvT, MLA_HEADS, MLA_QK_PAD, "mla_attn")

    wr = jnp.pad(w_router[l], ((0, 0), (0, LANES - N_EXPERTS)))
    br = jnp.pad(b_router[l], (0, LANES - N_EXPERTS), constant_values=NEG_BIG).reshape(1, LANES)
    x1, idx, gates = _post_call(
        o_a.reshape(T, -1), o_b.reshape(T, -1), x2, w_o[l].astype(BF16),
        ln1_g[l].reshape(1, D), ln1_b[l].reshape(1, D), wr, br)

    bm = MOE_BM
    n_blocks = T * TOP_K // bm + N_EXPERTS
    pos, slot_tok, block_e, n_used = _route(idx[:, :TOP_K], bm, n_blocks)
    y = _experts_call(n_used, block_e, slot_tok, x1, w_gate_up[l], b_gate_up[l], w_down[l], b_down[l], n_blocks)
    out = _combine_call(pos, y, x1, gates, ln2_g[l].reshape(1, D), ln2_b[l].reshape(1, D))
    return out.reshape(B, S, D)
```

```python
import functools
import math

import numpy as np
import jax
import jax.numpy as jnp
from jax import lax
from jax.experimental import pallas as pl
from jax.experimental.pallas import tpu as pltpu

D_MODEL = 1024
DIFF_HEADS = 4
DIFF_HEAD_DIM = 64
DIFF_V_DIM = 128
MLA_HEADS = 4
MLA_V_DIM = 128
MLA_NOPE_DIM = 128
MLA_ROPE_DIM = 64
MLA_QK_DIM = MLA_NOPE_DIM + MLA_ROPE_DIM
MLA_Q_RANK = 256
MLA_KV_RANK = 128
ROPE_THETA = 10000.0
N_EXPERTS = 32
TOP_K = 4
D_FF = 1024
SWIGLU_LIMIT = 7.0
SWIGLU_ALPHA = 1.702
LN_EPS = 1e-5
SUBLN_EPS = 1e-5
MLA_RMS_EPS = 1e-6
DEPTH = 1
DN_ALPHA = (2.0 * DEPTH) ** 0.25
LAMBDA_INIT = 0.8 - 0.6 * math.exp(-0.3 * 0)

LANES = 128
MLA_QK_PAD = 2 * LANES
VMEM_LIMIT = 56 * 1024 * 1024

ATTN_TILE = 512
ATTN_QCHUNK = 256
POST_TM = 512
MOE_BM = 256
COMB_TM = 256

NEG_BIG = -1e30
LOG2E = math.log2(math.e)
F32 = jnp.float32
BF16 = jnp.bfloat16


def _dot(a, b):
    return jnp.dot(a, b, preferred_element_type=F32)


def _dot_nt(a, b):
    return lax.dot_general(a, b, (((1,), (1,)), ((), ())), preferred_element_type=F32)


def _rope128(blk, cos, sin):
    return blk * cos + pltpu.roll(blk, 64, axis=1) * sin


def _rope128_t(blk, cos, sin):
    half = LANES // 2
    rolled = jnp.concatenate([blk[half:], blk[:half]], axis=0)
    return blk * cos + rolled * sin


def _rms_rows(t, g, eps):
    return t * lax.rsqrt(jnp.mean(t * t, axis=-1, keepdims=True) + eps) * g


def _rms_cols(t, g, eps):
    return t * lax.rsqrt(jnp.mean(t * t, axis=0, keepdims=True) + eps) * g


def _proj_kernel(x_ref, cos_ref, sin_ref, cosT_ref, sinT_ref,
                 wdqT_ref, wdk_ref, wdvT_ref, wcqT_ref, wckv_ref, wckvT_ref, wkr_ref,
                 gq_ref, gkv_ref, gkvc_ref, wuqT_ref, wuk_ref, wuvT_ref,
                 dqT_ref, dk_ref, dvT_ref, mqT_ref, mk_ref, mvT_ref):
    xb = x_ref[...].astype(BF16)
    cos, sin = cos_ref[...], sin_ref[...]
    cosT, sinT = cosT_ref[...], sinT_ref[...]

    dq_scale = DIFF_HEAD_DIM ** -0.5 * LOG2E
    mq_scale = MLA_QK_DIM ** -0.5 * LOG2E

    dqT = _dot_nt(wdqT_ref[...], xb)
    for h in range(DIFF_HEADS):
        sl = slice(h * LANES, (h + 1) * LANES)
        dqT_ref[sl, :] = (_rope128_t(dqT[sl], cosT, sinT) * dq_scale).astype(BF16)
    dvT_ref[...] = _dot_nt(wdvT_ref[...], xb).astype(BF16)
    dk = _dot(xb, wdk_ref[...])
    for h in range(DIFF_HEADS):
        sl = slice(h * LANES, (h + 1) * LANES)
        dk_ref[:, sl] = _rope128(dk[:, sl], cos, sin).astype(BF16)

    cqT = _rms_cols(_dot_nt(wcqT_ref[...], xb), gq_ref[...], MLA_RMS_EPS)
    qT = _dot(wuqT_ref[...], cqT.astype(BF16))
    for h in range(MLA_HEADS):
        nope = slice(h * MLA_QK_PAD, h * MLA_QK_PAD + LANES)
        ropes = slice(h * MLA_QK_PAD + LANES, (h + 1) * MLA_QK_PAD)
        mqT_ref[nope, :] = (qT[nope] * mq_scale).astype(BF16)
        mqT_ref[ropes, :] = (_rope128_t(qT[ropes], cosT, sinT) * mq_scale).astype(BF16)

    ckvT = _rms_cols(_dot_nt(wckvT_ref[...], xb), gkvc_ref[...], MLA_RMS_EPS)
    mvT_ref[...] = _dot(wuvT_ref[...], ckvT.astype(BF16)).astype(BF16)
    ckv = _rms_rows(_dot(xb, wckv_ref[...]), gkv_ref[...], MLA_RMS_EPS)
    k_nope = _dot(ckv.astype(BF16), wuk_ref[...])
    k_pe = _rope128(_dot(xb, wkr_ref[...]), cos, sin).astype(BF16)
    for h in range(MLA_HEADS):
        mk_ref[:, h * MLA_QK_PAD:h * MLA_QK_PAD + LANES] = k_nope[:, h * LANES:(h + 1) * LANES].astype(BF16)
        mk_ref[:, h * MLA_QK_PAD + LANES:(h + 1) * MLA_QK_PAD] = k_pe


def _proj_call(x3, tabs, w):
    B, S, D = x3.shape
    tm = ATTN_TILE
    nt = S // tm
    cos_t, sin_t, cosT, sinT = tabs
    weights = [w["dqT"], w["dk"], w["dvT"], w["cqT"], w["ckv"], w["ckvT"], w["kr"],
               w["gq"], w["gkv"], w["gkvc"], w["uqT"], w["uk"], w["uvT"]]
    tok = lambda b, i: (b, i, 0)
    feat = lambda b, i: (b, i, 0, 0)
    in_specs = [pl.BlockSpec((None, tm, D), tok),
                pl.BlockSpec((None, tm, LANES), tok), pl.BlockSpec((None, tm, LANES), tok),
                pl.BlockSpec((None, LANES, tm), lambda b, i: (b, 0, i)),
                pl.BlockSpec((None, LANES, tm), lambda b, i: (b, 0, i))]
    in_specs += [pl.BlockSpec(a.shape, lambda b, i: (0, 0)) for a in weights]
    mq_w = MLA_HEADS * MLA_QK_PAD
    out_specs = [pl.BlockSpec((None, None, 512, tm), feat), pl.BlockSpec((None, tm, 512), tok),
                 pl.BlockSpec((None, None, 512, tm), feat), pl.BlockSpec((None, None, mq_w, tm), feat),
                 pl.BlockSpec((None, tm, mq_w), tok), pl.BlockSpec((None, None, 512, tm), feat)]
    out_shape = [jax.ShapeDtypeStruct((B, nt, 512, tm), BF16), jax.ShapeDtypeStruct((B, S, 512), BF16),
                 jax.ShapeDtypeStruct((B, nt, 512, tm), BF16), jax.ShapeDtypeStruct((B, nt, mq_w, tm), BF16),
                 jax.ShapeDtypeStruct((B, S, mq_w), BF16), jax.ShapeDtypeStruct((B, nt, 512, tm), BF16)]
    return pl.pallas_call(
        _proj_kernel,
        grid=(B, nt),
        in_specs=in_specs,
        out_specs=out_specs,
        out_shape=out_shape,
        compiler_params=pltpu.CompilerParams(dimension_semantics=("arbitrary", "arbitrary"),
                                             vmem_limit_bytes=VMEM_LIMIT),
        name="proj",
    )(x3, cos_t, sin_t, cosT, sinT, *weights)


def _flash_body(qTs, k_ref, vT_ref, acc_ref, m_ref, l_ref, *, t):
    i = pl.program_id(2)
    m_ref[...] = jnp.full(m_ref.shape, -jnp.inf, F32)
    l_ref[...] = jnp.zeros(l_ref.shape, F32)
    acc_ref[...] = jnp.zeros(acc_ref.shape, F32)

    qc = ATTN_QCHUNK
    units = [(c, h) for c in range(len(qTs)) for h in range(t // qc)]

    def scores(j, unit):
        c, h = unit
        k = k_ref[pl.ds(pl.multiple_of(j * t, t), t), :]
        return _dot(k, qTs[c][:, h * qc:(h + 1) * qc])

    def finish(j, unit, s, masked):
        c, h = unit
        sl = slice(h * qc, (h + 1) * qc)
        if masked:
            key = lax.broadcasted_iota(jnp.int32, s.shape, 0)
            qry = lax.broadcasted_iota(jnp.int32, s.shape, 1) + h * qc
            s = jnp.where(key <= qry, s, -jnp.inf)
        m_prev = m_ref[c, :, sl]
        m_new = jnp.maximum(m_prev, jnp.max(s, axis=0, keepdims=True))
        alpha = jnp.exp2(m_prev - m_new)
        p = jnp.exp2(s - m_new)
        l_ref[c, :, sl] = alpha * l_ref[c, :, sl] + jnp.sum(p, axis=0, keepdims=True)
        acc_ref[c, :, sl] = alpha * acc_ref[c, :, sl] + _dot(vT_ref[j], p.astype(BF16))
        m_ref[c, :, sl] = m_new

    def step(j, s, masked, j_next):
        for n, unit in enumerate(units):
            if n + 1 < len(units):
                s_next = scores(j, units[n + 1])
            elif j_next is not None:
                s_next = scores(j_next, units[0])
            else:
                s_next = None
            finish(j, unit, s, masked)
            s = s_next
        return s

    s = lax.fori_loop(0, i, lambda j, s: step(j, s, False, j + 1), scores(0, units[0]))
    step(i, s, True, None)


def _diff_attn_kernel(lq1_ref, lk1_ref, lq2_ref, lk2_ref, g_ref, qT_ref, k_ref, vT_ref, o_ref,
                      acc_ref, m_ref, l_ref, *, t):
    qT = qT_ref[...]
    row = lax.broadcasted_iota(jnp.int32, qT.shape, 0)
    first = (row % 64) < 32
    zero = jnp.zeros_like(qT)
    qTs = (jnp.where(first, qT, zero), jnp.where(first, zero, qT))
    _flash_body(qTs, k_ref, vT_ref, acc_ref, m_ref, l_ref, t=t)

    lam = (jnp.exp(jnp.sum(lq1_ref[...] * lk1_ref[...], axis=-1, keepdims=True))
           - jnp.exp(jnp.sum(lq2_ref[...] * lk2_ref[...], axis=-1, keepdims=True)) + LAMBDA_INIT)
    oT = acc_ref[0] / l_ref[0] - lam * (acc_ref[1] / l_ref[1])
    oT = _rms_cols(oT, g_ref[...], SUBLN_EPS) * (1.0 - LAMBDA_INIT)
    o_ref[...] = oT.T.astype(o_ref.dtype)


def _mla_attn_kernel(qT_ref, k_ref, vT_ref, o_ref, acc_ref, m_ref, l_ref, *, t):
    _flash_body((qT_ref[...],), k_ref, vT_ref, acc_ref, m_ref, l_ref, t=t)
    o_ref[...] = (acc_ref[0] / l_ref[0]).T.astype(o_ref.dtype)


def _attn_call(kernel, n_sm, extra, qT, k, vT, heads, dk, name):
    B, nt, _, t = qT.shape
    S = nt * t
    dv = LANES
    in_specs = [pl.BlockSpec(a.shape, lambda b, h, i: (0, 0)) for a in extra]
    in_specs += [
        pl.BlockSpec((None, None, dk, t), lambda b, h, i: (b, i, h, 0)),
        pl.BlockSpec((None, S, dk), lambda b, h, i: (b, 0, h)),
        pl.BlockSpec((None, nt, dv, t), lambda b, h, i: (b, 0, h, 0)),
    ]
    return pl.pallas_call(
        functools.partial(kernel, t=t),
        grid=(B, heads, nt),
        in_specs=in_specs,
        out_specs=pl.BlockSpec((None, t, dv), lambda b, h, i: (b, i, h)),
        out_shape=jax.ShapeDtypeStruct((B, S, heads * dv), BF16),
        scratch_shapes=[
            pltpu.VMEM((n_sm, dv, t), F32),
            pltpu.VMEM((n_sm, 1, t), F32),
            pltpu.VMEM((n_sm, 1, t), F32),
        ],
        compiler_params=pltpu.CompilerParams(
            dimension_semantics=("arbitrary", "arbitrary", "arbitrary"), vmem_limit_bytes=VMEM_LIMIT),
        name=name,
    )(*extra, qT, k, vT)


def _layer_norm(y, g, b):
    mu = jnp.mean(y, axis=-1, keepdims=True)
    d = y - mu
    var = jnp.mean(d * d, axis=-1, keepdims=True)
    return d * lax.rsqrt(var + LN_EPS) * g + b


def _post_kernel(oa_ref, ob_ref, x_ref, wo_ref, g_ref, b_ref, wr_ref, br_ref,
                 x1_ref, idx_ref, gate_ref):
    half = oa_ref.shape[1]
    mixed = _dot(oa_ref[...], wo_ref[:half, :]) + _dot(ob_ref[...], wo_ref[half:, :])
    x1 = _layer_norm(DN_ALPHA * x_ref[...] + mixed, g_ref[...], b_ref[...])
    x1_ref[...] = x1

    logits = jnp.dot(x1, wr_ref[...], preferred_element_type=F32, precision=lax.Precision.HIGHEST) + br_ref[...]
    lane = lax.broadcasted_iota(jnp.int32, logits.shape, 1).astype(F32)
    work = logits
    vals, idxs = [], []
    for _ in range(TOP_K):
        m = jnp.max(work, axis=-1, keepdims=True)
        idx = jnp.min(jnp.where(work == m, lane, float(LANES)), axis=-1, keepdims=True)
        vals.append(m)
        idxs.append(idx)
        work = jnp.where(lane == idx, NEG_BIG, work)
    es = [jnp.exp(v - vals[0]) for v in vals]
    den = es[0] + es[1] + es[2] + es[3]
    idx_out = jnp.zeros(logits.shape, F32)
    gate_out = jnp.zeros(logits.shape, F32)
    for k in range(TOP_K):
        idx_out = jnp.where(lane == float(k), idxs[k], idx_out)
        gate_out = jnp.where(lane == float(k), es[k] / den, gate_out)
    idx_ref[...] = idx_out.astype(jnp.int32)
    gate_ref[...] = gate_out


def _post_call(oa, ob, x2, wo, g, b, wr, br):
    T = x2.shape[0]
    tm = POST_TM
    row = lambda i: (i, 0)
    full = lambda i: (0, 0)
    return pl.pallas_call(
        _post_kernel,
        grid=(T // tm,),
        in_specs=[
            pl.BlockSpec((tm, oa.shape[1]), row), pl.BlockSpec((tm, ob.shape[1]), row),
            pl.BlockSpec((tm, D_MODEL), row), pl.BlockSpec(wo.shape, full),
            pl.BlockSpec(g.shape, full), pl.BlockSpec(b.shape, full),
            pl.BlockSpec(wr.shape, full), pl.BlockSpec(br.shape, full),
        ],
        out_specs=[
            pl.BlockSpec((tm, D_MODEL), row), pl.BlockSpec((tm, LANES), row), pl.BlockSpec((tm, LANES), row),
        ],
        out_shape=[
            jax.ShapeDtypeStruct((T, D_MODEL), F32),
            jax.ShapeDtypeStruct((T, LANES), jnp.int32), jax.ShapeDtypeStruct((T, LANES), F32),
        ],
        compiler_params=pltpu.CompilerParams(dimension_semantics=("arbitrary",), vmem_limit_bytes=VMEM_LIMIT),
        name="post",
    )(oa, ob, x2, wo, g, b, wr, br)


def _route_kernel(e_ref, pos_ref, be_ref, nu_ref, *, bm):
    e = e_ref[...]
    R = e.shape[0]
    r_i = lax.broadcasted_iota(jnp.int32, (LANES, LANES), 0)
    c_i = lax.broadcasted_iota(jnp.int32, (LANES, LANES), 1)
    lane_incl = (r_i <= c_i).astype(BF16)
    rr = lax.broadcasted_iota(jnp.int32, (R, R), 0)
    rc = lax.broadcasted_iota(jnp.int32, (R, R), 1)
    rows_before = (rc < rr).astype(BF16)
    blk = (lax.broadcasted_iota(jnp.int32, be_ref.shape, 0) * LANES
           + lax.broadcasted_iota(jnp.int32, be_ref.shape, 1)).astype(F32) * float(bm)

    pos = jnp.zeros(e.shape, F32)
    block_e = jnp.zeros(be_ref.shape, F32)
    start = jnp.zeros((1, 1), F32)
    for ex in range(N_EXPERTS):
        hit = e == ex
        m = jnp.where(hit, 1.0, 0.0).astype(BF16)
        incl = _dot(m, lane_incl)
        row_tot = jnp.broadcast_to(incl[:, LANES - 1:LANES], e.shape)
        before_rows = _dot(rows_before, row_tot.astype(BF16))
        rank = incl - 1.0 + before_rows
        pos = jnp.where(hit, start + rank, pos)
        count = before_rows[R - 1:R, 0:1] + incl[R - 1:R, LANES - 1:LANES]
        start = start + jnp.floor((count + (bm - 1.0)) * (1.0 / bm)) * bm
        block_e = block_e + jnp.where(blk >= start, 1.0, 0.0)
    pos_ref[...] = pos.astype(jnp.int32)
    be_ref[...] = jnp.minimum(block_e, N_EXPERTS - 1.0).astype(jnp.int32)
    nu_ref[...] = jnp.broadcast_to(start * (1.0 / bm), nu_ref.shape).astype(jnp.int32)


def _route_call(flat_e, bm, n_blocks):
    A = flat_e.shape[0]
    R = A // LANES
    be_rows = 8 * pl.cdiv(pl.cdiv(n_blocks, LANES), 8)
    full = lambda i: (0, 0)
    pos, be, nu = pl.pallas_call(
        functools.partial(_route_kernel, bm=bm),
        grid=(1,),
        in_specs=[pl.BlockSpec((R, LANES), full)],
        out_specs=[pl.BlockSpec((R, LANES), full), pl.BlockSpec((be_rows, LANES), full),
                   pl.BlockSpec((8, LANES), full)],
        out_shape=[jax.ShapeDtypeStruct((R, LANES), jnp.int32), jax.ShapeDtypeStruct((be_rows, LANES), jnp.int32),
                   jax.ShapeDtypeStruct((8, LANES), jnp.int32)],
        compiler_params=pltpu.CompilerParams(dimension_semantics=("arbitrary",), vmem_limit_bytes=VMEM_LIMIT),
        name="route",
    )(flat_e.reshape(R, LANES))
    return pos.reshape(A), be.reshape(-1)[:n_blocks], nu[0, :1]


def _invert_kernel(pos_ref, zeros_hbm, slot_ref, sem, *, n):
    init = pltpu.make_async_copy(zeros_hbm, slot_ref, sem)
    init.start()
    init.wait()

    def body(t, carry):
        for k in range(TOP_K):
            slot_ref[pos_ref[t * TOP_K + k]] = t
        return carry

    lax.fori_loop(0, n // TOP_K, body, 0, unroll=8)


def _invert_call(pos, cap):
    A = pos.shape[0]
    return pl.pallas_call(
        functools.partial(_invert_kernel, n=A),
        grid_spec=pltpu.PrefetchScalarGridSpec(
            num_scalar_prefetch=1,
            grid=(1,),
            in_specs=[pl.BlockSpec(memory_space=pl.ANY)],
            out_specs=pl.BlockSpec(memory_space=pltpu.SMEM),
            scratch_shapes=[pltpu.SemaphoreType.DMA],
        ),
        out_shape=jax.ShapeDtypeStruct((cap,), jnp.int32),
        compiler_params=pltpu.CompilerParams(dimension_semantics=("arbitrary",)),
        name="invert",
    )(pos, jnp.zeros((cap,), jnp.int32))


def _gather_copy(src_hbm, row, dst, dst_row, sem):
    return pltpu.make_async_copy(src_hbm.at[pl.ds(row, 1), :], dst.at[pl.ds(dst_row, 1), :], sem)


def _token_copy(src_hbm, row, dst, dst_row, sem):
    return pltpu.make_async_copy(src_hbm.at[row], dst.at[pl.ds(dst_row, 1), :], sem)


def _experts_kernel(nused_ref, be_ref, tok_ref, xp_hbm, wgu_ref, bgu_ref, wd_ref, bd_ref, y_ref,
                    xs_ref, wgu_b, wd_b, sem, *, bm):
    j = pl.program_id(0)
    n_used = nused_ref[0]

    def issue(blk, slot):
        def body(r, carry):
            _token_copy(xp_hbm, tok_ref[blk * bm + r], xs_ref.at[slot], r, sem.at[slot]).start()
            return carry
        lax.fori_loop(0, bm, body, 0, unroll=8)

    def wait(slot):
        pltpu.make_async_copy(xs_ref.at[slot], xs_ref.at[slot], sem.at[slot]).wait()

    @pl.when(j == 0)
    def _():
        issue(0, 0)

    new_expert = jnp.logical_or(j == 0, be_ref[j] != be_ref[jnp.maximum(j - 1, 0)])

    @pl.when(jnp.logical_and(j < n_used, new_expert))
    def _():
        wgu_b[...] = wgu_ref[...].astype(BF16)
        wd_b[...] = wd_ref[...].astype(BF16)

    @pl.when(j < n_used)
    def _():
        slot = j % 2
        wait(slot)
        nxt = jnp.minimum(j + 1, n_used - 1)
        for r in range(bm):
            _token_copy(xp_hbm, tok_ref[nxt * bm + r], xs_ref.at[1 - slot], r,
                        sem.at[1 - slot]).start(priority=r % 2)
        h = _dot(xs_ref[slot].astype(BF16), wgu_b[...]) + bgu_ref[...]
        gate = jnp.minimum(h[:, :D_FF], SWIGLU_LIMIT)
        up = jnp.clip(h[:, D_FF:], -SWIGLU_LIMIT, SWIGLU_LIMIT)
        act = (up + 1.0) * (gate * jax.nn.sigmoid(gate * SWIGLU_ALPHA))
        y_ref[...] = _dot(act.astype(BF16), wd_b[...]) + bd_ref[...]

    @pl.when(j == n_used - 1)
    def _():
        wait(1 - j % 2)

    @pl.when(j >= n_used)
    def _():
        y_ref[...] = jnp.zeros(y_ref.shape, F32)


def _experts_call(n_used, block_e, slot_tok, xp, wgu, bgu, wd, bd, n_blocks):
    bm = MOE_BM
    E = N_EXPERTS
    grid_spec = pltpu.PrefetchScalarGridSpec(
        num_scalar_prefetch=3,
        grid=(n_blocks,),
        in_specs=[
            pl.BlockSpec(memory_space=pl.ANY),
            pl.BlockSpec((None, D_MODEL, 2 * D_FF), lambda j, nu, be, tok: (be[j], 0, 0)),
            pl.BlockSpec((None, 1, 2 * D_FF), lambda j, nu, be, tok: (be[j], 0, 0)),
            pl.BlockSpec((None, D_FF, D_MODEL), lambda j, nu, be, tok: (be[j], 0, 0)),
            pl.BlockSpec((None, 1, D_MODEL), lambda j, nu, be, tok: (be[j], 0, 0)),
        ],
        out_specs=pl.BlockSpec((bm, D_MODEL), lambda j, nu, be, tok: (j, 0)),
        scratch_shapes=[
            pltpu.VMEM((2, bm, D_MODEL), F32),
            pltpu.VMEM((D_MODEL, 2 * D_FF), BF16),
            pltpu.VMEM((D_FF, D_MODEL), BF16),
            pltpu.SemaphoreType.DMA((2,)),
        ],
    )
    return pl.pallas_call(
        functools.partial(_experts_kernel, bm=bm),
        grid_spec=grid_spec,
        out_shape=jax.ShapeDtypeStruct((n_blocks * bm, D_MODEL), F32),
        compiler_params=pltpu.CompilerParams(dimension_semantics=("arbitrary",), vmem_limit_bytes=VMEM_LIMIT),
        name="experts",
    )(n_used, block_e, slot_tok, xp.reshape(xp.shape[0], 1, xp.shape[1]),
      wgu, bgu.reshape(E, 1, 2 * D_FF), wd, bd.reshape(E, 1, D_MODEL))


def _combine_kernel(pos_ref, y_hbm, x1_ref, gate_ref, g_ref, b_ref, o_ref, ys_ref, sem, *, tm):
    i = pl.program_id(0)
    n = pl.num_programs(0)

    def issue(tile, slot):
        def body(t, carry):
            for k in range(TOP_K):
                row = pos_ref[(tile * tm + t) * TOP_K + k]
                _gather_copy(y_hbm, row, ys_ref.at[slot, k], t, sem.at[slot]).start(priority=k % 2)
            return carry
        lax.fori_loop(0, tm, body, 0, unroll=4)

    def wait(slot):
        for k in range(TOP_K):
            pltpu.make_async_copy(y_hbm.at[pl.ds(0, tm), :], ys_ref.at[slot, k], sem.at[slot]).wait()

    @pl.when(i == 0)
    def _():
        issue(0, 0)

    @pl.when(i + 1 < n)
    def _():
        issue(i + 1, (i + 1) % 2)

    slot = i % 2
    wait(slot)
    gates = gate_ref[...]
    y = gates[:, 0:1] * ys_ref[slot, 0]
    for k in range(1, TOP_K):
        y = y + gates[:, k:k + 1] * ys_ref[slot, k]
    o_ref[...] = _layer_norm(DN_ALPHA * x1_ref[...] + y, g_ref[...], b_ref[...])


def _combine_call(pos, y, x1, gates, g, b):
    T = x1.shape[0]
    tm = COMB_TM
    grid_spec = pltpu.PrefetchScalarGridSpec(
        num_scalar_prefetch=1,
        grid=(T // tm,),
        in_specs=[
            pl.BlockSpec(memory_space=pl.ANY),
            pl.BlockSpec((tm, D_MODEL), lambda i, pos: (i, 0)),
            pl.BlockSpec((tm, LANES), lambda i, pos: (i, 0)),
            pl.BlockSpec(g.shape, lambda i, pos: (0, 0)),
            pl.BlockSpec(b.shape, lambda i, pos: (0, 0)),
        ],
        out_specs=pl.BlockSpec((tm, D_MODEL), lambda i, pos: (i, 0)),
        scratch_shapes=[
            pltpu.VMEM((2, TOP_K, tm, D_MODEL), F32),
            pltpu.SemaphoreType.DMA((2,)),
        ],
    )
    return pl.pallas_call(
        functools.partial(_combine_kernel, tm=tm),
        grid_spec=grid_spec,
        out_shape=jax.ShapeDtypeStruct((T, D_MODEL), F32),
        compiler_params=pltpu.CompilerParams(dimension_semantics=("arbitrary",), vmem_limit_bytes=VMEM_LIMIT),
        name="combine",
    )(pos, y, x1, gates, g, b)


def _rope_lane_order(n_sub):
    half = DIFF_HEAD_DIM // 2
    per = LANES // 2 // n_sub
    assert per == half or n_sub == 1
    cols = []
    for part in range(2):
        for sub in range(n_sub):
            cols.extend(sub * 64 + part * half + d for d in range(half))
    return np.asarray(cols)


def _prep_weights(w_in, mla_q_norm_g, w_uq, mla_kv_norm_g, w_ukv):
    o_dq, o_dk, o_dv, o_cq, o_ckv, o_kr = 0, 512, 1024, 1536, 1792, 1920
    head_order = _rope_lane_order(2)
    diff_cols = np.concatenate([h * LANES + head_order for h in range(DIFF_HEADS)])
    w = {}
    w["dqT"] = w_in[:, o_dq + diff_cols].T.astype(BF16)
    w["dk"] = w_in[:, o_dk + diff_cols].astype(BF16)
    w["dvT"] = w_in[:, o_dv:o_cq].T.astype(BF16)
    w["cqT"] = w_in[:, o_cq:o_ckv].T.astype(BF16)
    w["ckv"] = w_in[:, o_ckv:o_kr].astype(BF16)
    w["ckvT"] = w["ckv"].T

    def spread_rope(cols64):
        z = jnp.zeros((cols64.shape[0], 32), cols64.dtype)
        return jnp.concatenate([cols64[:, :32], z, cols64[:, 32:], z], axis=1)

    w["kr"] = spread_rope(w_in[:, o_kr:o_kr + MLA_ROPE_DIM]).astype(BF16)
    uq = []
    for h in range(MLA_HEADS):
        base = h * MLA_QK_DIM
        uq.append(w_uq[:, base:base + MLA_NOPE_DIM])
        uq.append(spread_rope(w_uq[:, base + MLA_NOPE_DIM:base + MLA_QK_DIM]))
    w["uqT"] = jnp.concatenate(uq, axis=1).T.astype(BF16)
    per = MLA_NOPE_DIM + MLA_V_DIM
    w["uk"] = jnp.concatenate([w_ukv[:, h * per:h * per + MLA_NOPE_DIM] for h in range(MLA_HEADS)], axis=1).astype(BF16)
    w["uvT"] = jnp.concatenate(
        [w_ukv[:, h * per + MLA_NOPE_DIM:(h + 1) * per] for h in range(MLA_HEADS)], axis=1).T.astype(BF16)
    w["gq"] = mla_q_norm_g.reshape(MLA_Q_RANK, 1)
    w["gkv"] = mla_kv_norm_g.reshape(1, MLA_KV_RANK)
    w["gkvc"] = mla_kv_norm_g.reshape(MLA_KV_RANK, 1)
    return w


def _rope_tables(positions):
    half = MLA_ROPE_DIM // 2
    inv_freq = 1.0 / (ROPE_THETA ** (jnp.arange(0, MLA_ROPE_DIM, 2, dtype=F32) / MLA_ROPE_DIM))
    ang = positions.astype(F32)[..., None] * inv_freq
    ang = jnp.tile(ang, (1, 1, LANES // half))
    sign = jnp.where(jnp.arange(LANES) < LANES // 2, -1.0, 1.0).astype(F32)
    cos_t, sin_t = jnp.cos(ang), jnp.sin(ang) * sign
    return cos_t, sin_t, cos_t.transpose(0, 2, 1), sin_t.transpose(0, 2, 1)


def _route(top_idx, bm, n_blocks):
    T = top_idx.shape[0]
    A = T * TOP_K
    pos, block_e, n_used = _route_call(top_idx.reshape(A), bm, n_blocks)
    slot_tok = _invert_call(pos, n_blocks * bm)
    return pos, slot_tok, block_e, n_used


def kernel(x, positions, w_in, lambda_q1, lambda_k1, lambda_q2, lambda_k2, subln_g, mla_q_norm_g, w_uq,
           mla_kv_norm_g, w_ukv, w_o, ln1_g, ln1_b, w_router, b_router, w_gate_up, b_gate_up, w_down, b_down,
           ln2_g, ln2_b):
    B, S, D = x.shape
    T = B * S
    l = 0
    x2 = x.reshape(T, D)
    w = _prep_weights(w_in[l], mla_q_norm_g[l], w_uq[l], mla_kv_norm_g[l], w_ukv[l])

    dqT, dk, dvT, mqT, mk, mvT = _proj_call(x, _rope_tables(positions), w)
    lam_vecs = [v[l].reshape(1, DIFF_HEAD_DIM) for v in (lambda_q1, lambda_k1, lambda_q2, lambda_k2)]
    o_a = _attn_call(_diff_attn_kernel, 2, lam_vecs + [subln_g[l].reshape(DIFF_V_DIM, 1)],
                     dqT, dk, dvT, DIFF_HEADS, LANES, "diff_attn")
    o_b = _attn_call(_mla_attn_kernel, 1, [], mqT, mk, mvT, MLA_HEADS, MLA_QK_PAD, "mla_attn")

    wr = jnp.pad(w_router[l], ((0, 0), (0, LANES - N_EXPERTS)))
    br = jnp.pad(b_router[l], (0, LANES - N_EXPERTS), constant_values=NEG_BIG).reshape(1, LANES)
    x1, idx, gates = _post_call(
        o_a.reshape(T, -1), o_b.reshape(T, -1), x2, w_o[l].astype(BF16),
        ln1_g[l].reshape(1, D), ln1_b[l].reshape(1, D), wr, br)

    bm = MOE_BM
    n_blocks = T * TOP_K // bm + N_EXPERTS
    pos, slot_tok, block_e, n_used = _route(idx[:, :TOP_K], bm, n_blocks)
    y = _experts_call(n_used, block_e, slot_tok, x1, w_gate_up[l], b_gate_up[l], w_down[l], b_down[l], n_blocks)
    out = _combine_call(pos, y, x1, gates, ln2_g[l].reshape(1, D), ln2_b[l].reshape(1, D))
    return out.reshape(B, S, D)
```

```python
import functools
import math

import numpy as np
import jax
import jax.numpy as jnp
from jax import lax
from jax.experimental import pallas as pl
from jax.experimental.pallas import tpu as pltpu

D_MODEL = 1024
DIFF_HEADS = 4
DIFF_HEAD_DIM = 64
DIFF_V_DIM = 128
MLA_HEADS = 4
MLA_V_DIM = 128
MLA_NOPE_DIM = 128
MLA_ROPE_DIM = 64
MLA_QK_DIM = MLA_NOPE_DIM + MLA_ROPE_DIM
MLA_Q_RANK = 256
MLA_KV_RANK = 128
ROPE_THETA = 10000.0
N_EXPERTS = 32
TOP_K = 4
D_FF = 1024
SWIGLU_LIMIT = 7.0
SWIGLU_ALPHA = 1.702
LN_EPS = 1e-5
SUBLN_EPS = 1e-5
MLA_RMS_EPS = 1e-6
DEPTH = 1
DN_ALPHA = (2.0 * DEPTH) ** 0.25
LAMBDA_INIT = 0.8 - 0.6 * math.exp(-0.3 * 0)

LANES = 128
MLA_QK_PAD = 2 * LANES
VMEM_LIMIT = 56 * 1024 * 1024

ATTN_TILE = 512
ATTN_QCHUNK = 256
POST_TM = 512
MOE_BM = 256
COMB_TM = 256

NEG_BIG = -1e30
LOG2E = math.log2(math.e)
F32 = jnp.float32
BF16 = jnp.bfloat16


def _dot(a, b):
    return jnp.dot(a, b, preferred_element_type=F32)


def _dot_nt(a, b):
    return lax.dot_general(a, b, (((1,), (1,)), ((), ())), preferred_element_type=F32)


def _rope128(blk, cos, sin):
    return blk * cos + pltpu.roll(blk, 64, axis=1) * sin


def _rope128_t(blk, cos, sin):
    half = LANES // 2
    rolled = jnp.concatenate([blk[half:], blk[:half]], axis=0)
    return blk * cos + rolled * sin


def _rms_rows(t, g, eps):
    return t * lax.rsqrt(jnp.mean(t * t, axis=-1, keepdims=True) + eps) * g


def _rms_cols(t, g, eps):
    return t * lax.rsqrt(jnp.mean(t * t, axis=0, keepdims=True) + eps) * g


def _proj_kernel(x_ref, cos_ref, sin_ref, cosT_ref, sinT_ref,
                 wdqT_ref, wdk_ref, wdvT_ref, wcqT_ref, wckv_ref, wckvT_ref, wkr_ref,
                 gq_ref, gkv_ref, gkvc_ref, wuqT_ref, wuk_ref, wuvT_ref,
                 dqT_ref, dk_ref, dvT_ref, mqT_ref, mk_ref, mvT_ref):
    xb = x_ref[...].astype(BF16)
    cos, sin = cos_ref[...], sin_ref[...]
    cosT, sinT = cosT_ref[...], sinT_ref[...]

    dq_scale = DIFF_HEAD_DIM ** -0.5 * LOG2E
    mq_scale = MLA_QK_DIM ** -0.5 * LOG2E

    dqT = _dot_nt(wdqT_ref[...], xb)
    for h in range(DIFF_HEADS):
        sl = slice(h * LANES, (h + 1) * LANES)
        dqT_ref[sl, :] = (_rope128_t(dqT[sl], cosT, sinT) * dq_scale).astype(BF16)
    dvT_ref[...] = _dot_nt(wdvT_ref[...], xb).astype(BF16)
    dk = _dot(xb, wdk_ref[...])
    for h in range(DIFF_HEADS):
        sl = slice(h * LANES, (h + 1) * LANES)
        dk_ref[:, sl] = _rope128(dk[:, sl], cos, sin).astype(BF16)

    cqT = _rms_cols(_dot_nt(wcqT_ref[...], xb), gq_ref[...], MLA_RMS_EPS)
    qT = _dot(wuqT_ref[...], cqT.astype(BF16))
    for h in range(MLA_HEADS):
        nope = slice(h * MLA_QK_PAD, h * MLA_QK_PAD + LANES)
        ropes = slice(h * MLA_QK_PAD + LANES, (h + 1) * MLA_QK_PAD)
        mqT_ref[nope, :] = (qT[nope] * mq_scale).astype(BF16)
        mqT_ref[ropes, :] = (_rope128_t(qT[ropes], cosT, sinT) * mq_scale).astype(BF16)

    ckvT = _rms_cols(_dot_nt(wckvT_ref[...], xb), gkvc_ref[...], MLA_RMS_EPS)
    mvT_ref[...] = _dot(wuvT_ref[...], ckvT.astype(BF16)).astype(BF16)
    ckv = _rms_rows(_dot(xb, wckv_ref[...]), gkv_ref[...], MLA_RMS_EPS)
    k_nope = _dot(ckv.astype(BF16), wuk_ref[...])
    k_pe = _rope128(_dot(xb, wkr_ref[...]), cos, sin).astype(BF16)
    for h in range(MLA_HEADS):
        mk_ref[:, h * MLA_QK_PAD:h * MLA_QK_PAD + LANES] = k_nope[:, h * LANES:(h + 1) * LANES].astype(BF16)
        mk_ref[:, h * MLA_QK_PAD + LANES:(h + 1) * MLA_QK_PAD] = k_pe


def _proj_call(x3, tabs, w):
    B, S, D = x3.shape
    tm = ATTN_TILE
    nt = S // tm
    cos_t, sin_t, cosT, sinT = tabs
    weights = [w["dqT"], w["dk"], w["dvT"], w["cqT"], w["ckv"], w["ckvT"], w["kr"],
               w["gq"], w["gkv"], w["gkvc"], w["uqT"], w["uk"], w["uvT"]]
    tok = lambda b, i: (b, i, 0)
    feat = lambda b, i: (b, i, 0, 0)
    in_specs = [pl.BlockSpec((None, tm, D), tok),
                pl.BlockSpec((None, tm, LANES), tok), pl.BlockSpec((None, tm, LANES), tok),
                pl.BlockSpec((None, LANES, tm), lambda b, i: (b, 0, i)),
                pl.BlockSpec((None, LANES, tm), lambda b, i: (b, 0, i))]
    in_specs += [pl.BlockSpec(a.shape, lambda b, i: (0, 0)) for a in weights]
    mq_w = MLA_HEADS * MLA_QK_PAD
    out_specs = [pl.BlockSpec((None, None, 512, tm), feat), pl.BlockSpec((None, tm, 512), tok),
                 pl.BlockSpec((None, None, 512, tm), feat), pl.BlockSpec((None, None, mq_w, tm), feat),
                 pl.BlockSpec((None, tm, mq_w), tok), pl.BlockSpec((None, None, 512, tm), feat)]
    out_shape = [jax.ShapeDtypeStruct((B, nt, 512, tm), BF16), jax.ShapeDtypeStruct((B, S, 512), BF16),
                 jax.ShapeDtypeStruct((B, nt, 512, tm), BF16), jax.ShapeDtypeStruct((B, nt, mq_w, tm), BF16),
                 jax.ShapeDtypeStruct((B, S, mq_w), BF16), jax.ShapeDtypeStruct((B, nt, 512, tm), BF16)]
    return pl.pallas_call(
        _proj_kernel,
        grid=(B, nt),
        in_specs=in_specs,
        out_specs=out_specs,
        out_shape=out_shape,
        compiler_params=pltpu.CompilerParams(dimension_semantics=("arbitrary", "arbitrary"),
                                             vmem_limit_bytes=VMEM_LIMIT),
        name="proj",
    )(x3, cos_t, sin_t, cosT, sinT, *weights)


def _flash_body(qTs, k_ref, vT_ref, acc_ref, m_ref, l_ref, *, t):
    i = pl.program_id(2)
    m_ref[...] = jnp.full(m_ref.shape, -jnp.inf, F32)
    l_ref[...] = jnp.zeros(l_ref.shape, F32)
    acc_ref[...] = jnp.zeros(acc_ref.shape, F32)

    qc = ATTN_QCHUNK
    units = [(c, h) for c in range(len(qTs)) for h in range(t // qc)]

    def scores(j, unit):
        c, h = unit
        k = k_ref[pl.ds(pl.multiple_of(j * t, t), t), :]
        return _dot(k, qTs[c][:, h * qc:(h + 1) * qc])

    def finish(j, unit, s, masked):
        c, h = unit
        sl = slice(h * qc, (h + 1) * qc)
        if masked:
            key = lax.broadcasted_iota(jnp.int32, s.shape, 0)
            qry = lax.broadcasted_iota(jnp.int32, s.shape, 1) + h * qc
            s = jnp.where(key <= qry, s, -jnp.inf)
        m_prev = m_ref[c, :, sl]
        m_new = jnp.maximum(m_prev, jnp.max(s, axis=0, keepdims=True))
        alpha = jnp.exp2(m_prev - m_new)
        p = jnp.exp2(s - m_new)
        l_ref[c, :, sl] = alpha * l_ref[c, :, sl] + jnp.sum(p, axis=0, keepdims=True)
        acc_ref[c, :, sl] = alpha * acc_ref[c, :, sl] + _dot(vT_ref[j], p.astype(BF16))
        m_ref[c, :, sl] = m_new

    def step(j, s, masked, j_next):
        for n, unit in enumerate(units):
            if n + 1 < len(units):
                s_next = scores(j, units[n + 1])
            elif j_next is not None:
                s_next = scores(j_next, units[0])
            else:
                s_next = None
            finish(j, unit, s, masked)
            s = s_next
        return s

    s = lax.fori_loop(0, i, lambda j, s: step(j, s, False, j + 1), scores(0, units[0]))
    step(i, s, True, None)


def _diff_attn_kernel(lq1_ref, lk1_ref, lq2_ref, lk2_ref, g_ref, qT_ref, k_ref, vT_ref, o_ref,
                      acc_ref, m_ref, l_ref, *, t):
    qT = qT_ref[...]
    row = lax.broadcasted_iota(jnp.int32, qT.shape, 0)
    first = (row % 64) < 32
    zero = jnp.zeros_like(qT)
    qTs = (jnp.where(first, qT, zero), jnp.where(first, zero, qT))
    _flash_body(qTs, k_ref, vT_ref, acc_ref, m_ref, l_ref, t=t)

    lam = (jnp.exp(jnp.sum(lq1_ref[...] * lk1_ref[...], axis=-1, keepdims=True))
           - jnp.exp(jnp.sum(lq2_ref[...] * lk2_ref[...], axis=-1, keepdims=True)) + LAMBDA_INIT)
    oT = acc_ref[0] / l_ref[0] - lam * (acc_ref[1] / l_ref[1])
    oT = _rms_cols(oT, g_ref[...], SUBLN_EPS) * (1.0 - LAMBDA_INIT)
    o_ref[...] = oT.T.astype(o_ref.dtype)


def _mla_attn_kernel(qT_ref, k_ref, vT_ref, o_ref, acc_ref, m_ref, l_ref, *, t):
    _flash_body((qT_ref[...],), k_ref, vT_ref, acc_ref, m_ref, l_ref, t=t)
    o_ref[...] = (acc_ref[0] / l_ref[0]).T.astype(o_ref.dtype)


def _attn_call(kernel, n_sm, extra, qT, k, vT, heads, dk, name):
    B, nt, _, t = qT.shape
    S = nt * t
    dv = LANES
    in_specs = [pl.BlockSpec(a.shape, lambda b, h, i: (0, 0)) for a in extra]
    in_specs += [
        pl.BlockSpec((None, None, dk, t), lambda b, h, i: (b, i, h, 0)),
        pl.BlockSpec((None, S, dk), lambda b, h, i: (b, 0, h)),
        pl.BlockSpec((None, nt, dv, t), lambda b, h, i: (b, 0, h, 0)),
    ]
    return pl.pallas_call(
        functools.partial(kernel, t=t),
        grid=(B, heads, nt),
        in_specs=in_specs,
        out_specs=pl.BlockSpec((None, t, dv), lambda b, h, i: (b, i, h)),
        out_shape=jax.ShapeDtypeStruct((B, S, heads * dv), BF16),
        scratch_shapes=[
            pltpu.VMEM((n_sm, dv, t), F32),
            pltpu.VMEM((n_sm, 1, t), F32),
            pltpu.VMEM((n_sm, 1, t), F32),
        ],
        compiler_params=pltpu.CompilerParams(
            dimension_semantics=("arbitrary", "arbitrary", "arbitrary"), vmem_limit_bytes=VMEM_LIMIT),
        name=name,
    )(*extra, qT, k, vT)


def _layer_norm(y, g, b):
    mu = jnp.mean(y, axis=-1, keepdims=True)
    d = y - mu
    var = jnp.mean(d * d, axis=-1, keepdims=True)
    return d * lax.rsqrt(var + LN_EPS) * g + b


def _post_kernel(oa_ref, ob_ref, x_ref, wo_ref, g_ref, b_ref, wr_ref, br_ref,
                 x1_ref, idx_ref, gate_ref):
    half = oa_ref.shape[1]
    mixed = _dot(oa_ref[...], wo_ref[:half, :]) + _dot(ob_ref[...], wo_ref[half:, :])
    x1 = _layer_norm(DN_ALPHA * x_ref[...] + mixed, g_ref[...], b_ref[...])
    x1_ref[...] = x1

    logits = jnp.dot(x1, wr_ref[...], preferred_element_type=F32, precision=lax.Precision.HIGHEST) + br_ref[...]
    lane = lax.broadcasted_iota(jnp.int32, logits.shape, 1).astype(F32)
    work = logits
    vals, idxs = [], []
    for _ in range(TOP_K):
        m = jnp.max(work, axis=-1, keepdims=True)
        idx = jnp.min(jnp.where(work == m, lane, float(LANES)), axis=-1, keepdims=True)
        vals.append(m)
        idxs.append(idx)
        work = jnp.where(lane == idx, NEG_BIG, work)
    es = [jnp.exp(v - vals[0]) for v in vals]
    den = es[0] + es[1] + es[2] + es[3]
    idx_out = jnp.zeros(logits.shape, F32)
    gate_out = jnp.zeros(logits.shape, F32)
    for k in range(TOP_K):
        idx_out = jnp.where(lane == float(k), idxs[k], idx_out)
        gate_out = jnp.where(lane == float(k), es[k] / den, gate_out)
    idx_ref[...] = idx_out.astype(jnp.int32)
    gate_ref[...] = gate_out


def _post_call(oa, ob, x2, wo, g, b, wr, br):
    T = x2.shape[0]
    tm = POST_TM
    row = lambda i: (i, 0)
    full = lambda i: (0, 0)
    return pl.pallas_call(
        _post_kernel,
        grid=(T // tm,),
        in_specs=[
            pl.BlockSpec((tm, oa.shape[1]), row), pl.BlockSpec((tm, ob.shape[1]), row),
            pl.BlockSpec((tm, D_MODEL), row), pl.BlockSpec(wo.shape, full),
            pl.BlockSpec(g.shape, full), pl.BlockSpec(b.shape, full),
            pl.BlockSpec(wr.shape, full), pl.BlockSpec(br.shape, full),
        ],
        out_specs=[
            pl.BlockSpec((tm, D_MODEL), row), pl.BlockSpec((tm, LANES), row), pl.BlockSpec((tm, LANES), row),
        ],
        out_shape=[
            jax.ShapeDtypeStruct((T, D_MODEL), F32),
            jax.ShapeDtypeStruct((T, LANES), jnp.int32), jax.ShapeDtypeStruct((T, LANES), F32),
        ],
        compiler_params=pltpu.CompilerParams(dimension_semantics=("arbitrary",), vmem_limit_bytes=VMEM_LIMIT),
        name="post",
    )(oa, ob, x2, wo, g, b, wr, br)


def _route_kernel(e_ref, pos_ref, gs_ref, *, bm):
    e = e_ref[...]
    R = e.shape[0]
    r_i = lax.broadcasted_iota(jnp.int32, (LANES, LANES), 0)
    c_i = lax.broadcasted_iota(jnp.int32, (LANES, LANES), 1)
    lane_incl = (r_i <= c_i).astype(BF16)
    rr = lax.broadcasted_iota(jnp.int32, (R, R), 0)
    rc = lax.broadcasted_iota(jnp.int32, (R, R), 1)
    rows_before = (rc < rr).astype(BF16)
    gs_lane = lax.broadcasted_iota(jnp.int32, gs_ref.shape, 1)

    pos = jnp.zeros(e.shape, F32)
    gs = jnp.zeros(gs_ref.shape, F32)
    start = jnp.zeros((1, 1), F32)
    for ex in range(N_EXPERTS):
        hit = e == ex
        m = jnp.where(hit, 1.0, 0.0).astype(BF16)
        incl = _dot(m, lane_incl)
        row_tot = jnp.broadcast_to(incl[:, LANES - 1:LANES], e.shape)
        before_rows = _dot(rows_before, row_tot.astype(BF16))
        rank = incl - 1.0 + before_rows
        pos = jnp.where(hit, start + rank, pos)
        count = before_rows[R - 1:R, 0:1] + incl[R - 1:R, LANES - 1:LANES]
        start = start + jnp.floor((count + (bm - 1.0)) * (1.0 / bm)) * bm
        gs = jnp.where(gs_lane == ex + 1, start * (1.0 / bm), gs)
    pos_ref[...] = pos.astype(jnp.int32)
    gs_ref[...] = gs.astype(jnp.int32)


def _route_call(flat_e, bm):
    A = flat_e.shape[0]
    R = A // LANES
    full = lambda i: (0, 0)
    pos, gs = pl.pallas_call(
        functools.partial(_route_kernel, bm=bm),
        grid=(1,),
        in_specs=[pl.BlockSpec((R, LANES), full)],
        out_specs=[pl.BlockSpec((R, LANES), full), pl.BlockSpec((8, LANES), full)],
        out_shape=[jax.ShapeDtypeStruct((R, LANES), jnp.int32), jax.ShapeDtypeStruct((8, LANES), jnp.int32)],
        compiler_params=pltpu.CompilerParams(dimension_semantics=("arbitrary",), vmem_limit_bytes=VMEM_LIMIT),
        name="route",
    )(flat_e.reshape(R, LANES))
    return pos.reshape(A), gs[0]


def _invert_kernel(pos_ref, zeros_hbm, slot_ref, sem, *, n):
    init = pltpu.make_async_copy(zeros_hbm, slot_ref, sem)
    init.start()
    init.wait()

    def body(t, carry):
        for k in range(TOP_K):
            slot_ref[pos_ref[t * TOP_K + k]] = t
        return carry

    lax.fori_loop(0, n // TOP_K, body, 0, unroll=8)


def _invert_call(pos, cap):
    A = pos.shape[0]
    return pl.pallas_call(
        functools.partial(_invert_kernel, n=A),
        grid_spec=pltpu.PrefetchScalarGridSpec(
            num_scalar_prefetch=1,
            grid=(1,),
            in_specs=[pl.BlockSpec(memory_space=pl.ANY)],
            out_specs=pl.BlockSpec(memory_space=pltpu.SMEM),
            scratch_shapes=[pltpu.SemaphoreType.DMA],
        ),
        out_shape=jax.ShapeDtypeStruct((cap,), jnp.int32),
        compiler_params=pltpu.CompilerParams(dimension_semantics=("arbitrary",)),
        name="invert",
    )(pos, jnp.zeros((cap,), jnp.int32))


def _gather_copy(src_hbm, row, dst, dst_row, sem):
    return pltpu.make_async_copy(src_hbm.at[pl.ds(row, 1), :], dst.at[pl.ds(dst_row, 1), :], sem)


def _token_copy(src_hbm, row, dst, dst_row, sem):
    return pltpu.make_async_copy(src_hbm.at[row], dst.at[pl.ds(dst_row, 1), :], sem)


def _experts_kernel(gs_ref, tok_ref, x_hbm, wgu_ref, bgu_ref, wd_ref, bd_ref, y_hbm,
                    xs_ref, ys_ref, wgu_b, wd_b, gsem, ysem, *, bm, n_blocks):
    e = pl.program_id(0)
    n_used = gs_ref[N_EXPERTS]

    def gather_wait(slot):
        pltpu.make_async_copy(xs_ref.at[slot], xs_ref.at[slot], gsem.at[slot]).wait()

    def y_copy(blk, slot):
        return pltpu.make_async_copy(ys_ref.at[slot], y_hbm.at[pl.ds(blk * bm, bm), :], ysem.at[slot])

    @pl.when(e == 0)
    def _():
        def body(r, carry):
            _token_copy(x_hbm, tok_ref[r], xs_ref.at[0], r, gsem.at[0]).start()
            return carry
        lax.fori_loop(0, bm, body, 0, unroll=8)

    wgu_b[...] = wgu_ref[...].astype(BF16)
    wd_b[...] = wd_ref[...].astype(BF16)

    def block(blk, carry):
        slot = blk % 2
        gather_wait(slot)
        nxt = jnp.minimum(blk + 1, n_used - 1)
        for r in range(bm):
            _token_copy(x_hbm, tok_ref[nxt * bm + r], xs_ref.at[1 - slot], r,
                        gsem.at[1 - slot]).start(priority=r % 2)
        h = _dot(xs_ref[slot].astype(BF16), wgu_b[...]) + bgu_ref[...]
        gate = jnp.minimum(h[:, :D_FF], SWIGLU_LIMIT)
        up = jnp.clip(h[:, D_FF:], -SWIGLU_LIMIT, SWIGLU_LIMIT)
        act = (up + 1.0) * (gate * jax.nn.sigmoid(gate * SWIGLU_ALPHA))
        y = _dot(act.astype(BF16), wd_b[...]) + bd_ref[...]

        @pl.when(blk >= 2)
        def _():
            y_copy(blk - 2, slot).wait()

        ys_ref[slot] = y
        y_copy(blk, slot).start()
        return carry

    lax.fori_loop(gs_ref[e], gs_ref[e + 1], block, 0)

    @pl.when(e == N_EXPERTS - 1)
    def _():
        gather_wait(n_used % 2)
        y_copy(n_used - 1, (n_used - 1) % 2).wait()

        @pl.when(n_used >= 2)
        def _():
            y_copy(n_used - 2, n_used % 2).wait()

        ys_ref[0] = jnp.zeros(ys_ref.shape[1:], F32)

        def fill(blk, carry):
            y_copy(blk, 0).start()
            y_copy(blk, 0).wait()
            return carry

        lax.fori_loop(n_used, n_blocks, fill, 0)


def _experts_call(gs, slot_tok, x1, wgu, bgu, wd, bd, n_blocks):
    bm = MOE_BM
    E = N_EXPERTS
    T = x1.shape[0]
    by_expert = lambda e, gs, tok: (e, 0, 0)
    grid_spec = pltpu.PrefetchScalarGridSpec(
        num_scalar_prefetch=2,
        grid=(E,),
        in_specs=[
            pl.BlockSpec(memory_space=pl.ANY),
            pl.BlockSpec((None, D_MODEL, 2 * D_FF), by_expert),
            pl.BlockSpec((None, 1, 2 * D_FF), by_expert),
            pl.BlockSpec((None, D_FF, D_MODEL), by_expert),
            pl.BlockSpec((None, 1, D_MODEL), by_expert),
        ],
        out_specs=pl.BlockSpec(memory_space=pl.ANY),
        scratch_shapes=[
            pltpu.VMEM((2, bm, D_MODEL), F32),
            pltpu.VMEM((2, bm, D_MODEL), F32),
            pltpu.VMEM((D_MODEL, 2 * D_FF), BF16),
            pltpu.VMEM((D_FF, D_MODEL), BF16),
            pltpu.SemaphoreType.DMA((2,)),
            pltpu.SemaphoreType.DMA((2,)),
        ],
    )
    return pl.pallas_call(
        functools.partial(_experts_kernel, bm=bm, n_blocks=n_blocks),
        grid_spec=grid_spec,
        out_shape=jax.ShapeDtypeStruct((n_blocks * bm, D_MODEL), F32),
        compiler_params=pltpu.CompilerParams(dimension_semantics=("arbitrary",), vmem_limit_bytes=VMEM_LIMIT),
        name="experts",
    )(gs, slot_tok, x1.reshape(T, 1, D_MODEL), wgu, bgu.reshape(E, 1, 2 * D_FF), wd, bd.reshape(E, 1, D_MODEL))


def _combine_kernel(pos_ref, y_hbm, x1_ref, gate_ref, g_ref, b_ref, o_ref, ys_ref, sem, *, tm):
    i = pl.program_id(0)
    n = pl.num_programs(0)

    def issue(tile, slot):
        def body(t, carry):
            for k in range(TOP_K):
                row = pos_ref[(tile * tm + t) * TOP_K + k]
                _gather_copy(y_hbm, row, ys_ref.at[slot, k], t, sem.at[slot]).start(priority=k % 2)
            return carry
        lax.fori_loop(0, tm, body, 0, unroll=4)

    def wait(slot):
        for k in range(TOP_K):
            pltpu.make_async_copy(y_hbm.at[pl.ds(0, tm), :], ys_ref.at[slot, k], sem.at[slot]).wait()

    @pl.when(i == 0)
    def _():
        issue(0, 0)

    @pl.when(i + 1 < n)
    def _():
        issue(i + 1, (i + 1) % 2)

    slot = i % 2
    wait(slot)
    gates = gate_ref[...]
    y = gates[:, 0:1] * ys_ref[slot, 0]
    for k in range(1, TOP_K):
        y = y + gates[:, k:k + 1] * ys_ref[slot, k]
    o_ref[...] = _layer_norm(DN_ALPHA * x1_ref[...] + y, g_ref[...], b_ref[...])


def _combine_call(pos, y, x1, gates, g, b):
    T = x1.shape[0]
    tm = COMB_TM
    grid_spec = pltpu.PrefetchScalarGridSpec(
        num_scalar_prefetch=1,
        grid=(T // tm,),
        in_specs=[
            pl.BlockSpec(memory_space=pl.ANY),
            pl.BlockSpec((tm, D_MODEL), lambda i, pos: (i, 0)),
            pl.BlockSpec((tm, LANES), lambda i, pos: (i, 0)),
            pl.BlockSpec(g.shape, lambda i, pos: (0, 0)),
            pl.BlockSpec(b.shape, lambda i, pos: (0, 0)),
        ],
        out_specs=pl.BlockSpec((tm, D_MODEL), lambda i, pos: (i, 0)),
        scratch_shapes=[
            pltpu.VMEM((2, TOP_K, tm, D_MODEL), F32),
            pltpu.SemaphoreType.DMA((2,)),
        ],
    )
    return pl.pallas_call(
        functools.partial(_combine_kernel, tm=tm),
        grid_spec=grid_spec,
        out_shape=jax.ShapeDtypeStruct((T, D_MODEL), F32),
        compiler_params=pltpu.CompilerParams(dimension_semantics=("arbitrary",), vmem_limit_bytes=VMEM_LIMIT),
        name="combine",
    )(pos, y, x1, gates, g, b)


def _rope_lane_order(n_sub):
    half = DIFF_HEAD_DIM // 2
    per = LANES // 2 // n_sub
    assert per == half or n_sub == 1
    cols = []
    for part in range(2):
        for sub in range(n_sub):
            cols.extend(sub * 64 + part * half + d for d in range(half))
    return np.asarray(cols)


def _prep_weights(w_in, mla_q_norm_g, w_uq, mla_kv_norm_g, w_ukv):
    o_dq, o_dk, o_dv, o_cq, o_ckv, o_kr = 0, 512, 1024, 1536, 1792, 1920
    head_order = _rope_lane_order(2)
    diff_cols = np.concatenate([h * LANES + head_order for h in range(DIFF_HEADS)])
    w = {}
    w["dqT"] = w_in[:, o_dq + diff_cols].T.astype(BF16)
    w["dk"] = w_in[:, o_dk + diff_cols].astype(BF16)
    w["dvT"] = w_in[:, o_dv:o_cq].T.astype(BF16)
    w["cqT"] = w_in[:, o_cq:o_ckv].T.astype(BF16)
    w["ckv"] = w_in[:, o_ckv:o_kr].astype(BF16)
    w["ckvT"] = w["ckv"].T

    def spread_rope(cols64):
        z = jnp.zeros((cols64.shape[0], 32), cols64.dtype)
        return jnp.concatenate([cols64[:, :32], z, cols64[:, 32:], z], axis=1)

    w["kr"] = spread_rope(w_in[:, o_kr:o_kr + MLA_ROPE_DIM]).astype(BF16)
    uq = []
    for h in range(MLA_HEADS):
        base = h * MLA_QK_DIM
        uq.append(w_uq[:, base:base + MLA_NOPE_DIM])
        uq.append(spread_rope(w_uq[:, base + MLA_NOPE_DIM:base + MLA_QK_DIM]))
    w["uqT"] = jnp.concatenate(uq, axis=1).T.astype(BF16)
    per = MLA_NOPE_DIM + MLA_V_DIM
    w["uk"] = jnp.concatenate([w_ukv[:, h * per:h * per + MLA_NOPE_DIM] for h in range(MLA_HEADS)], axis=1).astype(BF16)
    w["uvT"] = jnp.concatenate(
        [w_ukv[:, h * per + MLA_NOPE_DIM:(h + 1) * per] for h in range(MLA_HEADS)], axis=1).T.astype(BF16)
    w["gq"] = mla_q_norm_g.reshape(MLA_Q_RANK, 1)
    w["gkv"] = mla_kv_norm_g.reshape(1, MLA_KV_RANK)
    w["gkvc"] = mla_kv_norm_g.reshape(MLA_KV_RANK, 1)
    return w


def _rope_tables(positions):
    half = MLA_ROPE_DIM // 2
    inv_freq = 1.0 / (ROPE_THETA ** (jnp.arange(0, MLA_ROPE_DIM, 2, dtype=F32) / MLA_ROPE_DIM))
    ang = positions.astype(F32)[..., None] * inv_freq
    ang = jnp.tile(ang, (1, 1, LANES // half))
    sign = jnp.where(jnp.arange(LANES) < LANES // 2, -1.0, 1.0).astype(F32)
    cos_t, sin_t = jnp.cos(ang), jnp.sin(ang) * sign
    return cos_t, sin_t, cos_t.transpose(0, 2, 1), sin_t.transpose(0, 2, 1)


def _route(top_idx, bm, n_blocks):
    T = top_idx.shape[0]
    A = T * TOP_K
    pos, gs = _route_call(top_idx.reshape(A), bm)
    slot_tok = _invert_call(pos, n_blocks * bm)
    return pos, slot_tok, gs


def kernel(x, positions, w_in, lambda_q1, lambda_k1, lambda_q2, lambda_k2, subln_g, mla_q_norm_g, w_uq,
           mla_kv_norm_g, w_ukv, w_o, ln1_g, ln1_b, w_router, b_router, w_gate_up, b_gate_up, w_down, b_down,
           ln2_g, ln2_b):
    B, S, D = x.shape
    T = B * S
    l = 0
    x2 = x.reshape(T, D)
    w = _prep_weights(w_in[l], mla_q_norm_g[l], w_uq[l], mla_kv_norm_g[l], w_ukv[l])

    dqT, dk, dvT, mqT, mk, mvT = _proj_call(x, _rope_tables(positions), w)
    lam_vecs = [v[l].reshape(1, DIFF_HEAD_DIM) for v in (lambda_q1, lambda_k1, lambda_q2, lambda_k2)]
    o_a = _attn_call(_diff_attn_kernel, 2, lam_vecs + [subln_g[l].reshape(DIFF_V_DIM, 1)],
                     dqT, dk, dvT, DIFF_HEADS, LANES, "diff_attn")
    o_b = _attn_call(_mla_attn_kernel, 1, [], mqT, mk, mvT, MLA_HEADS, MLA_QK_PAD, "mla_attn")

    wr = jnp.pad(w_router[l], ((0, 0), (0, LANES - N_EXPERTS)))
    br = jnp.pad(b_router[l], (0, LANES - N_EXPERTS), constant_values=NEG_BIG).reshape(1, LANES)
    x1, idx, gates = _post_call(
        o_a.reshape(T, -1), o_b.reshape(T, -1), x2, w_o[l].astype(BF16),
        ln1_g[l].reshape(1, D), ln1_b[l].reshape(1, D), wr, br)

    bm = MOE_BM
    n_blocks = T * TOP_K // bm + N_EXPERTS
    pos, slot_tok, gs = _route(idx[:, :TOP_K], bm, n_blocks)
    y = _experts_call(gs, slot_tok, x1, w_gate_up[l], b_gate_up[l], w_down[l], b_down[l], n_blocks)
    out = _combine_call(pos, y, x1, gates, ln2_g[l].reshape(1, D), ln2_b[l].reshape(1, D))
    return out.reshape(B, S, D)
```

```python
import functools
import math

import numpy as np
import jax
import jax.numpy as jnp
from jax import lax
from jax.experimental import pallas as pl
from jax.experimental.pallas import tpu as pltpu

D_MODEL = 1024
DIFF_HEADS = 4
DIFF_HEAD_DIM = 64
DIFF_V_DIM = 128
MLA_HEADS = 4
MLA_V_DIM = 128
MLA_NOPE_DIM = 128
MLA_ROPE_DIM = 64
MLA_QK_DIM = MLA_NOPE_DIM + MLA_ROPE_DIM
MLA_Q_RANK = 256
MLA_KV_RANK = 128
ROPE_THETA = 10000.0
N_EXPERTS = 32
TOP_K = 4
D_FF = 1024
SWIGLU_LIMIT = 7.0
SWIGLU_ALPHA = 1.702
LN_EPS = 1e-5
SUBLN_EPS = 1e-5
MLA_RMS_EPS = 1e-6
DEPTH = 1
DN_ALPHA = (2.0 * DEPTH) ** 0.25
LAMBDA_INIT = 0.8 - 0.6 * math.exp(-0.3 * 0)

LANES = 128
MLA_QK_PAD = 2 * LANES
VMEM_LIMIT = 56 * 1024 * 1024

ATTN_TILE = 512
ATTN_QCHUNK = 256
POST_TM = 512
MOE_BM = 256
ROW_CHUNK = 8
DISP_TILE = 512
DISP_PCHUNK = 256
DISP_LCAP = -(-(DISP_TILE * TOP_K + N_EXPERTS * (ROW_CHUNK - 1)) // DISP_PCHUNK) * DISP_PCHUNK

NEG_BIG = -1e30
LOG2E = math.log2(math.e)
F32 = jnp.float32
BF16 = jnp.bfloat16


def _dot(a, b):
    return jnp.dot(a, b, preferred_element_type=F32)


def _dot_nt(a, b):
    return lax.dot_general(a, b, (((1,), (1,)), ((), ())), preferred_element_type=F32)


def _rope128(blk, cos, sin):
    return blk * cos + pltpu.roll(blk, 64, axis=1) * sin


def _rope128_t(blk, cos, sin):
    half = LANES // 2
    rolled = jnp.concatenate([blk[half:], blk[:half]], axis=0)
    return blk * cos + rolled * sin


def _rms_rows(t, g, eps):
    return t * lax.rsqrt(jnp.mean(t * t, axis=-1, keepdims=True) + eps) * g


def _rms_cols(t, g, eps):
    return t * lax.rsqrt(jnp.mean(t * t, axis=0, keepdims=True) + eps) * g


def _proj_kernel(x_ref, cos_ref, sin_ref, cosT_ref, sinT_ref,
                 wdqT_ref, wdk_ref, wdvT_ref, wcqT_ref, wckv_ref, wckvT_ref, wkr_ref,
                 gq_ref, gkv_ref, gkvc_ref, wuqT_ref, wuk_ref, wuvT_ref,
                 dqT_ref, dk_ref, dvT_ref, mqT_ref, mk_ref, mvT_ref):
    xb = x_ref[...].astype(BF16)
    cos, sin = cos_ref[...], sin_ref[...]
    cosT, sinT = cosT_ref[...], sinT_ref[...]

    dq_scale = DIFF_HEAD_DIM ** -0.5 * LOG2E
    mq_scale = MLA_QK_DIM ** -0.5 * LOG2E

    dqT = _dot_nt(wdqT_ref[...], xb)
    for h in range(DIFF_HEADS):
        sl = slice(h * LANES, (h + 1) * LANES)
        dqT_ref[sl, :] = (_rope128_t(dqT[sl], cosT, sinT) * dq_scale).astype(BF16)
    dvT_ref[...] = _dot_nt(wdvT_ref[...], xb).astype(BF16)
    dk = _dot(xb, wdk_ref[...])
    for h in range(DIFF_HEADS):
        sl = slice(h * LANES, (h + 1) * LANES)
        dk_ref[:, sl] = _rope128(dk[:, sl], cos, sin).astype(BF16)

    cqT = _rms_cols(_dot_nt(wcqT_ref[...], xb), gq_ref[...], MLA_RMS_EPS)
    qT = _dot(wuqT_ref[...], cqT.astype(BF16))
    for h in range(MLA_HEADS):
        nope = slice(h * MLA_QK_PAD, h * MLA_QK_PAD + LANES)
        ropes = slice(h * MLA_QK_PAD + LANES, (h + 1) * MLA_QK_PAD)
        mqT_ref[nope, :] = (qT[nope] * mq_scale).astype(BF16)
        mqT_ref[ropes, :] = (_rope128_t(qT[ropes], cosT, sinT) * mq_scale).astype(BF16)

    ckvT = _rms_cols(_dot_nt(wckvT_ref[...], xb), gkvc_ref[...], MLA_RMS_EPS)
    mvT_ref[...] = _dot(wuvT_ref[...], ckvT.astype(BF16)).astype(BF16)
    ckv = _rms_rows(_dot(xb, wckv_ref[...]), gkv_ref[...], MLA_RMS_EPS)
    k_nope = _dot(ckv.astype(BF16), wuk_ref[...])
    k_pe = _rope128(_dot(xb, wkr_ref[...]), cos, sin).astype(BF16)
    for h in range(MLA_HEADS):
        mk_ref[:, h * MLA_QK_PAD:h * MLA_QK_PAD + LANES] = k_nope[:, h * LANES:(h + 1) * LANES].astype(BF16)
        mk_ref[:, h * MLA_QK_PAD + LANES:(h + 1) * MLA_QK_PAD] = k_pe


def _proj_call(x3, tabs, w):
    B, S, D = x3.shape
    tm = ATTN_TILE
    nt = S // tm
    cos_t, sin_t, cosT, sinT = tabs
    weights = [w["dqT"], w["dk"], w["dvT"], w["cqT"], w["ckv"], w["ckvT"], w["kr"],
               w["gq"], w["gkv"], w["gkvc"], w["uqT"], w["uk"], w["uvT"]]
    tok = lambda b, i: (b, i, 0)
    feat = lambda b, i: (b, i, 0, 0)
    in_specs = [pl.BlockSpec((None, tm, D), tok),
                pl.BlockSpec((None, tm, LANES), tok), pl.BlockSpec((None, tm, LANES), tok),
                pl.BlockSpec((None, LANES, tm), lambda b, i: (b, 0, i)),
                pl.BlockSpec((None, LANES, tm), lambda b, i: (b, 0, i))]
    in_specs += [pl.BlockSpec(a.shape, lambda b, i: (0, 0)) for a in weights]
    mq_w = MLA_HEADS * MLA_QK_PAD
    out_specs = [pl.BlockSpec((None, None, 512, tm), feat), pl.BlockSpec((None, tm, 512), tok),
                 pl.BlockSpec((None, None, 512, tm), feat), pl.BlockSpec((None, None, mq_w, tm), feat),
                 pl.BlockSpec((None, tm, mq_w), tok), pl.BlockSpec((None, None, 512, tm), feat)]
    out_shape = [jax.ShapeDtypeStruct((B, nt, 512, tm), BF16), jax.ShapeDtypeStruct((B, S, 512), BF16),
                 jax.ShapeDtypeStruct((B, nt, 512, tm), BF16), jax.ShapeDtypeStruct((B, nt, mq_w, tm), BF16),
                 jax.ShapeDtypeStruct((B, S, mq_w), BF16), jax.ShapeDtypeStruct((B, nt, 512, tm), BF16)]
    return pl.pallas_call(
        _proj_kernel,
        grid=(B, nt),
        in_specs=in_specs,
        out_specs=out_specs,
        out_shape=out_shape,
        compiler_params=pltpu.CompilerParams(dimension_semantics=("arbitrary", "arbitrary"),
                                             vmem_limit_bytes=VMEM_LIMIT),
        name="proj",
    )(x3, cos_t, sin_t, cosT, sinT, *weights)


def _flash_body(qTs, k_ref, vT_ref, acc_ref, m_ref, l_ref, *, t):
    i = pl.program_id(2)
    m_ref[...] = jnp.full(m_ref.shape, -jnp.inf, F32)
    l_ref[...] = jnp.zeros(l_ref.shape, F32)
    acc_ref[...] = jnp.zeros(acc_ref.shape, F32)

    qc = ATTN_QCHUNK
    units = [(c, h) for c in range(len(qTs)) for h in range(t // qc)]

    def scores(j, unit):
        c, h = unit
        k = k_ref[pl.ds(pl.multiple_of(j * t, t), t), :]
        return _dot(k, qTs[c][:, h * qc:(h + 1) * qc])

    def finish(j, unit, s, masked):
        c, h = unit
        sl = slice(h * qc, (h + 1) * qc)
        if masked:
            key = lax.broadcasted_iota(jnp.int32, s.shape, 0)
            qry = lax.broadcasted_iota(jnp.int32, s.shape, 1) + h * qc
            s = jnp.where(key <= qry, s, -jnp.inf)
        m_prev = m_ref[c, :, sl]
        m_new = jnp.maximum(m_prev, jnp.max(s, axis=0, keepdims=True))
        alpha = jnp.exp2(m_prev - m_new)
        p = jnp.exp2(s - m_new)
        l_ref[c, :, sl] = alpha * l_ref[c, :, sl] + jnp.sum(p, axis=0, keepdims=True)
        acc_ref[c, :, sl] = alpha * acc_ref[c, :, sl] + _dot(vT_ref[j], p.astype(BF16))
        m_ref[c, :, sl] = m_new

    def step(j, s, masked, j_next):
        for n, unit in enumerate(units):
            if n + 1 < len(units):
                s_next = scores(j, units[n + 1])
            elif j_next is not None:
                s_next = scores(j_next, units[0])
            else:
                s_next = None
            finish(j, unit, s, masked)
            s = s_next
        return s

    s = lax.fori_loop(0, i, lambda j, s: step(j, s, False, j + 1), scores(0, units[0]))
    step(i, s, True, None)


def _diff_attn_kernel(lq1_ref, lk1_ref, lq2_ref, lk2_ref, g_ref, qT_ref, k_ref, vT_ref, o_ref,
                      acc_ref, m_ref, l_ref, *, t):
    qT = qT_ref[...]
    row = lax.broadcasted_iota(jnp.int32, qT.shape, 0)
    first = (row % 64) < 32
    zero = jnp.zeros_like(qT)
    qTs = (jnp.where(first, qT, zero), jnp.where(first, zero, qT))
    _flash_body(qTs, k_ref, vT_ref, acc_ref, m_ref, l_ref, t=t)

    lam = (jnp.exp(jnp.sum(lq1_ref[...] * lk1_ref[...], axis=-1, keepdims=True))
           - jnp.exp(jnp.sum(lq2_ref[...] * lk2_ref[...], axis=-1, keepdims=True)) + LAMBDA_INIT)
    oT = acc_ref[0] / l_ref[0] - lam * (acc_ref[1] / l_ref[1])
    oT = _rms_cols(oT, g_ref[...], SUBLN_EPS) * (1.0 - LAMBDA_INIT)
    o_ref[...] = oT.T.astype(o_ref.dtype)


def _mla_attn_kernel(qT_ref, k_ref, vT_ref, o_ref, acc_ref, m_ref, l_ref, *, t):
    _flash_body((qT_ref[...],), k_ref, vT_ref, acc_ref, m_ref, l_ref, t=t)
    o_ref[...] = (acc_ref[0] / l_ref[0]).T.astype(o_ref.dtype)


def _attn_call(kernel, n_sm, extra, qT, k, vT, heads, dk, name):
    B, nt, _, t = qT.shape
    S = nt * t
    dv = LANES
    in_specs = [pl.BlockSpec(a.shape, lambda b, h, i: (0, 0)) for a in extra]
    in_specs += [
        pl.BlockSpec((None, None, dk, t), lambda b, h, i: (b, i, h, 0)),
        pl.BlockSpec((None, S, dk), lambda b, h, i: (b, 0, h)),
        pl.BlockSpec((None, nt, dv, t), lambda b, h, i: (b, 0, h, 0)),
    ]
    return pl.pallas_call(
        functools.partial(kernel, t=t),
        grid=(B, heads, nt),
        in_specs=in_specs,
        out_specs=pl.BlockSpec((None, t, dv), lambda b, h, i: (b, i, h)),
        out_shape=jax.ShapeDtypeStruct((B, S, heads * dv), BF16),
        scratch_shapes=[
            pltpu.VMEM((n_sm, dv, t), F32),
            pltpu.VMEM((n_sm, 1, t), F32),
            pltpu.VMEM((n_sm, 1, t), F32),
        ],
        compiler_params=pltpu.CompilerParams(
            dimension_semantics=("arbitrary", "arbitrary", "arbitrary"), vmem_limit_bytes=VMEM_LIMIT),
        name=name,
    )(*extra, qT, k, vT)


def _layer_norm(y, g, b):
    mu = jnp.mean(y, axis=-1, keepdims=True)
    d = y - mu
    var = jnp.mean(d * d, axis=-1, keepdims=True)
    return d * lax.rsqrt(var + LN_EPS) * g + b


def _post_kernel(oa_ref, ob_ref, x_ref, wo_ref, g_ref, b_ref, wr_ref, br_ref,
                 x1_ref, idx_ref, gate_ref):
    half = oa_ref.shape[1]
    mixed = _dot(oa_ref[...], wo_ref[:half, :]) + _dot(ob_ref[...], wo_ref[half:, :])
    x1 = _layer_norm(DN_ALPHA * x_ref[...] + mixed, g_ref[...], b_ref[...])
    x1_ref[...] = x1

    logits = jnp.dot(x1, wr_ref[...], preferred_element_type=F32, precision=lax.Precision.HIGHEST) + br_ref[...]
    lane = lax.broadcasted_iota(jnp.int32, logits.shape, 1).astype(F32)
    work = logits
    vals, idxs = [], []
    for _ in range(TOP_K):
        m = jnp.max(work, axis=-1, keepdims=True)
        idx = jnp.min(jnp.where(work == m, lane, float(LANES)), axis=-1, keepdims=True)
        vals.append(m)
        idxs.append(idx)
        work = jnp.where(lane == idx, NEG_BIG, work)
    es = [jnp.exp(v - vals[0]) for v in vals]
    den = es[0] + es[1] + es[2] + es[3]
    idx_out = jnp.zeros(logits.shape, F32)
    gate_out = jnp.zeros(logits.shape, F32)
    for k in range(TOP_K):
        idx_out = jnp.where(lane == float(k), idxs[k], idx_out)
        gate_out = jnp.where(lane == float(k), es[k] / den, gate_out)
    idx_ref[...] = idx_out.astype(jnp.int32)
    gate_ref[...] = gate_out


def _post_call(oa, ob, x2, wo, g, b, wr, br):
    T = x2.shape[0]
    tm = POST_TM
    row = lambda i: (i, 0)
    full = lambda i: (0, 0)
    return pl.pallas_call(
        _post_kernel,
        grid=(T // tm,),
        in_specs=[
            pl.BlockSpec((tm, oa.shape[1]), row), pl.BlockSpec((tm, ob.shape[1]), row),
            pl.BlockSpec((tm, D_MODEL), row), pl.BlockSpec(wo.shape, full),
            pl.BlockSpec(g.shape, full), pl.BlockSpec(b.shape, full),
            pl.BlockSpec(wr.shape, full), pl.BlockSpec(br.shape, full),
        ],
        out_specs=[
            pl.BlockSpec((tm, D_MODEL), row), pl.BlockSpec((tm, LANES), row), pl.BlockSpec((tm, LANES), row),
        ],
        out_shape=[
            jax.ShapeDtypeStruct((T, D_MODEL), F32),
            jax.ShapeDtypeStruct((T, LANES), jnp.int32), jax.ShapeDtypeStruct((T, LANES), F32),
        ],
        compiler_params=pltpu.CompilerParams(dimension_semantics=("arbitrary",), vmem_limit_bytes=VMEM_LIMIT),
        name="post",
    )(oa, ob, x2, wo, g, b, wr, br)


def _lane_cumsum(x, n):
    lane = lax.broadcasted_iota(jnp.int32, x.shape, 1)
    s = 1
    while s < n:
        x = x + jnp.where(lane >= s, pltpu.roll(x, s, axis=1), 0.0)
        s *= 2
    return x


def _route_kernel(e_ref, lpos_ref, nch_ref, gbase_ref, meta_ref, be_ref, *, bm, rows_per_tile):
    e = e_ref[...]
    R = e.shape[0]
    nt = R // rows_per_tile
    r_i = lax.broadcasted_iota(jnp.int32, (LANES, LANES), 0)
    c_i = lax.broadcasted_iota(jnp.int32, (LANES, LANES), 1)
    lane_incl = (r_i <= c_i).astype(BF16)
    rr = lax.broadcasted_iota(jnp.int32, (R, R), 0)
    rc = lax.broadcasted_iota(jnp.int32, (R, R), 1)
    same_tile = (rr // rows_per_tile) == (rc // rows_per_tile)
    rows_before = jnp.logical_and(rc < rr, same_tile).astype(BF16)
    tr = lax.broadcasted_iota(jnp.int32, (nt, R), 0)
    tc = lax.broadcasted_iota(jnp.int32, (nt, R), 1)
    tile_rows = (tc // rows_per_tile == tr).astype(BF16)
    er = lax.broadcasted_iota(jnp.int32, (R, nt), 0)
    ec = lax.broadcasted_iota(jnp.int32, (R, nt), 1)
    row_tile = (er // rows_per_tile == ec).astype(F32)
    lt_r = lax.broadcasted_iota(jnp.int32, (nt, nt), 0)
    lt_c = lax.broadcasted_iota(jnp.int32, (nt, nt), 1)
    tiles_before = (lt_c < lt_r).astype(BF16)
    lane_t = lax.broadcasted_iota(jnp.int32, (nt, LANES), 1)

    rank = jnp.zeros(e.shape, F32)
    cnt = jnp.zeros((nt, LANES), F32)
    for ex in range(N_EXPERTS):
        hit = e == ex
        m = jnp.where(hit, 1.0, 0.0).astype(BF16)
        incl = _dot(m, lane_incl)
        row_tot = jnp.broadcast_to(incl[:, LANES - 1:LANES], e.shape).astype(BF16)
        before = _dot(rows_before, row_tot)
        rank = jnp.where(hit, incl - 1.0 + before, rank)
        cnt = jnp.where(lane_t == ex, _dot(tile_rows, row_tot), cnt)

    chunk = float(ROW_CHUNK)
    cnt8 = jnp.floor((cnt + (chunk - 1.0)) * (1.0 / chunk)) * chunk
    lstart = _lane_cumsum(cnt8, N_EXPERTS) - cnt8
    tile_pre = _dot(tiles_before, cnt8.astype(BF16))
    tot8 = tile_pre[nt - 1:nt] + cnt8[nt - 1:nt]
    tot_bm = jnp.floor((tot8 + (bm - 1.0)) * (1.0 / bm)) * bm
    end_incl = _lane_cumsum(tot_bm, N_EXPERTS)
    ebase = end_incl - tot_bm

    lrow = jnp.dot(row_tile, lstart, preferred_element_type=F32,
                   precision=lax.Precision.HIGHEST)
    lstart_a = jnp.zeros(e.shape, F32)
    for ex in range(N_EXPERTS):
        lstart_a = jnp.where(e == ex, lrow[:, ex:ex + 1], lstart_a)
    lpos_ref[...] = (lstart_a + rank).astype(jnp.int32)
    nch_ref[...] = (cnt8 * (1.0 / chunk)).astype(jnp.int32)
    gbase_ref[...] = (ebase + tile_pre).astype(jnp.int32)

    sub = lax.broadcasted_iota(jnp.int32, meta_ref.shape, 0)
    meta = jnp.where(sub == 0, ebase + tot8, 0.0)
    meta = jnp.where(sub == 1, (tot_bm - tot8) * (1.0 / chunk), meta)
    meta = jnp.where(sub == 2, end_incl[:, N_EXPERTS - 1:N_EXPERTS] * (1.0 / bm), meta)
    meta_ref[...] = meta.astype(jnp.int32)

    blk = (lax.broadcasted_iota(jnp.int32, be_ref.shape, 0) * LANES
           + lax.broadcasted_iota(jnp.int32, be_ref.shape, 1)).astype(F32) * float(bm)
    block_e = jnp.zeros(be_ref.shape, F32)
    for ex in range(N_EXPERTS):
        block_e = block_e + jnp.where(blk >= end_incl[:, ex:ex + 1], 1.0, 0.0)
    be_ref[...] = jnp.minimum(block_e, N_EXPERTS - 1.0).astype(jnp.int32)


def _route_call(flat_e, bm, n_blocks):
    A = flat_e.shape[0]
    R = A // LANES
    rows_per_tile = DISP_TILE * TOP_K // LANES
    nt = R // rows_per_tile
    be_rows = 8 * pl.cdiv(pl.cdiv(n_blocks, LANES), 8)
    full = lambda i: (0, 0)
    shapes = [(R, LANES), (nt, LANES), (nt, LANES), (8, LANES), (be_rows, LANES)]
    lpos, nch, gbase, meta, be = pl.pallas_call(
        functools.partial(_route_kernel, bm=bm, rows_per_tile=rows_per_tile),
        grid=(1,),
        in_specs=[pl.BlockSpec((R, LANES), full)],
        out_specs=[pl.BlockSpec(s, full) for s in shapes],
        out_shape=[jax.ShapeDtypeStruct(s, jnp.int32) for s in shapes],
        compiler_params=pltpu.CompilerParams(dimension_semantics=("arbitrary",), vmem_limit_bytes=VMEM_LIMIT),
        name="route",
    )(flat_e.reshape(R, LANES))
    E = N_EXPERTS
    return dict(lpos=lpos.reshape(A // TOP_K, TOP_K), nch=nch[:, :E].reshape(-1), gbase=gbase[:, :E].reshape(-1),
                tail_start=meta[0, :E], tail_n=meta[1, :E], n_used=meta[2, :1], block_e=be.reshape(-1)[:n_blocks])


def _chunk_rows(ref, first):
    return ref.at[pl.ds(pl.multiple_of(first, ROW_CHUNK), ROW_CHUNK), :]


def _group_copies(nch_ref, gbase_ref, tile, buf, hbm, sem, to_hbm):
    def per_expert(ex, done):
        n = nch_ref[tile * N_EXPERTS + ex]
        g0 = gbase_ref[tile * N_EXPERTS + ex]

        def per_chunk(c, carry):
            local = _chunk_rows(buf, (done + c) * ROW_CHUNK)
            remote = _chunk_rows(hbm, g0 + c * ROW_CHUNK)
            if to_hbm:
                pltpu.make_async_copy(local, remote, sem).start()
            else:
                pltpu.make_async_copy(remote, local, sem).start()
            return carry

        lax.fori_loop(0, n, per_chunk, 0)
        return done + n

    return lax.fori_loop(0, N_EXPERTS, per_expert, 0)


def _wait_chunks(n, buf, hbm, sem):
    def body(c, carry):
        pltpu.make_async_copy(_chunk_rows(buf, 0), _chunk_rows(hbm, 0), sem).wait()
        return carry
    lax.fori_loop(0, n, body, 0)


def _dispatch_kernel(nch_ref, gbase_ref, tstart_ref, tn_ref, nu_ref, x_ref, lposT_ref, xs_hbm,
                     buf_ref, zero_ref, sem, zsem, cnt_ref, *, bm, n_blocks):
    i = pl.program_id(0)
    nt = pl.num_programs(0)
    slot = i % 2

    @pl.when(i >= 2)
    def _():
        _wait_chunks(cnt_ref[slot], buf_ref.at[slot], xs_hbm, sem.at[slot])

    xb = x_ref[...].astype(BF16)
    lposT = lposT_ref[...]
    rows = DISP_PCHUNK
    for rb in range(DISP_LCAP // rows):
        r = lax.broadcasted_iota(jnp.int32, (rows, xb.shape[0]), 0) + rb * rows
        hit = r == lposT[0:1]
        for k in range(1, TOP_K):
            hit = jnp.logical_or(hit, r == lposT[k:k + 1])
        perm = jnp.where(hit, 1.0, 0.0).astype(BF16)
        buf_ref[slot, rb * rows:(rb + 1) * rows, :] = _dot(perm, xb)
    cnt_ref[slot] = _group_copies(nch_ref, gbase_ref, i, buf_ref.at[slot], xs_hbm, sem.at[slot], True)

    @pl.when(i == nt - 1)
    def _():
        zero_ref[...] = jnp.zeros(zero_ref.shape, F32)

        def per_expert(ex, done):
            first = tstart_ref[ex]

            def per_chunk(c, carry):
                pltpu.make_async_copy(_chunk_rows(zero_ref, 0), _chunk_rows(xs_hbm, first + c * ROW_CHUNK),
                                      zsem.at[0]).start()
                return carry

            lax.fori_loop(0, tn_ref[ex], per_chunk, 0)
            return done + tn_ref[ex]

        n_tail = lax.fori_loop(0, N_EXPERTS, per_expert, 0)

        def per_block(b, carry):
            pltpu.make_async_copy(zero_ref, xs_hbm.at[pl.ds(pl.multiple_of(b * bm, bm), bm), :], zsem.at[1]).start()
            return carry

        lax.fori_loop(nu_ref[0], n_blocks, per_block, 0)
        _wait_chunks(n_tail, zero_ref, xs_hbm, zsem.at[0])

        def wait_block(b, carry):
            pltpu.make_async_copy(zero_ref, xs_hbm.at[pl.ds(0, bm), :], zsem.at[1]).wait()
            return carry

        lax.fori_loop(nu_ref[0], n_blocks, wait_block, 0)
        _wait_chunks(cnt_ref[slot], buf_ref.at[slot], xs_hbm, sem.at[slot])

        @pl.when(nt >= 2)
        def _():
            _wait_chunks(cnt_ref[1 - slot], buf_ref.at[1 - slot], xs_hbm, sem.at[1 - slot])


def _dispatch_call(rt, x1, lposT, bm, n_blocks):
    T = x1.shape[0]
    tile = DISP_TILE
    grid_spec = pltpu.PrefetchScalarGridSpec(
        num_scalar_prefetch=5,
        grid=(T // tile,),
        in_specs=[
            pl.BlockSpec((tile, D_MODEL), lambda i, *_: (i, 0)),
            pl.BlockSpec((None, TOP_K, tile), lambda i, *_: (i, 0, 0)),
        ],
        out_specs=pl.BlockSpec(memory_space=pl.ANY),
        scratch_shapes=[
            pltpu.VMEM((2, DISP_LCAP, D_MODEL), F32),
            pltpu.VMEM((bm, D_MODEL), F32),
            pltpu.SemaphoreType.DMA((2,)),
            pltpu.SemaphoreType.DMA((2,)),
            pltpu.SMEM((2,), jnp.int32),
        ],
    )
    return pl.pallas_call(
        functools.partial(_dispatch_kernel, bm=bm, n_blocks=n_blocks),
        grid_spec=grid_spec,
        out_shape=jax.ShapeDtypeStruct((n_blocks * bm, D_MODEL), F32),
        compiler_params=pltpu.CompilerParams(dimension_semantics=("arbitrary",), vmem_limit_bytes=VMEM_LIMIT),
        name="dispatch",
    )(rt["nch"], rt["gbase"], rt["tail_start"], rt["tail_n"], rt["n_used"], x1, lposT)


def _experts_kernel(nused_ref, be_ref, xs_ref, wgu_ref, bgu_ref, wd_ref, bd_ref, y_ref, wgu_b, wd_b):
    j = pl.program_id(0)
    n_used = nused_ref[0]
    new_expert = jnp.logical_or(j == 0, be_ref[j] != be_ref[jnp.maximum(j - 1, 0)])

    @pl.when(jnp.logical_and(j < n_used, new_expert))
    def _():
        wgu_b[...] = wgu_ref[...].astype(BF16)
        wd_b[...] = wd_ref[...].astype(BF16)

    @pl.when(j < n_used)
    def _():
        h = _dot(xs_ref[...].astype(BF16), wgu_b[...]) + bgu_ref[...]
        gate = jnp.minimum(h[:, :D_FF], SWIGLU_LIMIT)
        up = jnp.clip(h[:, D_FF:], -SWIGLU_LIMIT, SWIGLU_LIMIT)
        act = (up + 1.0) * (gate * jax.nn.sigmoid(gate * SWIGLU_ALPHA))
        y_ref[...] = _dot(act.astype(BF16), wd_b[...]) + bd_ref[...]

    @pl.when(j >= n_used)
    def _():
        y_ref[...] = jnp.zeros(y_ref.shape, F32)


def _experts_call(rt, xs, wgu, bgu, wd, bd, bm, n_blocks):
    E = N_EXPERTS
    by_expert = lambda j, nu, be: (be[j], 0, 0)
    grid_spec = pltpu.PrefetchScalarGridSpec(
        num_scalar_prefetch=2,
        grid=(n_blocks,),
        in_specs=[
            pl.BlockSpec((bm, D_MODEL), lambda j, nu, be: (jnp.minimum(j, nu[0] - 1), 0)),
            pl.BlockSpec((None, D_MODEL, 2 * D_FF), by_expert),
            pl.BlockSpec((None, 1, 2 * D_FF), by_expert),
            pl.BlockSpec((None, D_FF, D_MODEL), by_expert),
            pl.BlockSpec((None, 1, D_MODEL), by_expert),
        ],
        out_specs=pl.BlockSpec((bm, D_MODEL), lambda j, nu, be: (j, 0)),
        scratch_shapes=[
            pltpu.VMEM((D_MODEL, 2 * D_FF), BF16),
            pltpu.VMEM((D_FF, D_MODEL), BF16),
        ],
    )
    return pl.pallas_call(
        _experts_kernel,
        grid_spec=grid_spec,
        out_shape=jax.ShapeDtypeStruct((n_blocks * bm, D_MODEL), F32),
        compiler_params=pltpu.CompilerParams(dimension_semantics=("arbitrary",), vmem_limit_bytes=VMEM_LIMIT),
        name="experts",
    )(rt["n_used"], rt["block_e"], xs, wgu, bgu.reshape(E, 1, 2 * D_FF), wd, bd.reshape(E, 1, D_MODEL))


def _combine_kernel(nch_ref, gbase_ref, y_hbm, x1_ref, lpos_ref, gate_ref, g_ref, b_ref, o_ref,
                    buf_ref, sem, cnt_ref):
    i = pl.program_id(0)
    nt = pl.num_programs(0)
    slot = i % 2

    @pl.when(i == 0)
    def _():
        buf_ref[...] = jnp.zeros(buf_ref.shape, F32)
        cnt_ref[0] = _group_copies(nch_ref, gbase_ref, 0, buf_ref.at[0], y_hbm, sem.at[0], False)

    @pl.when(i + 1 < nt)
    def _():
        cnt_ref[1 - slot] = _group_copies(nch_ref, gbase_ref, i + 1, buf_ref.at[1 - slot], y_hbm,
                                          sem.at[1 - slot], False)

    _wait_chunks(cnt_ref[slot], buf_ref.at[slot], y_hbm, sem.at[slot])
    lpos = lpos_ref[...]
    gates = gate_ref[...]
    cols = DISP_PCHUNK
    y = jnp.zeros(o_ref.shape, F32)
    for cb in range(DISP_LCAP // cols):
        c = lax.broadcasted_iota(jnp.int32, (lpos.shape[0], cols), 1) + cb * cols
        w = jnp.zeros(c.shape, F32)
        for k in range(TOP_K):
            w = jnp.where(c == lpos[:, k:k + 1], gates[:, k:k + 1], w)
        y = y + _dot(w.astype(BF16), buf_ref[slot, cb * cols:(cb + 1) * cols, :].astype(BF16))
    o_ref[...] = _layer_norm(DN_ALPHA * x1_ref[...] + y, g_ref[...], b_ref[...])


def _combine_call(rt, y, x1, gates, g, b):
    T = x1.shape[0]
    tile = DISP_TILE
    grid_spec = pltpu.PrefetchScalarGridSpec(
        num_scalar_prefetch=2,
        grid=(T // tile,),
        in_specs=[
            pl.BlockSpec(memory_space=pl.ANY),
            pl.BlockSpec((tile, D_MODEL), lambda i, *_: (i, 0)),
            pl.BlockSpec((tile, TOP_K), lambda i, *_: (i, 0)),
            pl.BlockSpec((tile, LANES), lambda i, *_: (i, 0)),
            pl.BlockSpec(g.shape, lambda i, *_: (0, 0)),
            pl.BlockSpec(b.shape, lambda i, *_: (0, 0)),
        ],
        out_specs=pl.BlockSpec((tile, D_MODEL), lambda i, *_: (i, 0)),
        scratch_shapes=[
            pltpu.VMEM((2, DISP_LCAP, D_MODEL), F32),
            pltpu.SemaphoreType.DMA((2,)),
            pltpu.SMEM((2,), jnp.int32),
        ],
    )
    return pl.pallas_call(
        _combine_kernel,
        grid_spec=grid_spec,
        out_shape=jax.ShapeDtypeStruct((T, D_MODEL), F32),
        compiler_params=pltpu.CompilerParams(dimension_semantics=("arbitrary",), vmem_limit_bytes=VMEM_LIMIT),
        name="combine",
    )(rt["nch"], rt["gbase"], y, x1, rt["lpos"], gates, g, b)


def _rope_lane_order(n_sub):
    half = DIFF_HEAD_DIM // 2
    per = LANES // 2 // n_sub
    assert per == half or n_sub == 1
    cols = []
    for part in range(2):
        for sub in range(n_sub):
            cols.extend(sub * 64 + part * half + d for d in range(half))
    return np.asarray(cols)


def _prep_weights(w_in, mla_q_norm_g, w_uq, mla_kv_norm_g, w_ukv):
    o_dq, o_dk, o_dv, o_cq, o_ckv, o_kr = 0, 512, 1024, 1536, 1792, 1920
    head_order = _rope_lane_order(2)
    diff_cols = np.concatenate([h * LANES + head_order for h in range(DIFF_HEADS)])
    w = {}
    w["dqT"] = w_in[:, o_dq + diff_cols].T.astype(BF16)
    w["dk"] = w_in[:, o_dk + diff_cols].astype(BF16)
    w["dvT"] = w_in[:, o_dv:o_cq].T.astype(BF16)
    w["cqT"] = w_in[:, o_cq:o_ckv].T.astype(BF16)
    w["ckv"] = w_in[:, o_ckv:o_kr].astype(BF16)
    w["ckvT"] = w["ckv"].T

    def spread_rope(cols64):
        z = jnp.zeros((cols64.shape[0], 32), cols64.dtype)
        return jnp.concatenate([cols64[:, :32], z, cols64[:, 32:], z], axis=1)

    w["kr"] = spread_rope(w_in[:, o_kr:o_kr + MLA_ROPE_DIM]).astype(BF16)
    uq = []
    for h in range(MLA_HEADS):
        base = h * MLA_QK_DIM
        uq.append(w_uq[:, base:base + MLA_NOPE_DIM])
        uq.append(spread_rope(w_uq[:, base + MLA_NOPE_DIM:base + MLA_QK_DIM]))
    w["uqT"] = jnp.concatenate(uq, axis=1).T.astype(BF16)
    per = MLA_NOPE_DIM + MLA_V_DIM
    w["uk"] = jnp.concatenate([w_ukv[:, h * per:h * per + MLA_NOPE_DIM] for h in range(MLA_HEADS)], axis=1).astype(BF16)
    w["uvT"] = jnp.concatenate(
        [w_ukv[:, h * per + MLA_NOPE_DIM:(h + 1) * per] for h in range(MLA_HEADS)], axis=1).T.astype(BF16)
    w["gq"] = mla_q_norm_g.reshape(MLA_Q_RANK, 1)
    w["gkv"] = mla_kv_norm_g.reshape(1, MLA_KV_RANK)
    w["gkvc"] = mla_kv_norm_g.reshape(MLA_KV_RANK, 1)
    return w


def _rope_tables(positions):
    half = MLA_ROPE_DIM // 2
    inv_freq = 1.0 / (ROPE_THETA ** (jnp.arange(0, MLA_ROPE_DIM, 2, dtype=F32) / MLA_ROPE_DIM))
    ang = positions.astype(F32)[..., None] * inv_freq
    ang = jnp.tile(ang, (1, 1, LANES // half))
    sign = jnp.where(jnp.arange(LANES) < LANES // 2, -1.0, 1.0).astype(F32)
    cos_t, sin_t = jnp.cos(ang), jnp.sin(ang) * sign
    return cos_t, sin_t, cos_t.transpose(0, 2, 1), sin_t.transpose(0, 2, 1)


def kernel(x, positions, w_in, lambda_q1, lambda_k1, lambda_q2, lambda_k2, subln_g, mla_q_norm_g, w_uq,
           mla_kv_norm_g, w_ukv, w_o, ln1_g, ln1_b, w_router, b_router, w_gate_up, b_gate_up, w_down, b_down,
           ln2_g, ln2_b):
    B, S, D = x.shape
    T = B * S
    l = 0
    x2 = x.reshape(T, D)
    w = _prep_weights(w_in[l], mla_q_norm_g[l], w_uq[l], mla_kv_norm_g[l], w_ukv[l])

    dqT, dk, dvT, mqT, mk, mvT = _proj_call(x, _rope_tables(positions), w)
    lam_vecs = [v[l].reshape(1, DIFF_HEAD_DIM) for v in (lambda_q1, lambda_k1, lambda_q2, lambda_k2)]
    o_a = _attn_call(_diff_attn_kernel, 2, lam_vecs + [subln_g[l].reshape(DIFF_V_DIM, 1)],
                     dqT, dk, dvT, DIFF_HEADS, LANES, "diff_attn")
    o_b = _attn_call(_mla_attn_kernel, 1, [], mqT, mk, mvT, MLA_HEADS, MLA_QK_PAD, "mla_attn")

    wr = jnp.pad(w_router[l], ((0, 0), (0, LANES - N_EXPERTS)))
    br = jnp.pad(b_router[l], (0, LANES - N_EXPERTS), constant_values=NEG_BIG).reshape(1, LANES)
    x1, idx, gates = _post_call(
        o_a.reshape(T, -1), o_b.reshape(T, -1), x2, w_o[l].astype(BF16),
        ln1_g[l].reshape(1, D), ln1_b[l].reshape(1, D), wr, br)

    bm = MOE_BM
    A = T * TOP_K
    n_tiles = T // DISP_TILE
    n_blocks = pl.cdiv(A + n_tiles * N_EXPERTS * (ROW_CHUNK - 1) + N_EXPERTS * (bm - ROW_CHUNK), bm)
    rt = _route_call(idx[:, :TOP_K].reshape(A), bm, n_blocks)
    lposT = rt["lpos"].reshape(n_tiles, DISP_TILE, TOP_K).transpose(0, 2, 1)
    xs = _dispatch_call(rt, x1, lposT, bm, n_blocks)
    y = _experts_call(rt, xs, w_gate_up[l], b_gate_up[l], w_down[l], b_down[l], bm, n_blocks)
    out = _combine_call(rt, y, x1, gates, ln2_g[l].reshape(1, D), ln2_b[l].reshape(1, D))
    return out.reshape(B, S, D)
```

```python
import functools
import math

import numpy as np
import jax
import jax.numpy as jnp
from jax import lax
from jax.experimental import pallas as pl
from jax.experimental.pallas import tpu as pltpu

D_MODEL = 1024
DIFF_HEADS = 4
DIFF_HEAD_DIM = 64
DIFF_V_DIM = 128
MLA_HEADS = 4
MLA_V_DIM = 128
MLA_NOPE_DIM = 128
MLA_ROPE_DIM = 64
MLA_QK_DIM = MLA_NOPE_DIM + MLA_ROPE_DIM
MLA_Q_RANK = 256
MLA_KV_RANK = 128
ROPE_THETA = 10000.0
N_EXPERTS = 32
TOP_K = 4
D_FF = 1024
SWIGLU_LIMIT = 7.0
SWIGLU_ALPHA = 1.702
LN_EPS = 1e-5
SUBLN_EPS = 1e-5
MLA_RMS_EPS = 1e-6
DEPTH = 1
DN_ALPHA = (2.0 * DEPTH) ** 0.25
LAMBDA_INIT = 0.8 - 0.6 * math.exp(-0.3 * 0)

LANES = 128
MLA_QK_PAD = 2 * LANES
VMEM_LIMIT = 56 * 1024 * 1024

ATTN_TILE = 512
ATTN_QCHUNK = 512
POST_TM = 512
MOE_BM = 256
ROW_CHUNK = 8
BIG_CHUNK = 32
DISP_TILE = 512
DISP_PCHUNK = 256
DISP_LCAP = -(-(DISP_TILE * TOP_K + N_EXPERTS * (ROW_CHUNK - 1)) // DISP_PCHUNK) * DISP_PCHUNK

NEG_BIG = -1e30
LOG2E = math.log2(math.e)
F32 = jnp.float32
BF16 = jnp.bfloat16


def _dot(a, b):
    return jnp.dot(a, b, preferred_element_type=F32)


def _dot_nt(a, b):
    return lax.dot_general(a, b, (((1,), (1,)), ((), ())), preferred_element_type=F32)


def _rope128(blk, cos, sin):
    return blk * cos + pltpu.roll(blk, 64, axis=1) * sin


def _rope128_t(blk, cos, sin):
    half = LANES // 2
    rolled = jnp.concatenate([blk[half:], blk[:half]], axis=0)
    return blk * cos + rolled * sin


def _rms_rows(t, g, eps):
    return t * lax.rsqrt(jnp.mean(t * t, axis=-1, keepdims=True) + eps) * g


def _rms_cols(t, g, eps):
    return t * lax.rsqrt(jnp.mean(t * t, axis=0, keepdims=True) + eps) * g


def _proj_kernel(x_ref, cos_ref, sin_ref, cosT_ref, sinT_ref,
                 wdqT_ref, wdk_ref, wdvT_ref, wcqT_ref, wckv_ref, wckvT_ref, wkr_ref,
                 gq_ref, gkv_ref, gkvc_ref, wuqT_ref, wuk_ref, wuvT_ref,
                 dqT_ref, dk_ref, dvT_ref, mqT_ref, mk_ref, mvT_ref):
    xb = x_ref[...].astype(BF16)
    cos, sin = cos_ref[...], sin_ref[...]
    cosT, sinT = cosT_ref[...], sinT_ref[...]

    dq_scale = DIFF_HEAD_DIM ** -0.5 * LOG2E
    mq_scale = MLA_QK_DIM ** -0.5 * LOG2E

    dqT = _dot_nt(wdqT_ref[...], xb)
    for h in range(DIFF_HEADS):
        sl = slice(h * LANES, (h + 1) * LANES)
        dqT_ref[sl, :] = (_rope128_t(dqT[sl], cosT, sinT) * dq_scale).astype(BF16)
    dvT_ref[...] = _dot_nt(wdvT_ref[...], xb).astype(BF16)
    dk = _dot(xb, wdk_ref[...])
    for h in range(DIFF_HEADS):
        sl = slice(h * LANES, (h + 1) * LANES)
        dk_ref[:, sl] = _rope128(dk[:, sl], cos, sin).astype(BF16)

    cqT = _rms_cols(_dot_nt(wcqT_ref[...], xb), gq_ref[...], MLA_RMS_EPS)
    qT = _dot(wuqT_ref[...], cqT.astype(BF16))
    for h in range(MLA_HEADS):
        nope = slice(h * MLA_QK_PAD, h * MLA_QK_PAD + LANES)
        ropes = slice(h * MLA_QK_PAD + LANES, (h + 1) * MLA_QK_PAD)
        mqT_ref[nope, :] = (qT[nope] * mq_scale).astype(BF16)
        mqT_ref[ropes, :] = (_rope128_t(qT[ropes], cosT, sinT) * mq_scale).astype(BF16)

    ckvT = _rms_cols(_dot_nt(wckvT_ref[...], xb), gkvc_ref[...], MLA_RMS_EPS)
    mvT_ref[...] = _dot(wuvT_ref[...], ckvT.astype(BF16)).astype(BF16)
    ckv = _rms_rows(_dot(xb, wckv_ref[...]), gkv_ref[...], MLA_RMS_EPS)
    k_nope = _dot(ckv.astype(BF16), wuk_ref[...])
    k_pe = _rope128(_dot(xb, wkr_ref[...]), cos, sin).astype(BF16)
    for h in range(MLA_HEADS):
        mk_ref[:, h * MLA_QK_PAD:h * MLA_QK_PAD + LANES] = k_nope[:, h * LANES:(h + 1) * LANES].astype(BF16)
        mk_ref[:, h * MLA_QK_PAD + LANES:(h + 1) * MLA_QK_PAD] = k_pe


def _proj_call(x3, tabs, w):
    B, S, D = x3.shape
    tm = ATTN_TILE
    nt = S // tm
    cos_t, sin_t, cosT, sinT = tabs
    weights = [w["dqT"], w["dk"], w["dvT"], w["cqT"], w["ckv"], w["ckvT"], w["kr"],
               w["gq"], w["gkv"], w["gkvc"], w["uqT"], w["uk"], w["uvT"]]
    tok = lambda b, i: (b, i, 0)
    feat = lambda b, i: (b, i, 0, 0)
    in_specs = [pl.BlockSpec((None, tm, D), tok),
                pl.BlockSpec((None, tm, LANES), tok), pl.BlockSpec((None, tm, LANES), tok),
                pl.BlockSpec((None, LANES, tm), lambda b, i: (b, 0, i)),
                pl.BlockSpec((None, LANES, tm), lambda b, i: (b, 0, i))]
    in_specs += [pl.BlockSpec(a.shape, lambda b, i: (0, 0)) for a in weights]
    mq_w = MLA_HEADS * MLA_QK_PAD
    out_specs = [pl.BlockSpec((None, None, 512, tm), feat), pl.BlockSpec((None, tm, 512), tok),
                 pl.BlockSpec((None, None, 512, tm), feat), pl.BlockSpec((None, None, mq_w, tm), feat),
                 pl.BlockSpec((None, tm, mq_w), tok), pl.BlockSpec((None, None, 512, tm), feat)]
    out_shape = [jax.ShapeDtypeStruct((B, nt, 512, tm), BF16), jax.ShapeDtypeStruct((B, S, 512), BF16),
                 jax.ShapeDtypeStruct((B, nt, 512, tm), BF16), jax.ShapeDtypeStruct((B, nt, mq_w, tm), BF16),
                 jax.ShapeDtypeStruct((B, S, mq_w), BF16), jax.ShapeDtypeStruct((B, nt, 512, tm), BF16)]
    return pl.pallas_call(
        _proj_kernel,
        grid=(B, nt),
        in_specs=in_specs,
        out_specs=out_specs,
        out_shape=out_shape,
        compiler_params=pltpu.CompilerParams(dimension_semantics=("arbitrary", "arbitrary"),
                                             vmem_limit_bytes=VMEM_LIMIT),
        name="proj",
    )(x3, cos_t, sin_t, cosT, sinT, *weights)


def _flash_body(qTs, k_ref, vT_ref, acc_ref, m_ref, l_ref, *, t):
    i = pl.program_id(2)
    m_ref[...] = jnp.full(m_ref.shape, -jnp.inf, F32)
    l_ref[...] = jnp.zeros(l_ref.shape, F32)
    acc_ref[...] = jnp.zeros(acc_ref.shape, F32)

    qc = ATTN_QCHUNK
    units = [(c, h) for c in range(len(qTs)) for h in range(t // qc)]

    def scores(j, unit):
        c, h = unit
        k = k_ref[pl.ds(pl.multiple_of(j * t, t), t), :]
        return _dot(k, qTs[c][:, h * qc:(h + 1) * qc])

    def finish(j, unit, s, masked):
        c, h = unit
        sl = slice(h * qc, (h + 1) * qc)
        if masked:
            key = lax.broadcasted_iota(jnp.int32, s.shape, 0)
            qry = lax.broadcasted_iota(jnp.int32, s.shape, 1) + h * qc
            s = jnp.where(key <= qry, s, -jnp.inf)
        m_prev = m_ref[c, :, sl]
        m_new = jnp.maximum(m_prev, jnp.max(s, axis=0, keepdims=True))
        alpha = jnp.exp2(m_prev - m_new)
        p = jnp.exp2(s - m_new)
        l_ref[c, :, sl] = alpha * l_ref[c, :, sl] + jnp.sum(p, axis=0, keepdims=True)
        acc_ref[c, :, sl] = alpha * acc_ref[c, :, sl] + _dot(vT_ref[j], p.astype(BF16))
        m_ref[c, :, sl] = m_new

    def step(j, s, masked, j_next):
        for n, unit in enumerate(units):
            if n + 1 < len(units):
                s_next = scores(j, units[n + 1])
            elif j_next is not None:
                s_next = scores(j_next, units[0])
            else:
                s_next = None
            finish(j, unit, s, masked)
            s = s_next
        return s

    s = lax.fori_loop(0, i, lambda j, s: step(j, s, False, j + 1), scores(0, units[0]))
    step(i, s, True, None)


def _diff_attn_kernel(lq1_ref, lk1_ref, lq2_ref, lk2_ref, g_ref, qT_ref, k_ref, vT_ref, o_ref,
                      acc_ref, m_ref, l_ref, *, t):
    qT = qT_ref[...]
    row = lax.broadcasted_iota(jnp.int32, qT.shape, 0)
    first = (row % 64) < 32
    zero = jnp.zeros_like(qT)
    qTs = (jnp.where(first, qT, zero), jnp.where(first, zero, qT))
    _flash_body(qTs, k_ref, vT_ref, acc_ref, m_ref, l_ref, t=t)

    lam = (jnp.exp(jnp.sum(lq1_ref[...] * lk1_ref[...], axis=-1, keepdims=True))
           - jnp.exp(jnp.sum(lq2_ref[...] * lk2_ref[...], axis=-1, keepdims=True)) + LAMBDA_INIT)
    oT = acc_ref[0] / l_ref[0] - lam * (acc_ref[1] / l_ref[1])
    oT = _rms_cols(oT, g_ref[...], SUBLN_EPS) * (1.0 - LAMBDA_INIT)
    o_ref[...] = oT.T.astype(o_ref.dtype)


def _mla_attn_kernel(qT_ref, k_ref, vT_ref, o_ref, acc_ref, m_ref, l_ref, *, t):
    _flash_body((qT_ref[...],), k_ref, vT_ref, acc_ref, m_ref, l_ref, t=t)
    o_ref[...] = (acc_ref[0] / l_ref[0]).T.astype(o_ref.dtype)


def _attn_call(kernel, n_sm, extra, qT, k, vT, heads, dk, name):
    B, nt, _, t = qT.shape
    S = nt * t
    dv = LANES
    in_specs = [pl.BlockSpec(a.shape, lambda b, h, i: (0, 0)) for a in extra]
    in_specs += [
        pl.BlockSpec((None, None, dk, t), lambda b, h, i: (b, i, h, 0)),
        pl.BlockSpec((None, S, dk), lambda b, h, i: (b, 0, h)),
        pl.BlockSpec((None, nt, dv, t), lambda b, h, i: (b, 0, h, 0)),
    ]
    return pl.pallas_call(
        functools.partial(kernel, t=t),
        grid=(B, heads, nt),
        in_specs=in_specs,
        out_specs=pl.BlockSpec((None, t, dv), lambda b, h, i: (b, i, h)),
        out_shape=jax.ShapeDtypeStruct((B, S, heads * dv), BF16),
        scratch_shapes=[
            pltpu.VMEM((n_sm, dv, t), F32),
            pltpu.VMEM((n_sm, 1, t), F32),
            pltpu.VMEM((n_sm, 1, t), F32),
        ],
        compiler_params=pltpu.CompilerParams(
            dimension_semantics=("arbitrary", "arbitrary", "arbitrary"), vmem_limit_bytes=VMEM_LIMIT),
        name=name,
    )(*extra, qT, k, vT)


def _layer_norm(y, g, b):
    mu = jnp.mean(y, axis=-1, keepdims=True)
    d = y - mu
    var = jnp.mean(d * d, axis=-1, keepdims=True)
    return d * lax.rsqrt(var + LN_EPS) * g + b


def _post_kernel(oa_ref, ob_ref, x_ref, wo_ref, g_ref, b_ref, wr_ref, br_ref,
                 x1_ref, idx_ref, gate_ref):
    half = oa_ref.shape[1]
    mixed = _dot(oa_ref[...], wo_ref[:half, :]) + _dot(ob_ref[...], wo_ref[half:, :])
    x1 = _layer_norm(DN_ALPHA * x_ref[...] + mixed, g_ref[...], b_ref[...])
    x1_ref[...] = x1

    logits = jnp.dot(x1, wr_ref[...], preferred_element_type=F32, precision=lax.Precision.HIGHEST) + br_ref[...]
    lane = lax.broadcasted_iota(jnp.int32, logits.shape, 1).astype(F32)
    work = logits
    vals, idxs = [], []
    for _ in range(TOP_K):
        m = jnp.max(work, axis=-1, keepdims=True)
        idx = jnp.min(jnp.where(work == m, lane, float(LANES)), axis=-1, keepdims=True)
        vals.append(m)
        idxs.append(idx)
        work = jnp.where(lane == idx, NEG_BIG, work)
    es = [jnp.exp(v - vals[0]) for v in vals]
    den = es[0] + es[1] + es[2] + es[3]
    idx_out = jnp.zeros(logits.shape, F32)
    gate_out = jnp.zeros(logits.shape, F32)
    for k in range(TOP_K):
        idx_out = jnp.where(lane == float(k), idxs[k], idx_out)
        gate_out = jnp.where(lane == float(k), es[k] / den, gate_out)
    idx_ref[...] = idx_out.astype(jnp.int32)
    gate_ref[...] = gate_out


def _post_call(oa, ob, x2, wo, g, b, wr, br):
    T = x2.shape[0]
    tm = POST_TM
    row = lambda i: (i, 0)
    full = lambda i: (0, 0)
    return pl.pallas_call(
        _post_kernel,
        grid=(T // tm,),
        in_specs=[
            pl.BlockSpec((tm, oa.shape[1]), row), pl.BlockSpec((tm, ob.shape[1]), row),
            pl.BlockSpec((tm, D_MODEL), row), pl.BlockSpec(wo.shape, full),
            pl.BlockSpec(g.shape, full), pl.BlockSpec(b.shape, full),
            pl.BlockSpec(wr.shape, full), pl.BlockSpec(br.shape, full),
        ],
        out_specs=[
            pl.BlockSpec((tm, D_MODEL), row), pl.BlockSpec((tm, LANES), row), pl.BlockSpec((tm, LANES), row),
        ],
        out_shape=[
            jax.ShapeDtypeStruct((T, D_MODEL), F32),
            jax.ShapeDtypeStruct((T, LANES), jnp.int32), jax.ShapeDtypeStruct((T, LANES), F32),
        ],
        compiler_params=pltpu.CompilerParams(dimension_semantics=("arbitrary",), vmem_limit_bytes=VMEM_LIMIT),
        name="post",
    )(oa, ob, x2, wo, g, b, wr, br)


def _lane_cumsum(x, n):
    lane = lax.broadcasted_iota(jnp.int32, x.shape, 1)
    s = 1
    while s < n:
        x = x + jnp.where(lane >= s, pltpu.roll(x, s, axis=1), 0.0)
        s *= 2
    return x


def _route_kernel(e_ref, lpos_ref, nch_ref, gbase_ref, meta_ref, be_ref, *, bm, rows_per_tile):
    e = e_ref[...]
    R = e.shape[0]
    nt = R // rows_per_tile
    r_i = lax.broadcasted_iota(jnp.int32, (LANES, LANES), 0)
    c_i = lax.broadcasted_iota(jnp.int32, (LANES, LANES), 1)
    lane_incl = (r_i <= c_i).astype(BF16)
    rr = lax.broadcasted_iota(jnp.int32, (R, R), 0)
    rc = lax.broadcasted_iota(jnp.int32, (R, R), 1)
    same_tile = (rr // rows_per_tile) == (rc // rows_per_tile)
    rows_before = jnp.logical_and(rc < rr, same_tile).astype(BF16)
    tr = lax.broadcasted_iota(jnp.int32, (nt, R), 0)
    tc = lax.broadcasted_iota(jnp.int32, (nt, R), 1)
    tile_rows = (tc // rows_per_tile == tr).astype(BF16)
    er = lax.broadcasted_iota(jnp.int32, (R, nt), 0)
    ec = lax.broadcasted_iota(jnp.int32, (R, nt), 1)
    row_tile = (er // rows_per_tile == ec).astype(F32)
    lt_r = lax.broadcasted_iota(jnp.int32, (nt, nt), 0)
    lt_c = lax.broadcasted_iota(jnp.int32, (nt, nt), 1)
    tiles_before = (lt_c < lt_r).astype(BF16)
    lane_t = lax.broadcasted_iota(jnp.int32, (nt, LANES), 1)

    rank = jnp.zeros(e.shape, F32)
    cnt = jnp.zeros((nt, LANES), F32)
    for ex in range(N_EXPERTS):
        hit = e == ex
        m = jnp.where(hit, 1.0, 0.0).astype(BF16)
        incl = _dot(m, lane_incl)
        row_tot = jnp.broadcast_to(incl[:, LANES - 1:LANES], e.shape).astype(BF16)
        before = _dot(rows_before, row_tot)
        rank = jnp.where(hit, incl - 1.0 + before, rank)
        cnt = jnp.where(lane_t == ex, _dot(tile_rows, row_tot), cnt)

    chunk = float(ROW_CHUNK)
    cnt8 = jnp.floor((cnt + (chunk - 1.0)) * (1.0 / chunk)) * chunk
    lstart = _lane_cumsum(cnt8, N_EXPERTS) - cnt8
    tile_pre = _dot(tiles_before, cnt8.astype(BF16))
    tot8 = tile_pre[nt - 1:nt] + cnt8[nt - 1:nt]
    tot_bm = jnp.floor((tot8 + (bm - 1.0)) * (1.0 / bm)) * bm
    end_incl = _lane_cumsum(tot_bm, N_EXPERTS)
    ebase = end_incl - tot_bm

    lrow = jnp.dot(row_tile, lstart, preferred_element_type=F32,
                   precision=lax.Precision.HIGHEST)
    lstart_a = jnp.zeros(e.shape, F32)
    for ex in range(N_EXPERTS):
        lstart_a = jnp.where(e == ex, lrow[:, ex:ex + 1], lstart_a)
    lpos_ref[...] = (lstart_a + rank).astype(jnp.int32)
    nch_ref[...] = (cnt8 * (1.0 / chunk)).astype(jnp.int32)
    gbase_ref[...] = (ebase + tile_pre).astype(jnp.int32)

    sub = lax.broadcasted_iota(jnp.int32, meta_ref.shape, 0)
    meta = jnp.where(sub == 0, ebase + tot8, 0.0)
    meta = jnp.where(sub == 1, (tot_bm - tot8) * (1.0 / chunk), meta)
    meta = jnp.where(sub == 2, end_incl[:, N_EXPERTS - 1:N_EXPERTS] * (1.0 / bm), meta)
    meta_ref[...] = meta.astype(jnp.int32)

    blk = (lax.broadcasted_iota(jnp.int32, be_ref.shape, 0) * LANES
           + lax.broadcasted_iota(jnp.int32, be_ref.shape, 1)).astype(F32) * float(bm)
    block_e = jnp.zeros(be_ref.shape, F32)
    for ex in range(N_EXPERTS):
        block_e = block_e + jnp.where(blk >= end_incl[:, ex:ex + 1], 1.0, 0.0)
    be_ref[...] = jnp.minimum(block_e, N_EXPERTS - 1.0).astype(jnp.int32)


def _route_call(flat_e, bm, n_blocks):
    A = flat_e.shape[0]
    R = A // LANES
    rows_per_tile = DISP_TILE * TOP_K // LANES
    nt = R // rows_per_tile
    be_rows = 8 * pl.cdiv(pl.cdiv(n_blocks, LANES), 8)
    full = lambda i: (0, 0)
    shapes = [(R, LANES), (nt, LANES), (nt, LANES), (8, LANES), (be_rows, LANES)]
    lpos, nch, gbase, meta, be = pl.pallas_call(
        functools.partial(_route_kernel, bm=bm, rows_per_tile=rows_per_tile),
        grid=(1,),
        in_specs=[pl.BlockSpec((R, LANES), full)],
        out_specs=[pl.BlockSpec(s, full) for s in shapes],
        out_shape=[jax.ShapeDtypeStruct(s, jnp.int32) for s in shapes],
        compiler_params=pltpu.CompilerParams(dimension_semantics=("arbitrary",), vmem_limit_bytes=VMEM_LIMIT),
        name="route",
    )(flat_e.reshape(R, LANES))
    E = N_EXPERTS
    return dict(lpos=lpos.reshape(A // TOP_K, TOP_K), nch=nch[:, :E].reshape(-1), gbase=gbase[:, :E].reshape(-1),
                tail_start=meta[0, :E], tail_n=meta[1, :E], n_used=meta[2, :1], block_e=be.reshape(-1)[:n_blocks])


def _chunk_rows(ref, first, rows=ROW_CHUNK):
    return ref.at[pl.ds(pl.multiple_of(first, ROW_CHUNK), rows), :]


def _group_copies(nch_ref, gbase_ref, tile, buf, hbm, sem, to_hbm):
    per_big = BIG_CHUNK // ROW_CHUNK

    def copy(local, remote):
        if to_hbm:
            pltpu.make_async_copy(local, remote, sem).start()
        else:
            pltpu.make_async_copy(remote, local, sem).start()

    def per_expert(ex, done):
        n = nch_ref[tile * N_EXPERTS + ex]
        g0 = gbase_ref[tile * N_EXPERTS + ex]
        l0 = done * ROW_CHUNK
        n_big = lax.shift_right_logical(n, per_big.bit_length() - 1)

        def big(c, carry):
            copy(_chunk_rows(buf, l0 + c * BIG_CHUNK, BIG_CHUNK), _chunk_rows(hbm, g0 + c * BIG_CHUNK, BIG_CHUNK))
            return carry

        def small(c, carry):
            copy(_chunk_rows(buf, l0 + c * ROW_CHUNK), _chunk_rows(hbm, g0 + c * ROW_CHUNK))
            return carry

        lax.fori_loop(0, n_big, big, 0)
        lax.fori_loop(n_big * per_big, n, small, 0)
        return done + n

    return lax.fori_loop(0, N_EXPERTS, per_expert, 0)


def _wait_chunks(n, hbm, sem):
    rows = pl.multiple_of(n * ROW_CHUNK, ROW_CHUNK)

    @pl.when(n > 0)
    def _():
        pltpu.make_async_copy(hbm.at[pl.ds(0, rows), :], hbm.at[pl.ds(0, rows), :], sem).wait()


def _dispatch_kernel(nch_ref, gbase_ref, tstart_ref, tn_ref, nu_ref, x_ref, lposT_ref, xs_hbm,
                     buf_ref, zero_ref, sem, zsem, cnt_ref, *, bm, n_blocks):
    i = pl.program_id(0)
    nt = pl.num_programs(0)
    slot = i % 2

    @pl.when(i >= 2)
    def _():
        _wait_chunks(cnt_ref[slot], xs_hbm, sem.at[slot])

    xb = x_ref[...].astype(BF16)
    lposT = lposT_ref[...]
    rows = DISP_PCHUNK
    for rb in range(DISP_LCAP // rows):
        r = lax.broadcasted_iota(jnp.int32, (rows, xb.shape[0]), 0) + rb * rows
        hit = r == lposT[0:1]
        for k in range(1, TOP_K):
            hit = jnp.logical_or(hit, r == lposT[k:k + 1])
        perm = jnp.where(hit, 1.0, 0.0).astype(BF16)
        buf_ref[slot, rb * rows:(rb + 1) * rows, :] = _dot(perm, xb)
    cnt_ref[slot] = _group_copies(nch_ref, gbase_ref, i, buf_ref.at[slot], xs_hbm, sem.at[slot], True)

    @pl.when(i == nt - 1)
    def _():
        zero_ref[...] = jnp.zeros(zero_ref.shape, F32)

        def per_expert(ex, done):
            first = tstart_ref[ex]

            def per_chunk(c, carry):
                pltpu.make_async_copy(_chunk_rows(zero_ref, 0), _chunk_rows(xs_hbm, first + c * ROW_CHUNK),
                                      zsem.at[0]).start()
                return carry

            lax.fori_loop(0, tn_ref[ex], per_chunk, 0)
            return done + tn_ref[ex]

        n_tail = lax.fori_loop(0, N_EXPERTS, per_expert, 0)

        def per_block(b, carry):
            pltpu.make_async_copy(zero_ref, xs_hbm.at[pl.ds(pl.multiple_of(b * bm, bm), bm), :], zsem.at[1]).start()
            return carry

        lax.fori_loop(nu_ref[0], n_blocks, per_block, 0)
        _wait_chunks(n_tail, xs_hbm, zsem.at[0])

        def wait_block(b, carry):
            pltpu.make_async_copy(zero_ref, xs_hbm.at[pl.ds(0, bm), :], zsem.at[1]).wait()
            return carry

        lax.fori_loop(nu_ref[0], n_blocks, wait_block, 0)
        _wait_chunks(cnt_ref[slot], xs_hbm, sem.at[slot])

        @pl.when(nt >= 2)
        def _():
            _wait_chunks(cnt_ref[1 - slot], xs_hbm, sem.at[1 - slot])


def _dispatch_call(rt, x1, lposT, bm, n_blocks):
    T = x1.shape[0]
    tile = DISP_TILE
    grid_spec = pltpu.PrefetchScalarGridSpec(
        num_scalar_prefetch=5,
        grid=(T // tile,),
        in_specs=[
            pl.BlockSpec((tile, D_MODEL), lambda i, *_: (i, 0)),
            pl.BlockSpec((None, TOP_K, tile), lambda i, *_: (i, 0, 0)),
        ],
        out_specs=pl.BlockSpec(memory_space=pl.ANY),
        scratch_shapes=[
            pltpu.VMEM((2, DISP_LCAP, D_MODEL), F32),
            pltpu.VMEM((bm, D_MODEL), F32),
            pltpu.SemaphoreType.DMA((2,)),
            pltpu.SemaphoreType.DMA((2,)),
            pltpu.SMEM((2,), jnp.int32),
        ],
    )
    return pl.pallas_call(
        functools.partial(_dispatch_kernel, bm=bm, n_blocks=n_blocks),
        grid_spec=grid_spec,
        out_shape=jax.ShapeDtypeStruct((n_blocks * bm, D_MODEL), F32),
        compiler_params=pltpu.CompilerParams(dimension_semantics=("arbitrary",), vmem_limit_bytes=VMEM_LIMIT),
        name="dispatch",
    )(rt["nch"], rt["gbase"], rt["tail_start"], rt["tail_n"], rt["n_used"], x1, lposT)


def _experts_kernel(nused_ref, be_ref, xs_ref, wgu_ref, bgu_ref, wd_ref, bd_ref, y_ref, wgu_b, wd_b):
    j = pl.program_id(0)
    n_used = nused_ref[0]
    new_expert = jnp.logical_or(j == 0, be_ref[j] != be_ref[jnp.maximum(j - 1, 0)])

    @pl.when(jnp.logical_and(j < n_used, new_expert))
    def _():
        wgu_b[...] = wgu_ref[...].astype(BF16)
        wd_b[...] = wd_ref[...].astype(BF16)

    @pl.when(j < n_used)
    def _():
        h = _dot(xs_ref[...].astype(BF16), wgu_b[...]) + bgu_ref[...]
        gate = jnp.minimum(h[:, :D_FF], SWIGLU_LIMIT)
        up = jnp.clip(h[:, D_FF:], -SWIGLU_LIMIT, SWIGLU_LIMIT)
        act = (up + 1.0) * (gate * jax.nn.sigmoid(gate * SWIGLU_ALPHA))
        y_ref[...] = _dot(act.astype(BF16), wd_b[...]) + bd_ref[...]

    @pl.when(j >= n_used)
    def _():
        y_ref[...] = jnp.zeros(y_ref.shape, F32)


def _experts_call(rt, xs, wgu, bgu, wd, bd, bm, n_blocks):
    E = N_EXPERTS
    by_expert = lambda j, nu, be: (be[j], 0, 0)
    grid_spec = pltpu.PrefetchScalarGridSpec(
        num_scalar_prefetch=2,
        grid=(n_blocks,),
        in_specs=[
            pl.BlockSpec((bm, D_MODEL), lambda j, nu, be: (jnp.minimum(j, nu[0] - 1), 0)),
            pl.BlockSpec((None, D_MODEL, 2 * D_FF), by_expert),
            pl.BlockSpec((None, 1, 2 * D_FF), by_expert),
            pl.BlockSpec((None, D_FF, D_MODEL), by_expert),
            pl.BlockSpec((None, 1, D_MODEL), by_expert),
        ],
        out_specs=pl.BlockSpec((bm, D_MODEL), lambda j, nu, be: (j, 0)),
        scratch_shapes=[
            pltpu.VMEM((D_MODEL, 2 * D_FF), BF16),
            pltpu.VMEM((D_FF, D_MODEL), BF16),
        ],
    )
    return pl.pallas_call(
        _experts_kernel,
        grid_spec=grid_spec,
        out_shape=jax.ShapeDtypeStruct((n_blocks * bm, D_MODEL), F32),
        compiler_params=pltpu.CompilerParams(dimension_semantics=("arbitrary",), vmem_limit_bytes=VMEM_LIMIT),
        name="experts",
    )(rt["n_used"], rt["block_e"], xs, wgu, bgu.reshape(E, 1, 2 * D_FF), wd, bd.reshape(E, 1, D_MODEL))


def _combine_kernel(nch_ref, gbase_ref, y_hbm, x1_ref, lpos_ref, gate_ref, g_ref, b_ref, o_ref,
                    buf_ref, sem, cnt_ref):
    i = pl.program_id(0)
    nt = pl.num_programs(0)
    slot = i % 2

    @pl.when(i == 0)
    def _():
        buf_ref[...] = jnp.zeros(buf_ref.shape, F32)
        cnt_ref[0] = _group_copies(nch_ref, gbase_ref, 0, buf_ref.at[0], y_hbm, sem.at[0], False)

    @pl.when(i + 1 < nt)
    def _():
        cnt_ref[1 - slot] = _group_copies(nch_ref, gbase_ref, i + 1, buf_ref.at[1 - slot], y_hbm,
                                          sem.at[1 - slot], False)

    _wait_chunks(cnt_ref[slot], y_hbm, sem.at[slot])
    lpos = lpos_ref[...]
    gates = gate_ref[...]
    cols = DISP_PCHUNK
    y = jnp.zeros(o_ref.shape, F32)
    for cb in range(DISP_LCAP // cols):
        c = lax.broadcasted_iota(jnp.int32, (lpos.shape[0], cols), 1) + cb * cols
        w = jnp.zeros(c.shape, F32)
        for k in range(TOP_K):
            w = jnp.where(c == lpos[:, k:k + 1], gates[:, k:k + 1], w)
        y = y + _dot(w.astype(BF16), buf_ref[slot, cb * cols:(cb + 1) * cols, :].astype(BF16))
    o_ref[...] = _layer_norm(DN_ALPHA * x1_ref[...] + y, g_ref[...], b_ref[...])


def _combine_call(rt, y, x1, gates, g, b):
    T = x1.shape[0]
    tile = DISP_TILE
    grid_spec = pltpu.PrefetchScalarGridSpec(
        num_scalar_prefetch=2,
        grid=(T // tile,),
        in_specs=[
            pl.BlockSpec(memory_space=pl.ANY),
            pl.BlockSpec((tile, D_MODEL), lambda i, *_: (i, 0)),
            pl.BlockSpec((tile, TOP_K), lambda i, *_: (i, 0)),
            pl.BlockSpec((tile, LANES), lambda i, *_: (i, 0)),
            pl.BlockSpec(g.shape, lambda i, *_: (0, 0)),
            pl.BlockSpec(b.shape, lambda i, *_: (0, 0)),
        ],
        out_specs=pl.BlockSpec((tile, D_MODEL), lambda i, *_: (i, 0)),
        scratch_shapes=[
            pltpu.VMEM((2, DISP_LCAP, D_MODEL), F32),
            pltpu.SemaphoreType.DMA((2,)),
            pltpu.SMEM((2,), jnp.int32),
        ],
    )
    return pl.pallas_call(
        _combine_kernel,
        grid_spec=grid_spec,
        out_shape=jax.ShapeDtypeStruct((T, D_MODEL), F32),
        compiler_params=pltpu.CompilerParams(dimension_semantics=("arbitrary",), vmem_limit_bytes=VMEM_LIMIT),
        name="combine",
    )(rt["nch"], rt["gbase"], y, x1, rt["lpos"], gates, g, b)


def _rope_lane_order(n_sub):
    half = DIFF_HEAD_DIM // 2
    per = LANES // 2 // n_sub
    assert per == half or n_sub == 1
    cols = []
    for part in range(2):
        for sub in range(n_sub):
            cols.extend(sub * 64 + part * half + d for d in range(half))
    return np.asarray(cols)


def _prep_weights(w_in, mla_q_norm_g, w_uq, mla_kv_norm_g, w_ukv):
    o_dq, o_dk, o_dv, o_cq, o_ckv, o_kr = 0, 512, 1024, 1536, 1792, 1920
    head_order = _rope_lane_order(2)
    diff_cols = np.concatenate([h * LANES + head_order for h in range(DIFF_HEADS)])
    w = {}
    w["dqT"] = w_in[:, o_dq + diff_cols].T.astype(BF16)
    w["dk"] = w_in[:, o_dk + diff_cols].astype(BF16)
    w["dvT"] = w_in[:, o_dv:o_cq].T.astype(BF16)
    w["cqT"] = w_in[:, o_cq:o_ckv].T.astype(BF16)
    w["ckv"] = w_in[:, o_ckv:o_kr].astype(BF16)
    w["ckvT"] = w["ckv"].T

    def spread_rope(cols64):
        z = jnp.zeros((cols64.shape[0], 32), cols64.dtype)
        return jnp.concatenate([cols64[:, :32], z, cols64[:, 32:], z], axis=1)

    w["kr"] = spread_rope(w_in[:, o_kr:o_kr + MLA_ROPE_DIM]).astype(BF16)
    uq = []
    for h in range(MLA_HEADS):
        base = h * MLA_QK_DIM
        uq.append(w_uq[:, base:base + MLA_NOPE_DIM])
        uq.append(spread_rope(w_uq[:, base + MLA_NOPE_DIM:base + MLA_QK_DIM]))
    w["uqT"] = jnp.concatenate(uq, axis=1).T.astype(BF16)
    per = MLA_NOPE_DIM + MLA_V_DIM
    w["uk"] = jnp.concatenate([w_ukv[:, h * per:h * per + MLA_NOPE_DIM] for h in range(MLA_HEADS)], axis=1).astype(BF16)
    w["uvT"] = jnp.concatenate(
        [w_ukv[:, h * per + MLA_NOPE_DIM:(h + 1) * per] for h in range(MLA_HEADS)], axis=1).T.astype(BF16)
    w["gq"] = mla_q_norm_g.reshape(MLA_Q_RANK, 1)
    w["gkv"] = mla_kv_norm_g.reshape(1, MLA_KV_RANK)
    w["gkvc"] = mla_kv_norm_g.reshape(MLA_KV_RANK, 1)
    return w


def _rope_tables(positions):
    half = MLA_ROPE_DIM // 2
    inv_freq = 1.0 / (ROPE_THETA ** (jnp.arange(0, MLA_ROPE_DIM, 2, dtype=F32) / MLA_ROPE_DIM))
    ang = positions.astype(F32)[..., None] * inv_freq
    ang = jnp.tile(ang, (1, 1, LANES // half))
    sign = jnp.where(jnp.arange(LANES) < LANES // 2, -1.0, 1.0).astype(F32)
    cos_t, sin_t = jnp.cos(ang), jnp.sin(ang) * sign
    return cos_t, sin_t, cos_t.transpose(0, 2, 1), sin_t.transpose(0, 2, 1)


def kernel(x, positions, w_in, lambda_q1, lambda_k1, lambda_q2, lambda_k2, subln_g, mla_q_norm_g, w_uq,
           mla_kv_norm_g, w_ukv, w_o, ln1_g, ln1_b, w_router, b_router, w_gate_up, b_gate_up, w_down, b_down,
           ln2_g, ln2_b):
    B, S, D = x.shape
    T = B * S
    l = 0
    x2 = x.reshape(T, D)
    w = _prep_weights(w_in[l], mla_q_norm_g[l], w_uq[l], mla_kv_norm_g[l], w_ukv[l])

    dqT, dk, dvT, mqT, mk, mvT = _proj_call(x, _rope_tables(positions), w)
    lam_vecs = [v[l].reshape(1, DIFF_HEAD_DIM) for v in (lambda_q1, lambda_k1, lambda_q2, lambda_k2)]
    o_a = _attn_call(_diff_attn_kernel, 2, lam_vecs + [subln_g[l].reshape(DIFF_V_DIM, 1)],
                     dqT, dk, dvT, DIFF_HEADS, LANES, "diff_attn")
    o_b = _attn_call(_mla_attn_kernel, 1, [], mqT, mk, mvT, MLA_HEADS, MLA_QK_PAD, "mla_attn")

    wr = jnp.pad(w_router[l], ((0, 0), (0, LANES - N_EXPERTS)))
    br = jnp.pad(b_router[l], (0, LANES - N_EXPERTS), constant_values=NEG_BIG).reshape(1, LANES)
    x1, idx, gates = _post_call(
        o_a.reshape(T, -1), o_b.reshape(T, -1), x2, w_o[l].astype(BF16),
        ln1_g[l].reshape(1, D), ln1_b[l].reshape(1, D), wr, br)

    bm = MOE_BM
    A = T * TOP_K
    n_tiles = T // DISP_TILE
    n_blocks = pl.cdiv(A + n_tiles * N_EXPERTS * (ROW_CHUNK - 1) + N_EXPERTS * (bm - ROW_CHUNK), bm)
    rt = _route_call(idx[:, :TOP_K].reshape(A), bm, n_blocks)
    lposT = rt["lpos"].reshape(n_tiles, DISP_TILE, TOP_K).transpose(0, 2, 1)
    xs = _dispatch_call(rt, x1, lposT, bm, n_blocks)
    y = _experts_call(rt, xs, w_gate_up[l], b_gate_up[l], w_down[l], b_down[l], bm, n_blocks)
    out = _combine_call(rt, y, x1, gates, ln2_g[l].reshape(1, D), ln2_b[l].reshape(1, D))
    return out.reshape(B, S, D)
```

```python
import functools
import math

import numpy as np
import jax
import jax.numpy as jnp
from jax import lax
from jax.experimental import pallas as pl
from jax.experimental.pallas import tpu as pltpu

D_MODEL = 1024
DIFF_HEADS = 4
DIFF_HEAD_DIM = 64
DIFF_V_DIM = 128
MLA_HEADS = 4
MLA_V_DIM = 128
MLA_NOPE_DIM = 128
MLA_ROPE_DIM = 64
MLA_QK_DIM = MLA_NOPE_DIM + MLA_ROPE_DIM
MLA_Q_RANK = 256
MLA_KV_RANK = 128
ROPE_THETA = 10000.0
N_EXPERTS = 32
TOP_K = 4
D_FF = 1024
SWIGLU_LIMIT = 7.0
SWIGLU_ALPHA = 1.702
LN_EPS = 1e-5
SUBLN_EPS = 1e-5
MLA_RMS_EPS = 1e-6
DEPTH = 1
DN_ALPHA = (2.0 * DEPTH) ** 0.25
LAMBDA_INIT = 0.8 - 0.6 * math.exp(-0.3 * 0)

LANES = 128
MLA_QK_PAD = 2 * LANES
BF16_ROWS = 16
VAL_ROWS = LANES + BF16_ROWS
VMEM_LIMIT = 56 * 1024 * 1024

ATTN_TILE = 512
POST_TM = 512
MOE_BM = 256
ROW_CHUNK = 8
BIG_CHUNK = 32
DISP_TILE = 512
DISP_PCHUNK = 256
DISP_LCAP = -(-(DISP_TILE * TOP_K + N_EXPERTS * (ROW_CHUNK - 1)) // DISP_PCHUNK) * DISP_PCHUNK

NEG_BIG = -1e30
LOG2E = math.log2(math.e)
F32 = jnp.float32
BF16 = jnp.bfloat16


def _dot(a, b):
    return jnp.dot(a, b, preferred_element_type=F32)


def _dot_nt(a, b):
    return lax.dot_general(a, b, (((1,), (1,)), ((), ())), preferred_element_type=F32)


def _rope128(blk, cos, sin):
    return blk * cos + pltpu.roll(blk, 64, axis=1) * sin


def _rope128_t(blk, cos, sin):
    half = LANES // 2
    rolled = jnp.concatenate([blk[half:], blk[:half]], axis=0)
    return blk * cos + rolled * sin


def _rms_rows(t, g, eps):
    return t * lax.rsqrt(jnp.mean(t * t, axis=-1, keepdims=True) + eps) * g


def _rms_cols(t, g, eps):
    return t * lax.rsqrt(jnp.mean(t * t, axis=0, keepdims=True) + eps) * g


def _store_values(vT_ref, vT, heads):
    ones = jnp.ones((BF16_ROWS, vT.shape[1]), BF16)
    for h in range(heads):
        vT_ref[h * VAL_ROWS:h * VAL_ROWS + LANES, :] = vT[h * LANES:(h + 1) * LANES].astype(BF16)
        vT_ref[h * VAL_ROWS + LANES:(h + 1) * VAL_ROWS, :] = ones


def _proj_kernel(x_ref, cos_ref, sin_ref, cosT_ref, sinT_ref,
                 wdqT_ref, wdk_ref, wdvT_ref, wcqT_ref, wckv_ref, wckvT_ref, wkr_ref,
                 gq_ref, gkv_ref, gkvc_ref, wuqT_ref, wuk_ref, wuvT_ref,
                 dqT_ref, dk_ref, dvT_ref, mqT_ref, mk_ref, mvT_ref):
    xb = x_ref[...].astype(BF16)
    cos, sin = cos_ref[...], sin_ref[...]
    cosT, sinT = cosT_ref[...], sinT_ref[...]

    dq_scale = DIFF_HEAD_DIM ** -0.5 * LOG2E
    mq_scale = MLA_QK_DIM ** -0.5 * LOG2E

    dqT = _dot_nt(wdqT_ref[...], xb)
    for h in range(DIFF_HEADS):
        sl = slice(h * LANES, (h + 1) * LANES)
        dqT_ref[sl, :] = (_rope128_t(dqT[sl], cosT, sinT) * dq_scale).astype(BF16)
    _store_values(dvT_ref, _dot_nt(wdvT_ref[...], xb), DIFF_HEADS)
    dk = _dot(xb, wdk_ref[...])
    for h in range(DIFF_HEADS):
        sl = slice(h * LANES, (h + 1) * LANES)
        dk_ref[:, sl] = _rope128(dk[:, sl], cos, sin).astype(BF16)

    cqT = _rms_cols(_dot_nt(wcqT_ref[...], xb), gq_ref[...], MLA_RMS_EPS)
    qT = _dot(wuqT_ref[...], cqT.astype(BF16))
    for h in range(MLA_HEADS):
        nope = slice(h * MLA_QK_PAD, h * MLA_QK_PAD + LANES)
        ropes = slice(h * MLA_QK_PAD + LANES, (h + 1) * MLA_QK_PAD)
        mqT_ref[nope, :] = (qT[nope] * mq_scale).astype(BF16)
        mqT_ref[ropes, :] = (_rope128_t(qT[ropes], cosT, sinT) * mq_scale).astype(BF16)

    ckvT = _rms_cols(_dot_nt(wckvT_ref[...], xb), gkvc_ref[...], MLA_RMS_EPS)
    _store_values(mvT_ref, _dot(wuvT_ref[...], ckvT.astype(BF16)), MLA_HEADS)
    ckv = _rms_rows(_dot(xb, wckv_ref[...]), gkv_ref[...], MLA_RMS_EPS)
    k_nope = _dot(ckv.astype(BF16), wuk_ref[...])
    k_pe = _rope128(_dot(xb, wkr_ref[...]), cos, sin).astype(BF16)
    for h in range(MLA_HEADS):
        mk_ref[:, h * MLA_QK_PAD:h * MLA_QK_PAD + LANES] = k_nope[:, h * LANES:(h + 1) * LANES].astype(BF16)
        mk_ref[:, h * MLA_QK_PAD + LANES:(h + 1) * MLA_QK_PAD] = k_pe


def _proj_call(x3, tabs, w):
    B, S, D = x3.shape
    tm = ATTN_TILE
    nt = S // tm
    cos_t, sin_t, cosT, sinT = tabs
    weights = [w["dqT"], w["dk"], w["dvT"], w["cqT"], w["ckv"], w["ckvT"], w["kr"],
               w["gq"], w["gkv"], w["gkvc"], w["uqT"], w["uk"], w["uvT"]]
    tok = lambda b, i: (b, i, 0)
    feat = lambda b, i: (b, i, 0, 0)
    in_specs = [pl.BlockSpec((None, tm, D), tok),
                pl.BlockSpec((None, tm, LANES), tok), pl.BlockSpec((None, tm, LANES), tok),
                pl.BlockSpec((None, LANES, tm), lambda b, i: (b, 0, i)),
                pl.BlockSpec((None, LANES, tm), lambda b, i: (b, 0, i))]
    in_specs += [pl.BlockSpec(a.shape, lambda b, i: (0, 0)) for a in weights]
    mq_w = MLA_HEADS * MLA_QK_PAD
    dv_w, mv_w = DIFF_HEADS * VAL_ROWS, MLA_HEADS * VAL_ROWS
    out_specs = [pl.BlockSpec((None, None, 512, tm), feat), pl.BlockSpec((None, tm, 512), tok),
                 pl.BlockSpec((None, None, dv_w, tm), feat), pl.BlockSpec((None, None, mq_w, tm), feat),
                 pl.BlockSpec((None, tm, mq_w), tok), pl.BlockSpec((None, None, mv_w, tm), feat)]
    out_shape = [jax.ShapeDtypeStruct((B, nt, 512, tm), BF16), jax.ShapeDtypeStruct((B, S, 512), BF16),
                 jax.ShapeDtypeStruct((B, nt, dv_w, tm), BF16), jax.ShapeDtypeStruct((B, nt, mq_w, tm), BF16),
                 jax.ShapeDtypeStruct((B, S, mq_w), BF16), jax.ShapeDtypeStruct((B, nt, mv_w, tm), BF16)]
    return pl.pallas_call(
        _proj_kernel,
        grid=(B, nt),
        in_specs=in_specs,
        out_specs=out_specs,
        out_shape=out_shape,
        compiler_params=pltpu.CompilerParams(dimension_semantics=("arbitrary", "arbitrary"),
                                             vmem_limit_bytes=VMEM_LIMIT),
        name="proj",
    )(x3, cos_t, sin_t, cosT, sinT, *weights)


def _flash_body(qTs, keys_of, values_of, acc_ref, m_ref, s_refs, *, t):
    i = pl.program_id(2)
    m_ref[...] = jnp.full(m_ref.shape, -jnp.inf, F32)
    acc_ref[...] = jnp.zeros(acc_ref.shape, F32)

    def scores(j, u):
        s_refs[u][...] = _dot(keys_of(u, j), qTs[u])

    def finish(j, u, masked):
        s = s_refs[u][...]
        if masked:
            key = lax.broadcasted_iota(jnp.int32, s.shape, 0)
            qry = lax.broadcasted_iota(jnp.int32, s.shape, 1)
            s = jnp.where(key <= qry, s, -jnp.inf)
        m_prev = m_ref[u]
        m_new = jnp.maximum(m_prev, jnp.max(s, axis=0, keepdims=True))
        alpha = jnp.exp2(m_prev - m_new)
        p = jnp.exp2((s - m_new).astype(BF16))
        acc_ref[u] = alpha * acc_ref[u] + _dot(values_of(u, j), p)
        m_ref[u] = m_new

    def full_step(j, carry):
        scores(j, 1)
        finish(j, 0, False)
        scores(j + 1, 0)
        finish(j, 1, False)
        return carry

    scores(0, 0)
    lax.fori_loop(0, i, full_step, 0)
    scores(i, 1)
    finish(i, 0, True)
    finish(i, 1, True)


def _key_tile(k_ref, j, t, lanes=slice(None)):
    return k_ref[pl.ds(pl.multiple_of(j * t, t), t), lanes]


def _normalized(acc_ref, u):
    return acc_ref[u, :LANES, :] / acc_ref[u, LANES:LANES + 1, :]


def _diff_attn_kernel(lq1_ref, lk1_ref, lq2_ref, lk2_ref, g_ref, qT_ref, k_ref, vT_ref, o_ref,
                      acc_ref, m_ref, sa_ref, sb_ref, *, t):
    qT = qT_ref[...]
    row = lax.broadcasted_iota(jnp.int32, qT.shape, 0)
    first = (row % 64) < 32
    zero = jnp.zeros_like(qT)
    qTs = (jnp.where(first, qT, zero), jnp.where(first, zero, qT))
    _flash_body(qTs, lambda u, j: _key_tile(k_ref, j, t), lambda u, j: vT_ref[j],
                acc_ref, m_ref, (sa_ref, sb_ref), t=t)

    lam = (jnp.exp(jnp.sum(lq1_ref[...] * lk1_ref[...], axis=-1, keepdims=True))
           - jnp.exp(jnp.sum(lq2_ref[...] * lk2_ref[...], axis=-1, keepdims=True)) + LAMBDA_INIT)
    oT = _normalized(acc_ref, 0) - lam * _normalized(acc_ref, 1)
    oT = _rms_cols(oT, g_ref[...], SUBLN_EPS) * (1.0 - LAMBDA_INIT)
    o_ref[...] = oT.T.astype(o_ref.dtype)


def _mla_attn_kernel(qT_ref, k_ref, vT_ref, o_ref, acc_ref, m_ref, sa_ref, sb_ref, *, t):
    dk = MLA_QK_PAD
    qTs = tuple(qT_ref[u * dk:(u + 1) * dk, :] for u in range(2))
    _flash_body(qTs, lambda u, j: _key_tile(k_ref, j, t, slice(u * dk, (u + 1) * dk)),
                lambda u, j: vT_ref[j, u * VAL_ROWS:(u + 1) * VAL_ROWS, :],
                acc_ref, m_ref, (sa_ref, sb_ref), t=t)
    for u in range(2):
        o_ref[:, u * LANES:(u + 1) * LANES] = _normalized(acc_ref, u).T.astype(o_ref.dtype)


def _attn_call(kernel, extra, qT, k, vT, groups, heads_per_group, dk, name):
    B, nt, _, t = qT.shape
    S = nt * t
    hp = heads_per_group
    in_specs = [pl.BlockSpec(a.shape, lambda b, h, i: (0, 0)) for a in extra]
    in_specs += [
        pl.BlockSpec((None, None, hp * dk, t), lambda b, h, i: (b, i, h, 0)),
        pl.BlockSpec((None, S, hp * dk), lambda b, h, i: (b, 0, h)),
        pl.BlockSpec((None, nt, hp * VAL_ROWS, t), lambda b, h, i: (b, 0, h, 0)),
    ]
    return pl.pallas_call(
        functools.partial(kernel, t=t),
        grid=(B, groups, nt),
        in_specs=in_specs,
        out_specs=pl.BlockSpec((None, t, hp * LANES), lambda b, h, i: (b, i, h)),
        out_shape=jax.ShapeDtypeStruct((B, S, groups * hp * LANES), BF16),
        scratch_shapes=[
            pltpu.VMEM((2, VAL_ROWS, t), F32),
            pltpu.VMEM((2, 1, t), F32),
            pltpu.VMEM((t, t), F32),
            pltpu.VMEM((t, t), F32),
        ],
        compiler_params=pltpu.CompilerParams(
            dimension_semantics=("arbitrary", "arbitrary", "arbitrary"), vmem_limit_bytes=VMEM_LIMIT),
        name=name,
    )(*extra, qT, k, vT)


def _layer_norm(y, g, b):
    mu = jnp.mean(y, axis=-1, keepdims=True)
    d = y - mu
    var = jnp.mean(d * d, axis=-1, keepdims=True)
    return d * lax.rsqrt(var + LN_EPS) * g + b


def _post_kernel(oa_ref, ob_ref, x_ref, wo_ref, g_ref, b_ref, wr_ref, br_ref,
                 x1_ref, idx_ref, gate_ref):
    half = oa_ref.shape[1]
    mixed = _dot(oa_ref[...], wo_ref[:half, :]) + _dot(ob_ref[...], wo_ref[half:, :])
    x1 = _layer_norm(DN_ALPHA * x_ref[...] + mixed, g_ref[...], b_ref[...])
    x1_ref[...] = x1

    logits = jnp.dot(x1, wr_ref[...], preferred_element_type=F32, precision=lax.Precision.HIGHEST) + br_ref[...]
    lane = lax.broadcasted_iota(jnp.int32, logits.shape, 1).astype(F32)
    work = logits
    vals, idxs = [], []
    for _ in range(TOP_K):
        m = jnp.max(work, axis=-1, keepdims=True)
        idx = jnp.min(jnp.where(work == m, lane, float(LANES)), axis=-1, keepdims=True)
        vals.append(m)
        idxs.append(idx)
        work = jnp.where(lane == idx, NEG_BIG, work)
    es = [jnp.exp(v - vals[0]) for v in vals]
    den = es[0] + es[1] + es[2] + es[3]
    idx_out = jnp.zeros(logits.shape, F32)
    gate_out = jnp.zeros(logits.shape, F32)
    for k in range(TOP_K):
        idx_out = jnp.where(lane == float(k), idxs[k], idx_out)
        gate_out = jnp.where(lane == float(k), es[k] / den, gate_out)
    idx_ref[...] = idx_out.astype(jnp.int32)
    gate_ref[...] = gate_out


def _post_call(oa, ob, x2, wo, g, b, wr, br):
    T = x2.shape[0]
    tm = POST_TM
    row = lambda i: (i, 0)
    full = lambda i: (0, 0)
    return pl.pallas_call(
        _post_kernel,
        grid=(T // tm,),
        in_specs=[
            pl.BlockSpec((tm, oa.shape[1]), row), pl.BlockSpec((tm, ob.shape[1]), row),
            pl.BlockSpec((tm, D_MODEL), row), pl.BlockSpec(wo.shape, full),
            pl.BlockSpec(g.shape, full), pl.BlockSpec(b.shape, full),
            pl.BlockSpec(wr.shape, full), pl.BlockSpec(br.shape, full),
        ],
        out_specs=[
            pl.BlockSpec((tm, D_MODEL), row), pl.BlockSpec((tm, LANES), row), pl.BlockSpec((tm, LANES), row),
        ],
        out_shape=[
            jax.ShapeDtypeStruct((T, D_MODEL), F32),
            jax.ShapeDtypeStruct((T, LANES), jnp.int32), jax.ShapeDtypeStruct((T, LANES), F32),
        ],
        compiler_params=pltpu.CompilerParams(dimension_semantics=("arbitrary",), vmem_limit_bytes=VMEM_LIMIT),
        name="post",
    )(oa, ob, x2, wo, g, b, wr, br)


def _lane_cumsum(x, n):
    lane = lax.broadcasted_iota(jnp.int32, x.shape, 1)
    s = 1
    while s < n:
        x = x + jnp.where(lane >= s, pltpu.roll(x, s, axis=1), 0.0)
        s *= 2
    return x


def _route_kernel(e_ref, lpos_ref, nch_ref, gbase_ref, meta_ref, be_ref, *, bm, rows_per_tile):
    e = e_ref[...]
    R = e.shape[0]
    nt = R // rows_per_tile
    r_i = lax.broadcasted_iota(jnp.int32, (LANES, LANES), 0)
    c_i = lax.broadcasted_iota(jnp.int32, (LANES, LANES), 1)
    lane_incl = (r_i <= c_i).astype(BF16)
    rr = lax.broadcasted_iota(jnp.int32, (R, R), 0)
    rc = lax.broadcasted_iota(jnp.int32, (R, R), 1)
    same_tile = (rr // rows_per_tile) == (rc // rows_per_tile)
    rows_before = jnp.logical_and(rc < rr, same_tile).astype(BF16)
    tr = lax.broadcasted_iota(jnp.int32, (nt, R), 0)
    tc = lax.broadcasted_iota(jnp.int32, (nt, R), 1)
    tile_rows = (tc // rows_per_tile == tr).astype(BF16)
    er = lax.broadcasted_iota(jnp.int32, (R, nt), 0)
    ec = lax.broadcasted_iota(jnp.int32, (R, nt), 1)
    row_tile = (er // rows_per_tile == ec).astype(F32)
    lt_r = lax.broadcasted_iota(jnp.int32, (nt, nt), 0)
    lt_c = lax.broadcasted_iota(jnp.int32, (nt, nt), 1)
    tiles_before = (lt_c < lt_r).astype(BF16)
    lane_t = lax.broadcasted_iota(jnp.int32, (nt, LANES), 1)

    rank = jnp.zeros(e.shape, F32)
    cnt = jnp.zeros((nt, LANES), F32)
    for ex in range(N_EXPERTS):
        hit = e == ex
        m = jnp.where(hit, 1.0, 0.0).astype(BF16)
        incl = _dot(m, lane_incl)
        row_tot = jnp.broadcast_to(incl[:, LANES - 1:LANES], e.shape).astype(BF16)
        before = _dot(rows_before, row_tot)
        rank = jnp.where(hit, incl - 1.0 + before, rank)
        cnt = jnp.where(lane_t == ex, _dot(tile_rows, row_tot), cnt)

    chunk = float(ROW_CHUNK)
    cnt8 = jnp.floor((cnt + (chunk - 1.0)) * (1.0 / chunk)) * chunk
    lstart = _lane_cumsum(cnt8, N_EXPERTS) - cnt8
    tile_pre = _dot(tiles_before, cnt8.astype(BF16))
    tot8 = tile_pre[nt - 1:nt] + cnt8[nt - 1:nt]
    tot_bm = jnp.floor((tot8 + (bm - 1.0)) * (1.0 / bm)) * bm
    end_incl = _lane_cumsum(tot_bm, N_EXPERTS)
    ebase = end_incl - tot_bm

    lrow = jnp.dot(row_tile, lstart, preferred_element_type=F32,
                   precision=lax.Precision.HIGHEST)
    lstart_a = jnp.zeros(e.shape, F32)
    for ex in range(N_EXPERTS):
        lstart_a = jnp.where(e == ex, lrow[:, ex:ex + 1], lstart_a)
    lpos_ref[...] = (lstart_a + rank).astype(jnp.int32)
    nch_ref[...] = (cnt8 * (1.0 / chunk)).astype(jnp.int32)
    gbase_ref[...] = (ebase + tile_pre).astype(jnp.int32)

    sub = lax.broadcasted_iota(jnp.int32, meta_ref.shape, 0)
    meta = jnp.where(sub == 0, ebase + tot8, 0.0)
    meta = jnp.where(sub == 1, (tot_bm - tot8) * (1.0 / chunk), meta)
    meta = jnp.where(sub == 2, end_incl[:, N_EXPERTS - 1:N_EXPERTS] * (1.0 / bm), meta)
    meta_ref[...] = meta.astype(jnp.int32)

    blk = (lax.broadcasted_iota(jnp.int32, be_ref.shape, 0) * LANES
           + lax.broadcasted_iota(jnp.int32, be_ref.shape, 1)).astype(F32) * float(bm)
    block_e = jnp.zeros(be_ref.shape, F32)
    for ex in range(N_EXPERTS):
        block_e = block_e + jnp.where(blk >= end_incl[:, ex:ex + 1], 1.0, 0.0)
    be_ref[...] = jnp.minimum(block_e, N_EXPERTS - 1.0).astype(jnp.int32)


def _route_call(flat_e, bm, n_blocks):
    A = flat_e.shape[0]
    R = A // LANES
    rows_per_tile = DISP_TILE * TOP_K // LANES
    nt = R // rows_per_tile
    be_rows = 8 * pl.cdiv(pl.cdiv(n_blocks, LANES), 8)
    full = lambda i: (0, 0)
    shapes = [(R, LANES), (nt, LANES), (nt, LANES), (8, LANES), (be_rows, LANES)]
    lpos, nch, gbase, meta, be = pl.pallas_call(
        functools.partial(_route_kernel, bm=bm, rows_per_tile=rows_per_tile),
        grid=(1,),
        in_specs=[pl.BlockSpec((R, LANES), full)],
        out_specs=[pl.BlockSpec(s, full) for s in shapes],
        out_shape=[jax.ShapeDtypeStruct(s, jnp.int32) for s in shapes],
        compiler_params=pltpu.CompilerParams(dimension_semantics=("arbitrary",), vmem_limit_bytes=VMEM_LIMIT),
        name="route",
    )(flat_e.reshape(R, LANES))
    E = N_EXPERTS
    return dict(lpos=lpos.reshape(A // TOP_K, TOP_K), nch=nch[:, :E].reshape(-1), gbase=gbase[:, :E].reshape(-1),
                tail_start=meta[0, :E], tail_n=meta[1, :E], n_used=meta[2, :1], block_e=be.reshape(-1)[:n_blocks])


def _chunk_rows(ref, first, rows=ROW_CHUNK):
    return ref.at[pl.ds(pl.multiple_of(first, ROW_CHUNK), rows), :]


def _group_copies(nch_ref, gbase_ref, tile, buf, hbm, sem, to_hbm):
    per_big = BIG_CHUNK // ROW_CHUNK

    def copy(local, remote):
        if to_hbm:
            pltpu.make_async_copy(local, remote, sem).start()
        else:
            pltpu.make_async_copy(remote, local, sem).start()

    def per_expert(ex, done):
        n = nch_ref[tile * N_EXPERTS + ex]
        g0 = gbase_ref[tile * N_EXPERTS + ex]
        l0 = done * ROW_CHUNK
        n_big = lax.shift_right_logical(n, per_big.bit_length() - 1)

        def big(c, carry):
            copy(_chunk_rows(buf, l0 + c * BIG_CHUNK, BIG_CHUNK), _chunk_rows(hbm, g0 + c * BIG_CHUNK, BIG_CHUNK))
            return carry

        def small(c, carry):
            copy(_chunk_rows(buf, l0 + c * ROW_CHUNK), _chunk_rows(hbm, g0 + c * ROW_CHUNK))
            return carry

        lax.fori_loop(0, n_big, big, 0)
        lax.fori_loop(n_big * per_big, n, small, 0)
        return done + n

    return lax.fori_loop(0, N_EXPERTS, per_expert, 0)


def _wait_chunks(n, hbm, sem):
    rows = pl.multiple_of(n * ROW_CHUNK, ROW_CHUNK)

    @pl.when(n > 0)
    def _():
        pltpu.make_async_copy(hbm.at[pl.ds(0, rows), :], hbm.at[pl.ds(0, rows), :], sem).wait()


def _dispatch_kernel(nch_ref, gbase_ref, tstart_ref, tn_ref, nu_ref, x_ref, lposT_ref, xs_hbm,
                     buf_ref, zero_ref, sem, zsem, cnt_ref, *, bm, n_blocks):
    i = pl.program_id(0)
    nt = pl.num_programs(0)
    slot = i % 2

    @pl.when(i >= 2)
    def _():
        _wait_chunks(cnt_ref[slot], xs_hbm, sem.at[slot])

    xb = x_ref[...].astype(BF16)
    lposT = lposT_ref[...]
    rows = DISP_PCHUNK
    for rb in range(DISP_LCAP // rows):
        r = lax.broadcasted_iota(jnp.int32, (rows, xb.shape[0]), 0) + rb * rows
        hit = r == lposT[0:1]
        for k in range(1, TOP_K):
            hit = jnp.logical_or(hit, r == lposT[k:k + 1])
        perm = jnp.where(hit, 1.0, 0.0).astype(BF16)
        buf_ref[slot, rb * rows:(rb + 1) * rows, :] = _dot(perm, xb)
    cnt_ref[slot] = _group_copies(nch_ref, gbase_ref, i, buf_ref.at[slot], xs_hbm, sem.at[slot], True)

    @pl.when(i == nt - 1)
    def _():
        zero_ref[...] = jnp.zeros(zero_ref.shape, F32)

        def per_expert(ex, done):
            first = tstart_ref[ex]

            def per_chunk(c, carry):
                pltpu.make_async_copy(_chunk_rows(zero_ref, 0), _chunk_rows(xs_hbm, first + c * ROW_CHUNK),
                                      zsem.at[0]).start()
                return carry

            lax.fori_loop(0, tn_ref[ex], per_chunk, 0)
            return done + tn_ref[ex]

        n_tail = lax.fori_loop(0, N_EXPERTS, per_expert, 0)

        def per_block(b, carry):
            pltpu.make_async_copy(zero_ref, xs_hbm.at[pl.ds(pl.multiple_of(b * bm, bm), bm), :], zsem.at[1]).start()
            return carry

        lax.fori_loop(nu_ref[0], n_blocks, per_block, 0)
        _wait_chunks(n_tail, xs_hbm, zsem.at[0])

        def wait_block(b, carry):
            pltpu.make_async_copy(zero_ref, xs_hbm.at[pl.ds(0, bm), :], zsem.at[1]).wait()
            return carry

        lax.fori_loop(nu_ref[0], n_blocks, wait_block, 0)
        _wait_chunks(cnt_ref[slot], xs_hbm, sem.at[slot])

        @pl.when(nt >= 2)
        def _():
            _wait_chunks(cnt_ref[1 - slot], xs_hbm, sem.at[1 - slot])


def _dispatch_call(rt, x1, lposT, bm, n_blocks):
    T = x1.shape[0]
    tile = DISP_TILE
    grid_spec = pltpu.PrefetchScalarGridSpec(
        num_scalar_prefetch=5,
        grid=(T // tile,),
        in_specs=[
            pl.BlockSpec((tile, D_MODEL), lambda i, *_: (i, 0)),
            pl.BlockSpec((None, TOP_K, tile), lambda i, *_: (i, 0, 0)),
        ],
        out_specs=pl.BlockSpec(memory_space=pl.ANY),
        scratch_shapes=[
            pltpu.VMEM((2, DISP_LCAP, D_MODEL), F32),
            pltpu.VMEM((bm, D_MODEL), F32),
            pltpu.SemaphoreType.DMA((2,)),
            pltpu.SemaphoreType.DMA((2,)),
            pltpu.SMEM((2,), jnp.int32),
        ],
    )
    return pl.pallas_call(
        functools.partial(_dispatch_kernel, bm=bm, n_blocks=n_blocks),
        grid_spec=grid_spec,
        out_shape=jax.ShapeDtypeStruct((n_blocks * bm, D_MODEL), F32),
        compiler_params=pltpu.CompilerParams(dimension_semantics=("arbitrary",), vmem_limit_bytes=VMEM_LIMIT),
        name="dispatch",
    )(rt["nch"], rt["gbase"], rt["tail_start"], rt["tail_n"], rt["n_used"], x1, lposT)


def _experts_kernel(nused_ref, be_ref, xs_ref, wgu_ref, bgu_ref, wd_ref, bd_ref, y_ref, wgu_b, wd_b):
    j = pl.program_id(0)
    n_used = nused_ref[0]
    new_expert = jnp.logical_or(j == 0, be_ref[j] != be_ref[jnp.maximum(j - 1, 0)])

    @pl.when(jnp.logical_and(j < n_used, new_expert))
    def _():
        wgu_b[...] = wgu_ref[...].astype(BF16)
        wd_b[...] = wd_ref[...].astype(BF16)

    @pl.when(j < n_used)
    def _():
        h = _dot(xs_ref[...].astype(BF16), wgu_b[...]) + bgu_ref[...]
        gate = jnp.minimum(h[:, :D_FF], SWIGLU_LIMIT)
        up = jnp.clip(h[:, D_FF:], -SWIGLU_LIMIT, SWIGLU_LIMIT)
        act = (up + 1.0) * (gate * jax.nn.sigmoid(gate * SWIGLU_ALPHA))
        y_ref[...] = _dot(act.astype(BF16), wd_b[...]) + bd_ref[...]

    @pl.when(j >= n_used)
    def _():
        y_ref[...] = jnp.zeros(y_ref.shape, F32)


def _experts_call(rt, xs, wgu, bgu, wd, bd, bm, n_blocks):
    E = N_EXPERTS
    by_expert = lambda j, nu, be: (be[j], 0, 0)
    grid_spec = pltpu.PrefetchScalarGridSpec(
        num_scalar_prefetch=2,
        grid=(n_blocks,),
        in_specs=[
            pl.BlockSpec((bm, D_MODEL), lambda j, nu, be: (jnp.minimum(j, nu[0] - 1), 0)),
            pl.BlockSpec((None, D_MODEL, 2 * D_FF), by_expert),
            pl.BlockSpec((None, 1, 2 * D_FF), by_expert),
            pl.BlockSpec((None, D_FF, D_MODEL), by_expert),
            pl.BlockSpec((None, 1, D_MODEL), by_expert),
        ],
        out_specs=pl.BlockSpec((bm, D_MODEL), lambda j, nu, be: (j, 0)),
        scratch_shapes=[
            pltpu.VMEM((D_MODEL, 2 * D_FF), BF16),
            pltpu.VMEM((D_FF, D_MODEL), BF16),
        ],
    )
    return pl.pallas_call(
        _experts_kernel,
        grid_spec=grid_spec,
        out_shape=jax.ShapeDtypeStruct((n_blocks * bm, D_MODEL), F32),
        compiler_params=pltpu.CompilerParams(dimension_semantics=("arbitrary",), vmem_limit_bytes=VMEM_LIMIT),
        name="experts",
    )(rt["n_used"], rt["block_e"], xs, wgu, bgu.reshape(E, 1, 2 * D_FF), wd, bd.reshape(E, 1, D_MODEL))


def _combine_kernel(nch_ref, gbase_ref, y_hbm, x1_ref, lpos_ref, gate_ref, g_ref, b_ref, o_ref,
                    buf_ref, sem, cnt_ref):
    i = pl.program_id(0)
    nt = pl.num_programs(0)
    slot = i % 2

    @pl.when(i == 0)
    def _():
        buf_ref[...] = jnp.zeros(buf_ref.shape, F32)
        cnt_ref[0] = _group_copies(nch_ref, gbase_ref, 0, buf_ref.at[0], y_hbm, sem.at[0], False)

    @pl.when(i + 1 < nt)
    def _():
        cnt_ref[1 - slot] = _group_copies(nch_ref, gbase_ref, i + 1, buf_ref.at[1 - slot], y_hbm,
                                          sem.at[1 - slot], False)

    _wait_chunks(cnt_ref[slot], y_hbm, sem.at[slot])
    lpos = lpos_ref[...]
    gates = gate_ref[...]
    cols = DISP_PCHUNK
    y = jnp.zeros(o_ref.shape, F32)
    for cb in range(DISP_LCAP // cols):
        c = lax.broadcasted_iota(jnp.int32, (lpos.shape[0], cols), 1) + cb * cols
        w = jnp.zeros(c.shape, F32)
        for k in range(TOP_K):
            w = jnp.where(c == lpos[:, k:k + 1], gates[:, k:k + 1], w)
        y = y + _dot(w.astype(BF16), buf_ref[slot, cb * cols:(cb + 1) * cols, :].astype(BF16))
    o_ref[...] = _layer_norm(DN_ALPHA * x1_ref[...] + y, g_ref[...], b_ref[...])


def _combine_call(rt, y, x1, gates, g, b):
    T = x1.shape[0]
    tile = DISP_TILE
    grid_spec = pltpu.PrefetchScalarGridSpec(
        num_scalar_prefetch=2,
        grid=(T // tile,),
        in_specs=[
            pl.BlockSpec(memory_space=pl.ANY),
            pl.BlockSpec((tile, D_MODEL), lambda i, *_: (i, 0)),
            pl.BlockSpec((tile, TOP_K), lambda i, *_: (i, 0)),
            pl.BlockSpec((tile, LANES), lambda i, *_: (i, 0)),
            pl.BlockSpec(g.shape, lambda i, *_: (0, 0)),
            pl.BlockSpec(b.shape, lambda i, *_: (0, 0)),
        ],
        out_specs=pl.BlockSpec((tile, D_MODEL), lambda i, *_: (i, 0)),
        scratch_shapes=[
            pltpu.VMEM((2, DISP_LCAP, D_MODEL), F32),
            pltpu.SemaphoreType.DMA((2,)),
            pltpu.SMEM((2,), jnp.int32),
        ],
    )
    return pl.pallas_call(
        _combine_kernel,
        grid_spec=grid_spec,
        out_shape=jax.ShapeDtypeStruct((T, D_MODEL), F32),
        compiler_params=pltpu.CompilerParams(dimension_semantics=("arbitrary",), vmem_limit_bytes=VMEM_LIMIT),
        name="combine",
    )(rt["nch"], rt["gbase"], y, x1, rt["lpos"], gates, g, b)


def _rope_lane_order(n_sub):
    half = DIFF_HEAD_DIM // 2
    per = LANES // 2 // n_sub
    assert per == half or n_sub == 1
    cols = []
    for part in range(2):
        for sub in range(n_sub):
            cols.extend(sub * 64 + part * half + d for d in range(half))
    return np.asarray(cols)


def _prep_weights(w_in, mla_q_norm_g, w_uq, mla_kv_norm_g, w_ukv):
    o_dq, o_dk, o_dv, o_cq, o_ckv, o_kr = 0, 512, 1024, 1536, 1792, 1920
    head_order = _rope_lane_order(2)
    diff_cols = np.concatenate([h * LANES + head_order for h in range(DIFF_HEADS)])
    w = {}
    w["dqT"] = w_in[:, o_dq + diff_cols].T.astype(BF16)
    w["dk"] = w_in[:, o_dk + diff_cols].astype(BF16)
    w["dvT"] = w_in[:, o_dv:o_cq].T.astype(BF16)
    w["cqT"] = w_in[:, o_cq:o_ckv].T.astype(BF16)
    w["ckv"] = w_in[:, o_ckv:o_kr].astype(BF16)
    w["ckvT"] = w["ckv"].T

    def spread_rope(cols64):
        z = jnp.zeros((cols64.shape[0], 32), cols64.dtype)
        return jnp.concatenate([cols64[:, :32], z, cols64[:, 32:], z], axis=1)

    w["kr"] = spread_rope(w_in[:, o_kr:o_kr + MLA_ROPE_DIM]).astype(BF16)
    uq = []
    for h in range(MLA_HEADS):
        base = h * MLA_QK_DIM
        uq.append(w_uq[:, base:base + MLA_NOPE_DIM])
        uq.append(spread_rope(w_uq[:, base + MLA_NOPE_DIM:base + MLA_QK_DIM]))
    w["uqT"] = jnp.concatenate(uq, axis=1).T.astype(BF16)
    per = MLA_NOPE_DIM + MLA_V_DIM
    w["uk"] = jnp.concatenate([w_ukv[:, h * per:h * per + MLA_NOPE_DIM] for h in range(MLA_HEADS)], axis=1).astype(BF16)
    w["uvT"] = jnp.concatenate(
        [w_ukv[:, h * per + MLA_NOPE_DIM:(h + 1) * per] for h in range(MLA_HEADS)], axis=1).T.astype(BF16)
    w["gq"] = mla_q_norm_g.reshape(MLA_Q_RANK, 1)
    w["gkv"] = mla_kv_norm_g.reshape(1, MLA_KV_RANK)
    w["gkvc"] = mla_kv_norm_g.reshape(MLA_KV_RANK, 1)
    return w


def _rope_tables(positions):
    half = MLA_ROPE_DIM // 2
    inv_freq = 1.0 / (ROPE_THETA ** (jnp.arange(0, MLA_ROPE_DIM, 2, dtype=F32) / MLA_ROPE_DIM))
    ang = positions.astype(F32)[..., None] * inv_freq
    ang = jnp.tile(ang, (1, 1, LANES // half))
    sign = jnp.where(jnp.arange(LANES) < LANES // 2, -1.0, 1.0).astype(F32)
    cos_t, sin_t = jnp.cos(ang), jnp.sin(ang) * sign
    return cos_t, sin_t, cos_t.transpose(0, 2, 1), sin_t.transpose(0, 2, 1)


def kernel(x, positions, w_in, lambda_q1, lambda_k1, lambda_q2, lambda_k2, subln_g, mla_q_norm_g, w_uq,
           mla_kv_norm_g, w_ukv, w_o, ln1_g, ln1_b, w_router, b_router, w_gate_up, b_gate_up, w_down, b_down,
           ln2_g, ln2_b):
    B, S, D = x.shape
    T = B * S
    l = 0
    x2 = x.reshape(T, D)
    w = _prep_weights(w_in[l], mla_q_norm_g[l], w_uq[l], mla_kv_norm_g[l], w_ukv[l])

    dqT, dk, dvT, mqT, mk, mvT = _proj_call(x, _rope_tables(positions), w)
    lam_vecs = [v[l].reshape(1, DIFF_HEAD_DIM) for v in (lambda_q1, lambda_k1, lambda_q2, lambda_k2)]
    o_a = _attn_call(_diff_attn_kernel, lam_vecs + [subln_g[l].reshape(DIFF_V_DIM, 1)],
                     dqT, dk, dvT, DIFF_HEADS, 1, LANES, "diff_attn")
    o_b = _attn_call(_mla_attn_kernel, [], mqT, mk, mvT, MLA_HEADS // 2, 2, MLA_QK_PAD, "mla_attn")

    wr = jnp.pad(w_router[l], ((0, 0), (0, LANES - N_EXPERTS)))
    br = jnp.pad(b_router[l], (0, LANES - N_EXPERTS), constant_values=NEG_BIG).reshape(1, LANES)
    x1, idx, gates = _post_call(
        o_a.reshape(T, -1), o_b.reshape(T, -1), x2, w_o[l].astype(BF16),
        ln1_g[l].reshape(1, D), ln1_b[l].reshape(1, D), wr, br)

    bm = MOE_BM
    A = T * TOP_K
    n_tiles = T // DISP_TILE
    n_blocks = pl.cdiv(A + n_tiles * N_EXPERTS * (ROW_CHUNK - 1) + N_EXPERTS * (bm - ROW_CHUNK), bm)
    rt = _route_call(idx[:, :TOP_K].reshape(A), bm, n_blocks)
    lposT = rt["lpos"].reshape(n_tiles, DISP_TILE, TOP_K).transpose(0, 2, 1)
    xs = _dispatch_call(rt, x1, lposT, bm, n_blocks)
    y = _experts_call(rt, xs, w_gate_up[l], b_gate_up[l], w_down[l], b_down[l], bm, n_blocks)
    out = _combine_call(rt, y, x1, gates, ln2_g[l].reshape(1, D), ln2_b[l].reshape(1, D))
    return out.reshape(B, S, D)
```

```python
import functools
import math

import numpy as np
import jax
import jax.numpy as jnp
from jax import lax
from jax.experimental import pallas as pl
from jax.experimental.pallas import tpu as pltpu

D_MODEL = 1024
DIFF_HEADS = 4
DIFF_HEAD_DIM = 64
DIFF_V_DIM = 128
MLA_HEADS = 4
MLA_V_DIM = 128
MLA_NOPE_DIM = 128
MLA_ROPE_DIM = 64
MLA_QK_DIM = MLA_NOPE_DIM + MLA_ROPE_DIM
MLA_Q_RANK = 256
MLA_KV_RANK = 128
ROPE_THETA = 10000.0
N_EXPERTS = 32
TOP_K = 4
D_FF = 1024
SWIGLU_LIMIT = 7.0
SWIGLU_ALPHA = 1.702
LN_EPS = 1e-5
SUBLN_EPS = 1e-5
MLA_RMS_EPS = 1e-6
DEPTH = 1
DN_ALPHA = (2.0 * DEPTH) ** 0.25
LAMBDA_INIT = 0.8 - 0.6 * math.exp(-0.3 * 0)

LANES = 128
MLA_QK_PAD = 2 * LANES
BF16_ROWS = 16
VAL_ROWS = LANES + BF16_ROWS
VMEM_LIMIT = 56 * 1024 * 1024

ATTN_TILE = 512
POST_TM = 512
MOE_BM = 256
ROW_CHUNK = 8
BIG_CHUNK = 32
DISP_TILE = 512
DISP_PCHUNK = 256
DISP_LCAP = -(-(DISP_TILE * TOP_K + N_EXPERTS * (ROW_CHUNK - 1)) // DISP_PCHUNK) * DISP_PCHUNK

NEG_BIG = -1e30
LOG2E = math.log2(math.e)
F32 = jnp.float32
BF16 = jnp.bfloat16


def _dot(a, b):
    return jnp.dot(a, b, preferred_element_type=F32)


def _dot_nt(a, b):
    return lax.dot_general(a, b, (((1,), (1,)), ((), ())), preferred_element_type=F32)


def _rope128(blk, cos, sin):
    return blk * cos + pltpu.roll(blk, 64, axis=1) * sin


def _rope128_t(blk, cos, sin):
    half = LANES // 2
    rolled = jnp.concatenate([blk[half:], blk[:half]], axis=0)
    return blk * cos + rolled * sin


def _rms_rows(t, g, eps):
    return t * lax.rsqrt(jnp.mean(t * t, axis=-1, keepdims=True) + eps) * g


def _rms_cols(t, g, eps):
    return t * lax.rsqrt(jnp.mean(t * t, axis=0, keepdims=True) + eps) * g


def _store_values(vT_ref, vT, heads):
    ones = jnp.ones((BF16_ROWS, vT.shape[1]), BF16)
    for h in range(heads):
        vT_ref[h * VAL_ROWS:h * VAL_ROWS + LANES, :] = vT[h * LANES:(h + 1) * LANES].astype(BF16)
        vT_ref[h * VAL_ROWS + LANES:(h + 1) * VAL_ROWS, :] = ones


def _proj_kernel(x_ref, cos_ref, sin_ref, cosT_ref, sinT_ref,
                 wdqT_ref, wdk_ref, wdvT_ref, wcqT_ref, wckv_ref, wckvT_ref, wkr_ref,
                 gq_ref, gkv_ref, gkvc_ref, wuqT_ref, wuk_ref, wuvT_ref,
                 dqT_ref, dk_ref, dvT_ref, mqT_ref, mk_ref, mvT_ref):
    xb = x_ref[...].astype(BF16)
    cos, sin = cos_ref[...], sin_ref[...]
    cosT, sinT = cosT_ref[...], sinT_ref[...]

    dq_scale = DIFF_HEAD_DIM ** -0.5 * LOG2E
    mq_scale = MLA_QK_DIM ** -0.5 * LOG2E

    dqT = _dot_nt(wdqT_ref[...], xb)
    for h in range(DIFF_HEADS):
        sl = slice(h * LANES, (h + 1) * LANES)
        dqT_ref[sl, :] = (_rope128_t(dqT[sl], cosT, sinT) * dq_scale).astype(BF16)
    _store_values(dvT_ref, _dot_nt(wdvT_ref[...], xb), DIFF_HEADS)
    dk = _dot(xb, wdk_ref[...])
    for h in range(DIFF_HEADS):
        sl = slice(h * LANES, (h + 1) * LANES)
        dk_ref[:, sl] = _rope128(dk[:, sl], cos, sin).astype(BF16)

    cqT = _rms_cols(_dot_nt(wcqT_ref[...], xb), gq_ref[...], MLA_RMS_EPS)
    qT = _dot(wuqT_ref[...], cqT.astype(BF16))
    for h in range(MLA_HEADS):
        nope = slice(h * MLA_QK_PAD, h * MLA_QK_PAD + LANES)
        ropes = slice(h * MLA_QK_PAD + LANES, (h + 1) * MLA_QK_PAD)
        mqT_ref[nope, :] = (qT[nope] * mq_scale).astype(BF16)
        mqT_ref[ropes, :] = (_rope128_t(qT[ropes], cosT, sinT) * mq_scale).astype(BF16)

    ckvT = _rms_cols(_dot_nt(wckvT_ref[...], xb), gkvc_ref[...], MLA_RMS_EPS)
    _store_values(mvT_ref, _dot(wuvT_ref[...], ckvT.astype(BF16)), MLA_HEADS)
    ckv = _rms_rows(_dot(xb, wckv_ref[...]), gkv_ref[...], MLA_RMS_EPS)
    k_nope = _dot(ckv.astype(BF16), wuk_ref[...])
    k_pe = _rope128(_dot(xb, wkr_ref[...]), cos, sin).astype(BF16)
    for h in range(MLA_HEADS):
        mk_ref[:, h * MLA_QK_PAD:h * MLA_QK_PAD + LANES] = k_nope[:, h * LANES:(h + 1) * LANES].astype(BF16)
        mk_ref[:, h * MLA_QK_PAD + LANES:(h + 1) * MLA_QK_PAD] = k_pe


def _proj_call(x3, tabs, w):
    B, S, D = x3.shape
    tm = ATTN_TILE
    nt = S // tm
    cos_t, sin_t, cosT, sinT = tabs
    weights = [w["dqT"], w["dk"], w["dvT"], w["cqT"], w["ckv"], w["ckvT"], w["kr"],
               w["gq"], w["gkv"], w["gkvc"], w["uqT"], w["uk"], w["uvT"]]
    tok = lambda b, i: (b, i, 0)
    feat = lambda b, i: (b, i, 0, 0)
    in_specs = [pl.BlockSpec((None, tm, D), tok),
                pl.BlockSpec((None, tm, LANES), tok), pl.BlockSpec((None, tm, LANES), tok),
                pl.BlockSpec((None, LANES, tm), lambda b, i: (b, 0, i)),
                pl.BlockSpec((None, LANES, tm), lambda b, i: (b, 0, i))]
    in_specs += [pl.BlockSpec(a.shape, lambda b, i: (0, 0)) for a in weights]
    mq_w = MLA_HEADS * MLA_QK_PAD
    dv_w, mv_w = DIFF_HEADS * VAL_ROWS, MLA_HEADS * VAL_ROWS
    out_specs = [pl.BlockSpec((None, None, 512, tm), feat), pl.BlockSpec((None, tm, 512), tok),
                 pl.BlockSpec((None, None, dv_w, tm), feat), pl.BlockSpec((None, None, mq_w, tm), feat),
                 pl.BlockSpec((None, tm, mq_w), tok), pl.BlockSpec((None, None, mv_w, tm), feat)]
    out_shape = [jax.ShapeDtypeStruct((B, nt, 512, tm), BF16), jax.ShapeDtypeStruct((B, S, 512), BF16),
                 jax.ShapeDtypeStruct((B, nt, dv_w, tm), BF16), jax.ShapeDtypeStruct((B, nt, mq_w, tm), BF16),
                 jax.ShapeDtypeStruct((B, S, mq_w), BF16), jax.ShapeDtypeStruct((B, nt, mv_w, tm), BF16)]
    return pl.pallas_call(
        _proj_kernel,
        grid=(B, nt),
        in_specs=in_specs,
        out_specs=out_specs,
        out_shape=out_shape,
        compiler_params=pltpu.CompilerParams(dimension_semantics=("arbitrary", "arbitrary"),
                                             vmem_limit_bytes=VMEM_LIMIT),
        name="proj",
    )(x3, cos_t, sin_t, cosT, sinT, *weights)


def _flash_body(qTs, keys_of, values_of, acc_ref, m_ref, s_refs, *, t):
    i = pl.program_id(2)
    m_ref[...] = jnp.full(m_ref.shape, -jnp.inf, F32)
    acc_ref[...] = jnp.zeros(acc_ref.shape, F32)

    def scores(j, u):
        s_refs[u][...] = _dot(keys_of(u, j), qTs[u])

    def finish(j, u, masked):
        s = s_refs[u][...]
        if masked:
            key = lax.broadcasted_iota(jnp.int32, s.shape, 0)
            qry = lax.broadcasted_iota(jnp.int32, s.shape, 1)
            s = jnp.where(key <= qry, s, -jnp.inf)
        m_prev = m_ref[u]
        m_new = jnp.maximum(m_prev, jnp.max(s, axis=0, keepdims=True))
        alpha = jnp.exp2(m_prev - m_new)
        p = jnp.exp2((s - m_new).astype(BF16))
        acc_ref[u] = alpha * acc_ref[u] + _dot(values_of(u, j), p)
        m_ref[u] = m_new

    def full_step(j, carry):
        scores(j, 1)
        finish(j, 0, False)
        scores(j + 1, 0)
        finish(j, 1, False)
        return carry

    scores(0, 0)
    lax.fori_loop(0, i, full_step, 0)
    scores(i, 1)
    finish(i, 0, True)
    finish(i, 1, True)


def _key_tile(k_ref, j, t, lanes=slice(None)):
    return k_ref[pl.ds(pl.multiple_of(j * t, t), t), lanes]


def _normalized(acc_ref, u):
    return acc_ref[u, :LANES, :] / acc_ref[u, LANES:LANES + 1, :]


def _diff_attn_kernel(lq1_ref, lk1_ref, lq2_ref, lk2_ref, g_ref, qT_ref, k_ref, vT_ref, o_ref,
                      acc_ref, m_ref, sa_ref, sb_ref, *, t):
    qT = qT_ref[...]
    row = lax.broadcasted_iota(jnp.int32, qT.shape, 0)
    first = (row % 64) < 32
    zero = jnp.zeros_like(qT)
    qTs = (jnp.where(first, qT, zero), jnp.where(first, zero, qT))
    _flash_body(qTs, lambda u, j: _key_tile(k_ref, j, t), lambda u, j: vT_ref[j],
                acc_ref, m_ref, (sa_ref, sb_ref), t=t)

    lam = (jnp.exp(jnp.sum(lq1_ref[...] * lk1_ref[...], axis=-1, keepdims=True))
           - jnp.exp(jnp.sum(lq2_ref[...] * lk2_ref[...], axis=-1, keepdims=True)) + LAMBDA_INIT)
    oT = _normalized(acc_ref, 0) - lam * _normalized(acc_ref, 1)
    oT = _rms_cols(oT, g_ref[...], SUBLN_EPS) * (1.0 - LAMBDA_INIT)
    o_ref[...] = oT.T.astype(o_ref.dtype)


def _mla_attn_kernel(qT_ref, k_ref, vT_ref, o_ref, acc_ref, m_ref, sa_ref, sb_ref, *, t):
    dk = MLA_QK_PAD
    qTs = tuple(qT_ref[u * dk:(u + 1) * dk, :] for u in range(2))
    _flash_body(qTs, lambda u, j: _key_tile(k_ref, j, t, slice(u * dk, (u + 1) * dk)),
                lambda u, j: vT_ref[j, u * VAL_ROWS:(u + 1) * VAL_ROWS, :],
                acc_ref, m_ref, (sa_ref, sb_ref), t=t)
    for u in range(2):
        o_ref[:, u * LANES:(u + 1) * LANES] = _normalized(acc_ref, u).T.astype(o_ref.dtype)


def _attn_call(kernel, extra, qT, k, vT, groups, heads_per_group, dk, name):
    B, nt, _, t = qT.shape
    S = nt * t
    hp = heads_per_group
    in_specs = [pl.BlockSpec(a.shape, lambda b, h, i: (0, 0)) for a in extra]
    in_specs += [
        pl.BlockSpec((None, None, hp * dk, t), lambda b, h, i: (b, i, h, 0)),
        pl.BlockSpec((None, S, hp * dk), lambda b, h, i: (b, 0, h)),
        pl.BlockSpec((None, nt, hp * VAL_ROWS, t), lambda b, h, i: (b, 0, h, 0)),
    ]
    return pl.pallas_call(
        functools.partial(kernel, t=t),
        grid=(B, groups, nt),
        in_specs=in_specs,
        out_specs=pl.BlockSpec((None, t, hp * LANES), lambda b, h, i: (b, i, h)),
        out_shape=jax.ShapeDtypeStruct((B, S, groups * hp * LANES), BF16),
        scratch_shapes=[
            pltpu.VMEM((2, VAL_ROWS, t), F32),
            pltpu.VMEM((2, 1, t), F32),
            pltpu.VMEM((t, t), F32),
            pltpu.VMEM((t, t), F32),
        ],
        compiler_params=pltpu.CompilerParams(
            dimension_semantics=("arbitrary", "arbitrary", "arbitrary"), vmem_limit_bytes=VMEM_LIMIT),
        name=name,
    )(*extra, qT, k, vT)


def _layer_norm(y, g, b):
    mu = jnp.mean(y, axis=-1, keepdims=True)
    d = y - mu
    var = jnp.mean(d * d, axis=-1, keepdims=True)
    return d * lax.rsqrt(var + LN_EPS) * g + b


def _post_kernel(oa_ref, ob_ref, x_ref, wo_ref, g_ref, b_ref, wr_ref, br_ref,
                 x1_ref, idx_ref, gate_ref):
    half = oa_ref.shape[1]
    mixed = _dot(oa_ref[...], wo_ref[:half, :]) + _dot(ob_ref[...], wo_ref[half:, :])
    x1 = _layer_norm(DN_ALPHA * x_ref[...] + mixed, g_ref[...], b_ref[...])
    x1_ref[...] = x1

    logits = jnp.dot(x1, wr_ref[...], preferred_element_type=F32, precision=lax.Precision.HIGHEST) + br_ref[...]
    lane = lax.broadcasted_iota(jnp.int32, logits.shape, 1).astype(F32)
    work = logits
    vals, idxs = [], []
    for _ in range(TOP_K):
        m = jnp.max(work, axis=-1, keepdims=True)
        idx = jnp.min(jnp.where(work == m, lane, float(LANES)), axis=-1, keepdims=True)
        vals.append(m)
        idxs.append(idx)
        work = jnp.where(lane == idx, NEG_BIG, work)
    es = [jnp.exp(v - vals[0]) for v in vals]
    den = es[0] + es[1] + es[2] + es[3]
    idx_out = jnp.zeros(logits.shape, F32)
    gate_out = jnp.zeros(logits.shape, F32)
    for k in range(TOP_K):
        idx_out = jnp.where(lane == float(k), idxs[k], idx_out)
        gate_out = jnp.where(lane == float(k), es[k] / den, gate_out)
    idx_ref[...] = idx_out.astype(jnp.int32)
    gate_ref[...] = gate_out


def _post_call(oa, ob, x2, wo, g, b, wr, br):
    T = x2.shape[0]
    tm = POST_TM
    row = lambda i: (i, 0)
    full = lambda i: (0, 0)
    return pl.pallas_call(
        _post_kernel,
        grid=(T // tm,),
        in_specs=[
            pl.BlockSpec((tm, oa.shape[1]), row), pl.BlockSpec((tm, ob.shape[1]), row),
            pl.BlockSpec((tm, D_MODEL), row), pl.BlockSpec(wo.shape, full),
            pl.BlockSpec(g.shape, full), pl.BlockSpec(b.shape, full),
            pl.BlockSpec(wr.shape, full), pl.BlockSpec(br.shape, full),
        ],
        out_specs=[
            pl.BlockSpec((tm, D_MODEL), row), pl.BlockSpec((tm, LANES), row), pl.BlockSpec((tm, LANES), row),
        ],
        out_shape=[
            jax.ShapeDtypeStruct((T, D_MODEL), F32),
            jax.ShapeDtypeStruct((T, LANES), jnp.int32), jax.ShapeDtypeStruct((T, LANES), F32),
        ],
        compiler_params=pltpu.CompilerParams(dimension_semantics=("arbitrary",), vmem_limit_bytes=VMEM_LIMIT),
        name="post",
    )(oa, ob, x2, wo, g, b, wr, br)


def _lane_cumsum(x, n):
    lane = lax.broadcasted_iota(jnp.int32, x.shape, 1)
    s = 1
    while s < n:
        x = x + jnp.where(lane >= s, pltpu.roll(x, s, axis=1), 0.0)
        s *= 2
    return x


def _route_kernel(e_ref, lpos_ref, nch_ref, gbase_ref, meta_ref, *, bm, rows_per_tile):
    e = e_ref[...]
    R = e.shape[0]
    nt = R // rows_per_tile
    r_i = lax.broadcasted_iota(jnp.int32, (LANES, LANES), 0)
    c_i = lax.broadcasted_iota(jnp.int32, (LANES, LANES), 1)
    lane_incl = (r_i <= c_i).astype(BF16)
    rr = lax.broadcasted_iota(jnp.int32, (R, R), 0)
    rc = lax.broadcasted_iota(jnp.int32, (R, R), 1)
    same_tile = (rr // rows_per_tile) == (rc // rows_per_tile)
    rows_before = jnp.logical_and(rc < rr, same_tile).astype(BF16)
    tr = lax.broadcasted_iota(jnp.int32, (nt, R), 0)
    tc = lax.broadcasted_iota(jnp.int32, (nt, R), 1)
    tile_rows = (tc // rows_per_tile == tr).astype(BF16)
    er = lax.broadcasted_iota(jnp.int32, (R, nt), 0)
    ec = lax.broadcasted_iota(jnp.int32, (R, nt), 1)
    row_tile = (er // rows_per_tile == ec).astype(F32)
    lt_r = lax.broadcasted_iota(jnp.int32, (nt, nt), 0)
    lt_c = lax.broadcasted_iota(jnp.int32, (nt, nt), 1)
    tiles_before = (lt_c < lt_r).astype(BF16)
    lane_t = lax.broadcasted_iota(jnp.int32, (nt, LANES), 1)

    rank = jnp.zeros(e.shape, F32)
    cnt = jnp.zeros((nt, LANES), F32)
    for ex in range(N_EXPERTS):
        hit = e == ex
        m = jnp.where(hit, 1.0, 0.0).astype(BF16)
        incl = _dot(m, lane_incl)
        row_tot = jnp.broadcast_to(incl[:, LANES - 1:LANES], e.shape).astype(BF16)
        before = _dot(rows_before, row_tot)
        rank = jnp.where(hit, incl - 1.0 + before, rank)
        cnt = jnp.where(lane_t == ex, _dot(tile_rows, row_tot), cnt)

    chunk = float(ROW_CHUNK)
    cnt8 = jnp.floor((cnt + (chunk - 1.0)) * (1.0 / chunk)) * chunk
    lstart = _lane_cumsum(cnt8, N_EXPERTS) - cnt8
    tile_pre = _dot(tiles_before, cnt8.astype(BF16))
    tot8 = tile_pre[nt - 1:nt] + cnt8[nt - 1:nt]
    tot_bm = jnp.floor((tot8 + (bm - 1.0)) * (1.0 / bm)) * bm
    end_incl = _lane_cumsum(tot_bm, N_EXPERTS)
    ebase = end_incl - tot_bm

    lrow = jnp.dot(row_tile, lstart, preferred_element_type=F32,
                   precision=lax.Precision.HIGHEST)
    lstart_a = jnp.zeros(e.shape, F32)
    for ex in range(N_EXPERTS):
        lstart_a = jnp.where(e == ex, lrow[:, ex:ex + 1], lstart_a)
    lpos_ref[...] = (lstart_a + rank).astype(jnp.int32)
    nch_ref[...] = (cnt8 * (1.0 / chunk)).astype(jnp.int32)
    gbase_ref[...] = (ebase + tile_pre).astype(jnp.int32)

    sub = lax.broadcasted_iota(jnp.int32, meta_ref.shape, 0)
    meta = jnp.where(sub == 0, ebase + tot8, 0.0)
    meta = jnp.where(sub == 1, (tot_bm - tot8) * (1.0 / chunk), meta)
    n_used = end_incl[:, N_EXPERTS - 1:N_EXPERTS] * (1.0 / bm)
    meta = jnp.where(sub == 2, n_used, meta)
    lane_m = lax.broadcasted_iota(jnp.int32, meta_ref.shape, 1)
    meta = jnp.where(sub == 3, jnp.where(lane_m < N_EXPERTS, ebase * (1.0 / bm), n_used), meta)
    meta_ref[...] = meta.astype(jnp.int32)


def _route_call(flat_e, bm):
    A = flat_e.shape[0]
    R = A // LANES
    rows_per_tile = DISP_TILE * TOP_K // LANES
    nt = R // rows_per_tile
    full = lambda i: (0, 0)
    shapes = [(R, LANES), (nt, LANES), (nt, LANES), (8, LANES)]
    lpos, nch, gbase, meta = pl.pallas_call(
        functools.partial(_route_kernel, bm=bm, rows_per_tile=rows_per_tile),
        grid=(1,),
        in_specs=[pl.BlockSpec((R, LANES), full)],
        out_specs=[pl.BlockSpec(s, full) for s in shapes],
        out_shape=[jax.ShapeDtypeStruct(s, jnp.int32) for s in shapes],
        compiler_params=pltpu.CompilerParams(dimension_semantics=("arbitrary",), vmem_limit_bytes=VMEM_LIMIT),
        name="route",
    )(flat_e.reshape(R, LANES))
    E = N_EXPERTS
    return dict(lpos=lpos.reshape(A // TOP_K, TOP_K), nch=nch[:, :E].reshape(-1), gbase=gbase[:, :E].reshape(-1),
                tail_start=meta[0, :E], tail_n=meta[1, :E], n_used=meta[2, :1], first_block=meta[3, :E + 1])


def _chunk_rows(ref, first, rows=ROW_CHUNK):
    return ref.at[pl.ds(pl.multiple_of(first, ROW_CHUNK), rows), :]


def _group_copies(nch_ref, gbase_ref, tile, buf, hbm, sem, to_hbm):
    per_big = BIG_CHUNK // ROW_CHUNK

    def copy(local, remote):
        if to_hbm:
            pltpu.make_async_copy(local, remote, sem).start()
        else:
            pltpu.make_async_copy(remote, local, sem).start()

    def per_expert(ex, done):
        n = nch_ref[tile * N_EXPERTS + ex]
        g0 = gbase_ref[tile * N_EXPERTS + ex]
        l0 = done * ROW_CHUNK
        n_big = lax.shift_right_logical(n, per_big.bit_length() - 1)

        def big(c, carry):
            copy(_chunk_rows(buf, l0 + c * BIG_CHUNK, BIG_CHUNK), _chunk_rows(hbm, g0 + c * BIG_CHUNK, BIG_CHUNK))
            return carry

        def small(c, carry):
            copy(_chunk_rows(buf, l0 + c * ROW_CHUNK), _chunk_rows(hbm, g0 + c * ROW_CHUNK))
            return carry

        lax.fori_loop(0, n_big, big, 0)
        lax.fori_loop(n_big * per_big, n, small, 0)
        return done + n

    return lax.fori_loop(0, N_EXPERTS, per_expert, 0)


def _wait_chunks(n, hbm, sem):
    rows = pl.multiple_of(n * ROW_CHUNK, ROW_CHUNK)

    @pl.when(n > 0)
    def _():
        pltpu.make_async_copy(hbm.at[pl.ds(0, rows), :], hbm.at[pl.ds(0, rows), :], sem).wait()


def _dispatch_kernel(nch_ref, gbase_ref, tstart_ref, tn_ref, nu_ref, x_ref, lposT_ref, xs_hbm,
                     buf_ref, zero_ref, sem, zsem, cnt_ref, *, bm, n_blocks):
    i = pl.program_id(0)
    nt = pl.num_programs(0)
    slot = i % 2

    @pl.when(i >= 2)
    def _():
        _wait_chunks(cnt_ref[slot], xs_hbm, sem.at[slot])

    xb = x_ref[...].astype(BF16)
    lposT = lposT_ref[...]
    rows = DISP_PCHUNK
    for rb in range(DISP_LCAP // rows):
        r = lax.broadcasted_iota(jnp.int32, (rows, xb.shape[0]), 0) + rb * rows
        hit = r == lposT[0:1]
        for k in range(1, TOP_K):
            hit = jnp.logical_or(hit, r == lposT[k:k + 1])
        perm = jnp.where(hit, 1.0, 0.0).astype(BF16)
        buf_ref[slot, rb * rows:(rb + 1) * rows, :] = _dot(perm, xb)
    cnt_ref[slot] = _group_copies(nch_ref, gbase_ref, i, buf_ref.at[slot], xs_hbm, sem.at[slot], True)

    @pl.when(i == nt - 1)
    def _():
        zero_ref[...] = jnp.zeros(zero_ref.shape, F32)

        def per_expert(ex, done):
            first = tstart_ref[ex]

            def per_chunk(c, carry):
                pltpu.make_async_copy(_chunk_rows(zero_ref, 0), _chunk_rows(xs_hbm, first + c * ROW_CHUNK),
                                      zsem.at[0]).start()
                return carry

            lax.fori_loop(0, tn_ref[ex], per_chunk, 0)
            return done + tn_ref[ex]

        n_tail = lax.fori_loop(0, N_EXPERTS, per_expert, 0)

        def per_block(b, carry):
            pltpu.make_async_copy(zero_ref, xs_hbm.at[pl.ds(pl.multiple_of(b * bm, bm), bm), :], zsem.at[1]).start()
            return carry

        lax.fori_loop(nu_ref[0], n_blocks, per_block, 0)
        _wait_chunks(n_tail, xs_hbm, zsem.at[0])

        def wait_block(b, carry):
            pltpu.make_async_copy(zero_ref, xs_hbm.at[pl.ds(0, bm), :], zsem.at[1]).wait()
            return carry

        lax.fori_loop(nu_ref[0], n_blocks, wait_block, 0)
        _wait_chunks(cnt_ref[slot], xs_hbm, sem.at[slot])

        @pl.when(nt >= 2)
        def _():
            _wait_chunks(cnt_ref[1 - slot], xs_hbm, sem.at[1 - slot])


def _dispatch_call(rt, x1, lposT, bm, n_blocks):
    T = x1.shape[0]
    tile = DISP_TILE
    grid_spec = pltpu.PrefetchScalarGridSpec(
        num_scalar_prefetch=5,
        grid=(T // tile,),
        in_specs=[
            pl.BlockSpec((tile, D_MODEL), lambda i, *_: (i, 0)),
            pl.BlockSpec((None, TOP_K, tile), lambda i, *_: (i, 0, 0)),
        ],
        out_specs=pl.BlockSpec(memory_space=pl.ANY),
        scratch_shapes=[
            pltpu.VMEM((2, DISP_LCAP, D_MODEL), F32),
            pltpu.VMEM((bm, D_MODEL), F32),
            pltpu.SemaphoreType.DMA((2,)),
            pltpu.SemaphoreType.DMA((2,)),
            pltpu.SMEM((2,), jnp.int32),
        ],
    )
    return pl.pallas_call(
        functools.partial(_dispatch_kernel, bm=bm, n_blocks=n_blocks),
        grid_spec=grid_spec,
        out_shape=jax.ShapeDtypeStruct((n_blocks * bm, D_MODEL), F32),
        compiler_params=pltpu.CompilerParams(dimension_semantics=("arbitrary",), vmem_limit_bytes=VMEM_LIMIT),
        name="dispatch",
    )(rt["nch"], rt["gbase"], rt["tail_start"], rt["tail_n"], rt["n_used"], x1, lposT)


def _experts_kernel(fb_ref, xs_hbm, wgu_ref, bgu_ref, wd_ref, bd_ref, y_hbm,
                    xs_ref, ys_ref, wgu_b, wd_b, xsem, ysem, *, bm, n_blocks):
    e = pl.program_id(0)
    n_used = fb_ref[N_EXPERTS]

    def x_copy(blk, slot):
        return pltpu.make_async_copy(xs_hbm.at[pl.ds(pl.multiple_of(blk * bm, bm), bm), :], xs_ref.at[slot],
                                     xsem.at[slot])

    def y_copy(blk, slot):
        return pltpu.make_async_copy(ys_ref.at[slot], y_hbm.at[pl.ds(pl.multiple_of(blk * bm, bm), bm), :],
                                     ysem.at[slot])

    @pl.when(e == 0)
    def _():
        x_copy(0, 0).start()

    wgu_b[...] = wgu_ref[...].astype(BF16)
    wd_b[...] = wd_ref[...].astype(BF16)

    def block(blk, carry):
        slot = blk % 2
        x_copy(blk, slot).wait()

        @pl.when(blk + 1 < n_used)
        def _():
            x_copy(blk + 1, 1 - slot).start()

        h = _dot(xs_ref[slot].astype(BF16), wgu_b[...]) + bgu_ref[...]
        gate = jnp.minimum(h[:, :D_FF], SWIGLU_LIMIT)
        up = jnp.clip(h[:, D_FF:], -SWIGLU_LIMIT, SWIGLU_LIMIT)
        act = (up + 1.0) * (gate * jax.nn.sigmoid(gate * SWIGLU_ALPHA))
        y = _dot(act.astype(BF16), wd_b[...]) + bd_ref[...]

        @pl.when(blk >= 2)
        def _():
            y_copy(blk - 2, slot).wait()

        ys_ref[slot] = y
        y_copy(blk, slot).start()
        return carry

    lax.fori_loop(fb_ref[e], fb_ref[e + 1], block, 0)

    @pl.when(e == N_EXPERTS - 1)
    def _():
        y_copy(n_used - 1, (n_used - 1) % 2).wait()

        @pl.when(n_used >= 2)
        def _():
            y_copy(n_used - 2, n_used % 2).wait()

        ys_ref[0] = jnp.zeros(ys_ref.shape[1:], F32)

        def fill(blk, carry):
            y_copy(blk, 0).start()
            y_copy(blk, 0).wait()
            return carry

        lax.fori_loop(n_used, n_blocks, fill, 0)


def _experts_call(rt, xs, wgu, bgu, wd, bd, bm, n_blocks):
    E = N_EXPERTS
    by_expert = lambda e, fb: (e, 0, 0)
    grid_spec = pltpu.PrefetchScalarGridSpec(
        num_scalar_prefetch=1,
        grid=(E,),
        in_specs=[
            pl.BlockSpec(memory_space=pl.ANY),
            pl.BlockSpec((None, D_MODEL, 2 * D_FF), by_expert),
            pl.BlockSpec((None, 1, 2 * D_FF), by_expert),
            pl.BlockSpec((None, D_FF, D_MODEL), by_expert),
            pl.BlockSpec((None, 1, D_MODEL), by_expert),
        ],
        out_specs=pl.BlockSpec(memory_space=pl.ANY),
        scratch_shapes=[
            pltpu.VMEM((2, bm, D_MODEL), F32),
            pltpu.VMEM((2, bm, D_MODEL), F32),
            pltpu.VMEM((D_MODEL, 2 * D_FF), BF16),
            pltpu.VMEM((D_FF, D_MODEL), BF16),
            pltpu.SemaphoreType.DMA((2,)),
            pltpu.SemaphoreType.DMA((2,)),
        ],
    )
    return pl.pallas_call(
        functools.partial(_experts_kernel, bm=bm, n_blocks=n_blocks),
        grid_spec=grid_spec,
        out_shape=jax.ShapeDtypeStruct((n_blocks * bm, D_MODEL), F32),
        compiler_params=pltpu.CompilerParams(dimension_semantics=("arbitrary",), vmem_limit_bytes=VMEM_LIMIT),
        name="experts",
    )(rt["first_block"], xs, wgu, bgu.reshape(E, 1, 2 * D_FF), wd, bd.reshape(E, 1, D_MODEL))


def _combine_kernel(nch_ref, gbase_ref, y_hbm, x1_ref, lpos_ref, gate_ref, g_ref, b_ref, o_ref,
                    buf_ref, sem, cnt_ref):
    i = pl.program_id(0)
    nt = pl.num_programs(0)
    slot = i % 2

    @pl.when(i == 0)
    def _():
        buf_ref[...] = jnp.zeros(buf_ref.shape, F32)
        cnt_ref[0] = _group_copies(nch_ref, gbase_ref, 0, buf_ref.at[0], y_hbm, sem.at[0], False)

    @pl.when(i + 1 < nt)
    def _():
        cnt_ref[1 - slot] = _group_copies(nch_ref, gbase_ref, i + 1, buf_ref.at[1 - slot], y_hbm,
                                          sem.at[1 - slot], False)

    _wait_chunks(cnt_ref[slot], y_hbm, sem.at[slot])
    lpos = lpos_ref[...]
    gates = gate_ref[...]
    cols = DISP_PCHUNK
    y = jnp.zeros(o_ref.shape, F32)
    for cb in range(DISP_LCAP // cols):
        c = lax.broadcasted_iota(jnp.int32, (lpos.shape[0], cols), 1) + cb * cols
        w = jnp.zeros(c.shape, F32)
        for k in range(TOP_K):
            w = jnp.where(c == lpos[:, k:k + 1], gates[:, k:k + 1], w)
        y = y + _dot(w.astype(BF16), buf_ref[slot, cb * cols:(cb + 1) * cols, :].astype(BF16))
    o_ref[...] = _layer_norm(DN_ALPHA * x1_ref[...] + y, g_ref[...], b_ref[...])


def _combine_call(rt, y, x1, gates, g, b):
    T = x1.shape[0]
    tile = DISP_TILE
    grid_spec = pltpu.PrefetchScalarGridSpec(
        num_scalar_prefetch=2,
        grid=(T // tile,),
        in_specs=[
            pl.BlockSpec(memory_space=pl.ANY),
            pl.BlockSpec((tile, D_MODEL), lambda i, *_: (i, 0)),
            pl.BlockSpec((tile, TOP_K), lambda i, *_: (i, 0)),
            pl.BlockSpec((tile, LANES), lambda i, *_: (i, 0)),
            pl.BlockSpec(g.shape, lambda i, *_: (0, 0)),
            pl.BlockSpec(b.shape, lambda i, *_: (0, 0)),
        ],
        out_specs=pl.BlockSpec((tile, D_MODEL), lambda i, *_: (i, 0)),
        scratch_shapes=[
            pltpu.VMEM((2, DISP_LCAP, D_MODEL), F32),
            pltpu.SemaphoreType.DMA((2,)),
            pltpu.SMEM((2,), jnp.int32),
        ],
    )
    return pl.pallas_call(
        _combine_kernel,
        grid_spec=grid_spec,
        out_shape=jax.ShapeDtypeStruct((T, D_MODEL), F32),
        compiler_params=pltpu.CompilerParams(dimension_semantics=("arbitrary",), vmem_limit_bytes=VMEM_LIMIT),
        name="combine",
    )(rt["nch"], rt["gbase"], y, x1, rt["lpos"], gates, g, b)


def _rope_lane_order(n_sub):
    half = DIFF_HEAD_DIM // 2
    per = LANES // 2 // n_sub
    assert per == half or n_sub == 1
    cols = []
    for part in range(2):
        for sub in range(n_sub):
            cols.extend(sub * 64 + part * half + d for d in range(half))
    return np.asarray(cols)


def _prep_weights(w_in, mla_q_norm_g, w_uq, mla_kv_norm_g, w_ukv):
    o_dq, o_dk, o_dv, o_cq, o_ckv, o_kr = 0, 512, 1024, 1536, 1792, 1920
    head_order = _rope_lane_order(2)
    diff_cols = np.concatenate([h * LANES + head_order for h in range(DIFF_HEADS)])
    w = {}
    w["dqT"] = w_in[:, o_dq + diff_cols].T.astype(BF16)
    w["dk"] = w_in[:, o_dk + diff_cols].astype(BF16)
    w["dvT"] = w_in[:, o_dv:o_cq].T.astype(BF16)
    w["cqT"] = w_in[:, o_cq:o_ckv].T.astype(BF16)
    w["ckv"] = w_in[:, o_ckv:o_kr].astype(BF16)
    w["ckvT"] = w["ckv"].T

    def spread_rope(cols64):
        z = jnp.zeros((cols64.shape[0], 32), cols64.dtype)
        return jnp.concatenate([cols64[:, :32], z, cols64[:, 32:], z], axis=1)

    w["kr"] = spread_rope(w_in[:, o_kr:o_kr + MLA_ROPE_DIM]).astype(BF16)
    uq = []
    for h in range(MLA_HEADS):
        base = h * MLA_QK_DIM
        uq.append(w_uq[:, base:base + MLA_NOPE_DIM])
        uq.append(spread_rope(w_uq[:, base + MLA_NOPE_DIM:base + MLA_QK_DIM]))
    w["uqT"] = jnp.concatenate(uq, axis=1).T.astype(BF16)
    per = MLA_NOPE_DIM + MLA_V_DIM
    w["uk"] = jnp.concatenate([w_ukv[:, h * per:h * per + MLA_NOPE_DIM] for h in range(MLA_HEADS)], axis=1).astype(BF16)
    w["uvT"] = jnp.concatenate(
        [w_ukv[:, h * per + MLA_NOPE_DIM:(h + 1) * per] for h in range(MLA_HEADS)], axis=1).T.astype(BF16)
    w["gq"] = mla_q_norm_g.reshape(MLA_Q_RANK, 1)
    w["gkv"] = mla_kv_norm_g.reshape(1, MLA_KV_RANK)
    w["gkvc"] = mla_kv_norm_g.reshape(MLA_KV_RANK, 1)
    return w


def _rope_tables(positions):
    half = MLA_ROPE_DIM // 2
    inv_freq = 1.0 / (ROPE_THETA ** (jnp.arange(0, MLA_ROPE_DIM, 2, dtype=F32) / MLA_ROPE_DIM))
    ang = positions.astype(F32)[..., None] * inv_freq
    ang = jnp.tile(ang, (1, 1, LANES // half))
    sign = jnp.where(jnp.arange(LANES) < LANES // 2, -1.0, 1.0).astype(F32)
    cos_t, sin_t = jnp.cos(ang), jnp.sin(ang) * sign
    return cos_t, sin_t, cos_t.transpose(0, 2, 1), sin_t.transpose(0, 2, 1)


def kernel(x, positions, w_in, lambda_q1, lambda_k1, lambda_q2, lambda_k2, subln_g, mla_q_norm_g, w_uq,
           mla_kv_norm_g, w_ukv, w_o, ln1_g, ln1_b, w_router, b_router, w_gate_up, b_gate_up, w_down, b_down,
           ln2_g, ln2_b):
    B, S, D = x.shape
    T = B * S
    l = 0
    x2 = x.reshape(T, D)
    w = _prep_weights(w_in[l], mla_q_norm_g[l], w_uq[l], mla_kv_norm_g[l], w_ukv[l])

    dqT, dk, dvT, mqT, mk, mvT = _proj_call(x, _rope_tables(positions), w)
    lam_vecs = [v[l].reshape(1, DIFF_HEAD_DIM) for v in (lambda_q1, lambda_k1, lambda_q2, lambda_k2)]
    o_a = _attn_call(_diff_attn_kernel, lam_vecs + [subln_g[l].reshape(DIFF_V_DIM, 1)],
                     dqT, dk, dvT, DIFF_HEADS, 1, LANES, "diff_attn")
    o_b = _attn_call(_mla_attn_kernel, [], mqT, mk, mvT, MLA_HEADS // 2, 2, MLA_QK_PAD, "mla_attn")

    wr = jnp.pad(w_router[l], ((0, 0), (0, LANES - N_EXPERTS)))
    br = jnp.pad(b_router[l], (0, LANES - N_EXPERTS), constant_values=NEG_BIG).reshape(1, LANES)
    x1, idx, gates = _post_call(
        o_a.reshape(T, -1), o_b.reshape(T, -1), x2, w_o[l].astype(BF16),
        ln1_g[l].reshape(1, D), ln1_b[l].reshape(1, D), wr, br)

    bm = MOE_BM
    A = T * TOP_K
    n_tiles = T // DISP_TILE
    n_blocks = pl.cdiv(A + n_tiles * N_EXPERTS * (ROW_CHUNK - 1) + N_EXPERTS * (bm - ROW_CHUNK), bm)
    rt = _route_call(idx[:, :TOP_K].reshape(A), bm)
    lposT = rt["lpos"].reshape(n_tiles, DISP_TILE, TOP_K).transpose(0, 2, 1)
    xs = _dispatch_call(rt, x1, lposT, bm, n_blocks)
    y = _experts_call(rt, xs, w_gate_up[l], b_gate_up[l], w_down[l], b_down[l], bm, n_blocks)
    out = _combine_call(rt, y, x1, gates, ln2_g[l].reshape(1, D), ln2_b[l].reshape(1, D))
    return out.reshape(B, S, D)
```

```python
import functools
import math

import numpy as np
import jax
import jax.numpy as jnp
from jax import lax
from jax.experimental import pallas as pl
from jax.experimental.pallas import tpu as pltpu

D_MODEL = 1024
DIFF_HEADS = 4
DIFF_HEAD_DIM = 64
DIFF_V_DIM = 128
MLA_HEADS = 4
MLA_V_DIM = 128
MLA_NOPE_DIM = 128
MLA_ROPE_DIM = 64
MLA_QK_DIM = MLA_NOPE_DIM + MLA_ROPE_DIM
MLA_Q_RANK = 256
MLA_KV_RANK = 128
ROPE_THETA = 10000.0
N_EXPERTS = 32
TOP_K = 4
D_FF = 1024
SWIGLU_LIMIT = 7.0
SWIGLU_ALPHA = 1.702
LN_EPS = 1e-5
SUBLN_EPS = 1e-5
MLA_RMS_EPS = 1e-6
DEPTH = 1
DN_ALPHA = (2.0 * DEPTH) ** 0.25
LAMBDA_INIT = 0.8 - 0.6 * math.exp(-0.3 * 0)

LANES = 128
MLA_QK_PAD = 2 * LANES
BF16_ROWS = 16
VAL_ROWS = LANES + BF16_ROWS
VMEM_LIMIT = 56 * 1024 * 1024

ATTN_TILE = 512
POST_TM = 512
MOE_BM = 256
ROW_CHUNK = 8
BIG_CHUNK = 32
DISP_TILE = 512
DISP_PCHUNK = 256
DISP_LCAP = -(-(DISP_TILE * TOP_K + N_EXPERTS * (ROW_CHUNK - 1)) // DISP_PCHUNK) * DISP_PCHUNK

NEG_BIG = -1e30
LOG2E = math.log2(math.e)
F32 = jnp.float32
BF16 = jnp.bfloat16


def _dot(a, b):
    return jnp.dot(a, b, preferred_element_type=F32)


def _dot_nt(a, b):
    return lax.dot_general(a, b, (((1,), (1,)), ((), ())), preferred_element_type=F32)


def _rope128(blk, cos, sin):
    return blk * cos + pltpu.roll(blk, 64, axis=1) * sin


def _rope128_t(blk, cos, sin):
    half = LANES // 2
    rolled = jnp.concatenate([blk[half:], blk[:half]], axis=0)
    return blk * cos + rolled * sin


def _rms_rows(t, g, eps):
    return t * lax.rsqrt(jnp.mean(t * t, axis=-1, keepdims=True) + eps) * g


def _rms_cols(t, g, eps):
    return t * lax.rsqrt(jnp.mean(t * t, axis=0, keepdims=True) + eps) * g


def _store_values(vT_ref, vT, heads):
    ones = jnp.ones((BF16_ROWS, vT.shape[1]), BF16)
    for h in range(heads):
        vT_ref[h * VAL_ROWS:h * VAL_ROWS + LANES, :] = vT[h * LANES:(h + 1) * LANES].astype(BF16)
        vT_ref[h * VAL_ROWS + LANES:(h + 1) * VAL_ROWS, :] = ones


def _proj_kernel(x_ref, cos_ref, sin_ref, cosT_ref, sinT_ref,
                 wdqT_ref, wdk_ref, wdvT_ref, wcqT_ref, wckv_ref, wckvT_ref, wkr_ref,
                 gq_ref, gkv_ref, gkvc_ref, wuqT_ref, wuk_ref, wuvT_ref,
                 dqT_ref, dk_ref, dvT_ref, mqT_ref, mk_ref, mvT_ref):
    xb = x_ref[...].astype(BF16)
    cos, sin = cos_ref[...], sin_ref[...]
    cosT, sinT = cosT_ref[...], sinT_ref[...]

    dq_scale = DIFF_HEAD_DIM ** -0.5 * LOG2E
    mq_scale = MLA_QK_DIM ** -0.5 * LOG2E

    dqT = _dot_nt(wdqT_ref[...], xb)
    for h in range(DIFF_HEADS):
        sl = slice(h * LANES, (h + 1) * LANES)
        dqT_ref[sl, :] = (_rope128_t(dqT[sl], cosT, sinT) * dq_scale).astype(BF16)
    _store_values(dvT_ref, _dot_nt(wdvT_ref[...], xb), DIFF_HEADS)
    dk = _dot(xb, wdk_ref[...])
    for h in range(DIFF_HEADS):
        sl = slice(h * LANES, (h + 1) * LANES)
        dk_ref[:, sl] = _rope128(dk[:, sl], cos, sin).astype(BF16)

    cqT = _rms_cols(_dot_nt(wcqT_ref[...], xb), gq_ref[...], MLA_RMS_EPS)
    qT = _dot(wuqT_ref[...], cqT.astype(BF16))
    for h in range(MLA_HEADS):
        nope = slice(h * MLA_QK_PAD, h * MLA_QK_PAD + LANES)
        ropes = slice(h * MLA_QK_PAD + LANES, (h + 1) * MLA_QK_PAD)
        mqT_ref[nope, :] = (qT[nope] * mq_scale).astype(BF16)
        mqT_ref[ropes, :] = (_rope128_t(qT[ropes], cosT, sinT) * mq_scale).astype(BF16)

    ckvT = _rms_cols(_dot_nt(wckvT_ref[...], xb), gkvc_ref[...], MLA_RMS_EPS)
    _store_values(mvT_ref, _dot(wuvT_ref[...], ckvT.astype(BF16)), MLA_HEADS)
    ckv = _rms_rows(_dot(xb, wckv_ref[...]), gkv_ref[...], MLA_RMS_EPS)
    k_nope = _dot(ckv.astype(BF16), wuk_ref[...])
    k_pe = _rope128(_dot(xb, wkr_ref[...]), cos, sin).astype(BF16)
    for h in range(MLA_HEADS):
        mk_ref[:, h * MLA_QK_PAD:h * MLA_QK_PAD + LANES] = k_nope[:, h * LANES:(h + 1) * LANES].astype(BF16)
        mk_ref[:, h * MLA_QK_PAD + LANES:(h + 1) * MLA_QK_PAD] = k_pe


def _proj_call(x3, tabs, w):
    B, S, D = x3.shape
    tm = ATTN_TILE
    nt = S // tm
    cos_t, sin_t, cosT, sinT = tabs
    weights = [w["dqT"], w["dk"], w["dvT"], w["cqT"], w["ckv"], w["ckvT"], w["kr"],
               w["gq"], w["gkv"], w["gkvc"], w["uqT"], w["uk"], w["uvT"]]
    tok = lambda b, i: (b, i, 0)
    feat = lambda b, i: (b, i, 0, 0)
    in_specs = [pl.BlockSpec((None, tm, D), tok),
                pl.BlockSpec((None, tm, LANES), tok), pl.BlockSpec((None, tm, LANES), tok),
                pl.BlockSpec((None, LANES, tm), lambda b, i: (b, 0, i)),
                pl.BlockSpec((None, LANES, tm), lambda b, i: (b, 0, i))]
    in_specs += [pl.BlockSpec(a.shape, lambda b, i: (0, 0)) for a in weights]
    mq_w = MLA_HEADS * MLA_QK_PAD
    dv_w, mv_w = DIFF_HEADS * VAL_ROWS, MLA_HEADS * VAL_ROWS
    out_specs = [pl.BlockSpec((None, None, 512, tm), feat), pl.BlockSpec((None, tm, 512), tok),
                 pl.BlockSpec((None, None, dv_w, tm), feat), pl.BlockSpec((None, None, mq_w, tm), feat),
                 pl.BlockSpec((None, tm, mq_w), tok), pl.BlockSpec((None, None, mv_w, tm), feat)]
    out_shape = [jax.ShapeDtypeStruct((B, nt, 512, tm), BF16), jax.ShapeDtypeStruct((B, S, 512), BF16),
                 jax.ShapeDtypeStruct((B, nt, dv_w, tm), BF16), jax.ShapeDtypeStruct((B, nt, mq_w, tm), BF16),
                 jax.ShapeDtypeStruct((B, S, mq_w), BF16), jax.ShapeDtypeStruct((B, nt, mv_w, tm), BF16)]
    return pl.pallas_call(
        _proj_kernel,
        grid=(B, nt),
        in_specs=in_specs,
        out_specs=out_specs,
        out_shape=out_shape,
        compiler_params=pltpu.CompilerParams(dimension_semantics=("arbitrary", "arbitrary"),
                                             vmem_limit_bytes=VMEM_LIMIT),
        name="proj",
    )(x3, cos_t, sin_t, cosT, sinT, *weights)


def _flash_body(qTs, keys_of, values_of, acc_ref, m_ref, s_refs, *, t):
    i = pl.program_id(2)
    m_ref[...] = jnp.full(m_ref.shape, -jnp.inf, F32)
    acc_ref[...] = jnp.zeros(acc_ref.shape, F32)

    def scores(j, u):
        s_refs[u][...] = _dot(keys_of(u, j), qTs[u])

    def finish(j, u, masked):
        s = s_refs[u][...]
        if masked:
            key = lax.broadcasted_iota(jnp.int32, s.shape, 0)
            qry = lax.broadcasted_iota(jnp.int32, s.shape, 1)
            s = jnp.where(key <= qry, s, -jnp.inf)
        m_prev = m_ref[u]
        m_new = jnp.maximum(m_prev, jnp.max(s, axis=0, keepdims=True))
        alpha = jnp.exp2(m_prev - m_new)
        p = jnp.exp2((s - m_new).astype(BF16))
        acc_ref[u] = alpha * acc_ref[u] + _dot(values_of(u, j), p)
        m_ref[u] = m_new

    def full_step(j, carry):
        scores(j, 1)
        finish(j, 0, False)
        scores(j + 1, 0)
        finish(j, 1, False)
        return carry

    def two_steps(jj, carry):
        return full_step(2 * jj + 1, full_step(2 * jj, carry))

    scores(0, 0)
    lax.fori_loop(0, lax.shift_right_logical(i, 1), two_steps, 0)

    @pl.when(i % 2 == 1)
    def _():
        full_step(i - 1, 0)

    scores(i, 1)
    finish(i, 0, True)
    finish(i, 1, True)


def _key_tile(k_ref, j, t, lanes=slice(None)):
    return k_ref[pl.ds(pl.multiple_of(j * t, t), t), lanes]


def _normalized(acc_ref, u):
    return acc_ref[u, :LANES, :] / acc_ref[u, LANES:LANES + 1, :]


def _diff_attn_kernel(lq1_ref, lk1_ref, lq2_ref, lk2_ref, g_ref, qT_ref, k_ref, vT_ref, o_ref,
                      acc_ref, m_ref, sa_ref, sb_ref, *, t):
    qT = qT_ref[...]
    row = lax.broadcasted_iota(jnp.int32, qT.shape, 0)
    first = (row % 64) < 32
    zero = jnp.zeros_like(qT)
    qTs = (jnp.where(first, qT, zero), jnp.where(first, zero, qT))
    _flash_body(qTs, lambda u, j: _key_tile(k_ref, j, t), lambda u, j: vT_ref[j],
                acc_ref, m_ref, (sa_ref, sb_ref), t=t)

    lam = (jnp.exp(jnp.sum(lq1_ref[...] * lk1_ref[...], axis=-1, keepdims=True))
           - jnp.exp(jnp.sum(lq2_ref[...] * lk2_ref[...], axis=-1, keepdims=True)) + LAMBDA_INIT)
    oT = _normalized(acc_ref, 0) - lam * _normalized(acc_ref, 1)
    oT = _rms_cols(oT, g_ref[...], SUBLN_EPS) * (1.0 - LAMBDA_INIT)
    o_ref[...] = oT.T.astype(o_ref.dtype)


def _mla_attn_kernel(qT_ref, k_ref, vT_ref, o_ref, acc_ref, m_ref, sa_ref, sb_ref, *, t):
    dk = MLA_QK_PAD
    qTs = tuple(qT_ref[u * dk:(u + 1) * dk, :] for u in range(2))
    _flash_body(qTs, lambda u, j: _key_tile(k_ref, j, t, slice(u * dk, (u + 1) * dk)),
                lambda u, j: vT_ref[j, u * VAL_ROWS:(u + 1) * VAL_ROWS, :],
                acc_ref, m_ref, (sa_ref, sb_ref), t=t)
    for u in range(2):
        o_ref[:, u * LANES:(u + 1) * LANES] = _normalized(acc_ref, u).T.astype(o_ref.dtype)


def _attn_call(kernel, extra, qT, k, vT, groups, heads_per_group, dk, name):
    B, nt, _, t = qT.shape
    S = nt * t
    hp = heads_per_group
    in_specs = [pl.BlockSpec(a.shape, lambda b, h, i: (0, 0)) for a in extra]
    in_specs += [
        pl.BlockSpec((None, None, hp * dk, t), lambda b, h, i: (b, i, h, 0)),
        pl.BlockSpec((None, S, hp * dk), lambda b, h, i: (b, 0, h)),
        pl.BlockSpec((None, nt, hp * VAL_ROWS, t), lambda b, h, i: (b, 0, h, 0)),
    ]
    return pl.pallas_call(
        functools.partial(kernel, t=t),
        grid=(B, groups, nt),
        in_specs=in_specs,
        out_specs=pl.BlockSpec((None, t, hp * LANES), lambda b, h, i: (b, i, h)),
        out_shape=jax.ShapeDtypeStruct((B, S, groups * hp * LANES), BF16),
        scratch_shapes=[
            pltpu.VMEM((2, VAL_ROWS, t), F32),
            pltpu.VMEM((2, 1, t), F32),
            pltpu.VMEM((t, t), F32),
            pltpu.VMEM((t, t), F32),
        ],
        compiler_params=pltpu.CompilerParams(
            dimension_semantics=("arbitrary", "arbitrary", "arbitrary"), vmem_limit_bytes=VMEM_LIMIT),
        name=name,
    )(*extra, qT, k, vT)


def _layer_norm(y, g, b):
    mu = jnp.mean(y, axis=-1, keepdims=True)
    d = y - mu
    var = jnp.mean(d * d, axis=-1, keepdims=True)
    return d * lax.rsqrt(var + LN_EPS) * g + b


def _split_bf16(a):
    hi = a.astype(BF16)
    return hi, (a - hi.astype(F32)).astype(BF16)


def _post_kernel(oa_ref, ob_ref, x_ref, wo_ref, g_ref, b_ref, wrh_ref, wrl_ref, br_ref,
                 x1_ref, idx_ref, gate_ref):
    half = oa_ref.shape[1]
    mixed = _dot(oa_ref[...], wo_ref[:half, :]) + _dot(ob_ref[...], wo_ref[half:, :])
    x1 = _layer_norm(DN_ALPHA * x_ref[...] + mixed, g_ref[...], b_ref[...])
    x1_ref[...] = x1

    x_hi, x_lo = _split_bf16(x1)
    logits = (_dot(x_hi, wrh_ref[...]) + _dot(x_lo, wrh_ref[...]) + _dot(x_hi, wrl_ref[...])) + br_ref[...]
    lane = lax.broadcasted_iota(jnp.int32, logits.shape, 1).astype(F32)
    work = logits
    vals, idxs = [], []
    for _ in range(TOP_K):
        m = jnp.max(work, axis=-1, keepdims=True)
        idx = jnp.min(jnp.where(work == m, lane, float(LANES)), axis=-1, keepdims=True)
        vals.append(m)
        idxs.append(idx)
        work = jnp.where(lane == idx, NEG_BIG, work)
    es = [jnp.exp(v - vals[0]) for v in vals]
    den = es[0] + es[1] + es[2] + es[3]
    idx_out = jnp.zeros(logits.shape, F32)
    gate_out = jnp.zeros(logits.shape, F32)
    for k in range(TOP_K):
        idx_out = jnp.where(lane == float(k), idxs[k], idx_out)
        gate_out = jnp.where(lane == float(k), es[k] / den, gate_out)
    idx_ref[...] = idx_out.astype(jnp.int32)
    gate_ref[...] = gate_out


def _post_call(oa, ob, x2, wo, g, b, wr, br):
    T = x2.shape[0]
    tm = POST_TM
    row = lambda i: (i, 0)
    full = lambda i: (0, 0)
    return pl.pallas_call(
        _post_kernel,
        grid=(T // tm,),
        in_specs=[
            pl.BlockSpec((tm, oa.shape[1]), row), pl.BlockSpec((tm, ob.shape[1]), row),
            pl.BlockSpec((tm, D_MODEL), row), pl.BlockSpec(wo.shape, full),
            pl.BlockSpec(g.shape, full), pl.BlockSpec(b.shape, full),
            pl.BlockSpec(wr.shape, full), pl.BlockSpec(wr.shape, full), pl.BlockSpec(br.shape, full),
        ],
        out_specs=[
            pl.BlockSpec((tm, D_MODEL), row), pl.BlockSpec((tm, LANES), row), pl.BlockSpec((tm, LANES), row),
        ],
        out_shape=[
            jax.ShapeDtypeStruct((T, D_MODEL), F32),
            jax.ShapeDtypeStruct((T, LANES), jnp.int32), jax.ShapeDtypeStruct((T, LANES), F32),
        ],
        compiler_params=pltpu.CompilerParams(dimension_semantics=("arbitrary",), vmem_limit_bytes=VMEM_LIMIT),
        name="post",
    )(oa, ob, x2, wo, g, b, *_split_bf16(wr), br)


def _lane_cumsum(x, n):
    lane = lax.broadcasted_iota(jnp.int32, x.shape, 1)
    s = 1
    while s < n:
        x = x + jnp.where(lane >= s, pltpu.roll(x, s, axis=1), 0.0)
        s *= 2
    return x


def _route_kernel(e_ref, lpos_ref, nch_ref, gbase_ref, meta_ref, *, bm, rows_per_tile):
    e = e_ref[...]
    R = e.shape[0]
    nt = R // rows_per_tile
    r_i = lax.broadcasted_iota(jnp.int32, (LANES, LANES), 0)
    c_i = lax.broadcasted_iota(jnp.int32, (LANES, LANES), 1)
    lane_incl = (r_i <= c_i).astype(BF16)
    rr = lax.broadcasted_iota(jnp.int32, (R, R), 0)
    rc = lax.broadcasted_iota(jnp.int32, (R, R), 1)
    same_tile = (rr // rows_per_tile) == (rc // rows_per_tile)
    rows_before = jnp.logical_and(rc < rr, same_tile).astype(BF16)
    tr = lax.broadcasted_iota(jnp.int32, (nt, R), 0)
    tc = lax.broadcasted_iota(jnp.int32, (nt, R), 1)
    tile_rows = (tc // rows_per_tile == tr).astype(BF16)
    er = lax.broadcasted_iota(jnp.int32, (R, nt), 0)
    ec = lax.broadcasted_iota(jnp.int32, (R, nt), 1)
    row_tile = (er // rows_per_tile == ec).astype(F32)
    lt_r = lax.broadcasted_iota(jnp.int32, (nt, nt), 0)
    lt_c = lax.broadcasted_iota(jnp.int32, (nt, nt), 1)
    tiles_before = (lt_c < lt_r).astype(BF16)
    lane_t = lax.broadcasted_iota(jnp.int32, (nt, LANES), 1)

    rank = jnp.zeros(e.shape, F32)
    cnt = jnp.zeros((nt, LANES), F32)
    for ex in range(N_EXPERTS):
        hit = e == ex
        m = jnp.where(hit, 1.0, 0.0).astype(BF16)
        incl = _dot(m, lane_incl)
        row_tot = jnp.broadcast_to(incl[:, LANES - 1:LANES], e.shape).astype(BF16)
        before = _dot(rows_before, row_tot)
        rank = jnp.where(hit, incl - 1.0 + before, rank)
        cnt = jnp.where(lane_t == ex, _dot(tile_rows, row_tot), cnt)

    chunk = float(ROW_CHUNK)
    cnt8 = jnp.floor((cnt + (chunk - 1.0)) * (1.0 / chunk)) * chunk
    lstart = _lane_cumsum(cnt8, N_EXPERTS) - cnt8
    tile_pre = _dot(tiles_before, cnt8.astype(BF16))
    tot8 = tile_pre[nt - 1:nt] + cnt8[nt - 1:nt]
    tot_bm = jnp.floor((tot8 + (bm - 1.0)) * (1.0 / bm)) * bm
    end_incl = _lane_cumsum(tot_bm, N_EXPERTS)
    ebase = end_incl - tot_bm

    lrow = jnp.dot(row_tile, lstart, preferred_element_type=F32,
                   precision=lax.Precision.HIGHEST)
    lstart_a = jnp.zeros(e.shape, F32)
    for ex in range(N_EXPERTS):
        lstart_a = jnp.where(e == ex, lrow[:, ex:ex + 1], lstart_a)
    lpos_ref[...] = (lstart_a + rank).astype(jnp.int32)
    nch_ref[...] = (cnt8 * (1.0 / chunk)).astype(jnp.int32)
    gbase_ref[...] = (ebase + tile_pre).astype(jnp.int32)

    sub = lax.broadcasted_iota(jnp.int32, meta_ref.shape, 0)
    meta = jnp.where(sub == 0, ebase + tot8, 0.0)
    meta = jnp.where(sub == 1, (tot_bm - tot8) * (1.0 / chunk), meta)
    n_used = end_incl[:, N_EXPERTS - 1:N_EXPERTS] * (1.0 / bm)
    meta = jnp.where(sub == 2, n_used, meta)
    lane_m = lax.broadcasted_iota(jnp.int32, meta_ref.shape, 1)
    meta = jnp.where(sub == 3, jnp.where(lane_m < N_EXPERTS, ebase * (1.0 / bm), n_used), meta)
    meta_ref[...] = meta.astype(jnp.int32)


def _route_call(flat_e, bm):
    A = flat_e.shape[0]
    R = A // LANES
    rows_per_tile = DISP_TILE * TOP_K // LANES
    nt = R // rows_per_tile
    full = lambda i: (0, 0)
    shapes = [(R, LANES), (nt, LANES), (nt, LANES), (8, LANES)]
    lpos, nch, gbase, meta = pl.pallas_call(
        functools.partial(_route_kernel, bm=bm, rows_per_tile=rows_per_tile),
        grid=(1,),
        in_specs=[pl.BlockSpec((R, LANES), full)],
        out_specs=[pl.BlockSpec(s, full) for s in shapes],
        out_shape=[jax.ShapeDtypeStruct(s, jnp.int32) for s in shapes],
        compiler_params=pltpu.CompilerParams(dimension_semantics=("arbitrary",), vmem_limit_bytes=VMEM_LIMIT),
        name="route",
    )(flat_e.reshape(R, LANES))
    E = N_EXPERTS
    return dict(lpos=lpos.reshape(A // TOP_K, TOP_K), nch=nch[:, :E].reshape(-1), gbase=gbase[:, :E].reshape(-1),
                tail_start=meta[0, :E], tail_n=meta[1, :E], n_used=meta[2, :1], first_block=meta[3, :E + 1])


def _chunk_rows(ref, first, rows=ROW_CHUNK):
    return ref.at[pl.ds(pl.multiple_of(first, ROW_CHUNK), rows), :]


def _group_copies(nch_ref, gbase_ref, tile, buf, hbm, sem, to_hbm):
    per_big = BIG_CHUNK // ROW_CHUNK

    def copy(local, remote):
        if to_hbm:
            pltpu.make_async_copy(local, remote, sem).start()
        else:
            pltpu.make_async_copy(remote, local, sem).start()

    def per_expert(ex, done):
        n = nch_ref[tile * N_EXPERTS + ex]
        g0 = gbase_ref[tile * N_EXPERTS + ex]
        l0 = done * ROW_CHUNK
        n_big = lax.shift_right_logical(n, per_big.bit_length() - 1)

        def big(c, carry):
            copy(_chunk_rows(buf, l0 + c * BIG_CHUNK, BIG_CHUNK), _chunk_rows(hbm, g0 + c * BIG_CHUNK, BIG_CHUNK))
            return carry

        def small(c, carry):
            copy(_chunk_rows(buf, l0 + c * ROW_CHUNK), _chunk_rows(hbm, g0 + c * ROW_CHUNK))
            return carry

        lax.fori_loop(0, n_big, big, 0)
        lax.fori_loop(n_big * per_big, n, small, 0)
        return done + n

    return lax.fori_loop(0, N_EXPERTS, per_expert, 0)


def _wait_chunks(n, hbm, sem):
    rows = pl.multiple_of(n * ROW_CHUNK, ROW_CHUNK)

    @pl.when(n > 0)
    def _():
        pltpu.make_async_copy(hbm.at[pl.ds(0, rows), :], hbm.at[pl.ds(0, rows), :], sem).wait()


def _dispatch_kernel(nch_ref, gbase_ref, tstart_ref, tn_ref, nu_ref, x_ref, lposT_ref, xs_hbm,
                     buf_ref, zero_ref, sem, zsem, cnt_ref, *, bm, n_blocks):
    i = pl.program_id(0)
    nt = pl.num_programs(0)
    slot = i % 2

    @pl.when(i >= 2)
    def _():
        _wait_chunks(cnt_ref[slot], xs_hbm, sem.at[slot])

    xb = x_ref[...].astype(BF16)
    lposT = lposT_ref[...]
    rows = DISP_PCHUNK
    for rb in range(DISP_LCAP // rows):
        r = lax.broadcasted_iota(jnp.int32, (rows, xb.shape[0]), 0) + rb * rows
        hit = r == lposT[0:1]
        for k in range(1, TOP_K):
            hit = jnp.logical_or(hit, r == lposT[k:k + 1])
        perm = jnp.where(hit, 1.0, 0.0).astype(BF16)
        buf_ref[slot, rb * rows:(rb + 1) * rows, :] = _dot(perm, xb)
    cnt_ref[slot] = _group_copies(nch_ref, gbase_ref, i, buf_ref.at[slot], xs_hbm, sem.at[slot], True)

    @pl.when(i == nt - 1)
    def _():
        zero_ref[...] = jnp.zeros(zero_ref.shape, F32)

        def per_expert(ex, done):
            first = tstart_ref[ex]

            def per_chunk(c, carry):
                pltpu.make_async_copy(_chunk_rows(zero_ref, 0), _chunk_rows(xs_hbm, first + c * ROW_CHUNK),
                                      zsem.at[0]).start()
                return carry

            lax.fori_loop(0, tn_ref[ex], per_chunk, 0)
            return done + tn_ref[ex]

        n_tail = lax.fori_loop(0, N_EXPERTS, per_expert, 0)

        def per_block(b, carry):
            pltpu.make_async_copy(zero_ref, xs_hbm.at[pl.ds(pl.multiple_of(b * bm, bm), bm), :], zsem.at[1]).start()
            return carry

        lax.fori_loop(nu_ref[0], n_blocks, per_block, 0)
        _wait_chunks(n_tail, xs_hbm, zsem.at[0])

        def wait_block(b, carry):
            pltpu.make_async_copy(zero_ref, xs_hbm.at[pl.ds(0, bm), :], zsem.at[1]).wait()
            return carry

        lax.fori_loop(nu_ref[0], n_blocks, wait_block, 0)
        _wait_chunks(cnt_ref[slot], xs_hbm, sem.at[slot])

        @pl.when(nt >= 2)
        def _():
            _wait_chunks(cnt_ref[1 - slot], xs_hbm, sem.at[1 - slot])


def _dispatch_call(rt, x1, lposT, bm, n_blocks):
    T = x1.shape[0]
    tile = DISP_TILE
    grid_spec = pltpu.PrefetchScalarGridSpec(
        num_scalar_prefetch=5,
        grid=(T // tile,),
        in_specs=[
            pl.BlockSpec((tile, D_MODEL), lambda i, *_: (i, 0)),
            pl.BlockSpec((None, TOP_K, tile), lambda i, *_: (i, 0, 0)),
        ],
        out_specs=pl.BlockSpec(memory_space=pl.ANY),
        scratch_shapes=[
            pltpu.VMEM((2, DISP_LCAP, D_MODEL), F32),
            pltpu.VMEM((bm, D_MODEL), F32),
            pltpu.SemaphoreType.DMA((2,)),
            pltpu.SemaphoreType.DMA((2,)),
            pltpu.SMEM((2,), jnp.int32),
        ],
    )
    return pl.pallas_call(
        functools.partial(_dispatch_kernel, bm=bm, n_blocks=n_blocks),
        grid_spec=grid_spec,
        out_shape=jax.ShapeDtypeStruct((n_blocks * bm, D_MODEL), F32),
        compiler_params=pltpu.CompilerParams(dimension_semantics=("arbitrary",), vmem_limit_bytes=VMEM_LIMIT),
        name="dispatch",
    )(rt["nch"], rt["gbase"], rt["tail_start"], rt["tail_n"], rt["n_used"], x1, lposT)


def _experts_kernel(fb_ref, xs_hbm, wgu_ref, bgu_ref, wd_ref, bd_ref, y_hbm,
                    xs_ref, ys_ref, wgu_b, wd_b, xsem, ysem, *, bm, n_blocks):
    e = pl.program_id(0)
    n_used = fb_ref[N_EXPERTS]

    def x_copy(blk, slot):
        return pltpu.make_async_copy(xs_hbm.at[pl.ds(pl.multiple_of(blk * bm, bm), bm), :], xs_ref.at[slot],
                                     xsem.at[slot])

    def y_copy(blk, slot):
        return pltpu.make_async_copy(ys_ref.at[slot], y_hbm.at[pl.ds(pl.multiple_of(blk * bm, bm), bm), :],
                                     ysem.at[slot])

    @pl.when(e == 0)
    def _():
        x_copy(0, 0).start()

    wgu_b[...] = wgu_ref[...].astype(BF16)
    wd_b[...] = wd_ref[...].astype(BF16)

    def block(blk, carry):
        slot = blk % 2
        x_copy(blk, slot).wait()

        @pl.when(blk + 1 < n_used)
        def _():
            x_copy(blk + 1, 1 - slot).start()

        h = _dot(xs_ref[slot].astype(BF16), wgu_b[...]) + bgu_ref[...]
        gate = jnp.minimum(h[:, :D_FF], SWIGLU_LIMIT)
        up = jnp.clip(h[:, D_FF:], -SWIGLU_LIMIT, SWIGLU_LIMIT)
        act = (up + 1.0) * (gate * jax.nn.sigmoid(gate * SWIGLU_ALPHA))
        y = _dot(act.astype(BF16), wd_b[...]) + bd_ref[...]

        @pl.when(blk >= 2)
        def _():
            y_copy(blk - 2, slot).wait()

        ys_ref[slot] = y
        y_copy(blk, slot).start()
        return carry

    lax.fori_loop(fb_ref[e], fb_ref[e + 1], block, 0)

    @pl.when(e == N_EXPERTS - 1)
    def _():
        y_copy(n_used - 1, (n_used - 1) % 2).wait()

        @pl.when(n_used >= 2)
        def _():
            y_copy(n_used - 2, n_used % 2).wait()

        ys_ref[0] = jnp.zeros(ys_ref.shape[1:], F32)

        def fill(blk, carry):
            y_copy(blk, 0).start()
            y_copy(blk, 0).wait()
            return carry

        lax.fori_loop(n_used, n_blocks, fill, 0)


def _experts_call(rt, xs, wgu, bgu, wd, bd, bm, n_blocks):
    E = N_EXPERTS
    by_expert = lambda e, fb: (e, 0, 0)
    grid_spec = pltpu.PrefetchScalarGridSpec(
        num_scalar_prefetch=1,
        grid=(E,),
        in_specs=[
            pl.BlockSpec(memory_space=pl.ANY),
            pl.BlockSpec((None, D_MODEL, 2 * D_FF), by_expert),
            pl.BlockSpec((None, 1, 2 * D_FF), by_expert),
            pl.BlockSpec((None, D_FF, D_MODEL), by_expert),
            pl.BlockSpec((None, 1, D_MODEL), by_expert),
        ],
        out_specs=pl.BlockSpec(memory_space=pl.ANY),
        scratch_shapes=[
            pltpu.VMEM((2, bm, D_MODEL), F32),
            pltpu.VMEM((2, bm, D_MODEL), F32),
            pltpu.VMEM((D_MODEL, 2 * D_FF), BF16),
            pltpu.VMEM((D_FF, D_MODEL), BF16),
            pltpu.SemaphoreType.DMA((2,)),
            pltpu.SemaphoreType.DMA((2,)),
        ],
    )
    return pl.pallas_call(
        functools.partial(_experts_kernel, bm=bm, n_blocks=n_blocks),
        grid_spec=grid_spec,
        out_shape=jax.ShapeDtypeStruct((n_blocks * bm, D_MODEL), F32),
        compiler_params=pltpu.CompilerParams(dimension_semantics=("arbitrary",), vmem_limit_bytes=VMEM_LIMIT),
        name="experts",
    )(rt["first_block"], xs, wgu, bgu.reshape(E, 1, 2 * D_FF), wd, bd.reshape(E, 1, D_MODEL))


def _combine_kernel(nch_ref, gbase_ref, y_hbm, x1_ref, lpos_ref, gate_ref, g_ref, b_ref, o_ref,
                    buf_ref, sem, cnt_ref):
    i = pl.program_id(0)
    nt = pl.num_programs(0)
    slot = i % 2

    @pl.when(i == 0)
    def _():
        buf_ref[...] = jnp.zeros(buf_ref.shape, F32)
        cnt_ref[0] = _group_copies(nch_ref, gbase_ref, 0, buf_ref.at[0], y_hbm, sem.at[0], False)

    @pl.when(i + 1 < nt)
    def _():
        cnt_ref[1 - slot] = _group_copies(nch_ref, gbase_ref, i + 1, buf_ref.at[1 - slot], y_hbm,
                                          sem.at[1 - slot], False)

    _wait_chunks(cnt_ref[slot], y_hbm, sem.at[slot])
    lpos = lpos_ref[...]
    gates = gate_ref[...]
    cols = DISP_PCHUNK
    y = jnp.zeros(o_ref.shape, F32)
    for cb in range(DISP_LCAP // cols):
        c = lax.broadcasted_iota(jnp.int32, (lpos.shape[0], cols), 1) + cb * cols
        w = jnp.zeros(c.shape, F32)
        for k in range(TOP_K):
            w = jnp.where(c == lpos[:, k:k + 1], gates[:, k:k + 1], w)
        y = y + _dot(w.astype(BF16), buf_ref[slot, cb * cols:(cb + 1) * cols, :].astype(BF16))
    o_ref[...] = _layer_norm(DN_ALPHA * x1_ref[...] + y, g_ref[...], b_ref[...])


def _combine_call(rt, y, x1, gates, g, b):
    T = x1.shape[0]
    tile = DISP_TILE
    grid_spec = pltpu.PrefetchScalarGridSpec(
        num_scalar_prefetch=2,
        grid=(T // tile,),
        in_specs=[
            pl.BlockSpec(memory_space=pl.ANY),
            pl.BlockSpec((tile, D_MODEL), lambda i, *_: (i, 0)),
            pl.BlockSpec((tile, TOP_K), lambda i, *_: (i, 0)),
            pl.BlockSpec((tile, LANES), lambda i, *_: (i, 0)),
            pl.BlockSpec(g.shape, lambda i, *_: (0, 0)),
            pl.BlockSpec(b.shape, lambda i, *_: (0, 0)),
        ],
        out_specs=pl.BlockSpec((tile, D_MODEL), lambda i, *_: (i, 0)),
        scratch_shapes=[
            pltpu.VMEM((2, DISP_LCAP, D_MODEL), F32),
            pltpu.SemaphoreType.DMA((2,)),
            pltpu.SMEM((2,), jnp.int32),
        ],
    )
    return pl.pallas_call(
        _combine_kernel,
        grid_spec=grid_spec,
        out_shape=jax.ShapeDtypeStruct((T, D_MODEL), F32),
        compiler_params=pltpu.CompilerParams(dimension_semantics=("arbitrary",), vmem_limit_bytes=VMEM_LIMIT),
        name="combine",
    )(rt["nch"], rt["gbase"], y, x1, rt["lpos"], gates, g, b)


def _rope_lane_order(n_sub):
    half = DIFF_HEAD_DIM // 2
    per = LANES // 2 // n_sub
    assert per == half or n_sub == 1
    cols = []
    for part in range(2):
        for sub in range(n_sub):
            cols.extend(sub * 64 + part * half + d for d in range(half))
    return np.asarray(cols)


def _prep_weights(w_in, mla_q_norm_g, w_uq, mla_kv_norm_g, w_ukv):
    o_dq, o_dk, o_dv, o_cq, o_ckv, o_kr = 0, 512, 1024, 1536, 1792, 1920
    head_order = _rope_lane_order(2)
    diff_cols = np.concatenate([h * LANES + head_order for h in range(DIFF_HEADS)])
    w = {}
    w["dqT"] = w_in[:, o_dq + diff_cols].T.astype(BF16)
    w["dk"] = w_in[:, o_dk + diff_cols].astype(BF16)
    w["dvT"] = w_in[:, o_dv:o_cq].T.astype(BF16)
    w["cqT"] = w_in[:, o_cq:o_ckv].T.astype(BF16)
    w["ckv"] = w_in[:, o_ckv:o_kr].astype(BF16)
    w["ckvT"] = w["ckv"].T

    def spread_rope(cols64):
        z = jnp.zeros((cols64.shape[0], 32), cols64.dtype)
        return jnp.concatenate([cols64[:, :32], z, cols64[:, 32:], z], axis=1)

    w["kr"] = spread_rope(w_in[:, o_kr:o_kr + MLA_ROPE_DIM]).astype(BF16)
    uq = []
    for h in range(MLA_HEADS):
        base = h * MLA_QK_DIM
        uq.append(w_uq[:, base:base + MLA_NOPE_DIM])
        uq.append(spread_rope(w_uq[:, base + MLA_NOPE_DIM:base + MLA_QK_DIM]))
    w["uqT"] = jnp.concatenate(uq, axis=1).T.astype(BF16)
    per = MLA_NOPE_DIM + MLA_V_DIM
    w["uk"] = jnp.concatenate([w_ukv[:, h * per:h * per + MLA_NOPE_DIM] for h in range(MLA_HEADS)], axis=1).astype(BF16)
    w["uvT"] = jnp.concatenate(
        [w_ukv[:, h * per + MLA_NOPE_DIM:(h + 1) * per] for h in range(MLA_HEADS)], axis=1).T.astype(BF16)
    w["gq"] = mla_q_norm_g.reshape(MLA_Q_RANK, 1)
    w["gkv"] = mla_kv_norm_g.reshape(1, MLA_KV_RANK)
    w["gkvc"] = mla_kv_norm_g.reshape(MLA_KV_RANK, 1)
    return w


def _rope_tables(positions):
    half = MLA_ROPE_DIM // 2
    inv_freq = 1.0 / (ROPE_THETA ** (jnp.arange(0, MLA_ROPE_DIM, 2, dtype=F32) / MLA_ROPE_DIM))
    ang = positions.astype(F32)[..., None] * inv_freq
    ang = jnp.tile(ang, (1, 1, LANES // half))
    sign = jnp.where(jnp.arange(LANES) < LANES // 2, -1.0, 1.0).astype(F32)
    cos_t, sin_t = jnp.cos(ang), jnp.sin(ang) * sign
    return cos_t, sin_t, cos_t.transpose(0, 2, 1), sin_t.transpose(0, 2, 1)


def kernel(x, positions, w_in, lambda_q1, lambda_k1, lambda_q2, lambda_k2, subln_g, mla_q_norm_g, w_uq,
           mla_kv_norm_g, w_ukv, w_o, ln1_g, ln1_b, w_router, b_router, w_gate_up, b_gate_up, w_down, b_down,
           ln2_g, ln2_b):
    B, S, D = x.shape
    T = B * S
    l = 0
    x2 = x.reshape(T, D)
    w = _prep_weights(w_in[l], mla_q_norm_g[l], w_uq[l], mla_kv_norm_g[l], w_ukv[l])

    dqT, dk, dvT, mqT, mk, mvT = _proj_call(x, _rope_tables(positions), w)
    lam_vecs = [v[l].reshape(1, DIFF_HEAD_DIM) for v in (lambda_q1, lambda_k1, lambda_q2, lambda_k2)]
    o_a = _attn_call(_diff_attn_kernel, lam_vecs + [subln_g[l].reshape(DIFF_V_DIM, 1)],
                     dqT, dk, dvT, DIFF_HEADS, 1, LANES, "diff_attn")
    o_b = _attn_call(_mla_attn_kernel, [], mqT, mk, mvT, MLA_HEADS // 2, 2, MLA_QK_PAD, "mla_attn")

    wr = jnp.pad(w_router[l], ((0, 0), (0, LANES - N_EXPERTS)))
    br = jnp.pad(b_router[l], (0, LANES - N_EXPERTS), constant_values=NEG_BIG).reshape(1, LANES)
    x1, idx, gates = _post_call(
        o_a.reshape(T, -1), o_b.reshape(T, -1), x2, w_o[l].astype(BF16),
        ln1_g[l].reshape(1, D), ln1_b[l].reshape(1, D), wr, br)

    bm = MOE_BM
    A = T * TOP_K
    n_tiles = T // DISP_TILE
    n_blocks = pl.cdiv(A + n_tiles * N_EXPERTS * (ROW_CHUNK - 1) + N_EXPERTS * (bm - ROW_CHUNK), bm)
    rt = _route_call(idx[:, :TOP_K].reshape(A), bm)
    lposT = rt["lpos"].reshape(n_tiles, DISP_TILE, TOP_K).transpose(0, 2, 1)
    xs = _dispatch_call(rt, x1, lposT, bm, n_blocks)
    y = _experts_call(rt, xs, w_gate_up[l], b_gate_up[l], w_down[l], b_down[l], bm, n_blocks)
    out = _combine_call(rt, y, x1, gates, ln2_g[l].reshape(1, D), ln2_b[l].reshape(1, D))
    return out.reshape(B, S, D)
```

```python
import functools
import math

import numpy as np
import jax
import jax.numpy as jnp
from jax import lax
from jax.experimental import pallas as pl
from jax.experimental.pallas import tpu as pltpu

D_MODEL = 1024
DIFF_HEADS = 4
DIFF_HEAD_DIM = 64
DIFF_V_DIM = 128
MLA_HEADS = 4
MLA_V_DIM = 128
MLA_NOPE_DIM = 128
MLA_ROPE_DIM = 64
MLA_QK_DIM = MLA_NOPE_DIM + MLA_ROPE_DIM
MLA_Q_RANK = 256
MLA_KV_RANK = 128
DIFF_Q_COLS = DIFF_K_COLS = DIFF_V_COLS = 512
ROPE_THETA = 10000.0
N_EXPERTS = 32
TOP_K = 4
D_FF = 1024
SWIGLU_LIMIT = 7.0
SWIGLU_ALPHA = 1.702
LN_EPS = 1e-5
SUBLN_EPS = 1e-5
MLA_RMS_EPS = 1e-6
DEPTH = 1
DN_ALPHA = (2.0 * DEPTH) ** 0.25
LAMBDA_INIT = 0.8 - 0.6 * math.exp(-0.3 * 0)

LANES = 128
MLA_QK_PAD = 2 * LANES
BF16_ROWS = 16
VAL_ROWS = LANES + BF16_ROWS
VMEM_LIMIT = 56 * 1024 * 1024

ATTN_TILE = 512
POST_TM = 512
MOE_BM = 256
ROW_CHUNK = 8
BIG_CHUNK = 32
DISP_TILE = 512
DISP_PCHUNK = 256
DISP_LCAP = -(-(DISP_TILE * TOP_K + N_EXPERTS * (ROW_CHUNK - 1)) // DISP_PCHUNK) * DISP_PCHUNK

NEG_BIG = -1e30
LOG2E = math.log2(math.e)
F32 = jnp.float32
BF16 = jnp.bfloat16


def _dot(a, b):
    return jnp.dot(a, b, preferred_element_type=F32)


def _dot_nt(a, b):
    return lax.dot_general(a, b, (((1,), (1,)), ((), ())), preferred_element_type=F32)


def _rope128(blk, cos, sin):
    return blk * cos + pltpu.roll(blk, 64, axis=1) * sin


def _rope128_t(blk, cos, sin):
    half = LANES // 2
    rolled = jnp.concatenate([blk[half:], blk[:half]], axis=0)
    return blk * cos + rolled * sin


def _rms_rows(t, g, eps):
    return t * lax.rsqrt(jnp.mean(t * t, axis=-1, keepdims=True) + eps) * g


def _rms_cols(t, g, eps):
    return t * lax.rsqrt(jnp.mean(t * t, axis=0, keepdims=True) + eps) * g


def _store_values(vT_ref, vT, heads):
    ones = jnp.ones((BF16_ROWS, vT.shape[1]), BF16)
    for h in range(heads):
        vT_ref[h * VAL_ROWS:h * VAL_ROWS + LANES, :] = vT[h * LANES:(h + 1) * LANES].astype(BF16)
        vT_ref[h * VAL_ROWS + LANES:(h + 1) * VAL_ROWS, :] = ones


def _proj_kernel(x_ref, cos_ref, sin_ref, wfeat_ref, wtok_ref, gq_ref, gkv_ref, gkvc_ref,
                 wuqT_ref, wuk_ref, wuvT_ref, dqT_ref, dk_ref, dvT_ref, mqT_ref, mk_ref, mvT_ref):
    xb = x_ref[...].astype(BF16)
    cos, sin = cos_ref[...], sin_ref[...]
    cosT, sinT = cos.T, sin.T

    dq_scale = DIFF_HEAD_DIM ** -0.5 * LOG2E
    mq_scale = MLA_QK_DIM ** -0.5 * LOG2E
    o_dv, o_cq, o_ckv = DIFF_Q_COLS, DIFF_Q_COLS + DIFF_V_COLS, DIFF_Q_COLS + DIFF_V_COLS + MLA_Q_RANK

    feat = _dot_nt(wfeat_ref[...], xb)
    for h in range(DIFF_HEADS):
        sl = slice(h * LANES, (h + 1) * LANES)
        dqT_ref[sl, :] = (_rope128_t(feat[sl], cosT, sinT) * dq_scale).astype(BF16)
    _store_values(dvT_ref, feat[o_dv:o_cq], DIFF_HEADS)

    cqT = _rms_cols(feat[o_cq:o_ckv], gq_ref[...], MLA_RMS_EPS)
    qT = _dot(wuqT_ref[...], cqT.astype(BF16))
    for h in range(MLA_HEADS):
        nope = slice(h * MLA_QK_PAD, h * MLA_QK_PAD + LANES)
        ropes = slice(h * MLA_QK_PAD + LANES, (h + 1) * MLA_QK_PAD)
        mqT_ref[nope, :] = (qT[nope] * mq_scale).astype(BF16)
        mqT_ref[ropes, :] = (_rope128_t(qT[ropes], cosT, sinT) * mq_scale).astype(BF16)

    ckvT = _rms_cols(feat[o_ckv:], gkvc_ref[...], MLA_RMS_EPS)
    _store_values(mvT_ref, _dot(wuvT_ref[...], ckvT.astype(BF16)), MLA_HEADS)

    tokm = _dot(xb, wtok_ref[...])
    for h in range(DIFF_HEADS):
        sl = slice(h * LANES, (h + 1) * LANES)
        dk_ref[:, sl] = _rope128(tokm[:, sl], cos, sin).astype(BF16)
    ckv = _rms_rows(tokm[:, DIFF_K_COLS:DIFF_K_COLS + MLA_KV_RANK], gkv_ref[...], MLA_RMS_EPS)
    k_nope = _dot(ckv.astype(BF16), wuk_ref[...])
    k_pe = _rope128(tokm[:, DIFF_K_COLS + MLA_KV_RANK:], cos, sin).astype(BF16)
    for h in range(MLA_HEADS):
        mk_ref[:, h * MLA_QK_PAD:h * MLA_QK_PAD + LANES] = k_nope[:, h * LANES:(h + 1) * LANES].astype(BF16)
        mk_ref[:, h * MLA_QK_PAD + LANES:(h + 1) * MLA_QK_PAD] = k_pe


def _proj_call(x3, tabs, w):
    B, S, D = x3.shape
    tm = ATTN_TILE
    nt = S // tm
    cos_t, sin_t = tabs
    weights = [w["feat"], w["tok"], w["gq"], w["gkv"], w["gkvc"], w["uqT"], w["uk"], w["uvT"]]
    tok = lambda b, i: (b, i, 0)
    feat = lambda b, i: (b, i, 0, 0)
    in_specs = [pl.BlockSpec((None, tm, D), tok),
                pl.BlockSpec((None, tm, LANES), tok), pl.BlockSpec((None, tm, LANES), tok)]
    in_specs += [pl.BlockSpec(a.shape, lambda b, i: (0, 0)) for a in weights]
    mq_w = MLA_HEADS * MLA_QK_PAD
    dv_w, mv_w = DIFF_HEADS * VAL_ROWS, MLA_HEADS * VAL_ROWS
    out_specs = [pl.BlockSpec((None, None, 512, tm), feat), pl.BlockSpec((None, tm, 512), tok),
                 pl.BlockSpec((None, None, dv_w, tm), feat), pl.BlockSpec((None, None, mq_w, tm), feat),
                 pl.BlockSpec((None, tm, mq_w), tok), pl.BlockSpec((None, None, mv_w, tm), feat)]
    out_shape = [jax.ShapeDtypeStruct((B, nt, 512, tm), BF16), jax.ShapeDtypeStruct((B, S, 512), BF16),
                 jax.ShapeDtypeStruct((B, nt, dv_w, tm), BF16), jax.ShapeDtypeStruct((B, nt, mq_w, tm), BF16),
                 jax.ShapeDtypeStruct((B, S, mq_w), BF16), jax.ShapeDtypeStruct((B, nt, mv_w, tm), BF16)]
    return pl.pallas_call(
        _proj_kernel,
        grid=(B, nt),
        in_specs=in_specs,
        out_specs=out_specs,
        out_shape=out_shape,
        compiler_params=pltpu.CompilerParams(dimension_semantics=("arbitrary", "arbitrary"),
                                             vmem_limit_bytes=VMEM_LIMIT),
        name="proj",
    )(x3, cos_t, sin_t, *weights)


def _flash_body(qTs, keys_of, values_of, acc_ref, m_ref, s_refs, *, t):
    i = pl.program_id(2)
    m_ref[...] = jnp.full(m_ref.shape, -jnp.inf, F32)
    acc_ref[...] = jnp.zeros(acc_ref.shape, F32)

    def scores(j, u):
        s_refs[u][...] = _dot(keys_of(u, j), qTs[u])

    def finish(j, u, masked):
        s = s_refs[u][...]
        if masked:
            key = lax.broadcasted_iota(jnp.int32, s.shape, 0)
            qry = lax.broadcasted_iota(jnp.int32, s.shape, 1)
            s = jnp.where(key <= qry, s, -jnp.inf)
        m_prev = m_ref[u]
        m_new = jnp.maximum(m_prev, jnp.max(s, axis=0, keepdims=True))
        alpha = jnp.exp2(m_prev - m_new)
        p = jnp.exp2((s - m_new).astype(BF16))
        acc_ref[u] = alpha * acc_ref[u] + _dot(values_of(u, j), p)
        m_ref[u] = m_new

    def full_step(j, carry):
        scores(j, 1)
        finish(j, 0, False)
        scores(j + 1, 0)
        finish(j, 1, False)
        return carry

    def two_steps(jj, carry):
        return full_step(2 * jj + 1, full_step(2 * jj, carry))

    scores(0, 0)
    lax.fori_loop(0, lax.shift_right_logical(i, 1), two_steps, 0)

    @pl.when(i % 2 == 1)
    def _():
        full_step(i - 1, 0)

    scores(i, 1)
    finish(i, 0, True)
    finish(i, 1, True)


def _key_tile(k_ref, j, t, lanes=slice(None)):
    return k_ref[pl.ds(pl.multiple_of(j * t, t), t), lanes]


def _normalized(acc_ref, u):
    return acc_ref[u, :LANES, :] / acc_ref[u, LANES:LANES + 1, :]


def _diff_attn_kernel(lq1_ref, lk1_ref, lq2_ref, lk2_ref, g_ref, qT_ref, k_ref, vT_ref, o_ref,
                      acc_ref, m_ref, sa_ref, sb_ref, *, t):
    qT = qT_ref[...]
    row = lax.broadcasted_iota(jnp.int32, qT.shape, 0)
    first = (row % 64) < 32
    zero = jnp.zeros_like(qT)
    qTs = (jnp.where(first, qT, zero), jnp.where(first, zero, qT))
    _flash_body(qTs, lambda u, j: _key_tile(k_ref, j, t), lambda u, j: vT_ref[j],
                acc_ref, m_ref, (sa_ref, sb_ref), t=t)

    lam = (jnp.exp(jnp.sum(lq1_ref[...] * lk1_ref[...], axis=-1, keepdims=True))
           - jnp.exp(jnp.sum(lq2_ref[...] * lk2_ref[...], axis=-1, keepdims=True)) + LAMBDA_INIT)
    oT = _normalized(acc_ref, 0) - lam * _normalized(acc_ref, 1)
    oT = _rms_cols(oT, g_ref[...], SUBLN_EPS) * (1.0 - LAMBDA_INIT)
    o_ref[...] = oT.T.astype(o_ref.dtype)


def _mla_attn_kernel(qT_ref, k_ref, vT_ref, o_ref, acc_ref, m_ref, sa_ref, sb_ref, *, t):
    dk = MLA_QK_PAD
    qTs = tuple(qT_ref[u * dk:(u + 1) * dk, :] for u in range(2))
    _flash_body(qTs, lambda u, j: _key_tile(k_ref, j, t, slice(u * dk, (u + 1) * dk)),
                lambda u, j: vT_ref[j, u * VAL_ROWS:(u + 1) * VAL_ROWS, :],
                acc_ref, m_ref, (sa_ref, sb_ref), t=t)
    for u in range(2):
        o_ref[:, u * LANES:(u + 1) * LANES] = _normalized(acc_ref, u).T.astype(o_ref.dtype)


def _attn_call(kernel, extra, qT, k, vT, groups, heads_per_group, dk, name):
    B, nt, _, t = qT.shape
    S = nt * t
    hp = heads_per_group
    in_specs = [pl.BlockSpec(a.shape, lambda b, h, i: (0, 0)) for a in extra]
    in_specs += [
        pl.BlockSpec((None, None, hp * dk, t), lambda b, h, i: (b, i, h, 0)),
        pl.BlockSpec((None, S, hp * dk), lambda b, h, i: (b, 0, h)),
        pl.BlockSpec((None, nt, hp * VAL_ROWS, t), lambda b, h, i: (b, 0, h, 0)),
    ]
    return pl.pallas_call(
        functools.partial(kernel, t=t),
        grid=(B, groups, nt),
        in_specs=in_specs,
        out_specs=pl.BlockSpec((None, t, hp * LANES), lambda b, h, i: (b, i, h)),
        out_shape=jax.ShapeDtypeStruct((B, S, groups * hp * LANES), BF16),
        scratch_shapes=[
            pltpu.VMEM((2, VAL_ROWS, t), F32),
            pltpu.VMEM((2, 1, t), F32),
            pltpu.VMEM((t, t), F32),
            pltpu.VMEM((t, t), F32),
        ],
        compiler_params=pltpu.CompilerParams(
            dimension_semantics=("arbitrary", "arbitrary", "arbitrary"), vmem_limit_bytes=VMEM_LIMIT),
        name=name,
    )(*extra, qT, k, vT)


def _layer_norm(y, g, b):
    mu = jnp.mean(y, axis=-1, keepdims=True)
    d = y - mu
    var = jnp.mean(d * d, axis=-1, keepdims=True)
    return d * lax.rsqrt(var + LN_EPS) * g + b


def _split_bf16(a):
    hi = a.astype(BF16)
    return hi, (a - hi.astype(F32)).astype(BF16)


def _post_kernel(oa_ref, ob_ref, x_ref, wo_ref, g_ref, b_ref, wrh_ref, wrl_ref, br_ref,
                 x1_ref, idx_ref, gate_ref):
    half = oa_ref.shape[1]
    mixed = _dot(oa_ref[...], wo_ref[:half, :]) + _dot(ob_ref[...], wo_ref[half:, :])
    x1 = _layer_norm(DN_ALPHA * x_ref[...] + mixed, g_ref[...], b_ref[...])
    x1_ref[...] = x1

    x_hi, x_lo = _split_bf16(x1)
    logits = (_dot(x_hi, wrh_ref[...]) + _dot(x_lo, wrh_ref[...]) + _dot(x_hi, wrl_ref[...])) + br_ref[...]
    lane = lax.broadcasted_iota(jnp.int32, logits.shape, 1).astype(F32)
    work = logits
    vals, idxs = [], []
    for _ in range(TOP_K):
        m = jnp.max(work, axis=-1, keepdims=True)
        idx = jnp.min(jnp.where(work == m, lane, float(LANES)), axis=-1, keepdims=True)
        vals.append(m)
        idxs.append(idx)
        work = jnp.where(lane == idx, NEG_BIG, work)
    es = [jnp.exp(v - vals[0]) for v in vals]
    den = es[0] + es[1] + es[2] + es[3]
    idx_out = jnp.zeros(logits.shape, F32)
    gate_out = jnp.zeros(logits.shape, F32)
    for k in range(TOP_K):
        idx_out = jnp.where(lane == float(k), idxs[k], idx_out)
        gate_out = jnp.where(lane == float(k), es[k] / den, gate_out)
    idx_ref[...] = idx_out.astype(jnp.int32)
    gate_ref[...] = gate_out


def _post_call(oa, ob, x2, wo, g, b, wr, br):
    T = x2.shape[0]
    tm = POST_TM
    row = lambda i: (i, 0)
    full = lambda i: (0, 0)
    return pl.pallas_call(
        _post_kernel,
        grid=(T // tm,),
        in_specs=[
            pl.BlockSpec((tm, oa.shape[1]), row), pl.BlockSpec((tm, ob.shape[1]), row),
            pl.BlockSpec((tm, D_MODEL), row), pl.BlockSpec(wo.shape, full),
            pl.BlockSpec(g.shape, full), pl.BlockSpec(b.shape, full),
            pl.BlockSpec(wr.shape, full), pl.BlockSpec(wr.shape, full), pl.BlockSpec(br.shape, full),
        ],
        out_specs=[
            pl.BlockSpec((tm, D_MODEL), row), pl.BlockSpec((tm, LANES), row), pl.BlockSpec((tm, LANES), row),
        ],
        out_shape=[
            jax.ShapeDtypeStruct((T, D_MODEL), F32),
            jax.ShapeDtypeStruct((T, LANES), jnp.int32), jax.ShapeDtypeStruct((T, LANES), F32),
        ],
        compiler_params=pltpu.CompilerParams(dimension_semantics=("arbitrary",), vmem_limit_bytes=VMEM_LIMIT),
        name="post",
    )(oa, ob, x2, wo, g, b, *_split_bf16(wr), br)


def _lane_cumsum(x, n):
    lane = lax.broadcasted_iota(jnp.int32, x.shape, 1)
    s = 1
    while s < n:
        x = x + jnp.where(lane >= s, pltpu.roll(x, s, axis=1), 0.0)
        s *= 2
    return x


def _route_kernel(e_ref, lpos_ref, nch_ref, gbase_ref, meta_ref, *, bm, rows_per_tile):
    e = e_ref[...]
    R = e.shape[0]
    nt = R // rows_per_tile
    r_i = lax.broadcasted_iota(jnp.int32, (LANES, LANES), 0)
    c_i = lax.broadcasted_iota(jnp.int32, (LANES, LANES), 1)
    lane_incl = (r_i <= c_i).astype(BF16)
    rr = lax.broadcasted_iota(jnp.int32, (R, R), 0)
    rc = lax.broadcasted_iota(jnp.int32, (R, R), 1)
    same_tile = (rr // rows_per_tile) == (rc // rows_per_tile)
    rows_before = jnp.logical_and(rc < rr, same_tile).astype(BF16)
    tr = lax.broadcasted_iota(jnp.int32, (nt, R), 0)
    tc = lax.broadcasted_iota(jnp.int32, (nt, R), 1)
    tile_rows = (tc // rows_per_tile == tr).astype(BF16)
    er = lax.broadcasted_iota(jnp.int32, (R, nt), 0)
    ec = lax.broadcasted_iota(jnp.int32, (R, nt), 1)
    row_tile = (er // rows_per_tile == ec).astype(F32)
    lt_r = lax.broadcasted_iota(jnp.int32, (nt, nt), 0)
    lt_c = lax.broadcasted_iota(jnp.int32, (nt, nt), 1)
    tiles_before = (lt_c < lt_r).astype(BF16)
    lane_t = lax.broadcasted_iota(jnp.int32, (nt, LANES), 1)

    rank = jnp.zeros(e.shape, F32)
    cnt = jnp.zeros((nt, LANES), F32)
    for ex in range(N_EXPERTS):
        hit = e == ex
        m = jnp.where(hit, 1.0, 0.0).astype(BF16)
        incl = _dot(m, lane_incl)
        row_tot = jnp.broadcast_to(incl[:, LANES - 1:LANES], e.shape).astype(BF16)
        before = _dot(rows_before, row_tot)
        rank = jnp.where(hit, incl - 1.0 + before, rank)
        cnt = jnp.where(lane_t == ex, _dot(tile_rows, row_tot), cnt)

    chunk = float(ROW_CHUNK)
    cnt8 = jnp.floor((cnt + (chunk - 1.0)) * (1.0 / chunk)) * chunk
    lstart = _lane_cumsum(cnt8, N_EXPERTS) - cnt8
    tile_pre = _dot(tiles_before, cnt8.astype(BF16))
    tot8 = tile_pre[nt - 1:nt] + cnt8[nt - 1:nt]
    tot_bm = jnp.floor((tot8 + (bm - 1.0)) * (1.0 / bm)) * bm
    end_incl = _lane_cumsum(tot_bm, N_EXPERTS)
    ebase = end_incl - tot_bm

    lrow = jnp.dot(row_tile, lstart, preferred_element_type=F32,
                   precision=lax.Precision.HIGHEST)
    lstart_a = jnp.zeros(e.shape, F32)
    for ex in range(N_EXPERTS):
        lstart_a = jnp.where(e == ex, lrow[:, ex:ex + 1], lstart_a)
    lpos_ref[...] = (lstart_a + rank).astype(jnp.int32)
    nch_ref[...] = (cnt8 * (1.0 / chunk)).astype(jnp.int32)
    gbase_ref[...] = (ebase + tile_pre).astype(jnp.int32)

    sub = lax.broadcasted_iota(jnp.int32, meta_ref.shape, 0)
    meta = jnp.where(sub == 0, ebase + tot8, 0.0)
    meta = jnp.where(sub == 1, (tot_bm - tot8) * (1.0 / chunk), meta)
    n_used = end_incl[:, N_EXPERTS - 1:N_EXPERTS] * (1.0 / bm)
    meta = jnp.where(sub == 2, n_used, meta)
    lane_m = lax.broadcasted_iota(jnp.int32, meta_ref.shape, 1)
    meta = jnp.where(sub == 3, jnp.where(lane_m < N_EXPERTS, ebase * (1.0 / bm), n_used), meta)
    meta_ref[...] = meta.astype(jnp.int32)


def _route_call(flat_e, bm):
    A = flat_e.shape[0]
    R = A // LANES
    rows_per_tile = DISP_TILE * TOP_K // LANES
    nt = R // rows_per_tile
    full = lambda i: (0, 0)
    shapes = [(R, LANES), (nt, LANES), (nt, LANES), (8, LANES)]
    lpos, nch, gbase, meta = pl.pallas_call(
        functools.partial(_route_kernel, bm=bm, rows_per_tile=rows_per_tile),
        grid=(1,),
        in_specs=[pl.BlockSpec((R, LANES), full)],
        out_specs=[pl.BlockSpec(s, full) for s in shapes],
        out_shape=[jax.ShapeDtypeStruct(s, jnp.int32) for s in shapes],
        compiler_params=pltpu.CompilerParams(dimension_semantics=("arbitrary",), vmem_limit_bytes=VMEM_LIMIT),
        name="route",
    )(flat_e.reshape(R, LANES))
    E = N_EXPERTS
    return dict(lpos=lpos.reshape(A // TOP_K, TOP_K), nch=nch[:, :E].reshape(-1), gbase=gbase[:, :E].reshape(-1),
                tail_start=meta[0, :E], tail_n=meta[1, :E], n_used=meta[2, :1], first_block=meta[3, :E + 1])


def _chunk_rows(ref, first, rows=ROW_CHUNK):
    return ref.at[pl.ds(pl.multiple_of(first, ROW_CHUNK), rows), :]


def _group_copies(nch_ref, gbase_ref, tile, buf, hbm, sem, to_hbm):
    per_big = BIG_CHUNK // ROW_CHUNK

    def copy(local, remote):
        if to_hbm:
            pltpu.make_async_copy(local, remote, sem).start()
        else:
            pltpu.make_async_copy(remote, local, sem).start()

    def per_expert(ex, done):
        n = nch_ref[tile * N_EXPERTS + ex]
        g0 = gbase_ref[tile * N_EXPERTS + ex]
        l0 = done * ROW_CHUNK
        n_big = lax.shift_right_logical(n, per_big.bit_length() - 1)

        def big(c, carry):
            copy(_chunk_rows(buf, l0 + c * BIG_CHUNK, BIG_CHUNK), _chunk_rows(hbm, g0 + c * BIG_CHUNK, BIG_CHUNK))
            return carry

        def small(c, carry):
            copy(_chunk_rows(buf, l0 + c * ROW_CHUNK), _chunk_rows(hbm, g0 + c * ROW_CHUNK))
            return carry

        lax.fori_loop(0, n_big, big, 0)
        lax.fori_loop(n_big * per_big, n, small, 0)
        return done + n

    return lax.fori_loop(0, N_EXPERTS, per_expert, 0)


def _wait_chunks(n, hbm, sem):
    rows = pl.multiple_of(n * ROW_CHUNK, ROW_CHUNK)

    @pl.when(n > 0)
    def _():
        pltpu.make_async_copy(hbm.at[pl.ds(0, rows), :], hbm.at[pl.ds(0, rows), :], sem).wait()


def _dispatch_kernel(nch_ref, gbase_ref, tstart_ref, tn_ref, nu_ref, x_ref, lposT_ref, xs_hbm,
                     buf_ref, zero_ref, sem, zsem, cnt_ref, *, bm, n_blocks):
    i = pl.program_id(0)
    nt = pl.num_programs(0)
    slot = i % 2

    @pl.when(i >= 2)
    def _():
        _wait_chunks(cnt_ref[slot], xs_hbm, sem.at[slot])

    xb = x_ref[...].astype(BF16)
    lposT = lposT_ref[...]
    rows = DISP_PCHUNK
    for rb in range(DISP_LCAP // rows):
        r = lax.broadcasted_iota(jnp.int32, (rows, xb.shape[0]), 0) + rb * rows
        hit = r == lposT[0:1]
        for k in range(1, TOP_K):
            hit = jnp.logical_or(hit, r == lposT[k:k + 1])
        perm = jnp.where(hit, 1.0, 0.0).astype(BF16)
        buf_ref[slot, rb * rows:(rb + 1) * rows, :] = _dot(perm, xb)
    cnt_ref[slot] = _group_copies(nch_ref, gbase_ref, i, buf_ref.at[slot], xs_hbm, sem.at[slot], True)

    @pl.when(i == nt - 1)
    def _():
        zero_ref[...] = jnp.zeros(zero_ref.shape, F32)

        def per_expert(ex, done):
            first = tstart_ref[ex]

            def per_chunk(c, carry):
                pltpu.make_async_copy(_chunk_rows(zero_ref, 0), _chunk_rows(xs_hbm, first + c * ROW_CHUNK),
                                      zsem.at[0]).start()
                return carry

            lax.fori_loop(0, tn_ref[ex], per_chunk, 0)
            return done + tn_ref[ex]

        n_tail = lax.fori_loop(0, N_EXPERTS, per_expert, 0)

        def per_block(b, carry):
            pltpu.make_async_copy(zero_ref, xs_hbm.at[pl.ds(pl.multiple_of(b * bm, bm), bm), :], zsem.at[1]).start()
            return carry

        lax.fori_loop(nu_ref[0], n_blocks, per_block, 0)
        _wait_chunks(n_tail, xs_hbm, zsem.at[0])

        def wait_block(b, carry):
            pltpu.make_async_copy(zero_ref, xs_hbm.at[pl.ds(0, bm), :], zsem.at[1]).wait()
            return carry

        lax.fori_loop(nu_ref[0], n_blocks, wait_block, 0)
        _wait_chunks(cnt_ref[slot], xs_hbm, sem.at[slot])

        @pl.when(nt >= 2)
        def _():
            _wait_chunks(cnt_ref[1 - slot], xs_hbm, sem.at[1 - slot])


def _dispatch_call(rt, x1, lposT, bm, n_blocks):
    T = x1.shape[0]
    tile = DISP_TILE
    grid_spec = pltpu.PrefetchScalarGridSpec(
        num_scalar_prefetch=5,
        grid=(T // tile,),
        in_specs=[
            pl.BlockSpec((tile, D_MODEL), lambda i, *_: (i, 0)),
            pl.BlockSpec((None, TOP_K, tile), lambda i, *_: (i, 0, 0)),
        ],
        out_specs=pl.BlockSpec(memory_space=pl.ANY),
        scratch_shapes=[
            pltpu.VMEM((2, DISP_LCAP, D_MODEL), F32),
            pltpu.VMEM((bm, D_MODEL), F32),
            pltpu.SemaphoreType.DMA((2,)),
            pltpu.SemaphoreType.DMA((2,)),
            pltpu.SMEM((2,), jnp.int32),
        ],
    )
    return pl.pallas_call(
        functools.partial(_dispatch_kernel, bm=bm, n_blocks=n_blocks),
        grid_spec=grid_spec,
        out_shape=jax.ShapeDtypeStruct((n_blocks * bm, D_MODEL), F32),
        compiler_params=pltpu.CompilerParams(dimension_semantics=("arbitrary",), vmem_limit_bytes=VMEM_LIMIT),
        name="dispatch",
    )(rt["nch"], rt["gbase"], rt["tail_start"], rt["tail_n"], rt["n_used"], x1, lposT)


def _experts_kernel(fb_ref, xs_hbm, wgu_ref, bgu_ref, wd_ref, bd_ref, y_hbm,
                    xs_ref, ys_ref, wgu_b, wd_b, xsem, ysem, *, bm, n_blocks):
    e = pl.program_id(0)
    n_used = fb_ref[N_EXPERTS]

    def x_copy(blk, slot):
        return pltpu.make_async_copy(xs_hbm.at[pl.ds(pl.multiple_of(blk * bm, bm), bm), :], xs_ref.at[slot],
                                     xsem.at[slot])

    def y_copy(blk, slot):
        return pltpu.make_async_copy(ys_ref.at[slot], y_hbm.at[pl.ds(pl.multiple_of(blk * bm, bm), bm), :],
                                     ysem.at[slot])

    @pl.when(e == 0)
    def _():
        x_copy(0, 0).start()

    wgu_b[...] = wgu_ref[...].astype(BF16)
    wd_b[...] = wd_ref[...].astype(BF16)

    def block(blk, carry):
        slot = blk % 2
        x_copy(blk, slot).wait()

        @pl.when(blk + 1 < n_used)
        def _():
            x_copy(blk + 1, 1 - slot).start()

        h = _dot(xs_ref[slot].astype(BF16), wgu_b[...]) + bgu_ref[...]
        gate = jnp.minimum(h[:, :D_FF], SWIGLU_LIMIT)
        up = jnp.clip(h[:, D_FF:], -SWIGLU_LIMIT, SWIGLU_LIMIT)
        act = (up + 1.0) * (gate * jax.nn.sigmoid(gate * SWIGLU_ALPHA))
        y = _dot(act.astype(BF16), wd_b[...]) + bd_ref[...]

        @pl.when(blk >= 2)
        def _():
            y_copy(blk - 2, slot).wait()

        ys_ref[slot] = y
        y_copy(blk, slot).start()
        return carry

    lax.fori_loop(fb_ref[e], fb_ref[e + 1], block, 0)

    @pl.when(e == N_EXPERTS - 1)
    def _():
        y_copy(n_used - 1, (n_used - 1) % 2).wait()

        @pl.when(n_used >= 2)
        def _():
            y_copy(n_used - 2, n_used % 2).wait()

        ys_ref[0] = jnp.zeros(ys_ref.shape[1:], F32)

        def fill(blk, carry):
            y_copy(blk, 0).start()
            y_copy(blk, 0).wait()
            return carry

        lax.fori_loop(n_used, n_blocks, fill, 0)


def _experts_call(rt, xs, wgu, bgu, wd, bd, bm, n_blocks):
    E = N_EXPERTS
    by_expert = lambda e, fb: (e, 0, 0)
    grid_spec = pltpu.PrefetchScalarGridSpec(
        num_scalar_prefetch=1,
        grid=(E,),
        in_specs=[
            pl.BlockSpec(memory_space=pl.ANY),
            pl.BlockSpec((None, D_MODEL, 2 * D_FF), by_expert),
            pl.BlockSpec((None, 1, 2 * D_FF), by_expert),
            pl.BlockSpec((None, D_FF, D_MODEL), by_expert),
            pl.BlockSpec((None, 1, D_MODEL), by_expert),
        ],
        out_specs=pl.BlockSpec(memory_space=pl.ANY),
        scratch_shapes=[
            pltpu.VMEM((2, bm, D_MODEL), F32),
            pltpu.VMEM((2, bm, D_MODEL), F32),
            pltpu.VMEM((D_MODEL, 2 * D_FF), BF16),
            pltpu.VMEM((D_FF, D_MODEL), BF16),
            pltpu.SemaphoreType.DMA((2,)),
            pltpu.SemaphoreType.DMA((2,)),
        ],
    )
    return pl.pallas_call(
        functools.partial(_experts_kernel, bm=bm, n_blocks=n_blocks),
        grid_spec=grid_spec,
        out_shape=jax.ShapeDtypeStruct((n_blocks * bm, D_MODEL), F32),
        compiler_params=pltpu.CompilerParams(dimension_semantics=("arbitrary",), vmem_limit_bytes=VMEM_LIMIT),
        name="experts",
    )(rt["first_block"], xs, wgu, bgu.reshape(E, 1, 2 * D_FF), wd, bd.reshape(E, 1, D_MODEL))


def _combine_kernel(nch_ref, gbase_ref, y_hbm, x1_ref, lpos_ref, gate_ref, g_ref, b_ref, o_ref,
                    buf_ref, sem, cnt_ref):
    i = pl.program_id(0)
    nt = pl.num_programs(0)
    slot = i % 2

    @pl.when(i == 0)
    def _():
        buf_ref[...] = jnp.zeros(buf_ref.shape, F32)
        cnt_ref[0] = _group_copies(nch_ref, gbase_ref, 0, buf_ref.at[0], y_hbm, sem.at[0], False)

    @pl.when(i + 1 < nt)
    def _():
        cnt_ref[1 - slot] = _group_copies(nch_ref, gbase_ref, i + 1, buf_ref.at[1 - slot], y_hbm,
                                          sem.at[1 - slot], False)

    _wait_chunks(cnt_ref[slot], y_hbm, sem.at[slot])
    lpos = lpos_ref[...]
    gates = gate_ref[...]
    cols = DISP_PCHUNK
    y = jnp.zeros(o_ref.shape, F32)
    for cb in range(DISP_LCAP // cols):
        c = lax.broadcasted_iota(jnp.int32, (lpos.shape[0], cols), 1) + cb * cols
        w = jnp.zeros(c.shape, F32)
        for k in range(TOP_K):
            w = jnp.where(c == lpos[:, k:k + 1], gates[:, k:k + 1], w)
        y = y + _dot(w.astype(BF16), buf_ref[slot, cb * cols:(cb + 1) * cols, :].astype(BF16))
    o_ref[...] = _layer_norm(DN_ALPHA * x1_ref[...] + y, g_ref[...], b_ref[...])


def _combine_call(rt, y, x1, gates, g, b):
    T = x1.shape[0]
    tile = DISP_TILE
    grid_spec = pltpu.PrefetchScalarGridSpec(
        num_scalar_prefetch=2,
        grid=(T // tile,),
        in_specs=[
            pl.BlockSpec(memory_space=pl.ANY),
            pl.BlockSpec((tile, D_MODEL), lambda i, *_: (i, 0)),
            pl.BlockSpec((tile, TOP_K), lambda i, *_: (i, 0)),
            pl.BlockSpec((tile, LANES), lambda i, *_: (i, 0)),
            pl.BlockSpec(g.shape, lambda i, *_: (0, 0)),
            pl.BlockSpec(b.shape, lambda i, *_: (0, 0)),
        ],
        out_specs=pl.BlockSpec((tile, D_MODEL), lambda i, *_: (i, 0)),
        scratch_shapes=[
            pltpu.VMEM((2, DISP_LCAP, D_MODEL), F32),
            pltpu.SemaphoreType.DMA((2,)),
            pltpu.SMEM((2,), jnp.int32),
        ],
    )
    return pl.pallas_call(
        _combine_kernel,
        grid_spec=grid_spec,
        out_shape=jax.ShapeDtypeStruct((T, D_MODEL), F32),
        compiler_params=pltpu.CompilerParams(dimension_semantics=("arbitrary",), vmem_limit_bytes=VMEM_LIMIT),
        name="combine",
    )(rt["nch"], rt["gbase"], y, x1, rt["lpos"], gates, g, b)


def _rope_lane_order(n_sub):
    half = DIFF_HEAD_DIM // 2
    per = LANES // 2 // n_sub
    assert per == half or n_sub == 1
    cols = []
    for part in range(2):
        for sub in range(n_sub):
            cols.extend(sub * 64 + part * half + d for d in range(half))
    return np.asarray(cols)


def _prep_weights(w_in, mla_q_norm_g, w_uq, mla_kv_norm_g, w_ukv):
    o_dq, o_dk, o_dv, o_cq, o_ckv, o_kr = 0, 512, 1024, 1536, 1792, 1920
    head_order = _rope_lane_order(2)
    diff_cols = np.concatenate([h * LANES + head_order for h in range(DIFF_HEADS)])
    w = {}
    ckv = w_in[:, o_ckv:o_kr]
    w["feat"] = jnp.concatenate([w_in[:, o_dq + diff_cols], w_in[:, o_dv:o_cq], w_in[:, o_cq:o_ckv], ckv],
                                axis=1).T.astype(BF16)

    def spread_rope(cols64):
        z = jnp.zeros((cols64.shape[0], 32), cols64.dtype)
        return jnp.concatenate([cols64[:, :32], z, cols64[:, 32:], z], axis=1)

    w["tok"] = jnp.concatenate([w_in[:, o_dk + diff_cols], ckv, spread_rope(w_in[:, o_kr:o_kr + MLA_ROPE_DIM])],
                               axis=1).astype(BF16)
    uq = []
    for h in range(MLA_HEADS):
        base = h * MLA_QK_DIM
        uq.append(w_uq[:, base:base + MLA_NOPE_DIM])
        uq.append(spread_rope(w_uq[:, base + MLA_NOPE_DIM:base + MLA_QK_DIM]))
    w["uqT"] = jnp.concatenate(uq, axis=1).T.astype(BF16)
    per = MLA_NOPE_DIM + MLA_V_DIM
    w["uk"] = jnp.concatenate([w_ukv[:, h * per:h * per + MLA_NOPE_DIM] for h in range(MLA_HEADS)], axis=1).astype(BF16)
    w["uvT"] = jnp.concatenate(
        [w_ukv[:, h * per + MLA_NOPE_DIM:(h + 1) * per] for h in range(MLA_HEADS)], axis=1).T.astype(BF16)
    w["gq"] = mla_q_norm_g.reshape(MLA_Q_RANK, 1)
    w["gkv"] = mla_kv_norm_g.reshape(1, MLA_KV_RANK)
    w["gkvc"] = mla_kv_norm_g.reshape(MLA_KV_RANK, 1)
    return w


def _rope_tables(positions):
    half = MLA_ROPE_DIM // 2
    inv_freq = 1.0 / (ROPE_THETA ** (jnp.arange(0, MLA_ROPE_DIM, 2, dtype=F32) / MLA_ROPE_DIM))
    ang = positions.astype(F32)[..., None] * inv_freq
    ang = jnp.tile(ang, (1, 1, LANES // half))
    sign = jnp.where(jnp.arange(LANES) < LANES // 2, -1.0, 1.0).astype(F32)
    return jnp.cos(ang), jnp.sin(ang) * sign


def kernel(x, positions, w_in, lambda_q1, lambda_k1, lambda_q2, lambda_k2, subln_g, mla_q_norm_g, w_uq,
           mla_kv_norm_g, w_ukv, w_o, ln1_g, ln1_b, w_router, b_router, w_gate_up, b_gate_up, w_down, b_down,
           ln2_g, ln2_b):
    B, S, D = x.shape
    T = B * S
    l = 0
    x2 = x.reshape(T, D)
    w = _prep_weights(w_in[l], mla_q_norm_g[l], w_uq[l], mla_kv_norm_g[l], w_ukv[l])

    dqT, dk, dvT, mqT, mk, mvT = _proj_call(x, _rope_tables(positions), w)
    lam_vecs = [v[l].reshape(1, DIFF_HEAD_DIM) for v in (lambda_q1, lambda_k1, lambda_q2, lambda_k2)]
    o_a = _attn_call(_diff_attn_kernel, lam_vecs + [subln_g[l].reshape(DIFF_V_DIM, 1)],
                     dqT, dk, dvT, DIFF_HEADS, 1, LANES, "diff_attn")
    o_b = _attn_call(_mla_attn_kernel, [], mqT, mk, mvT, MLA_HEADS // 2, 2, MLA_QK_PAD, "mla_attn")

    wr = jnp.pad(w_router[l], ((0, 0), (0, LANES - N_EXPERTS)))
    br = jnp.pad(b_router[l], (0, LANES - N_EXPERTS), constant_values=NEG_BIG).reshape(1, LANES)
    x1, idx, gates = _post_call(
        o_a.reshape(T, -1), o_b.reshape(T, -1), x2, w_o[l].astype(BF16),
        ln1_g[l].reshape(1, D), ln1_b[l].reshape(1, D), wr, br)

    bm = MOE_BM
    A = T * TOP_K
    n_tiles = T // DISP_TILE
    n_blocks = pl.cdiv(A + n_tiles * N_EXPERTS * (ROW_CHUNK - 1) + N_EXPERTS * (bm - ROW_CHUNK), bm)
    rt = _route_call(idx[:, :TOP_K].reshape(A), bm)
    lposT = rt["lpos"].reshape(n_tiles, DISP_TILE, TOP_K).transpose(0, 2, 1)
    xs = _dispatch_call(rt, x1, lposT, bm, n_blocks)
    y = _experts_call(rt, xs, w_gate_up[l], b_gate_up[l], w_down[l], b_down[l], bm, n_blocks)
    out = _combine_call(rt, y, x1, gates, ln2_g[l].reshape(1, D), ln2_b[l].reshape(1, D))
    return out.reshape(B, S, D)
```

```python
import functools
import math

import numpy as np
import jax
import jax.numpy as jnp
from jax import lax
from jax.experimental import pallas as pl
from jax.experimental.pallas import tpu as pltpu

D_MODEL = 1024
DIFF_HEADS = 4
DIFF_HEAD_DIM = 64
DIFF_V_DIM = 128
MLA_HEADS = 4
MLA_V_DIM = 128
MLA_NOPE_DIM = 128
MLA_ROPE_DIM = 64
MLA_QK_DIM = MLA_NOPE_DIM + MLA_ROPE_DIM
MLA_Q_RANK = 256
MLA_KV_RANK = 128
DIFF_Q_COLS = DIFF_K_COLS = DIFF_V_COLS = 512
ROPE_THETA = 10000.0
N_EXPERTS = 32
TOP_K = 4
D_FF = 1024
SWIGLU_LIMIT = 7.0
SWIGLU_ALPHA = 1.702
LN_EPS = 1e-5
SUBLN_EPS = 1e-5
MLA_RMS_EPS = 1e-6
DEPTH = 1
DN_ALPHA = (2.0 * DEPTH) ** 0.25
LAMBDA_INIT = 0.8 - 0.6 * math.exp(-0.3 * 0)

LANES = 128
MLA_QK_PAD = 2 * LANES
BF16_ROWS = 16
VAL_ROWS = LANES + BF16_ROWS
VMEM_LIMIT = 56 * 1024 * 1024

ATTN_TILE = 512
POST_TM = 512
MOE_BM = 256
ROW_CHUNK = 8
BIG_CHUNK = 32
DISP_TILE = 256
DISP_PCHUNK = 256
DISP_LCAP = -(-(DISP_TILE * TOP_K + N_EXPERTS * (ROW_CHUNK - 1)) // DISP_PCHUNK) * DISP_PCHUNK

NEG_BIG = -1e30
LOG2E = math.log2(math.e)
F32 = jnp.float32
BF16 = jnp.bfloat16


def _dot(a, b):
    return jnp.dot(a, b, preferred_element_type=F32)


def _dot_nt(a, b):
    return lax.dot_general(a, b, (((1,), (1,)), ((), ())), preferred_element_type=F32)


def _rope128(blk, cos, sin):
    return blk * cos + pltpu.roll(blk, 64, axis=1) * sin


def _rope128_t(blk, cos, sin):
    half = LANES // 2
    rolled = jnp.concatenate([blk[half:], blk[:half]], axis=0)
    return blk * cos + rolled * sin


def _rms_rows(t, g, eps):
    return t * lax.rsqrt(jnp.mean(t * t, axis=-1, keepdims=True) + eps) * g


def _rms_cols(t, g, eps):
    return t * lax.rsqrt(jnp.mean(t * t, axis=0, keepdims=True) + eps) * g


def _store_values(vT_ref, vT, heads):
    ones = jnp.ones((BF16_ROWS, vT.shape[1]), BF16)
    for h in range(heads):
        vT_ref[h * VAL_ROWS:h * VAL_ROWS + LANES, :] = vT[h * LANES:(h + 1) * LANES].astype(BF16)
        vT_ref[h * VAL_ROWS + LANES:(h + 1) * VAL_ROWS, :] = ones


def _proj_kernel(x_ref, cos_ref, sin_ref, wfeat_ref, wtok_ref, gq_ref, gkv_ref, gkvc_ref,
                 wuqT_ref, wuk_ref, wuvT_ref, dqT_ref, dk_ref, dvT_ref, mqT_ref, mk_ref, mvT_ref):
    xb = x_ref[...].astype(BF16)
    cos, sin = cos_ref[...], sin_ref[...]
    cosT, sinT = cos.T, sin.T

    dq_scale = DIFF_HEAD_DIM ** -0.5 * LOG2E
    mq_scale = MLA_QK_DIM ** -0.5 * LOG2E
    o_dv, o_cq, o_ckv = DIFF_Q_COLS, DIFF_Q_COLS + DIFF_V_COLS, DIFF_Q_COLS + DIFF_V_COLS + MLA_Q_RANK

    feat = _dot_nt(wfeat_ref[...], xb)
    for h in range(DIFF_HEADS):
        sl = slice(h * LANES, (h + 1) * LANES)
        dqT_ref[sl, :] = (_rope128_t(feat[sl], cosT, sinT) * dq_scale).astype(BF16)
    _store_values(dvT_ref, feat[o_dv:o_cq], DIFF_HEADS)

    cqT = _rms_cols(feat[o_cq:o_ckv], gq_ref[...], MLA_RMS_EPS)
    qT = _dot(wuqT_ref[...], cqT.astype(BF16))
    for h in range(MLA_HEADS):
        nope = slice(h * MLA_QK_PAD, h * MLA_QK_PAD + LANES)
        ropes = slice(h * MLA_QK_PAD + LANES, (h + 1) * MLA_QK_PAD)
        mqT_ref[nope, :] = (qT[nope] * mq_scale).astype(BF16)
        mqT_ref[ropes, :] = (_rope128_t(qT[ropes], cosT, sinT) * mq_scale).astype(BF16)

    ckvT = _rms_cols(feat[o_ckv:], gkvc_ref[...], MLA_RMS_EPS)
    _store_values(mvT_ref, _dot(wuvT_ref[...], ckvT.astype(BF16)), MLA_HEADS)

    tokm = _dot(xb, wtok_ref[...])
    for h in range(DIFF_HEADS):
        sl = slice(h * LANES, (h + 1) * LANES)
        dk_ref[:, sl] = _rope128(tokm[:, sl], cos, sin).astype(BF16)
    ckv = _rms_rows(tokm[:, DIFF_K_COLS:DIFF_K_COLS + MLA_KV_RANK], gkv_ref[...], MLA_RMS_EPS)
    k_nope = _dot(ckv.astype(BF16), wuk_ref[...])
    k_pe = _rope128(tokm[:, DIFF_K_COLS + MLA_KV_RANK:], cos, sin).astype(BF16)
    for h in range(MLA_HEADS):
        mk_ref[:, h * MLA_QK_PAD:h * MLA_QK_PAD + LANES] = k_nope[:, h * LANES:(h + 1) * LANES].astype(BF16)
        mk_ref[:, h * MLA_QK_PAD + LANES:(h + 1) * MLA_QK_PAD] = k_pe


def _proj_call(x3, tabs, w):
    B, S, D = x3.shape
    tm = ATTN_TILE
    nt = S // tm
    cos_t, sin_t = tabs
    weights = [w["feat"], w["tok"], w["gq"], w["gkv"], w["gkvc"], w["uqT"], w["uk"], w["uvT"]]
    tok = lambda b, i: (b, i, 0)
    feat = lambda b, i: (b, i, 0, 0)
    in_specs = [pl.BlockSpec((None, tm, D), tok),
                pl.BlockSpec((None, tm, LANES), tok), pl.BlockSpec((None, tm, LANES), tok)]
    in_specs += [pl.BlockSpec(a.shape, lambda b, i: (0, 0)) for a in weights]
    mq_w = MLA_HEADS * MLA_QK_PAD
    dv_w, mv_w = DIFF_HEADS * VAL_ROWS, MLA_HEADS * VAL_ROWS
    out_specs = [pl.BlockSpec((None, None, 512, tm), feat), pl.BlockSpec((None, tm, 512), tok),
                 pl.BlockSpec((None, None, dv_w, tm), feat), pl.BlockSpec((None, None, mq_w, tm), feat),
                 pl.BlockSpec((None, tm, mq_w), tok), pl.BlockSpec((None, None, mv_w, tm), feat)]
    out_shape = [jax.ShapeDtypeStruct((B, nt, 512, tm), BF16), jax.ShapeDtypeStruct((B, S, 512), BF16),
                 jax.ShapeDtypeStruct((B, nt, dv_w, tm), BF16), jax.ShapeDtypeStruct((B, nt, mq_w, tm), BF16),
                 jax.ShapeDtypeStruct((B, S, mq_w), BF16), jax.ShapeDtypeStruct((B, nt, mv_w, tm), BF16)]
    return pl.pallas_call(
        _proj_kernel,
        grid=(B, nt),
        in_specs=in_specs,
        out_specs=out_specs,
        out_shape=out_shape,
        compiler_params=pltpu.CompilerParams(dimension_semantics=("arbitrary", "arbitrary"),
                                             vmem_limit_bytes=VMEM_LIMIT),
        name="proj",
    )(x3, cos_t, sin_t, *weights)


def _flash_body(qTs, keys_of, values_of, acc_ref, m_ref, s_refs, *, t):
    i = pl.program_id(2)
    m_ref[...] = jnp.full(m_ref.shape, -jnp.inf, F32)
    acc_ref[...] = jnp.zeros(acc_ref.shape, F32)

    def scores(j, u):
        s_refs[u][...] = _dot(keys_of(u, j), qTs[u])

    def finish(j, u, masked):
        s = s_refs[u][...]
        if masked:
            key = lax.broadcasted_iota(jnp.int32, s.shape, 0)
            qry = lax.broadcasted_iota(jnp.int32, s.shape, 1)
            s = jnp.where(key <= qry, s, -jnp.inf)
        m_prev = m_ref[u]
        m_new = jnp.maximum(m_prev, jnp.max(s, axis=0, keepdims=True))
        alpha = jnp.exp2(m_prev - m_new)
        p = jnp.exp2((s - m_new).astype(BF16))
        acc_ref[u] = alpha * acc_ref[u] + _dot(values_of(u, j), p)
        m_ref[u] = m_new

    def full_step(j, carry):
        scores(j, 1)
        finish(j, 0, False)
        scores(j + 1, 0)
        finish(j, 1, False)
        return carry

    def two_steps(jj, carry):
        return full_step(2 * jj + 1, full_step(2 * jj, carry))

    scores(0, 0)
    lax.fori_loop(0, lax.shift_right_logical(i, 1), two_steps, 0)

    @pl.when(i % 2 == 1)
    def _():
        full_step(i - 1, 0)

    scores(i, 1)
    finish(i, 0, True)
    finish(i, 1, True)


def _key_tile(k_ref, j, t, lanes=slice(None)):
    return k_ref[pl.ds(pl.multiple_of(j * t, t), t), lanes]


def _normalized(acc_ref, u):
    return acc_ref[u, :LANES, :] / acc_ref[u, LANES:LANES + 1, :]


def _diff_attn_kernel(lq1_ref, lk1_ref, lq2_ref, lk2_ref, g_ref, qT_ref, k_ref, vT_ref, o_ref,
                      acc_ref, m_ref, sa_ref, sb_ref, *, t):
    qT = qT_ref[...]
    row = lax.broadcasted_iota(jnp.int32, qT.shape, 0)
    first = (row % 64) < 32
    zero = jnp.zeros_like(qT)
    qTs = (jnp.where(first, qT, zero), jnp.where(first, zero, qT))
    _flash_body(qTs, lambda u, j: _key_tile(k_ref, j, t), lambda u, j: vT_ref[j],
                acc_ref, m_ref, (sa_ref, sb_ref), t=t)

    lam = (jnp.exp(jnp.sum(lq1_ref[...] * lk1_ref[...], axis=-1, keepdims=True))
           - jnp.exp(jnp.sum(lq2_ref[...] * lk2_ref[...], axis=-1, keepdims=True)) + LAMBDA_INIT)
    oT = _normalized(acc_ref, 0) - lam * _normalized(acc_ref, 1)
    oT = _rms_cols(oT, g_ref[...], SUBLN_EPS) * (1.0 - LAMBDA_INIT)
    o_ref[...] = oT.T.astype(o_ref.dtype)


def _mla_attn_kernel(qT_ref, k_ref, vT_ref, o_ref, acc_ref, m_ref, sa_ref, sb_ref, *, t):
    dk = MLA_QK_PAD
    qTs = tuple(qT_ref[u * dk:(u + 1) * dk, :] for u in range(2))
    _flash_body(qTs, lambda u, j: _key_tile(k_ref, j, t, slice(u * dk, (u + 1) * dk)),
                lambda u, j: vT_ref[j, u * VAL_ROWS:(u + 1) * VAL_ROWS, :],
                acc_ref, m_ref, (sa_ref, sb_ref), t=t)
    for u in range(2):
        o_ref[:, u * LANES:(u + 1) * LANES] = _normalized(acc_ref, u).T.astype(o_ref.dtype)


def _attn_call(kernel, extra, qT, k, vT, groups, heads_per_group, dk, name):
    B, nt, _, t = qT.shape
    S = nt * t
    hp = heads_per_group
    in_specs = [pl.BlockSpec(a.shape, lambda b, h, i: (0, 0)) for a in extra]
    in_specs += [
        pl.BlockSpec((None, None, hp * dk, t), lambda b, h, i: (b, i, h, 0)),
        pl.BlockSpec((None, S, hp * dk), lambda b, h, i: (b, 0, h)),
        pl.BlockSpec((None, nt, hp * VAL_ROWS, t), lambda b, h, i: (b, 0, h, 0)),
    ]
    return pl.pallas_call(
        functools.partial(kernel, t=t),
        grid=(B, groups, nt),
        in_specs=in_specs,
        out_specs=pl.BlockSpec((None, t, hp * LANES), lambda b, h, i: (b, i, h)),
        out_shape=jax.ShapeDtypeStruct((B, S, groups * hp * LANES), BF16),
        scratch_shapes=[
            pltpu.VMEM((2, VAL_ROWS, t), F32),
            pltpu.VMEM((2, 1, t), F32),
            pltpu.VMEM((t, t), F32),
            pltpu.VMEM((t, t), F32),
        ],
        compiler_params=pltpu.CompilerParams(
            dimension_semantics=("arbitrary", "arbitrary", "arbitrary"), vmem_limit_bytes=VMEM_LIMIT),
        name=name,
    )(*extra, qT, k, vT)


def _layer_norm(y, g, b):
    mu = jnp.mean(y, axis=-1, keepdims=True)
    d = y - mu
    var = jnp.mean(d * d, axis=-1, keepdims=True)
    return d * lax.rsqrt(var + LN_EPS) * g + b


def _split_bf16(a):
    hi = a.astype(BF16)
    return hi, (a - hi.astype(F32)).astype(BF16)


def _post_kernel(oa_ref, ob_ref, x_ref, wo_ref, g_ref, b_ref, wrh_ref, wrl_ref, br_ref,
                 x1_ref, idx_ref, gate_ref):
    half = oa_ref.shape[1]
    mixed = _dot(oa_ref[...], wo_ref[:half, :]) + _dot(ob_ref[...], wo_ref[half:, :])
    x1 = _layer_norm(DN_ALPHA * x_ref[...] + mixed, g_ref[...], b_ref[...])
    x1_ref[...] = x1

    x_hi, x_lo = _split_bf16(x1)
    logits = (_dot(x_hi, wrh_ref[...]) + _dot(x_lo, wrh_ref[...]) + _dot(x_hi, wrl_ref[...])) + br_ref[...]
    lane = lax.broadcasted_iota(jnp.int32, logits.shape, 1).astype(F32)
    work = logits
    vals, idxs = [], []
    for _ in range(TOP_K):
        m = jnp.max(work, axis=-1, keepdims=True)
        idx = jnp.min(jnp.where(work == m, lane, float(LANES)), axis=-1, keepdims=True)
        vals.append(m)
        idxs.append(idx)
        work = jnp.where(lane == idx, NEG_BIG, work)
    es = [jnp.exp(v - vals[0]) for v in vals]
    den = es[0] + es[1] + es[2] + es[3]
    idx_out = jnp.zeros(logits.shape, F32)
    gate_out = jnp.zeros(logits.shape, F32)
    for k in range(TOP_K):
        idx_out = jnp.where(lane == float(k), idxs[k], idx_out)
        gate_out = jnp.where(lane == float(k), es[k] / den, gate_out)
    idx_ref[...] = idx_out.astype(jnp.int32)
    gate_ref[...] = gate_out


def _post_call(oa, ob, x2, wo, g, b, wr, br):
    T = x2.shape[0]
    tm = POST_TM
    row = lambda i: (i, 0)
    full = lambda i: (0, 0)
    return pl.pallas_call(
        _post_kernel,
        grid=(T // tm,),
        in_specs=[
            pl.BlockSpec((tm, oa.shape[1]), row), pl.BlockSpec((tm, ob.shape[1]), row),
            pl.BlockSpec((tm, D_MODEL), row), pl.BlockSpec(wo.shape, full),
            pl.BlockSpec(g.shape, full), pl.BlockSpec(b.shape, full),
            pl.BlockSpec(wr.shape, full), pl.BlockSpec(wr.shape, full), pl.BlockSpec(br.shape, full),
        ],
        out_specs=[
            pl.BlockSpec((tm, D_MODEL), row), pl.BlockSpec((tm, LANES), row), pl.BlockSpec((tm, LANES), row),
        ],
        out_shape=[
            jax.ShapeDtypeStruct((T, D_MODEL), F32),
            jax.ShapeDtypeStruct((T, LANES), jnp.int32), jax.ShapeDtypeStruct((T, LANES), F32),
        ],
        compiler_params=pltpu.CompilerParams(dimension_semantics=("arbitrary",), vmem_limit_bytes=VMEM_LIMIT),
        name="post",
    )(oa, ob, x2, wo, g, b, *_split_bf16(wr), br)


def _lane_cumsum(x, n):
    lane = lax.broadcasted_iota(jnp.int32, x.shape, 1)
    s = 1
    while s < n:
        x = x + jnp.where(lane >= s, pltpu.roll(x, s, axis=1), 0.0)
        s *= 2
    return x


def _route_kernel(e_ref, lpos_ref, nch_ref, gbase_ref, meta_ref, *, bm, rows_per_tile):
    e = e_ref[...]
    R = e.shape[0]
    nt = R // rows_per_tile
    r_i = lax.broadcasted_iota(jnp.int32, (LANES, LANES), 0)
    c_i = lax.broadcasted_iota(jnp.int32, (LANES, LANES), 1)
    lane_incl = (r_i <= c_i).astype(BF16)
    rr = lax.broadcasted_iota(jnp.int32, (R, R), 0)
    rc = lax.broadcasted_iota(jnp.int32, (R, R), 1)
    same_tile = (rr // rows_per_tile) == (rc // rows_per_tile)
    rows_before = jnp.logical_and(rc < rr, same_tile).astype(BF16)
    tr = lax.broadcasted_iota(jnp.int32, (nt, R), 0)
    tc = lax.broadcasted_iota(jnp.int32, (nt, R), 1)
    tile_rows = (tc // rows_per_tile == tr).astype(BF16)
    er = lax.broadcasted_iota(jnp.int32, (R, nt), 0)
    ec = lax.broadcasted_iota(jnp.int32, (R, nt), 1)
    row_tile = (er // rows_per_tile == ec).astype(F32)
    lt_r = lax.broadcasted_iota(jnp.int32, (nt, nt), 0)
    lt_c = lax.broadcasted_iota(jnp.int32, (nt, nt), 1)
    tiles_before = (lt_c < lt_r).astype(BF16)
    lane_t = lax.broadcasted_iota(jnp.int32, (nt, LANES), 1)

    rank = jnp.zeros(e.shape, F32)
    cnt = jnp.zeros((nt, LANES), F32)
    for ex in range(N_EXPERTS):
        hit = e == ex
        m = jnp.where(hit, 1.0, 0.0).astype(BF16)
        incl = _dot(m, lane_incl)
        row_tot = jnp.broadcast_to(incl[:, LANES - 1:LANES], e.shape).astype(BF16)
        before = _dot(rows_before, row_tot)
        rank = jnp.where(hit, incl - 1.0 + before, rank)
        cnt = jnp.where(lane_t == ex, _dot(tile_rows, row_tot), cnt)

    chunk = float(ROW_CHUNK)
    cnt8 = jnp.floor((cnt + (chunk - 1.0)) * (1.0 / chunk)) * chunk
    lstart = _lane_cumsum(cnt8, N_EXPERTS) - cnt8
    tile_pre = _dot(tiles_before, cnt8.astype(BF16))
    tot8 = tile_pre[nt - 1:nt] + cnt8[nt - 1:nt]
    tot_bm = jnp.floor((tot8 + (bm - 1.0)) * (1.0 / bm)) * bm
    end_incl = _lane_cumsum(tot_bm, N_EXPERTS)
    ebase = end_incl - tot_bm

    lrow = jnp.dot(row_tile, lstart, preferred_element_type=F32,
                   precision=lax.Precision.HIGHEST)
    lstart_a = jnp.zeros(e.shape, F32)
    for ex in range(N_EXPERTS):
        lstart_a = jnp.where(e == ex, lrow[:, ex:ex + 1], lstart_a)
    lpos_ref[...] = (lstart_a + rank).astype(jnp.int32)
    nch_ref[...] = (cnt8 * (1.0 / chunk)).astype(jnp.int32)
    gbase_ref[...] = (ebase + tile_pre).astype(jnp.int32)

    sub = lax.broadcasted_iota(jnp.int32, meta_ref.shape, 0)
    meta = jnp.where(sub == 0, ebase + tot8, 0.0)
    meta = jnp.where(sub == 1, (tot_bm - tot8) * (1.0 / chunk), meta)
    n_used = end_incl[:, N_EXPERTS - 1:N_EXPERTS] * (1.0 / bm)
    meta = jnp.where(sub == 2, n_used, meta)
    lane_m = lax.broadcasted_iota(jnp.int32, meta_ref.shape, 1)
    meta = jnp.where(sub == 3, jnp.where(lane_m < N_EXPERTS, ebase * (1.0 / bm), n_used), meta)
    meta_ref[...] = meta.astype(jnp.int32)


def _route_call(flat_e, bm):
    A = flat_e.shape[0]
    R = A // LANES
    rows_per_tile = DISP_TILE * TOP_K // LANES
    nt = R // rows_per_tile
    full = lambda i: (0, 0)
    shapes = [(R, LANES), (nt, LANES), (nt, LANES), (8, LANES)]
    lpos, nch, gbase, meta = pl.pallas_call(
        functools.partial(_route_kernel, bm=bm, rows_per_tile=rows_per_tile),
        grid=(1,),
        in_specs=[pl.BlockSpec((R, LANES), full)],
        out_specs=[pl.BlockSpec(s, full) for s in shapes],
        out_shape=[jax.ShapeDtypeStruct(s, jnp.int32) for s in shapes],
        compiler_params=pltpu.CompilerParams(dimension_semantics=("arbitrary",), vmem_limit_bytes=VMEM_LIMIT),
        name="route",
    )(flat_e.reshape(R, LANES))
    E = N_EXPERTS
    return dict(lpos=lpos.reshape(A // TOP_K, TOP_K), nch=nch[:, :E].reshape(-1), gbase=gbase[:, :E].reshape(-1),
                tail_start=meta[0, :E], tail_n=meta[1, :E], n_used=meta[2, :1], first_block=meta[3, :E + 1])


def _chunk_rows(ref, first, rows=ROW_CHUNK):
    return ref.at[pl.ds(pl.multiple_of(first, ROW_CHUNK), rows), :]


def _group_copies(nch_ref, gbase_ref, tile, buf, hbm, sem, to_hbm):
    per_big = BIG_CHUNK // ROW_CHUNK

    def copy(local, remote):
        if to_hbm:
            pltpu.make_async_copy(local, remote, sem).start()
        else:
            pltpu.make_async_copy(remote, local, sem).start()

    def per_expert(ex, done):
        n = nch_ref[tile * N_EXPERTS + ex]
        g0 = gbase_ref[tile * N_EXPERTS + ex]
        l0 = done * ROW_CHUNK
        n_big = lax.shift_right_logical(n, per_big.bit_length() - 1)

        def big(c, carry):
            copy(_chunk_rows(buf, l0 + c * BIG_CHUNK, BIG_CHUNK), _chunk_rows(hbm, g0 + c * BIG_CHUNK, BIG_CHUNK))
            return carry

        def small(c, carry):
            copy(_chunk_rows(buf, l0 + c * ROW_CHUNK), _chunk_rows(hbm, g0 + c * ROW_CHUNK))
            return carry

        lax.fori_loop(0, n_big, big, 0)
        lax.fori_loop(n_big * per_big, n, small, 0)
        return done + n

    return lax.fori_loop(0, N_EXPERTS, per_expert, 0)


def _wait_chunks(n, hbm, sem):
    rows = pl.multiple_of(n * ROW_CHUNK, ROW_CHUNK)

    @pl.when(n > 0)
    def _():
        pltpu.make_async_copy(hbm.at[pl.ds(0, rows), :], hbm.at[pl.ds(0, rows), :], sem).wait()


def _dispatch_kernel(nch_ref, gbase_ref, tstart_ref, tn_ref, nu_ref, x_ref, lposT_ref, xs_hbm,
                     buf_ref, zero_ref, sem, zsem, cnt_ref, *, bm, n_blocks):
    i = pl.program_id(0)
    nt = pl.num_programs(0)
    slot = i % 2

    @pl.when(i >= 2)
    def _():
        _wait_chunks(cnt_ref[slot], xs_hbm, sem.at[slot])

    xb = x_ref[...].astype(BF16)
    lposT = lposT_ref[...]
    rows = DISP_PCHUNK
    for rb in range(DISP_LCAP // rows):
        r = lax.broadcasted_iota(jnp.int32, (rows, xb.shape[0]), 0) + rb * rows
        hit = r == lposT[0:1]
        for k in range(1, TOP_K):
            hit = jnp.logical_or(hit, r == lposT[k:k + 1])
        perm = jnp.where(hit, 1.0, 0.0).astype(BF16)
        buf_ref[slot, rb * rows:(rb + 1) * rows, :] = _dot(perm, xb)
    cnt_ref[slot] = _group_copies(nch_ref, gbase_ref, i, buf_ref.at[slot], xs_hbm, sem.at[slot], True)

    @pl.when(i == nt - 1)
    def _():
        zero_ref[...] = jnp.zeros(zero_ref.shape, F32)

        def per_expert(ex, done):
            first = tstart_ref[ex]

            def per_chunk(c, carry):
                pltpu.make_async_copy(_chunk_rows(zero_ref, 0), _chunk_rows(xs_hbm, first + c * ROW_CHUNK),
                                      zsem.at[0]).start()
                return carry

            lax.fori_loop(0, tn_ref[ex], per_chunk, 0)
            return done + tn_ref[ex]

        n_tail = lax.fori_loop(0, N_EXPERTS, per_expert, 0)

        def per_block(b, carry):
            pltpu.make_async_copy(zero_ref, xs_hbm.at[pl.ds(pl.multiple_of(b * bm, bm), bm), :], zsem.at[1]).start()
            return carry

        lax.fori_loop(nu_ref[0], n_blocks, per_block, 0)
        _wait_chunks(n_tail, xs_hbm, zsem.at[0])

        def wait_block(b, carry):
            pltpu.make_async_copy(zero_ref, xs_hbm.at[pl.ds(0, bm), :], zsem.at[1]).wait()
            return carry

        lax.fori_loop(nu_ref[0], n_blocks, wait_block, 0)
        _wait_chunks(cnt_ref[slot], xs_hbm, sem.at[slot])

        @pl.when(nt >= 2)
        def _():
            _wait_chunks(cnt_ref[1 - slot], xs_hbm, sem.at[1 - slot])


def _dispatch_call(rt, x1, lposT, bm, n_blocks):
    T = x1.shape[0]
    tile = DISP_TILE
    grid_spec = pltpu.PrefetchScalarGridSpec(
        num_scalar_prefetch=5,
        grid=(T // tile,),
        in_specs=[
            pl.BlockSpec((tile, D_MODEL), lambda i, *_: (i, 0)),
            pl.BlockSpec((None, TOP_K, tile), lambda i, *_: (i, 0, 0)),
        ],
        out_specs=pl.BlockSpec(memory_space=pl.ANY),
        scratch_shapes=[
            pltpu.VMEM((2, DISP_LCAP, D_MODEL), F32),
            pltpu.VMEM((bm, D_MODEL), F32),
            pltpu.SemaphoreType.DMA((2,)),
            pltpu.SemaphoreType.DMA((2,)),
            pltpu.SMEM((2,), jnp.int32),
        ],
    )
    return pl.pallas_call(
        functools.partial(_dispatch_kernel, bm=bm, n_blocks=n_blocks),
        grid_spec=grid_spec,
        out_shape=jax.ShapeDtypeStruct((n_blocks * bm, D_MODEL), F32),
        compiler_params=pltpu.CompilerParams(dimension_semantics=("arbitrary",), vmem_limit_bytes=VMEM_LIMIT),
        name="dispatch",
    )(rt["nch"], rt["gbase"], rt["tail_start"], rt["tail_n"], rt["n_used"], x1, lposT)


def _experts_kernel(fb_ref, xs_hbm, wgu_ref, bgu_ref, wd_ref, bd_ref, y_hbm,
                    xs_ref, ys_ref, wgu_b, wd_b, xsem, ysem, *, bm, n_blocks):
    e = pl.program_id(0)
    n_used = fb_ref[N_EXPERTS]

    def x_copy(blk, slot):
        return pltpu.make_async_copy(xs_hbm.at[pl.ds(pl.multiple_of(blk * bm, bm), bm), :], xs_ref.at[slot],
                                     xsem.at[slot])

    def y_copy(blk, slot):
        return pltpu.make_async_copy(ys_ref.at[slot], y_hbm.at[pl.ds(pl.multiple_of(blk * bm, bm), bm), :],
                                     ysem.at[slot])

    @pl.when(e == 0)
    def _():
        x_copy(0, 0).start()

    wgu_b[...] = wgu_ref[...].astype(BF16)
    wd_b[...] = wd_ref[...].astype(BF16)

    def block(blk, carry):
        slot = blk % 2
        x_copy(blk, slot).wait()

        @pl.when(blk + 1 < n_used)
        def _():
            x_copy(blk + 1, 1 - slot).start()

        h = _dot(xs_ref[slot].astype(BF16), wgu_b[...]) + bgu_ref[...]
        gate = jnp.minimum(h[:, :D_FF], SWIGLU_LIMIT)
        up = jnp.clip(h[:, D_FF:], -SWIGLU_LIMIT, SWIGLU_LIMIT)
        act = (up + 1.0) * (gate * jax.nn.sigmoid(gate * SWIGLU_ALPHA))
        y = _dot(act.astype(BF16), wd_b[...]) + bd_ref[...]

        @pl.when(blk >= 2)
        def _():
            y_copy(blk - 2, slot).wait()

        ys_ref[slot] = y
        y_copy(blk, slot).start()
        return carry

    lax.fori_loop(fb_ref[e], fb_ref[e + 1], block, 0)

    @pl.when(e == N_EXPERTS - 1)
    def _():
        y_copy(n_used - 1, (n_used - 1) % 2).wait()

        @pl.when(n_used >= 2)
        def _():
            y_copy(n_used - 2, n_used % 2).wait()

        ys_ref[0] = jnp.zeros(ys_ref.shape[1:], F32)

        def fill(blk, carry):
            y_copy(blk, 0).start()
            y_copy(blk, 0).wait()
            return carry

        lax.fori_loop(n_used, n_blocks, fill, 0)


def _experts_call(rt, xs, wgu, bgu, wd, bd, bm, n_blocks):
    E = N_EXPERTS
    by_expert = lambda e, fb: (e, 0, 0)
    grid_spec = pltpu.PrefetchScalarGridSpec(
        num_scalar_prefetch=1,
        grid=(E,),
        in_specs=[
            pl.BlockSpec(memory_space=pl.ANY),
            pl.BlockSpec((None, D_MODEL, 2 * D_FF), by_expert),
            pl.BlockSpec((None, 1, 2 * D_FF), by_expert),
            pl.BlockSpec((None, D_FF, D_MODEL), by_expert),
            pl.BlockSpec((None, 1, D_MODEL), by_expert),
        ],
        out_specs=pl.BlockSpec(memory_space=pl.ANY),
        scratch_shapes=[
            pltpu.VMEM((2, bm, D_MODEL), F32),
            pltpu.VMEM((2, bm, D_MODEL), F32),
            pltpu.VMEM((D_MODEL, 2 * D_FF), BF16),
            pltpu.VMEM((D_FF, D_MODEL), BF16),
            pltpu.SemaphoreType.DMA((2,)),
            pltpu.SemaphoreType.DMA((2,)),
        ],
    )
    return pl.pallas_call(
        functools.partial(_experts_kernel, bm=bm, n_blocks=n_blocks),
        grid_spec=grid_spec,
        out_shape=jax.ShapeDtypeStruct((n_blocks * bm, D_MODEL), F32),
        compiler_params=pltpu.CompilerParams(dimension_semantics=("arbitrary",), vmem_limit_bytes=VMEM_LIMIT),
        name="experts",
    )(rt["first_block"], xs, wgu, bgu.reshape(E, 1, 2 * D_FF), wd, bd.reshape(E, 1, D_MODEL))


def _combine_kernel(nch_ref, gbase_ref, y_hbm, x1_ref, lpos_ref, gate_ref, g_ref, b_ref, o_ref,
                    buf_ref, sem, cnt_ref):
    i = pl.program_id(0)
    nt = pl.num_programs(0)
    slot = i % 2

    @pl.when(i == 0)
    def _():
        buf_ref[...] = jnp.zeros(buf_ref.shape, F32)
        cnt_ref[0] = _group_copies(nch_ref, gbase_ref, 0, buf_ref.at[0], y_hbm, sem.at[0], False)

    @pl.when(i + 1 < nt)
    def _():
        cnt_ref[1 - slot] = _group_copies(nch_ref, gbase_ref, i + 1, buf_ref.at[1 - slot], y_hbm,
                                          sem.at[1 - slot], False)

    _wait_chunks(cnt_ref[slot], y_hbm, sem.at[slot])
    lpos = lpos_ref[...]
    gates = gate_ref[...]
    cols = DISP_PCHUNK
    y = jnp.zeros(o_ref.shape, F32)
    for cb in range(DISP_LCAP // cols):
        c = lax.broadcasted_iota(jnp.int32, (lpos.shape[0], cols), 1) + cb * cols
        w = jnp.zeros(c.shape, F32)
        for k in range(TOP_K):
            w = jnp.where(c == lpos[:, k:k + 1], gates[:, k:k + 1], w)
        y = y + _dot(w.astype(BF16), buf_ref[slot, cb * cols:(cb + 1) * cols, :].astype(BF16))
    o_ref[...] = _layer_norm(DN_ALPHA * x1_ref[...] + y, g_ref[...], b_ref[...])


def _combine_call(rt, y, x1, gates, g, b):
    T = x1.shape[0]
    tile = DISP_TILE
    grid_spec = pltpu.PrefetchScalarGridSpec(
        num_scalar_prefetch=2,
        grid=(T // tile,),
        in_specs=[
            pl.BlockSpec(memory_space=pl.ANY),
            pl.BlockSpec((tile, D_MODEL), lambda i, *_: (i, 0)),
            pl.BlockSpec((tile, TOP_K), lambda i, *_: (i, 0)),
            pl.BlockSpec((tile, LANES), lambda i, *_: (i, 0)),
            pl.BlockSpec(g.shape, lambda i, *_: (0, 0)),
            pl.BlockSpec(b.shape, lambda i, *_: (0, 0)),
        ],
        out_specs=pl.BlockSpec((tile, D_MODEL), lambda i, *_: (i, 0)),
        scratch_shapes=[
            pltpu.VMEM((2, DISP_LCAP, D_MODEL), F32),
            pltpu.SemaphoreType.DMA((2,)),
            pltpu.SMEM((2,), jnp.int32),
        ],
    )
    return pl.pallas_call(
        _combine_kernel,
        grid_spec=grid_spec,
        out_shape=jax.ShapeDtypeStruct((T, D_MODEL), F32),
        compiler_params=pltpu.CompilerParams(dimension_semantics=("arbitrary",), vmem_limit_bytes=VMEM_LIMIT),
        name="combine",
    )(rt["nch"], rt["gbase"], y, x1, rt["lpos"], gates, g, b)


def _rope_lane_order(n_sub):
    half = DIFF_HEAD_DIM // 2
    per = LANES // 2 // n_sub
    assert per == half or n_sub == 1
    cols = []
    for part in range(2):
        for sub in range(n_sub):
            cols.extend(sub * 64 + part * half + d for d in range(half))
    return np.asarray(cols)


def _prep_weights(w_in, mla_q_norm_g, w_uq, mla_kv_norm_g, w_ukv):
    o_dq, o_dk, o_dv, o_cq, o_ckv, o_kr = 0, 512, 1024, 1536, 1792, 1920
    head_order = _rope_lane_order(2)
    diff_cols = np.concatenate([h * LANES + head_order for h in range(DIFF_HEADS)])
    w = {}
    ckv = w_in[:, o_ckv:o_kr]
    w["feat"] = jnp.concatenate([w_in[:, o_dq + diff_cols], w_in[:, o_dv:o_cq], w_in[:, o_cq:o_ckv], ckv],
                                axis=1).T.astype(BF16)

    def spread_rope(cols64):
        z = jnp.zeros((cols64.shape[0], 32), cols64.dtype)
        return jnp.concatenate([cols64[:, :32], z, cols64[:, 32:], z], axis=1)

    w["tok"] = jnp.concatenate([w_in[:, o_dk + diff_cols], ckv, spread_rope(w_in[:, o_kr:o_kr + MLA_ROPE_DIM])],
                               axis=1).astype(BF16)
    uq = []
    for h in range(MLA_HEADS):
        base = h * MLA_QK_DIM
        uq.append(w_uq[:, base:base + MLA_NOPE_DIM])
        uq.append(spread_rope(w_uq[:, base + MLA_NOPE_DIM:base + MLA_QK_DIM]))
    w["uqT"] = jnp.concatenate(uq, axis=1).T.astype(BF16)
    per = MLA_NOPE_DIM + MLA_V_DIM
    w["uk"] = jnp.concatenate([w_ukv[:, h * per:h * per + MLA_NOPE_DIM] for h in range(MLA_HEADS)], axis=1).astype(BF16)
    w["uvT"] = jnp.concatenate(
        [w_ukv[:, h * per + MLA_NOPE_DIM:(h + 1) * per] for h in range(MLA_HEADS)], axis=1).T.astype(BF16)
    w["gq"] = mla_q_norm_g.reshape(MLA_Q_RANK, 1)
    w["gkv"] = mla_kv_norm_g.reshape(1, MLA_KV_RANK)
    w["gkvc"] = mla_kv_norm_g.reshape(MLA_KV_RANK, 1)
    return w


def _rope_tables(positions):
    half = MLA_ROPE_DIM // 2
    inv_freq = 1.0 / (ROPE_THETA ** (jnp.arange(0, MLA_ROPE_DIM, 2, dtype=F32) / MLA_ROPE_DIM))
    ang = positions.astype(F32)[..., None] * inv_freq
    ang = jnp.tile(ang, (1, 1, LANES // half))
    sign = jnp.where(jnp.arange(LANES) < LANES // 2, -1.0, 1.0).astype(F32)
    return jnp.cos(ang), jnp.sin(ang) * sign


def kernel(x, positions, w_in, lambda_q1, lambda_k1, lambda_q2, lambda_k2, subln_g, mla_q_norm_g, w_uq,
           mla_kv_norm_g, w_ukv, w_o, ln1_g, ln1_b, w_router, b_router, w_gate_up, b_gate_up, w_down, b_down,
           ln2_g, ln2_b):
    B, S, D = x.shape
    T = B * S
    l = 0
    x2 = x.reshape(T, D)
    w = _prep_weights(w_in[l], mla_q_norm_g[l], w_uq[l], mla_kv_norm_g[l], w_ukv[l])

    dqT, dk, dvT, mqT, mk, mvT = _proj_call(x, _rope_tables(positions), w)
    lam_vecs = [v[l].reshape(1, DIFF_HEAD_DIM) for v in (lambda_q1, lambda_k1, lambda_q2, lambda_k2)]
    o_a = _attn_call(_diff_attn_kernel, lam_vecs + [subln_g[l].reshape(DIFF_V_DIM, 1)],
                     dqT, dk, dvT, DIFF_HEADS, 1, LANES, "diff_attn")
    o_b = _attn_call(_mla_attn_kernel, [], mqT, mk, mvT, MLA_HEADS // 2, 2, MLA_QK_PAD, "mla_attn")

    wr = jnp.pad(w_router[l], ((0, 0), (0, LANES - N_EXPERTS)))
    br = jnp.pad(b_router[l], (0, LANES - N_EXPERTS), constant_values=NEG_BIG).reshape(1, LANES)
    x1, idx, gates = _post_call(
        o_a.reshape(T, -1), o_b.reshape(T, -1), x2, w_o[l].astype(BF16),
        ln1_g[l].reshape(1, D), ln1_b[l].reshape(1, D), wr, br)

    bm = MOE_BM
    A = T * TOP_K
    n_tiles = T // DISP_TILE
    n_blocks = pl.cdiv(A + n_tiles * N_EXPERTS * (ROW_CHUNK - 1) + N_EXPERTS * (bm - ROW_CHUNK), bm)
    rt = _route_call(idx[:, :TOP_K].reshape(A), bm)
    lposT = rt["lpos"].reshape(n_tiles, DISP_TILE, TOP_K).transpose(0, 2, 1)
    xs = _dispatch_call(rt, x1, lposT, bm, n_blocks)
    y = _experts_call(rt, xs, w_gate_up[l], b_gate_up[l], w_down[l], b_down[l], bm, n_blocks)
    out = _combine_call(rt, y, x1, gates, ln2_g[l].reshape(1, D), ln2_b[l].reshape(1, D))
    return out.reshape(B, S, D)
```

```python
import functools
import math

import numpy as np
import jax
import jax.numpy as jnp
from jax import lax
from jax.experimental import pallas as pl
from jax.experimental.pallas import tpu as pltpu

D_MODEL = 1024
DIFF_HEADS = 4
DIFF_HEAD_DIM = 64
DIFF_V_DIM = 128
MLA_HEADS = 4
MLA_V_DIM = 128
MLA_NOPE_DIM = 128
MLA_ROPE_DIM = 64
MLA_QK_DIM = MLA_NOPE_DIM + MLA_ROPE_DIM
MLA_Q_RANK = 256
MLA_KV_RANK = 128
DIFF_Q_COLS = DIFF_K_COLS = DIFF_V_COLS = 512
ROPE_THETA = 10000.0
N_EXPERTS = 32
TOP_K = 4
D_FF = 1024
SWIGLU_LIMIT = 7.0
SWIGLU_ALPHA = 1.702
LN_EPS = 1e-5
SUBLN_EPS = 1e-5
MLA_RMS_EPS = 1e-6
DEPTH = 1
DN_ALPHA = (2.0 * DEPTH) ** 0.25
LAMBDA_INIT = 0.8 - 0.6 * math.exp(-0.3 * 0)

LANES = 128
MLA_QK_PAD = 2 * LANES
BF16_ROWS = 16
VAL_ROWS = LANES + BF16_ROWS
VMEM_LIMIT = 56 * 1024 * 1024

ATTN_TILE = 512
POST_TM = 512
MOE_BM = 256
ROW_CHUNK = BF16_ROWS
BIG_CHUNK = 32
DISP_TILE = 512
DISP_PCHUNK = 256
DISP_LCAP = -(-(DISP_TILE * TOP_K + N_EXPERTS * (ROW_CHUNK - 1)) // DISP_PCHUNK) * DISP_PCHUNK

NEG_BIG = -1e30
LOG2E = math.log2(math.e)
F32 = jnp.float32
BF16 = jnp.bfloat16


def _dot(a, b):
    return jnp.dot(a, b, preferred_element_type=F32)


def _dot_nt(a, b):
    return lax.dot_general(a, b, (((1,), (1,)), ((), ())), preferred_element_type=F32)


def _rope128(blk, cos, sin):
    return blk * cos + pltpu.roll(blk, 64, axis=1) * sin


def _rope128_t(blk, cos, sin):
    half = LANES // 2
    rolled = jnp.concatenate([blk[half:], blk[:half]], axis=0)
    return blk * cos + rolled * sin


def _rms_rows(t, g, eps):
    return t * lax.rsqrt(jnp.mean(t * t, axis=-1, keepdims=True) + eps) * g


def _rms_cols(t, g, eps):
    return t * lax.rsqrt(jnp.mean(t * t, axis=0, keepdims=True) + eps) * g


def _store_values(vT_ref, vT, heads):
    ones = jnp.ones((BF16_ROWS, vT.shape[1]), BF16)
    for h in range(heads):
        vT_ref[h * VAL_ROWS:h * VAL_ROWS + LANES, :] = vT[h * LANES:(h + 1) * LANES].astype(BF16)
        vT_ref[h * VAL_ROWS + LANES:(h + 1) * VAL_ROWS, :] = ones


def _proj_kernel(x_ref, cos_ref, sin_ref, wfeat_ref, wtok_ref, gq_ref, gkv_ref, gkvc_ref,
                 wuqT_ref, wuk_ref, wuvT_ref, dqT_ref, dk_ref, dvT_ref, mqT_ref, mk_ref, mvT_ref):
    xb = x_ref[...].astype(BF16)
    cos, sin = cos_ref[...], sin_ref[...]
    cosT, sinT = cos.T, sin.T

    dq_scale = DIFF_HEAD_DIM ** -0.5 * LOG2E
    mq_scale = MLA_QK_DIM ** -0.5 * LOG2E
    o_dv, o_cq, o_ckv = DIFF_Q_COLS, DIFF_Q_COLS + DIFF_V_COLS, DIFF_Q_COLS + DIFF_V_COLS + MLA_Q_RANK

    feat = _dot_nt(wfeat_ref[...], xb)
    for h in range(DIFF_HEADS):
        sl = slice(h * LANES, (h + 1) * LANES)
        dqT_ref[sl, :] = (_rope128_t(feat[sl], cosT, sinT) * dq_scale).astype(BF16)
    _store_values(dvT_ref, feat[o_dv:o_cq], DIFF_HEADS)

    cqT = _rms_cols(feat[o_cq:o_ckv], gq_ref[...], MLA_RMS_EPS)
    qT = _dot(wuqT_ref[...], cqT.astype(BF16))
    for h in range(MLA_HEADS):
        nope = slice(h * MLA_QK_PAD, h * MLA_QK_PAD + LANES)
        ropes = slice(h * MLA_QK_PAD + LANES, (h + 1) * MLA_QK_PAD)
        mqT_ref[nope, :] = (qT[nope] * mq_scale).astype(BF16)
        mqT_ref[ropes, :] = (_rope128_t(qT[ropes], cosT, sinT) * mq_scale).astype(BF16)

    ckvT = _rms_cols(feat[o_ckv:], gkvc_ref[...], MLA_RMS_EPS)
    _store_values(mvT_ref, _dot(wuvT_ref[...], ckvT.astype(BF16)), MLA_HEADS)

    tokm = _dot(xb, wtok_ref[...])
    for h in range(DIFF_HEADS):
        sl = slice(h * LANES, (h + 1) * LANES)
        dk_ref[:, sl] = _rope128(tokm[:, sl], cos, sin).astype(BF16)
    ckv = _rms_rows(tokm[:, DIFF_K_COLS:DIFF_K_COLS + MLA_KV_RANK], gkv_ref[...], MLA_RMS_EPS)
    k_nope = _dot(ckv.astype(BF16), wuk_ref[...])
    k_pe = _rope128(tokm[:, DIFF_K_COLS + MLA_KV_RANK:], cos, sin).astype(BF16)
    for h in range(MLA_HEADS):
        mk_ref[:, h * MLA_QK_PAD:h * MLA_QK_PAD + LANES] = k_nope[:, h * LANES:(h + 1) * LANES].astype(BF16)
        mk_ref[:, h * MLA_QK_PAD + LANES:(h + 1) * MLA_QK_PAD] = k_pe


def _proj_call(x3, tabs, w):
    B, S, D = x3.shape
    tm = ATTN_TILE
    nt = S // tm
    cos_t, sin_t = tabs
    weights = [w["feat"], w["tok"], w["gq"], w["gkv"], w["gkvc"], w["uqT"], w["uk"], w["uvT"]]
    tok = lambda b, i: (b, i, 0)
    feat = lambda b, i: (b, i, 0, 0)
    in_specs = [pl.BlockSpec((None, tm, D), tok),
                pl.BlockSpec((None, tm, LANES), tok), pl.BlockSpec((None, tm, LANES), tok)]
    in_specs += [pl.BlockSpec(a.shape, lambda b, i: (0, 0)) for a in weights]
    mq_w = MLA_HEADS * MLA_QK_PAD
    dv_w, mv_w = DIFF_HEADS * VAL_ROWS, MLA_HEADS * VAL_ROWS
    out_specs = [pl.BlockSpec((None, None, 512, tm), feat), pl.BlockSpec((None, tm, 512), tok),
                 pl.BlockSpec((None, None, dv_w, tm), feat), pl.BlockSpec((None, None, mq_w, tm), feat),
                 pl.BlockSpec((None, tm, mq_w), tok), pl.BlockSpec((None, None, mv_w, tm), feat)]
    out_shape = [jax.ShapeDtypeStruct((B, nt, 512, tm), BF16), jax.ShapeDtypeStruct((B, S, 512), BF16),
                 jax.ShapeDtypeStruct((B, nt, dv_w, tm), BF16), jax.ShapeDtypeStruct((B, nt, mq_w, tm), BF16),
                 jax.ShapeDtypeStruct((B, S, mq_w), BF16), jax.ShapeDtypeStruct((B, nt, mv_w, tm), BF16)]
    return pl.pallas_call(
        _proj_kernel,
        grid=(B, nt),
        in_specs=in_specs,
        out_specs=out_specs,
        out_shape=out_shape,
        compiler_params=pltpu.CompilerParams(dimension_semantics=("arbitrary", "arbitrary"),
                                             vmem_limit_bytes=VMEM_LIMIT),
        name="proj",
    )(x3, cos_t, sin_t, *weights)


def _flash_body(qTs, keys_of, values_of, acc_ref, m_ref, s_refs, *, t):
    i = pl.program_id(2)
    m_ref[...] = jnp.full(m_ref.shape, -jnp.inf, F32)
    acc_ref[...] = jnp.zeros(acc_ref.shape, F32)

    def scores(j, u):
        s_refs[u][...] = _dot(keys_of(u, j), qTs[u])

    def finish(j, u, masked):
        s = s_refs[u][...]
        if masked:
            key = lax.broadcasted_iota(jnp.int32, s.shape, 0)
            qry = lax.broadcasted_iota(jnp.int32, s.shape, 1)
            s = jnp.where(key <= qry, s, -jnp.inf)
        m_prev = m_ref[u]
        m_new = jnp.maximum(m_prev, jnp.max(s, axis=0, keepdims=True))
        alpha = jnp.exp2(m_prev - m_new)
        p = jnp.exp2((s - m_new).astype(BF16))
        acc_ref[u] = alpha * acc_ref[u] + _dot(values_of(u, j), p)
        m_ref[u] = m_new

    def full_step(j, carry):
        scores(j, 1)
        finish(j, 0, False)
        scores(j + 1, 0)
        finish(j, 1, False)
        return carry

    def two_steps(jj, carry):
        return full_step(2 * jj + 1, full_step(2 * jj, carry))

    scores(0, 0)
    lax.fori_loop(0, lax.shift_right_logical(i, 1), two_steps, 0)

    @pl.when(i % 2 == 1)
    def _():
        full_step(i - 1, 0)

    scores(i, 1)
    finish(i, 0, True)
    finish(i, 1, True)


def _key_tile(k_ref, j, t, lanes=slice(None)):
    return k_ref[pl.ds(pl.multiple_of(j * t, t), t), lanes]


def _normalized(acc_ref, u):
    return acc_ref[u, :LANES, :] / acc_ref[u, LANES:LANES + 1, :]


def _diff_attn_kernel(lq1_ref, lk1_ref, lq2_ref, lk2_ref, g_ref, qT_ref, k_ref, vT_ref, o_ref,
                      acc_ref, m_ref, sa_ref, sb_ref, *, t):
    qT = qT_ref[...]
    row = lax.broadcasted_iota(jnp.int32, qT.shape, 0)
    first = (row % 64) < 32
    zero = jnp.zeros_like(qT)
    qTs = (jnp.where(first, qT, zero), jnp.where(first, zero, qT))
    _flash_body(qTs, lambda u, j: _key_tile(k_ref, j, t), lambda u, j: vT_ref[j],
                acc_ref, m_ref, (sa_ref, sb_ref), t=t)

    lam = (jnp.exp(jnp.sum(lq1_ref[...] * lk1_ref[...], axis=-1, keepdims=True))
           - jnp.exp(jnp.sum(lq2_ref[...] * lk2_ref[...], axis=-1, keepdims=True)) + LAMBDA_INIT)
    oT = _normalized(acc_ref, 0) - lam * _normalized(acc_ref, 1)
    oT = _rms_cols(oT, g_ref[...], SUBLN_EPS) * (1.0 - LAMBDA_INIT)
    o_ref[...] = oT.T.astype(o_ref.dtype)


def _mla_attn_kernel(qT_ref, k_ref, vT_ref, o_ref, acc_ref, m_ref, sa_ref, sb_ref, *, t):
    dk = MLA_QK_PAD
    qTs = tuple(qT_ref[u * dk:(u + 1) * dk, :] for u in range(2))
    _flash_body(qTs, lambda u, j: _key_tile(k_ref, j, t, slice(u * dk, (u + 1) * dk)),
                lambda u, j: vT_ref[j, u * VAL_ROWS:(u + 1) * VAL_ROWS, :],
                acc_ref, m_ref, (sa_ref, sb_ref), t=t)
    for u in range(2):
        o_ref[:, u * LANES:(u + 1) * LANES] = _normalized(acc_ref, u).T.astype(o_ref.dtype)


def _attn_call(kernel, extra, qT, k, vT, groups, heads_per_group, dk, name):
    B, nt, _, t = qT.shape
    S = nt * t
    hp = heads_per_group
    in_specs = [pl.BlockSpec(a.shape, lambda b, h, i: (0, 0)) for a in extra]
    in_specs += [
        pl.BlockSpec((None, None, hp * dk, t), lambda b, h, i: (b, i, h, 0)),
        pl.BlockSpec((None, S, hp * dk), lambda b, h, i: (b, 0, h)),
        pl.BlockSpec((None, nt, hp * VAL_ROWS, t), lambda b, h, i: (b, 0, h, 0)),
    ]
    return pl.pallas_call(
        functools.partial(kernel, t=t),
        grid=(B, groups, nt),
        in_specs=in_specs,
        out_specs=pl.BlockSpec((None, t, hp * LANES), lambda b, h, i: (b, i, h)),
        out_shape=jax.ShapeDtypeStruct((B, S, groups * hp * LANES), BF16),
        scratch_shapes=[
            pltpu.VMEM((2, VAL_ROWS, t), F32),
            pltpu.VMEM((2, 1, t), F32),
            pltpu.VMEM((t, t), F32),
            pltpu.VMEM((t, t), F32),
        ],
        compiler_params=pltpu.CompilerParams(
            dimension_semantics=("arbitrary", "arbitrary", "arbitrary"), vmem_limit_bytes=VMEM_LIMIT),
        name=name,
    )(*extra, qT, k, vT)


def _layer_norm(y, g, b):
    mu = jnp.mean(y, axis=-1, keepdims=True)
    d = y - mu
    var = jnp.mean(d * d, axis=-1, keepdims=True)
    return d * lax.rsqrt(var + LN_EPS) * g + b


def _split_bf16(a):
    hi = a.astype(BF16)
    return hi, (a - hi.astype(F32)).astype(BF16)


def _post_kernel(oa_ref, ob_ref, x_ref, wo_ref, g_ref, b_ref, wrh_ref, wrl_ref, br_ref,
                 x1_ref, idx_ref, gate_ref):
    half = oa_ref.shape[1]
    mixed = _dot(oa_ref[...], wo_ref[:half, :]) + _dot(ob_ref[...], wo_ref[half:, :])
    x1 = _layer_norm(DN_ALPHA * x_ref[...] + mixed, g_ref[...], b_ref[...])
    x1_ref[...] = x1

    x_hi, x_lo = _split_bf16(x1)
    logits = (_dot(x_hi, wrh_ref[...]) + _dot(x_lo, wrh_ref[...]) + _dot(x_hi, wrl_ref[...])) + br_ref[...]
    lane = lax.broadcasted_iota(jnp.int32, logits.shape, 1).astype(F32)
    work = logits
    vals, idxs = [], []
    for _ in range(TOP_K):
        m = jnp.max(work, axis=-1, keepdims=True)
        idx = jnp.min(jnp.where(work == m, lane, float(LANES)), axis=-1, keepdims=True)
        vals.append(m)
        idxs.append(idx)
        work = jnp.where(lane == idx, NEG_BIG, work)
    es = [jnp.exp(v - vals[0]) for v in vals]
    den = es[0] + es[1] + es[2] + es[3]
    idx_out = jnp.zeros(logits.shape, F32)
    gate_out = jnp.zeros(logits.shape, F32)
    for k in range(TOP_K):
        idx_out = jnp.where(lane == float(k), idxs[k], idx_out)
        gate_out = jnp.where(lane == float(k), es[k] / den, gate_out)
    idx_ref[...] = idx_out.astype(jnp.int32)
    gate_ref[...] = gate_out


def _post_call(oa, ob, x2, wo, g, b, wr, br):
    T = x2.shape[0]
    tm = POST_TM
    row = lambda i: (i, 0)
    full = lambda i: (0, 0)
    return pl.pallas_call(
        _post_kernel,
        grid=(T // tm,),
        in_specs=[
            pl.BlockSpec((tm, oa.shape[1]), row), pl.BlockSpec((tm, ob.shape[1]), row),
            pl.BlockSpec((tm, D_MODEL), row), pl.BlockSpec(wo.shape, full),
            pl.BlockSpec(g.shape, full), pl.BlockSpec(b.shape, full),
            pl.BlockSpec(wr.shape, full), pl.BlockSpec(wr.shape, full), pl.BlockSpec(br.shape, full),
        ],
        out_specs=[
            pl.BlockSpec((tm, D_MODEL), row), pl.BlockSpec((tm, LANES), row), pl.BlockSpec((tm, LANES), row),
        ],
        out_shape=[
            jax.ShapeDtypeStruct((T, D_MODEL), F32),
            jax.ShapeDtypeStruct((T, LANES), jnp.int32), jax.ShapeDtypeStruct((T, LANES), F32),
        ],
        compiler_params=pltpu.CompilerParams(dimension_semantics=("arbitrary",), vmem_limit_bytes=VMEM_LIMIT),
        name="post",
    )(oa, ob, x2, wo, g, b, *_split_bf16(wr), br)


def _lane_cumsum(x, n):
    lane = lax.broadcasted_iota(jnp.int32, x.shape, 1)
    s = 1
    while s < n:
        x = x + jnp.where(lane >= s, pltpu.roll(x, s, axis=1), 0.0)
        s *= 2
    return x


def _route_kernel(e_ref, lpos_ref, nch_ref, gbase_ref, meta_ref, *, bm, rows_per_tile):
    e = e_ref[...]
    R = e.shape[0]
    nt = R // rows_per_tile
    r_i = lax.broadcasted_iota(jnp.int32, (LANES, LANES), 0)
    c_i = lax.broadcasted_iota(jnp.int32, (LANES, LANES), 1)
    lane_incl = (r_i <= c_i).astype(BF16)
    rr = lax.broadcasted_iota(jnp.int32, (R, R), 0)
    rc = lax.broadcasted_iota(jnp.int32, (R, R), 1)
    same_tile = (rr // rows_per_tile) == (rc // rows_per_tile)
    rows_before = jnp.logical_and(rc < rr, same_tile).astype(BF16)
    tr = lax.broadcasted_iota(jnp.int32, (nt, R), 0)
    tc = lax.broadcasted_iota(jnp.int32, (nt, R), 1)
    tile_rows = (tc // rows_per_tile == tr).astype(BF16)
    er = lax.broadcasted_iota(jnp.int32, (R, nt), 0)
    ec = lax.broadcasted_iota(jnp.int32, (R, nt), 1)
    row_tile = (er // rows_per_tile == ec).astype(F32)
    lt_r = lax.broadcasted_iota(jnp.int32, (nt, nt), 0)
    lt_c = lax.broadcasted_iota(jnp.int32, (nt, nt), 1)
    tiles_before = (lt_c < lt_r).astype(BF16)
    lane_t = lax.broadcasted_iota(jnp.int32, (nt, LANES), 1)

    rank = jnp.zeros(e.shape, F32)
    cnt = jnp.zeros((nt, LANES), F32)
    for ex in range(N_EXPERTS):
        hit = e == ex
        m = jnp.where(hit, 1.0, 0.0).astype(BF16)
        incl = _dot(m, lane_incl)
        row_tot = jnp.broadcast_to(incl[:, LANES - 1:LANES], e.shape).astype(BF16)
        before = _dot(rows_before, row_tot)
        rank = jnp.where(hit, incl - 1.0 + before, rank)
        cnt = jnp.where(lane_t == ex, _dot(tile_rows, row_tot), cnt)

    chunk = float(ROW_CHUNK)
    cnt8 = jnp.floor((cnt + (chunk - 1.0)) * (1.0 / chunk)) * chunk
    lstart = _lane_cumsum(cnt8, N_EXPERTS) - cnt8
    tile_pre = _dot(tiles_before, cnt8.astype(BF16))
    tot8 = tile_pre[nt - 1:nt] + cnt8[nt - 1:nt]
    tot_bm = jnp.floor((tot8 + (bm - 1.0)) * (1.0 / bm)) * bm
    end_incl = _lane_cumsum(tot_bm, N_EXPERTS)
    ebase = end_incl - tot_bm

    lrow = jnp.dot(row_tile, lstart, preferred_element_type=F32,
                   precision=lax.Precision.HIGHEST)
    lstart_a = jnp.zeros(e.shape, F32)
    for ex in range(N_EXPERTS):
        lstart_a = jnp.where(e == ex, lrow[:, ex:ex + 1], lstart_a)
    lpos_ref[...] = (lstart_a + rank).astype(jnp.int32)
    nch_ref[...] = (cnt8 * (1.0 / chunk)).astype(jnp.int32)
    gbase_ref[...] = (ebase + tile_pre).astype(jnp.int32)

    sub = lax.broadcasted_iota(jnp.int32, meta_ref.shape, 0)
    meta = jnp.where(sub == 0, ebase + tot8, 0.0)
    meta = jnp.where(sub == 1, (tot_bm - tot8) * (1.0 / chunk), meta)
    n_used = end_incl[:, N_EXPERTS - 1:N_EXPERTS] * (1.0 / bm)
    meta = jnp.where(sub == 2, n_used, meta)
    lane_m = lax.broadcasted_iota(jnp.int32, meta_ref.shape, 1)
    meta = jnp.where(sub == 3, jnp.where(lane_m < N_EXPERTS, ebase * (1.0 / bm), n_used), meta)
    meta_ref[...] = meta.astype(jnp.int32)


def _route_call(flat_e, bm):
    A = flat_e.shape[0]
    R = A // LANES
    rows_per_tile = DISP_TILE * TOP_K // LANES
    nt = R // rows_per_tile
    full = lambda i: (0, 0)
    shapes = [(R, LANES), (nt, LANES), (nt, LANES), (8, LANES)]
    lpos, nch, gbase, meta = pl.pallas_call(
        functools.partial(_route_kernel, bm=bm, rows_per_tile=rows_per_tile),
        grid=(1,),
        in_specs=[pl.BlockSpec((R, LANES), full)],
        out_specs=[pl.BlockSpec(s, full) for s in shapes],
        out_shape=[jax.ShapeDtypeStruct(s, jnp.int32) for s in shapes],
        compiler_params=pltpu.CompilerParams(dimension_semantics=("arbitrary",), vmem_limit_bytes=VMEM_LIMIT),
        name="route",
    )(flat_e.reshape(R, LANES))
    E = N_EXPERTS
    return dict(lpos=lpos.reshape(A // TOP_K, TOP_K), nch=nch[:, :E].reshape(-1), gbase=gbase[:, :E].reshape(-1),
                tail_start=meta[0, :E], tail_n=meta[1, :E], n_used=meta[2, :1], first_block=meta[3, :E + 1])


def _chunk_rows(ref, first, rows=ROW_CHUNK):
    return ref.at[pl.ds(pl.multiple_of(first, ROW_CHUNK), rows), :]


def _group_copies(nch_ref, gbase_ref, tile, buf, hbm, sem, to_hbm):
    per_big = BIG_CHUNK // ROW_CHUNK

    def copy(local, remote):
        if to_hbm:
            pltpu.make_async_copy(local, remote, sem).start()
        else:
            pltpu.make_async_copy(remote, local, sem).start()

    def per_expert(ex, done):
        n = nch_ref[tile * N_EXPERTS + ex]
        g0 = gbase_ref[tile * N_EXPERTS + ex]
        l0 = done * ROW_CHUNK
        n_big = lax.shift_right_logical(n, per_big.bit_length() - 1)

        def big(c, carry):
            copy(_chunk_rows(buf, l0 + c * BIG_CHUNK, BIG_CHUNK), _chunk_rows(hbm, g0 + c * BIG_CHUNK, BIG_CHUNK))
            return carry

        def small(c, carry):
            copy(_chunk_rows(buf, l0 + c * ROW_CHUNK), _chunk_rows(hbm, g0 + c * ROW_CHUNK))
            return carry

        lax.fori_loop(0, n_big, big, 0)
        lax.fori_loop(n_big * per_big, n, small, 0)
        return done + n

    return lax.fori_loop(0, N_EXPERTS, per_expert, 0)


def _wait_chunks(n, hbm, sem):
    rows = pl.multiple_of(n * ROW_CHUNK, ROW_CHUNK)

    @pl.when(n > 0)
    def _():
        pltpu.make_async_copy(hbm.at[pl.ds(0, rows), :], hbm.at[pl.ds(0, rows), :], sem).wait()


def _dispatch_kernel(nch_ref, gbase_ref, tstart_ref, tn_ref, nu_ref, x_ref, lposT_ref, xs_hbm,
                     buf_ref, zero_ref, sem, zsem, cnt_ref, *, bm, n_blocks):
    i = pl.program_id(0)
    nt = pl.num_programs(0)
    slot = i % 2

    @pl.when(i >= 2)
    def _():
        _wait_chunks(cnt_ref[slot], xs_hbm, sem.at[slot])

    xb = x_ref[...].astype(BF16)
    lposT = lposT_ref[...]
    rows = DISP_PCHUNK
    for rb in range(DISP_LCAP // rows):
        r = lax.broadcasted_iota(jnp.int32, (rows, xb.shape[0]), 0) + rb * rows
        hit = r == lposT[0:1]
        for k in range(1, TOP_K):
            hit = jnp.logical_or(hit, r == lposT[k:k + 1])
        perm = jnp.where(hit, 1.0, 0.0).astype(BF16)
        buf_ref[slot, rb * rows:(rb + 1) * rows, :] = _dot(perm, xb).astype(BF16)
    cnt_ref[slot] = _group_copies(nch_ref, gbase_ref, i, buf_ref.at[slot], xs_hbm, sem.at[slot], True)

    @pl.when(i == nt - 1)
    def _():
        zero_ref[...] = jnp.zeros(zero_ref.shape, BF16)

        def per_expert(ex, done):
            first = tstart_ref[ex]

            def per_chunk(c, carry):
                pltpu.make_async_copy(_chunk_rows(zero_ref, 0), _chunk_rows(xs_hbm, first + c * ROW_CHUNK),
                                      zsem.at[0]).start()
                return carry

            lax.fori_loop(0, tn_ref[ex], per_chunk, 0)
            return done + tn_ref[ex]

        n_tail = lax.fori_loop(0, N_EXPERTS, per_expert, 0)

        def per_block(b, carry):
            pltpu.make_async_copy(zero_ref, xs_hbm.at[pl.ds(pl.multiple_of(b * bm, bm), bm), :], zsem.at[1]).start()
            return carry

        lax.fori_loop(nu_ref[0], n_blocks, per_block, 0)
        _wait_chunks(n_tail, xs_hbm, zsem.at[0])

        def wait_block(b, carry):
            pltpu.make_async_copy(zero_ref, xs_hbm.at[pl.ds(0, bm), :], zsem.at[1]).wait()
            return carry

        lax.fori_loop(nu_ref[0], n_blocks, wait_block, 0)
        _wait_chunks(cnt_ref[slot], xs_hbm, sem.at[slot])

        @pl.when(nt >= 2)
        def _():
            _wait_chunks(cnt_ref[1 - slot], xs_hbm, sem.at[1 - slot])


def _dispatch_call(rt, x1, lposT, bm, n_blocks):
    T = x1.shape[0]
    tile = DISP_TILE
    grid_spec = pltpu.PrefetchScalarGridSpec(
        num_scalar_prefetch=5,
        grid=(T // tile,),
        in_specs=[
            pl.BlockSpec((tile, D_MODEL), lambda i, *_: (i, 0)),
            pl.BlockSpec((None, TOP_K, tile), lambda i, *_: (i, 0, 0)),
        ],
        out_specs=pl.BlockSpec(memory_space=pl.ANY),
        scratch_shapes=[
            pltpu.VMEM((2, DISP_LCAP, D_MODEL), BF16),
            pltpu.VMEM((bm, D_MODEL), BF16),
            pltpu.SemaphoreType.DMA((2,)),
            pltpu.SemaphoreType.DMA((2,)),
            pltpu.SMEM((2,), jnp.int32),
        ],
    )
    return pl.pallas_call(
        functools.partial(_dispatch_kernel, bm=bm, n_blocks=n_blocks),
        grid_spec=grid_spec,
        out_shape=jax.ShapeDtypeStruct((n_blocks * bm, D_MODEL), BF16),
        compiler_params=pltpu.CompilerParams(dimension_semantics=("arbitrary",), vmem_limit_bytes=VMEM_LIMIT),
        name="dispatch",
    )(rt["nch"], rt["gbase"], rt["tail_start"], rt["tail_n"], rt["n_used"], x1, lposT)


def _experts_kernel(fb_ref, xs_hbm, wgu_ref, bgu_ref, wd_ref, bd_ref, y_hbm,
                    xs_ref, ys_ref, wgu_b, wd_b, xsem, ysem, *, bm, n_blocks):
    e = pl.program_id(0)
    n_used = fb_ref[N_EXPERTS]

    def x_copy(blk, slot):
        return pltpu.make_async_copy(xs_hbm.at[pl.ds(pl.multiple_of(blk * bm, bm), bm), :], xs_ref.at[slot],
                                     xsem.at[slot])

    def y_copy(blk, slot):
        return pltpu.make_async_copy(ys_ref.at[slot], y_hbm.at[pl.ds(pl.multiple_of(blk * bm, bm), bm), :],
                                     ysem.at[slot])

    @pl.when(e == 0)
    def _():
        x_copy(0, 0).start()

    wgu_b[...] = wgu_ref[...].astype(BF16)
    wd_b[...] = wd_ref[...].astype(BF16)

    def block(blk, carry):
        slot = blk % 2
        x_copy(blk, slot).wait()

        @pl.when(blk + 1 < n_used)
        def _():
            x_copy(blk + 1, 1 - slot).start()

        h = _dot(xs_ref[slot], wgu_b[...]) + bgu_ref[...]
        gate = jnp.minimum(h[:, :D_FF], SWIGLU_LIMIT)
        up = jnp.clip(h[:, D_FF:], -SWIGLU_LIMIT, SWIGLU_LIMIT)
        act = (up + 1.0) * (gate * jax.nn.sigmoid(gate * SWIGLU_ALPHA))
        y = _dot(act.astype(BF16), wd_b[...]) + bd_ref[...]

        @pl.when(blk >= 2)
        def _():
            y_copy(blk - 2, slot).wait()

        ys_ref[slot] = y.astype(BF16)
        y_copy(blk, slot).start()
        return carry

    lax.fori_loop(fb_ref[e], fb_ref[e + 1], block, 0)

    @pl.when(e == N_EXPERTS - 1)
    def _():
        y_copy(n_used - 1, (n_used - 1) % 2).wait()

        @pl.when(n_used >= 2)
        def _():
            y_copy(n_used - 2, n_used % 2).wait()

        ys_ref[0] = jnp.zeros(ys_ref.shape[1:], BF16)

        def fill(blk, carry):
            y_copy(blk, 0).start()
            y_copy(blk, 0).wait()
            return carry

        lax.fori_loop(n_used, n_blocks, fill, 0)


def _experts_call(rt, xs, wgu, bgu, wd, bd, bm, n_blocks):
    E = N_EXPERTS
    by_expert = lambda e, fb: (e, 0, 0)
    grid_spec = pltpu.PrefetchScalarGridSpec(
        num_scalar_prefetch=1,
        grid=(E,),
        in_specs=[
            pl.BlockSpec(memory_space=pl.ANY),
            pl.BlockSpec((None, D_MODEL, 2 * D_FF), by_expert),
            pl.BlockSpec((None, 1, 2 * D_FF), by_expert),
            pl.BlockSpec((None, D_FF, D_MODEL), by_expert),
            pl.BlockSpec((None, 1, D_MODEL), by_expert),
        ],
        out_specs=pl.BlockSpec(memory_space=pl.ANY),
        scratch_shapes=[
            pltpu.VMEM((2, bm, D_MODEL), BF16),
            pltpu.VMEM((2, bm, D_MODEL), BF16),
            pltpu.VMEM((D_MODEL, 2 * D_FF), BF16),
            pltpu.VMEM((D_FF, D_MODEL), BF16),
            pltpu.SemaphoreType.DMA((2,)),
            pltpu.SemaphoreType.DMA((2,)),
        ],
    )
    return pl.pallas_call(
        functools.partial(_experts_kernel, bm=bm, n_blocks=n_blocks),
        grid_spec=grid_spec,
        out_shape=jax.ShapeDtypeStruct((n_blocks * bm, D_MODEL), BF16),
        compiler_params=pltpu.CompilerParams(dimension_semantics=("arbitrary",), vmem_limit_bytes=VMEM_LIMIT),
        name="experts",
    )(rt["first_block"], xs, wgu, bgu.reshape(E, 1, 2 * D_FF), wd, bd.reshape(E, 1, D_MODEL))


def _combine_kernel(nch_ref, gbase_ref, y_hbm, x1_ref, lpos_ref, gate_ref, g_ref, b_ref, o_ref,
                    buf_ref, sem, cnt_ref):
    i = pl.program_id(0)
    nt = pl.num_programs(0)
    slot = i % 2

    @pl.when(i == 0)
    def _():
        buf_ref[...] = jnp.zeros(buf_ref.shape, BF16)
        cnt_ref[0] = _group_copies(nch_ref, gbase_ref, 0, buf_ref.at[0], y_hbm, sem.at[0], False)

    @pl.when(i + 1 < nt)
    def _():
        cnt_ref[1 - slot] = _group_copies(nch_ref, gbase_ref, i + 1, buf_ref.at[1 - slot], y_hbm,
                                          sem.at[1 - slot], False)

    _wait_chunks(cnt_ref[slot], y_hbm, sem.at[slot])
    lpos = lpos_ref[...]
    gates = gate_ref[...]
    cols = DISP_PCHUNK
    y = jnp.zeros(o_ref.shape, F32)
    for cb in range(DISP_LCAP // cols):
        c = lax.broadcasted_iota(jnp.int32, (lpos.shape[0], cols), 1) + cb * cols
        w = jnp.zeros(c.shape, F32)
        for k in range(TOP_K):
            w = jnp.where(c == lpos[:, k:k + 1], gates[:, k:k + 1], w)
        y = y + _dot(w.astype(BF16), buf_ref[slot, cb * cols:(cb + 1) * cols, :])
    o_ref[...] = _layer_norm(DN_ALPHA * x1_ref[...] + y, g_ref[...], b_ref[...])


def _combine_call(rt, y, x1, gates, g, b):
    T = x1.shape[0]
    tile = DISP_TILE
    grid_spec = pltpu.PrefetchScalarGridSpec(
        num_scalar_prefetch=2,
        grid=(T // tile,),
        in_specs=[
            pl.BlockSpec(memory_space=pl.ANY),
            pl.BlockSpec((tile, D_MODEL), lambda i, *_: (i, 0)),
            pl.BlockSpec((tile, TOP_K), lambda i, *_: (i, 0)),
            pl.BlockSpec((tile, LANES), lambda i, *_: (i, 0)),
            pl.BlockSpec(g.shape, lambda i, *_: (0, 0)),
            pl.BlockSpec(b.shape, lambda i, *_: (0, 0)),
        ],
        out_specs=pl.BlockSpec((tile, D_MODEL), lambda i, *_: (i, 0)),
        scratch_shapes=[
            pltpu.VMEM((2, DISP_LCAP, D_MODEL), BF16),
            pltpu.SemaphoreType.DMA((2,)),
            pltpu.SMEM((2,), jnp.int32),
        ],
    )
    return pl.pallas_call(
        _combine_kernel,
        grid_spec=grid_spec,
        out_shape=jax.ShapeDtypeStruct((T, D_MODEL), F32),
        compiler_params=pltpu.CompilerParams(dimension_semantics=("arbitrary",), vmem_limit_bytes=VMEM_LIMIT),
        name="combine",
    )(rt["nch"], rt["gbase"], y, x1, rt["lpos"], gates, g, b)


def _rope_lane_order(n_sub):
    half = DIFF_HEAD_DIM // 2
    per = LANES // 2 // n_sub
    assert per == half or n_sub == 1
    cols = []
    for part in range(2):
        for sub in range(n_sub):
            cols.extend(sub * 64 + part * half + d for d in range(half))
    return np.asarray(cols)


def _prep_weights(w_in, mla_q_norm_g, w_uq, mla_kv_norm_g, w_ukv):
    o_dq, o_dk, o_dv, o_cq, o_ckv, o_kr = 0, 512, 1024, 1536, 1792, 1920
    head_order = _rope_lane_order(2)
    diff_cols = np.concatenate([h * LANES + head_order for h in range(DIFF_HEADS)])
    w = {}
    ckv = w_in[:, o_ckv:o_kr]
    w["feat"] = jnp.concatenate([w_in[:, o_dq + diff_cols], w_in[:, o_dv:o_cq], w_in[:, o_cq:o_ckv], ckv],
                                axis=1).T.astype(BF16)

    def spread_rope(cols64):
        z = jnp.zeros((cols64.shape[0], 32), cols64.dtype)
        return jnp.concatenate([cols64[:, :32], z, cols64[:, 32:], z], axis=1)

    w["tok"] = jnp.concatenate([w_in[:, o_dk + diff_cols], ckv, spread_rope(w_in[:, o_kr:o_kr + MLA_ROPE_DIM])],
                               axis=1).astype(BF16)
    uq = []
    for h in range(MLA_HEADS):
        base = h * MLA_QK_DIM
        uq.append(w_uq[:, base:base + MLA_NOPE_DIM])
        uq.append(spread_rope(w_uq[:, base + MLA_NOPE_DIM:base + MLA_QK_DIM]))
    w["uqT"] = jnp.concatenate(uq, axis=1).T.astype(BF16)
    per = MLA_NOPE_DIM + MLA_V_DIM
    w["uk"] = jnp.concatenate([w_ukv[:, h * per:h * per + MLA_NOPE_DIM] for h in range(MLA_HEADS)], axis=1).astype(BF16)
    w["uvT"] = jnp.concatenate(
        [w_ukv[:, h * per + MLA_NOPE_DIM:(h + 1) * per] for h in range(MLA_HEADS)], axis=1).T.astype(BF16)
    w["gq"] = mla_q_norm_g.reshape(MLA_Q_RANK, 1)
    w["gkv"] = mla_kv_norm_g.reshape(1, MLA_KV_RANK)
    w["gkvc"] = mla_kv_norm_g.reshape(MLA_KV_RANK, 1)
    return w


def _rope_tables(positions):
    half = MLA_ROPE_DIM // 2
    inv_freq = 1.0 / (ROPE_THETA ** (jnp.arange(0, MLA_ROPE_DIM, 2, dtype=F32) / MLA_ROPE_DIM))
    ang = positions.astype(F32)[..., None] * inv_freq
    ang = jnp.tile(ang, (1, 1, LANES // half))
    sign = jnp.where(jnp.arange(LANES) < LANES // 2, -1.0, 1.0).astype(F32)
    return jnp.cos(ang), jnp.sin(ang) * sign


def kernel(x, positions, w_in, lambda_q1, lambda_k1, lambda_q2, lambda_k2, subln_g, mla_q_norm_g, w_uq,
           mla_kv_norm_g, w_ukv, w_o, ln1_g, ln1_b, w_router, b_router, w_gate_up, b_gate_up, w_down, b_down,
           ln2_g, ln2_b):
    B, S, D = x.shape
    T = B * S
    l = 0
    x2 = x.reshape(T, D)
    w = _prep_weights(w_in[l], mla_q_norm_g[l], w_uq[l], mla_kv_norm_g[l], w_ukv[l])

    dqT, dk, dvT, mqT, mk, mvT = _proj_call(x, _rope_tables(positions), w)
    lam_vecs = [v[l].reshape(1, DIFF_HEAD_DIM) for v in (lambda_q1, lambda_k1, lambda_q2, lambda_k2)]
    o_a = _attn_call(_diff_attn_kernel, lam_vecs + [subln_g[l].reshape(DIFF_V_DIM, 1)],
                     dqT, dk, dvT, DIFF_HEADS, 1, LANES, "diff_attn")
    o_b = _attn_call(_mla_attn_kernel, [], mqT, mk, mvT, MLA_HEADS // 2, 2, MLA_QK_PAD, "mla_attn")

    wr = jnp.pad(w_router[l], ((0, 0), (0, LANES - N_EXPERTS)))
    br = jnp.pad(b_router[l], (0, LANES - N_EXPERTS), constant_values=NEG_BIG).reshape(1, LANES)
    x1, idx, gates = _post_call(
        o_a.reshape(T, -1), o_b.reshape(T, -1), x2, w_o[l].astype(BF16),
        ln1_g[l].reshape(1, D), ln1_b[l].reshape(1, D), wr, br)

    bm = MOE_BM
    A = T * TOP_K
    n_tiles = T // DISP_TILE
    n_blocks = pl.cdiv(A + n_tiles * N_EXPERTS * (ROW_CHUNK - 1) + N_EXPERTS * (bm - ROW_CHUNK), bm)
    rt = _route_call(idx[:, :TOP_K].reshape(A), bm)
    lposT = rt["lpos"].reshape(n_tiles, DISP_TILE, TOP_K).transpose(0, 2, 1)
    xs = _dispatch_call(rt, x1, lposT, bm, n_blocks)
    y = _experts_call(rt, xs, w_gate_up[l], b_gate_up[l], w_down[l], b_down[l], bm, n_blocks)
    out = _combine_call(rt, y, x1, gates, ln2_g[l].reshape(1, D), ln2_b[l].reshape(1, D))
    return out.reshape(B, S, D)
```

```python
import functools
import math

import numpy as np
import jax
import jax.numpy as jnp
from jax import lax
from jax.experimental import pallas as pl
from jax.experimental.pallas import tpu as pltpu

D_MODEL = 1024
DIFF_HEADS = 4
DIFF_HEAD_DIM = 64
DIFF_V_DIM = 128
MLA_HEADS = 4
MLA_V_DIM = 128
MLA_NOPE_DIM = 128
MLA_ROPE_DIM = 64
MLA_QK_DIM = MLA_NOPE_DIM + MLA_ROPE_DIM
MLA_Q_RANK = 256
MLA_KV_RANK = 128
DIFF_Q_COLS = DIFF_K_COLS = DIFF_V_COLS = 512
ROPE_THETA = 10000.0
N_EXPERTS = 32
TOP_K = 4
D_FF = 1024
SWIGLU_LIMIT = 7.0
SWIGLU_ALPHA = 1.702
LN_EPS = 1e-5
SUBLN_EPS = 1e-5
MLA_RMS_EPS = 1e-6
DEPTH = 1
DN_ALPHA = (2.0 * DEPTH) ** 0.25
LAMBDA_INIT = 0.8 - 0.6 * math.exp(-0.3 * 0)

LANES = 128
MLA_QK_PAD = 2 * LANES
BF16_ROWS = 16
VAL_ROWS = LANES + BF16_ROWS
VMEM_LIMIT = 56 * 1024 * 1024

ATTN_TILE = 512
POST_TM = 512
MOE_BM = 256
ROW_CHUNK = BF16_ROWS
BIG_CHUNK = 32
DISP_TILE = 512
DISP_PCHUNK = 256
DISP_LCAP = -(-(DISP_TILE * TOP_K + N_EXPERTS * (ROW_CHUNK - 1)) // DISP_PCHUNK) * DISP_PCHUNK

NEG_BIG = -1e30
LOG2E = math.log2(math.e)
F32 = jnp.float32
BF16 = jnp.bfloat16


def _dot(a, b):
    return jnp.dot(a, b, preferred_element_type=F32)


def _dot_nt(a, b):
    return lax.dot_general(a, b, (((1,), (1,)), ((), ())), preferred_element_type=F32)


def _rope128(blk, cos, sin):
    return blk * cos + pltpu.roll(blk, 64, axis=1) * sin


def _rope128_t(blk, cos, sin):
    half = LANES // 2
    rolled = jnp.concatenate([blk[half:], blk[:half]], axis=0)
    return blk * cos + rolled * sin


def _rms_rows(t, g, eps):
    return t * lax.rsqrt(jnp.mean(t * t, axis=-1, keepdims=True) + eps) * g


def _rms_cols(t, g, eps):
    return t * lax.rsqrt(jnp.mean(t * t, axis=0, keepdims=True) + eps) * g


def _store_values(vT_ref, vT, heads):
    ones = jnp.ones((BF16_ROWS, vT.shape[1]), BF16)
    for h in range(heads):
        vT_ref[h * VAL_ROWS:h * VAL_ROWS + LANES, :] = vT[h * LANES:(h + 1) * LANES].astype(BF16)
        vT_ref[h * VAL_ROWS + LANES:(h + 1) * VAL_ROWS, :] = ones


def _proj_kernel(x_ref, cos_ref, sin_ref, wfeat_ref, wtok_ref, gq_ref, gkv_ref, gkvc_ref,
                 wuqT_ref, wuk_ref, wuvT_ref, dqT_ref, dk_ref, dvT_ref, mqT_ref, mk_ref, mvT_ref):
    xb = x_ref[...].astype(BF16)
    cos, sin = cos_ref[...], sin_ref[...]
    cosT, sinT = cos.T, sin.T

    dq_scale = DIFF_HEAD_DIM ** -0.5 * LOG2E
    mq_scale = MLA_QK_DIM ** -0.5 * LOG2E
    o_dv, o_cq, o_ckv = DIFF_Q_COLS, DIFF_Q_COLS + DIFF_V_COLS, DIFF_Q_COLS + DIFF_V_COLS + MLA_Q_RANK

    feat = _dot_nt(wfeat_ref[...], xb)
    for h in range(DIFF_HEADS):
        sl = slice(h * LANES, (h + 1) * LANES)
        dqT_ref[sl, :] = (_rope128_t(feat[sl], cosT, sinT) * dq_scale).astype(BF16)
    _store_values(dvT_ref, feat[o_dv:o_cq], DIFF_HEADS)

    cqT = _rms_cols(feat[o_cq:o_ckv], gq_ref[...], MLA_RMS_EPS)
    qT = _dot(wuqT_ref[...], cqT.astype(BF16))
    for h in range(MLA_HEADS):
        nope = slice(h * MLA_QK_PAD, h * MLA_QK_PAD + LANES)
        ropes = slice(h * MLA_QK_PAD + LANES, (h + 1) * MLA_QK_PAD)
        mqT_ref[nope, :] = (qT[nope] * mq_scale).astype(BF16)
        mqT_ref[ropes, :] = (_rope128_t(qT[ropes], cosT, sinT) * mq_scale).astype(BF16)

    ckvT = _rms_cols(feat[o_ckv:], gkvc_ref[...], MLA_RMS_EPS)
    _store_values(mvT_ref, _dot(wuvT_ref[...], ckvT.astype(BF16)), MLA_HEADS)

    tokm = _dot(xb, wtok_ref[...])
    for h in range(DIFF_HEADS):
        sl = slice(h * LANES, (h + 1) * LANES)
        dk_ref[:, sl] = _rope128(tokm[:, sl], cos, sin).astype(BF16)
    ckv = _rms_rows(tokm[:, DIFF_K_COLS:DIFF_K_COLS + MLA_KV_RANK], gkv_ref[...], MLA_RMS_EPS)
    k_nope = _dot(ckv.astype(BF16), wuk_ref[...])
    k_pe = _rope128(tokm[:, DIFF_K_COLS + MLA_KV_RANK:], cos, sin).astype(BF16)
    for h in range(MLA_HEADS):
        mk_ref[:, h * MLA_QK_PAD:h * MLA_QK_PAD + LANES] = k_nope[:, h * LANES:(h + 1) * LANES].astype(BF16)
        mk_ref[:, h * MLA_QK_PAD + LANES:(h + 1) * MLA_QK_PAD] = k_pe


def _proj_call(x3, tabs, w):
    B, S, D = x3.shape
    tm = ATTN_TILE
    nt = S // tm
    cos_t, sin_t = tabs
    weights = [w["feat"], w["tok"], w["gq"], w["gkv"], w["gkvc"], w["uqT"], w["uk"], w["uvT"]]
    tok = lambda b, i: (b, i, 0)
    feat = lambda b, i: (b, i, 0, 0)
    in_specs = [pl.BlockSpec((None, tm, D), tok),
                pl.BlockSpec((None, tm, LANES), tok), pl.BlockSpec((None, tm, LANES), tok)]
    in_specs += [pl.BlockSpec(a.shape, lambda b, i: (0, 0)) for a in weights]
    mq_w = MLA_HEADS * MLA_QK_PAD
    dv_w, mv_w = DIFF_HEADS * VAL_ROWS, MLA_HEADS * VAL_ROWS
    out_specs = [pl.BlockSpec((None, None, 512, tm), feat), pl.BlockSpec((None, tm, 512), tok),
                 pl.BlockSpec((None, None, dv_w, tm), feat), pl.BlockSpec((None, None, mq_w, tm), feat),
                 pl.BlockSpec((None, tm, mq_w), tok), pl.BlockSpec((None, None, mv_w, tm), feat)]
    out_shape = [jax.ShapeDtypeStruct((B, nt, 512, tm), BF16), jax.ShapeDtypeStruct((B, S, 512), BF16),
                 jax.ShapeDtypeStruct((B, nt, dv_w, tm), BF16), jax.ShapeDtypeStruct((B, nt, mq_w, tm), BF16),
                 jax.ShapeDtypeStruct((B, S, mq_w), BF16), jax.ShapeDtypeStruct((B, nt, mv_w, tm), BF16)]
    return pl.pallas_call(
        _proj_kernel,
        grid=(B, nt),
        in_specs=in_specs,
        out_specs=out_specs,
        out_shape=out_shape,
        compiler_params=pltpu.CompilerParams(dimension_semantics=("arbitrary", "arbitrary"),
                                             vmem_limit_bytes=VMEM_LIMIT),
        name="proj",
    )(x3, cos_t, sin_t, *weights)


def _flash_body(qTs, keys_of, values_of, acc_ref, m_ref, s_refs, *, t):
    i = pl.program_id(2)
    m_ref[...] = jnp.full(m_ref.shape, -jnp.inf, F32)
    acc_ref[...] = jnp.zeros(acc_ref.shape, F32)

    def scores(j, u):
        s_refs[u][...] = _dot(keys_of(u, j), qTs[u])

    def finish(j, u, masked):
        s = s_refs[u][...]
        if masked:
            key = lax.broadcasted_iota(jnp.int32, s.shape, 0)
            qry = lax.broadcasted_iota(jnp.int32, s.shape, 1)
            s = jnp.where(key <= qry, s, -jnp.inf)
        m_prev = m_ref[u]
        m_new = jnp.maximum(m_prev, jnp.max(s, axis=0, keepdims=True))
        alpha = jnp.exp2(m_prev - m_new)
        p = jnp.exp2((s - m_new).astype(BF16))
        acc_ref[u] = alpha * acc_ref[u] + _dot(values_of(u, j), p)
        m_ref[u] = m_new

    def full_step(j, carry):
        scores(j, 1)
        finish(j, 0, False)
        scores(j + 1, 0)
        finish(j, 1, False)
        return carry

    def two_steps(jj, carry):
        return full_step(2 * jj + 1, full_step(2 * jj, carry))

    scores(0, 0)
    lax.fori_loop(0, lax.shift_right_logical(i, 1), two_steps, 0)

    @pl.when(i % 2 == 1)
    def _():
        full_step(i - 1, 0)

    scores(i, 1)
    finish(i, 0, True)
    finish(i, 1, True)


def _key_tile(k_ref, j, t, lanes=slice(None)):
    return k_ref[pl.ds(pl.multiple_of(j * t, t), t), lanes]


def _normalized(acc_ref, u):
    return acc_ref[u, :LANES, :] / acc_ref[u, LANES:LANES + 1, :]


def _diff_attn_kernel(lq1_ref, lk1_ref, lq2_ref, lk2_ref, g_ref, qT_ref, k_ref, vT_ref, o_ref,
                      acc_ref, m_ref, sa_ref, sb_ref, *, t):
    qT = qT_ref[...]
    row = lax.broadcasted_iota(jnp.int32, qT.shape, 0)
    first = (row % 64) < 32
    zero = jnp.zeros_like(qT)
    qTs = (jnp.where(first, qT, zero), jnp.where(first, zero, qT))
    _flash_body(qTs, lambda u, j: _key_tile(k_ref, j, t), lambda u, j: vT_ref[j],
                acc_ref, m_ref, (sa_ref, sb_ref), t=t)

    lam = (jnp.exp(jnp.sum(lq1_ref[...] * lk1_ref[...], axis=-1, keepdims=True))
           - jnp.exp(jnp.sum(lq2_ref[...] * lk2_ref[...], axis=-1, keepdims=True)) + LAMBDA_INIT)
    oT = _normalized(acc_ref, 0) - lam * _normalized(acc_ref, 1)
    oT = _rms_cols(oT, g_ref[...], SUBLN_EPS) * (1.0 - LAMBDA_INIT)
    o_ref[...] = oT.T.astype(o_ref.dtype)


def _mla_attn_kernel(qT_ref, k_ref, vT_ref, o_ref, acc_ref, m_ref, sa_ref, sb_ref, *, t):
    dk = MLA_QK_PAD
    qTs = tuple(qT_ref[u * dk:(u + 1) * dk, :] for u in range(2))
    _flash_body(qTs, lambda u, j: _key_tile(k_ref, j, t, slice(u * dk, (u + 1) * dk)),
                lambda u, j: vT_ref[j, u * VAL_ROWS:(u + 1) * VAL_ROWS, :],
                acc_ref, m_ref, (sa_ref, sb_ref), t=t)
    for u in range(2):
        o_ref[:, u * LANES:(u + 1) * LANES] = _normalized(acc_ref, u).T.astype(o_ref.dtype)


def _attn_call(kernel, extra, qT, k, vT, groups, heads_per_group, dk, name):
    B, nt, _, t = qT.shape
    S = nt * t
    hp = heads_per_group
    in_specs = [pl.BlockSpec(a.shape, lambda b, h, i: (0, 0)) for a in extra]
    in_specs += [
        pl.BlockSpec((None, None, hp * dk, t), lambda b, h, i: (b, i, h, 0)),
        pl.BlockSpec((None, S, hp * dk), lambda b, h, i: (b, 0, h)),
        pl.BlockSpec((None, nt, hp * VAL_ROWS, t), lambda b, h, i: (b, 0, h, 0)),
    ]
    return pl.pallas_call(
        functools.partial(kernel, t=t),
        grid=(B, groups, nt),
        in_specs=in_specs,
        out_specs=pl.BlockSpec((None, t, hp * LANES), lambda b, h, i: (b, i, h)),
        out_shape=jax.ShapeDtypeStruct((B, S, groups * hp * LANES), BF16),
        scratch_shapes=[
            pltpu.VMEM((2, VAL_ROWS, t), F32),
            pltpu.VMEM((2, 1, t), F32),
            pltpu.VMEM((t, t), F32),
            pltpu.VMEM((t, t), F32),
        ],
        compiler_params=pltpu.CompilerParams(
            dimension_semantics=("arbitrary", "arbitrary", "arbitrary"), vmem_limit_bytes=VMEM_LIMIT),
        name=name,
    )(*extra, qT, k, vT)


def _layer_norm(y, g, b):
    mu = jnp.mean(y, axis=-1, keepdims=True)
    d = y - mu
    var = jnp.mean(d * d, axis=-1, keepdims=True)
    return d * lax.rsqrt(var + LN_EPS) * g + b


def _split_bf16(a):
    hi = a.astype(BF16)
    return hi, (a - hi.astype(F32)).astype(BF16)


def _post_kernel(oa_ref, ob_ref, x_ref, wo_ref, g_ref, b_ref, wrh_ref, wrl_ref, br_ref,
                 x1_ref, idx_ref, gate_ref):
    half = oa_ref.shape[1]
    mixed = _dot(oa_ref[...], wo_ref[:half, :]) + _dot(ob_ref[...], wo_ref[half:, :])
    x1 = _layer_norm(DN_ALPHA * x_ref[...] + mixed, g_ref[...], b_ref[...])
    x1_ref[...] = x1

    x_hi, x_lo = _split_bf16(x1)
    logits = (_dot(x_hi, wrh_ref[...]) + _dot(x_lo, wrh_ref[...]) + _dot(x_hi, wrl_ref[...])) + br_ref[...]
    lane = lax.broadcasted_iota(jnp.int32, logits.shape, 1).astype(F32)
    work = logits
    vals, idxs = [], []
    for _ in range(TOP_K):
        m = jnp.max(work, axis=-1, keepdims=True)
        idx = jnp.min(jnp.where(work == m, lane, float(LANES)), axis=-1, keepdims=True)
        vals.append(m)
        idxs.append(idx)
        work = jnp.where(lane == idx, NEG_BIG, work)
    es = [jnp.exp(v - vals[0]) for v in vals]
    den = es[0] + es[1] + es[2] + es[3]
    idx_out = jnp.zeros(logits.shape, F32)
    gate_out = jnp.zeros(logits.shape, F32)
    for k in range(TOP_K):
        idx_out = jnp.where(lane == float(k), idxs[k], idx_out)
        gate_out = jnp.where(lane == float(k), es[k] / den, gate_out)
    idx_ref[...] = idx_out.astype(jnp.int32)
    gate_ref[...] = gate_out


def _post_call(oa, ob, x2, wo, g, b, wr, br):
    T = x2.shape[0]
    tm = POST_TM
    row = lambda i: (i, 0)
    full = lambda i: (0, 0)
    return pl.pallas_call(
        _post_kernel,
        grid=(T // tm,),
        in_specs=[
            pl.BlockSpec((tm, oa.shape[1]), row), pl.BlockSpec((tm, ob.shape[1]), row),
            pl.BlockSpec((tm, D_MODEL), row), pl.BlockSpec(wo.shape, full),
            pl.BlockSpec(g.shape, full), pl.BlockSpec(b.shape, full),
            pl.BlockSpec(wr.shape, full), pl.BlockSpec(wr.shape, full), pl.BlockSpec(br.shape, full),
        ],
        out_specs=[
            pl.BlockSpec((tm, D_MODEL), row), pl.BlockSpec((tm, LANES), row), pl.BlockSpec((tm, LANES), row),
        ],
        out_shape=[
            jax.ShapeDtypeStruct((T, D_MODEL), F32),
            jax.ShapeDtypeStruct((T, LANES), jnp.int32), jax.ShapeDtypeStruct((T, LANES), F32),
        ],
        compiler_params=pltpu.CompilerParams(dimension_semantics=("arbitrary",), vmem_limit_bytes=VMEM_LIMIT),
        name="post",
    )(oa, ob, x2, wo, g, b, *_split_bf16(wr), br)


def _lane_cumsum(x, n):
    lane = lax.broadcasted_iota(jnp.int32, x.shape, 1)
    s = 1
    while s < n:
        x = x + jnp.where(lane >= s, pltpu.roll(x, s, axis=1), 0.0)
        s *= 2
    return x


def _route_kernel(e_ref, lpos_ref, nch_ref, gbase_ref, meta_ref, *, bm, rows_per_tile):
    e = e_ref[...]
    R = e.shape[0]
    nt = R // rows_per_tile
    r_i = lax.broadcasted_iota(jnp.int32, (LANES, LANES), 0)
    c_i = lax.broadcasted_iota(jnp.int32, (LANES, LANES), 1)
    lane_incl = (r_i <= c_i).astype(BF16)
    rr = lax.broadcasted_iota(jnp.int32, (R, R), 0)
    rc = lax.broadcasted_iota(jnp.int32, (R, R), 1)
    same_tile = (rr // rows_per_tile) == (rc // rows_per_tile)
    rows_before = jnp.logical_and(rc < rr, same_tile).astype(BF16)
    tr = lax.broadcasted_iota(jnp.int32, (nt, R), 0)
    tc = lax.broadcasted_iota(jnp.int32, (nt, R), 1)
    tile_rows = (tc // rows_per_tile == tr).astype(BF16)
    er = lax.broadcasted_iota(jnp.int32, (R, nt), 0)
    ec = lax.broadcasted_iota(jnp.int32, (R, nt), 1)
    row_tile = (er // rows_per_tile == ec).astype(F32)
    lt_r = lax.broadcasted_iota(jnp.int32, (nt, nt), 0)
    lt_c = lax.broadcasted_iota(jnp.int32, (nt, nt), 1)
    tiles_before = (lt_c < lt_r).astype(BF16)
    lane_t = lax.broadcasted_iota(jnp.int32, (nt, LANES), 1)

    rank = jnp.zeros(e.shape, F32)
    cnt = jnp.zeros((nt, LANES), F32)
    for ex in range(N_EXPERTS):
        hit = e == ex
        m = jnp.where(hit, 1.0, 0.0).astype(BF16)
        incl = _dot(m, lane_incl)
        row_tot = jnp.broadcast_to(incl[:, LANES - 1:LANES], e.shape).astype(BF16)
        before = _dot(rows_before, row_tot)
        rank = jnp.where(hit, incl - 1.0 + before, rank)
        cnt = jnp.where(lane_t == ex, _dot(tile_rows, row_tot), cnt)

    chunk = float(ROW_CHUNK)
    cnt8 = jnp.floor((cnt + (chunk - 1.0)) * (1.0 / chunk)) * chunk
    lstart = _lane_cumsum(cnt8, N_EXPERTS) - cnt8
    tile_pre = _dot(tiles_before, cnt8.astype(BF16))
    tot8 = tile_pre[nt - 1:nt] + cnt8[nt - 1:nt]
    tot_bm = jnp.floor((tot8 + (bm - 1.0)) * (1.0 / bm)) * bm
    end_incl = _lane_cumsum(tot_bm, N_EXPERTS)
    ebase = end_incl - tot_bm

    lrow = jnp.dot(row_tile, lstart, preferred_element_type=F32,
                   precision=lax.Precision.HIGHEST)
    lstart_a = jnp.zeros(e.shape, F32)
    for ex in range(N_EXPERTS):
        lstart_a = jnp.where(e == ex, lrow[:, ex:ex + 1], lstart_a)
    lpos_ref[...] = (lstart_a + rank).astype(jnp.int32)
    nch_ref[...] = (cnt8 * (1.0 / chunk)).astype(jnp.int32)
    gbase_ref[...] = (ebase + tile_pre).astype(jnp.int32)

    sub = lax.broadcasted_iota(jnp.int32, meta_ref.shape, 0)
    meta = jnp.where(sub == 0, ebase + tot8, 0.0)
    meta = jnp.where(sub == 1, (tot_bm - tot8) * (1.0 / chunk), meta)
    n_used = end_incl[:, N_EXPERTS - 1:N_EXPERTS] * (1.0 / bm)
    meta = jnp.where(sub == 2, n_used, meta)
    lane_m = lax.broadcasted_iota(jnp.int32, meta_ref.shape, 1)
    meta = jnp.where(sub == 3, jnp.where(lane_m < N_EXPERTS, ebase * (1.0 / bm), n_used), meta)
    meta_ref[...] = meta.astype(jnp.int32)


def _route_call(flat_e, bm):
    A = flat_e.shape[0]
    R = A // LANES
    rows_per_tile = DISP_TILE * TOP_K // LANES
    nt = R // rows_per_tile
    full = lambda i: (0, 0)
    shapes = [(R, LANES), (nt, LANES), (nt, LANES), (8, LANES)]
    lpos, nch, gbase, meta = pl.pallas_call(
        functools.partial(_route_kernel, bm=bm, rows_per_tile=rows_per_tile),
        grid=(1,),
        in_specs=[pl.BlockSpec((R, LANES), full)],
        out_specs=[pl.BlockSpec(s, full) for s in shapes],
        out_shape=[jax.ShapeDtypeStruct(s, jnp.int32) for s in shapes],
        compiler_params=pltpu.CompilerParams(dimension_semantics=("arbitrary",), vmem_limit_bytes=VMEM_LIMIT),
        name="route",
    )(flat_e.reshape(R, LANES))
    E = N_EXPERTS
    return dict(lpos=lpos.reshape(A // TOP_K, TOP_K), nch=nch[:, :E].reshape(-1), gbase=gbase[:, :E].reshape(-1),
                tail_start=meta[0, :E], tail_n=meta[1, :E], n_used=meta[2, :1], first_block=meta[3, :E + 1])


def _chunk_rows(ref, first, rows=ROW_CHUNK):
    return ref.at[pl.ds(pl.multiple_of(first, ROW_CHUNK), rows), :]


def _group_copies(nch_ref, gbase_ref, tile, buf, hbm, sem, to_hbm):
    per_big = BIG_CHUNK // ROW_CHUNK

    def copy(local, remote):
        if to_hbm:
            pltpu.make_async_copy(local, remote, sem).start(priority=1)
        else:
            pltpu.make_async_copy(remote, local, sem).start(priority=1)

    def per_expert(ex, done):
        n = nch_ref[tile * N_EXPERTS + ex]
        g0 = gbase_ref[tile * N_EXPERTS + ex]
        l0 = done * ROW_CHUNK
        n_big = lax.shift_right_logical(n, per_big.bit_length() - 1)

        def big(c, carry):
            copy(_chunk_rows(buf, l0 + c * BIG_CHUNK, BIG_CHUNK), _chunk_rows(hbm, g0 + c * BIG_CHUNK, BIG_CHUNK))
            return carry

        def small(c, carry):
            copy(_chunk_rows(buf, l0 + c * ROW_CHUNK), _chunk_rows(hbm, g0 + c * ROW_CHUNK))
            return carry

        lax.fori_loop(0, n_big, big, 0)
        lax.fori_loop(n_big * per_big, n, small, 0)
        return done + n

    return lax.fori_loop(0, N_EXPERTS, per_expert, 0)


def _wait_chunks(n, hbm, sem):
    rows = pl.multiple_of(n * ROW_CHUNK, ROW_CHUNK)

    @pl.when(n > 0)
    def _():
        pltpu.make_async_copy(hbm.at[pl.ds(0, rows), :], hbm.at[pl.ds(0, rows), :], sem).wait()


def _dispatch_kernel(nch_ref, gbase_ref, tstart_ref, tn_ref, nu_ref, x_ref, lposT_ref, xs_hbm,
                     buf_ref, zero_ref, sem, zsem, cnt_ref, *, bm, n_blocks):
    i = pl.program_id(0)
    nt = pl.num_programs(0)
    slot = i % 2

    @pl.when(i >= 2)
    def _():
        _wait_chunks(cnt_ref[slot], xs_hbm, sem.at[slot])

    xb = x_ref[...].astype(BF16)
    lposT = lposT_ref[...]
    rows = DISP_PCHUNK
    for rb in range(DISP_LCAP // rows):
        r = lax.broadcasted_iota(jnp.int32, (rows, xb.shape[0]), 0) + rb * rows
        hit = r == lposT[0:1]
        for k in range(1, TOP_K):
            hit = jnp.logical_or(hit, r == lposT[k:k + 1])
        perm = jnp.where(hit, 1.0, 0.0).astype(BF16)
        buf_ref[slot, rb * rows:(rb + 1) * rows, :] = _dot(perm, xb).astype(BF16)
    cnt_ref[slot] = _group_copies(nch_ref, gbase_ref, i, buf_ref.at[slot], xs_hbm, sem.at[slot], True)

    @pl.when(i == nt - 1)
    def _():
        zero_ref[...] = jnp.zeros(zero_ref.shape, BF16)

        def per_expert(ex, done):
            first = tstart_ref[ex]

            def per_chunk(c, carry):
                pltpu.make_async_copy(_chunk_rows(zero_ref, 0), _chunk_rows(xs_hbm, first + c * ROW_CHUNK),
                                      zsem.at[0]).start()
                return carry

            lax.fori_loop(0, tn_ref[ex], per_chunk, 0)
            return done + tn_ref[ex]

        n_tail = lax.fori_loop(0, N_EXPERTS, per_expert, 0)

        def per_block(b, carry):
            pltpu.make_async_copy(zero_ref, xs_hbm.at[pl.ds(pl.multiple_of(b * bm, bm), bm), :], zsem.at[1]).start()
            return carry

        lax.fori_loop(nu_ref[0], n_blocks, per_block, 0)
        _wait_chunks(n_tail, xs_hbm, zsem.at[0])

        def wait_block(b, carry):
            pltpu.make_async_copy(zero_ref, xs_hbm.at[pl.ds(0, bm), :], zsem.at[1]).wait()
            return carry

        lax.fori_loop(nu_ref[0], n_blocks, wait_block, 0)
        _wait_chunks(cnt_ref[slot], xs_hbm, sem.at[slot])

        @pl.when(nt >= 2)
        def _():
            _wait_chunks(cnt_ref[1 - slot], xs_hbm, sem.at[1 - slot])


def _dispatch_call(rt, x1, lposT, bm, n_blocks):
    T = x1.shape[0]
    tile = DISP_TILE
    grid_spec = pltpu.PrefetchScalarGridSpec(
        num_scalar_prefetch=5,
        grid=(T // tile,),
        in_specs=[
            pl.BlockSpec((tile, D_MODEL), lambda i, *_: (i, 0)),
            pl.BlockSpec((None, TOP_K, tile), lambda i, *_: (i, 0, 0)),
        ],
        out_specs=pl.BlockSpec(memory_space=pl.ANY),
        scratch_shapes=[
            pltpu.VMEM((2, DISP_LCAP, D_MODEL), BF16),
            pltpu.VMEM((bm, D_MODEL), BF16),
            pltpu.SemaphoreType.DMA((2,)),
            pltpu.SemaphoreType.DMA((2,)),
            pltpu.SMEM((2,), jnp.int32),
        ],
    )
    return pl.pallas_call(
        functools.partial(_dispatch_kernel, bm=bm, n_blocks=n_blocks),
        grid_spec=grid_spec,
        out_shape=jax.ShapeDtypeStruct((n_blocks * bm, D_MODEL), BF16),
        compiler_params=pltpu.CompilerParams(dimension_semantics=("arbitrary",), vmem_limit_bytes=VMEM_LIMIT),
        name="dispatch",
    )(rt["nch"], rt["gbase"], rt["tail_start"], rt["tail_n"], rt["n_used"], x1, lposT)


def _experts_kernel(fb_ref, xs_hbm, wgu_ref, bgu_ref, wd_ref, bd_ref, y_hbm,
                    xs_ref, ys_ref, wgu_b, wd_b, xsem, ysem, *, bm, n_blocks):
    e = pl.program_id(0)
    n_used = fb_ref[N_EXPERTS]

    def x_copy(blk, slot):
        return pltpu.make_async_copy(xs_hbm.at[pl.ds(pl.multiple_of(blk * bm, bm), bm), :], xs_ref.at[slot],
                                     xsem.at[slot])

    def y_copy(blk, slot):
        return pltpu.make_async_copy(ys_ref.at[slot], y_hbm.at[pl.ds(pl.multiple_of(blk * bm, bm), bm), :],
                                     ysem.at[slot])

    @pl.when(e == 0)
    def _():
        x_copy(0, 0).start()

    wgu_b[...] = wgu_ref[...].astype(BF16)
    wd_b[...] = wd_ref[...].astype(BF16)

    def block(blk, carry):
        slot = blk % 2
        x_copy(blk, slot).wait()

        @pl.when(blk + 1 < n_used)
        def _():
            x_copy(blk + 1, 1 - slot).start(priority=1)

        h = _dot(xs_ref[slot], wgu_b[...]) + bgu_ref[...]
        gate = jnp.minimum(h[:, :D_FF], SWIGLU_LIMIT)
        up = jnp.clip(h[:, D_FF:], -SWIGLU_LIMIT, SWIGLU_LIMIT)
        act = (up + 1.0) * (gate * jax.nn.sigmoid(gate * SWIGLU_ALPHA))
        y = _dot(act.astype(BF16), wd_b[...]) + bd_ref[...]

        @pl.when(blk >= 2)
        def _():
            y_copy(blk - 2, slot).wait()

        ys_ref[slot] = y.astype(BF16)
        y_copy(blk, slot).start(priority=1)
        return carry

    lax.fori_loop(fb_ref[e], fb_ref[e + 1], block, 0)

    @pl.when(e == N_EXPERTS - 1)
    def _():
        y_copy(n_used - 1, (n_used - 1) % 2).wait()

        @pl.when(n_used >= 2)
        def _():
            y_copy(n_used - 2, n_used % 2).wait()

        ys_ref[0] = jnp.zeros(ys_ref.shape[1:], BF16)

        def fill(blk, carry):
            y_copy(blk, 0).start()
            y_copy(blk, 0).wait()
            return carry

        lax.fori_loop(n_used, n_blocks, fill, 0)


def _experts_call(rt, xs, wgu, bgu, wd, bd, bm, n_blocks):
    E = N_EXPERTS
    by_expert = lambda e, fb: (e, 0, 0)
    grid_spec = pltpu.PrefetchScalarGridSpec(
        num_scalar_prefetch=1,
        grid=(E,),
        in_specs=[
            pl.BlockSpec(memory_space=pl.ANY),
            pl.BlockSpec((None, D_MODEL, 2 * D_FF), by_expert),
            pl.BlockSpec((None, 1, 2 * D_FF), by_expert),
            pl.BlockSpec((None, D_FF, D_MODEL), by_expert),
            pl.BlockSpec((None, 1, D_MODEL), by_expert),
        ],
        out_specs=pl.BlockSpec(memory_space=pl.ANY),
        scratch_shapes=[
            pltpu.VMEM((2, bm, D_MODEL), BF16),
            pltpu.VMEM((2, bm, D_MODEL), BF16),
            pltpu.VMEM((D_MODEL, 2 * D_FF), BF16),
            pltpu.VMEM((D_FF, D_MODEL), BF16),
            pltpu.SemaphoreType.DMA((2,)),
            pltpu.SemaphoreType.DMA((2,)),
        ],
    )
    return pl.pallas_call(
        functools.partial(_experts_kernel, bm=bm, n_blocks=n_blocks),
        grid_spec=grid_spec,
        out_shape=jax.ShapeDtypeStruct((n_blocks * bm, D_MODEL), BF16),
        compiler_params=pltpu.CompilerParams(dimension_semantics=("arbitrary",), vmem_limit_bytes=VMEM_LIMIT),
        name="experts",
    )(rt["first_block"], xs, wgu, bgu.reshape(E, 1, 2 * D_FF), wd, bd.reshape(E, 1, D_MODEL))


def _combine_kernel(nch_ref, gbase_ref, y_hbm, x1_ref, lpos_ref, gate_ref, g_ref, b_ref, o_ref,
                    buf_ref, sem, cnt_ref):
    i = pl.program_id(0)
    nt = pl.num_programs(0)
    slot = i % 2

    @pl.when(i == 0)
    def _():
        buf_ref[...] = jnp.zeros(buf_ref.shape, BF16)
        cnt_ref[0] = _group_copies(nch_ref, gbase_ref, 0, buf_ref.at[0], y_hbm, sem.at[0], False)

    @pl.when(i + 1 < nt)
    def _():
        cnt_ref[1 - slot] = _group_copies(nch_ref, gbase_ref, i + 1, buf_ref.at[1 - slot], y_hbm,
                                          sem.at[1 - slot], False)

    _wait_chunks(cnt_ref[slot], y_hbm, sem.at[slot])
    lpos = lpos_ref[...]
    gates = gate_ref[...]
    cols = DISP_PCHUNK
    y = jnp.zeros(o_ref.shape, F32)
    for cb in range(DISP_LCAP // cols):
        c = lax.broadcasted_iota(jnp.int32, (lpos.shape[0], cols), 1) + cb * cols
        w = jnp.zeros(c.shape, F32)
        for k in range(TOP_K):
            w = jnp.where(c == lpos[:, k:k + 1], gates[:, k:k + 1], w)
        y = y + _dot(w.astype(BF16), buf_ref[slot, cb * cols:(cb + 1) * cols, :])
    o_ref[...] = _layer_norm(DN_ALPHA * x1_ref[...] + y, g_ref[...], b_ref[...])


def _combine_call(rt, y, x1, gates, g, b):
    T = x1.shape[0]
    tile = DISP_TILE
    grid_spec = pltpu.PrefetchScalarGridSpec(
        num_scalar_prefetch=2,
        grid=(T // tile,),
        in_specs=[
            pl.BlockSpec(memory_space=pl.ANY),
            pl.BlockSpec((tile, D_MODEL), lambda i, *_: (i, 0)),
            pl.BlockSpec((tile, TOP_K), lambda i, *_: (i, 0)),
            pl.BlockSpec((tile, LANES), lambda i, *_: (i, 0)),
            pl.BlockSpec(g.shape, lambda i, *_: (0, 0)),
            pl.BlockSpec(b.shape, lambda i, *_: (0, 0)),
        ],
        out_specs=pl.BlockSpec((tile, D_MODEL), lambda i, *_: (i, 0)),
        scratch_shapes=[
            pltpu.VMEM((2, DISP_LCAP, D_MODEL), BF16),
            pltpu.SemaphoreType.DMA((2,)),
            pltpu.SMEM((2,), jnp.int32),
        ],
    )
    return pl.pallas_call(
        _combine_kernel,
        grid_spec=grid_spec,
        out_shape=jax.ShapeDtypeStruct((T, D_MODEL), F32),
        compiler_params=pltpu.CompilerParams(dimension_semantics=("arbitrary",), vmem_limit_bytes=VMEM_LIMIT),
        name="combine",
    )(rt["nch"], rt["gbase"], y, x1, rt["lpos"], gates, g, b)


def _rope_lane_order(n_sub):
    half = DIFF_HEAD_DIM // 2
    per = LANES // 2 // n_sub
    assert per == half or n_sub == 1
    cols = []
    for part in range(2):
        for sub in range(n_sub):
            cols.extend(sub * 64 + part * half + d for d in range(half))
    return np.asarray(cols)


def _prep_weights(w_in, mla_q_norm_g, w_uq, mla_kv_norm_g, w_ukv):
    o_dq, o_dk, o_dv, o_cq, o_ckv, o_kr = 0, 512, 1024, 1536, 1792, 1920
    head_order = _rope_lane_order(2)
    diff_cols = np.concatenate([h * LANES + head_order for h in range(DIFF_HEADS)])
    w = {}
    ckv = w_in[:, o_ckv:o_kr]
    w["feat"] = jnp.concatenate([w_in[:, o_dq + diff_cols], w_in[:, o_dv:o_cq], w_in[:, o_cq:o_ckv], ckv],
                                axis=1).T.astype(BF16)

    def spread_rope(cols64):
        z = jnp.zeros((cols64.shape[0], 32), cols64.dtype)
        return jnp.concatenate([cols64[:, :32], z, cols64[:, 32:], z], axis=1)

    w["tok"] = jnp.concatenate([w_in[:, o_dk + diff_cols], ckv, spread_rope(w_in[:, o_kr:o_kr + MLA_ROPE_DIM])],
                               axis=1).astype(BF16)
    uq = []
    for h in range(MLA_HEADS):
        base = h * MLA_QK_DIM
        uq.append(w_uq[:, base:base + MLA_NOPE_DIM])
        uq.append(spread_rope(w_uq[:, base + MLA_NOPE_DIM:base + MLA_QK_DIM]))
    w["uqT"] = jnp.concatenate(uq, axis=1).T.astype(BF16)
    per = MLA_NOPE_DIM + MLA_V_DIM
    w["uk"] = jnp.concatenate([w_ukv[:, h * per:h * per + MLA_NOPE_DIM] for h in range(MLA_HEADS)], axis=1).astype(BF16)
    w["uvT"] = jnp.concatenate(
        [w_ukv[:, h * per + MLA_NOPE_DIM:(h + 1) * per] for h in range(MLA_HEADS)], axis=1).T.astype(BF16)
    w["gq"] = mla_q_norm_g.reshape(MLA_Q_RANK, 1)
    w["gkv"] = mla_kv_norm_g.reshape(1, MLA_KV_RANK)
    w["gkvc"] = mla_kv_norm_g.reshape(MLA_KV_RANK, 1)
    return w


def _rope_tables(positions):
    half = MLA_ROPE_DIM // 2
    inv_freq = 1.0 / (ROPE_THETA ** (jnp.arange(0, MLA_ROPE_DIM, 2, dtype=F32) / MLA_ROPE_DIM))
    ang = positions.astype(F32)[..., None] * inv_freq
    ang = jnp.tile(ang, (1, 1, LANES // half))
    sign = jnp.where(jnp.arange(LANES) < LANES // 2, -1.0, 1.0).astype(F32)
    return jnp.cos(ang), jnp.sin(ang) * sign


def kernel(x, positions, w_in, lambda_q1, lambda_k1, lambda_q2, lambda_k2, subln_g, mla_q_norm_g, w_uq,
           mla_kv_norm_g, w_ukv, w_o, ln1_g, ln1_b, w_router, b_router, w_gate_up, b_gate_up, w_down, b_down,
           ln2_g, ln2_b):
    B, S, D = x.shape
    T = B * S
    l = 0
    x2 = x.reshape(T, D)
    w = _prep_weights(w_in[l], mla_q_norm_g[l], w_uq[l], mla_kv_norm_g[l], w_ukv[l])

    dqT, dk, dvT, mqT, mk, mvT = _proj_call(x, _rope_tables(positions), w)
    lam_vecs = [v[l].reshape(1, DIFF_HEAD_DIM) for v in (lambda_q1, lambda_k1, lambda_q2, lambda_k2)]
    o_a = _attn_call(_diff_attn_kernel, lam_vecs + [subln_g[l].reshape(DIFF_V_DIM, 1)],
                     dqT, dk, dvT, DIFF_HEADS, 1, LANES, "diff_attn")
    o_b = _attn_call(_mla_attn_kernel, [], mqT, mk, mvT, MLA_HEADS // 2, 2, MLA_QK_PAD, "mla_attn")

    wr = jnp.pad(w_router[l], ((0, 0), (0, LANES - N_EXPERTS)))
    br = jnp.pad(b_router[l], (0, LANES - N_EXPERTS), constant_values=NEG_BIG).reshape(1, LANES)
    x1, idx, gates = _post_call(
        o_a.reshape(T, -1), o_b.reshape(T, -1), x2, w_o[l].astype(BF16),
        ln1_g[l].reshape(1, D), ln1_b[l].reshape(1, D), wr, br)

    bm = MOE_BM
    A = T * TOP_K
    n_tiles = T // DISP_TILE
    n_blocks = pl.cdiv(A + n_tiles * N_EXPERTS * (ROW_CHUNK - 1) + N_EXPERTS * (bm - ROW_CHUNK), bm)
    rt = _route_call(idx[:, :TOP_K].reshape(A), bm)
    lposT = rt["lpos"].reshape(n_tiles, DISP_TILE, TOP_K).transpose(0, 2, 1)
    xs = _dispatch_call(rt, x1, lposT, bm, n_blocks)
    y = _experts_call(rt, xs, w_gate_up[l], b_gate_up[l], w_down[l], b_down[l], bm, n_blocks)
    out = _combine_call(rt, y, x1, gates, ln2_g[l].reshape(1, D), ln2_b[l].reshape(1, D))
    return out.reshape(B, S, D)
```

```python
import functools
import math

import numpy as np
import jax
import jax.numpy as jnp
from jax import lax
from jax.experimental import pallas as pl
from jax.experimental.pallas import tpu as pltpu

D_MODEL = 1024
DIFF_HEADS = 4
DIFF_HEAD_DIM = 64
DIFF_V_DIM = 128
MLA_HEADS = 4
MLA_V_DIM = 128
MLA_NOPE_DIM = 128
MLA_ROPE_DIM = 64
MLA_QK_DIM = MLA_NOPE_DIM + MLA_ROPE_DIM
MLA_Q_RANK = 256
MLA_KV_RANK = 128
DIFF_Q_COLS = DIFF_K_COLS = DIFF_V_COLS = 512
ROPE_THETA = 10000.0
N_EXPERTS = 32
TOP_K = 4
D_FF = 1024
SWIGLU_LIMIT = 7.0
SWIGLU_ALPHA = 1.702
LN_EPS = 1e-5
SUBLN_EPS = 1e-5
MLA_RMS_EPS = 1e-6
DEPTH = 1
DN_ALPHA = (2.0 * DEPTH) ** 0.25
LAMBDA_INIT = 0.8 - 0.6 * math.exp(-0.3 * 0)

LANES = 128
MLA_QK_PAD = 2 * LANES
BF16_ROWS = 16
VAL_ROWS = LANES + BF16_ROWS
VMEM_LIMIT = 56 * 1024 * 1024

ATTN_TILE = 512
POST_TM = 512
MOE_BM = 256
X_AHEAD = 3
ROW_CHUNK = BF16_ROWS
BIG_CHUNK = 32
DISP_TILE = 512
DISP_PCHUNK = 256
DISP_LCAP = -(-(DISP_TILE * TOP_K + N_EXPERTS * (ROW_CHUNK - 1)) // DISP_PCHUNK) * DISP_PCHUNK

NEG_BIG = -1e30
LOG2E = math.log2(math.e)
F32 = jnp.float32
BF16 = jnp.bfloat16


def _dot(a, b):
    return jnp.dot(a, b, preferred_element_type=F32)


def _dot_nt(a, b):
    return lax.dot_general(a, b, (((1,), (1,)), ((), ())), preferred_element_type=F32)


def _rope128(blk, cos, sin):
    return blk * cos + pltpu.roll(blk, 64, axis=1) * sin


def _rope128_t(blk, cos, sin):
    half = LANES // 2
    rolled = jnp.concatenate([blk[half:], blk[:half]], axis=0)
    return blk * cos + rolled * sin


def _rms_rows(t, g, eps):
    return t * lax.rsqrt(jnp.mean(t * t, axis=-1, keepdims=True) + eps) * g


def _rms_cols(t, g, eps):
    return t * lax.rsqrt(jnp.mean(t * t, axis=0, keepdims=True) + eps) * g


def _store_values(vT_ref, vT, heads):
    ones = jnp.ones((BF16_ROWS, vT.shape[1]), BF16)
    for h in range(heads):
        vT_ref[h * VAL_ROWS:h * VAL_ROWS + LANES, :] = vT[h * LANES:(h + 1) * LANES].astype(BF16)
        vT_ref[h * VAL_ROWS + LANES:(h + 1) * VAL_ROWS, :] = ones


def _proj_kernel(x_ref, cos_ref, sin_ref, wfeat_ref, wtok_ref, gq_ref, gkv_ref, gkvc_ref,
                 wuqT_ref, wuk_ref, wuvT_ref, dqT_ref, dk_ref, dvT_ref, mqT_ref, mk_ref, mvT_ref):
    xb = x_ref[...].astype(BF16)
    cos, sin = cos_ref[...], sin_ref[...]
    cosT, sinT = cos.T, sin.T

    dq_scale = DIFF_HEAD_DIM ** -0.5 * LOG2E
    mq_scale = MLA_QK_DIM ** -0.5 * LOG2E
    o_dv, o_cq, o_ckv = DIFF_Q_COLS, DIFF_Q_COLS + DIFF_V_COLS, DIFF_Q_COLS + DIFF_V_COLS + MLA_Q_RANK

    feat = _dot_nt(wfeat_ref[...], xb)
    for h in range(DIFF_HEADS):
        sl = slice(h * LANES, (h + 1) * LANES)
        dqT_ref[sl, :] = (_rope128_t(feat[sl], cosT, sinT) * dq_scale).astype(BF16)
    _store_values(dvT_ref, feat[o_dv:o_cq], DIFF_HEADS)

    cqT = _rms_cols(feat[o_cq:o_ckv], gq_ref[...], MLA_RMS_EPS)
    qT = _dot(wuqT_ref[...], cqT.astype(BF16))
    for h in range(MLA_HEADS):
        nope = slice(h * MLA_QK_PAD, h * MLA_QK_PAD + LANES)
        ropes = slice(h * MLA_QK_PAD + LANES, (h + 1) * MLA_QK_PAD)
        mqT_ref[nope, :] = (qT[nope] * mq_scale).astype(BF16)
        mqT_ref[ropes, :] = (_rope128_t(qT[ropes], cosT, sinT) * mq_scale).astype(BF16)

    ckvT = _rms_cols(feat[o_ckv:], gkvc_ref[...], MLA_RMS_EPS)
    _store_values(mvT_ref, _dot(wuvT_ref[...], ckvT.astype(BF16)), MLA_HEADS)

    tokm = _dot(xb, wtok_ref[...])
    for h in range(DIFF_HEADS):
        sl = slice(h * LANES, (h + 1) * LANES)
        dk_ref[:, sl] = _rope128(tokm[:, sl], cos, sin).astype(BF16)
    ckv = _rms_rows(tokm[:, DIFF_K_COLS:DIFF_K_COLS + MLA_KV_RANK], gkv_ref[...], MLA_RMS_EPS)
    k_nope = _dot(ckv.astype(BF16), wuk_ref[...])
    k_pe = _rope128(tokm[:, DIFF_K_COLS + MLA_KV_RANK:], cos, sin).astype(BF16)
    for h in range(MLA_HEADS):
        mk_ref[:, h * MLA_QK_PAD:h * MLA_QK_PAD + LANES] = k_nope[:, h * LANES:(h + 1) * LANES].astype(BF16)
        mk_ref[:, h * MLA_QK_PAD + LANES:(h + 1) * MLA_QK_PAD] = k_pe


def _proj_call(x3, tabs, w):
    B, S, D = x3.shape
    tm = ATTN_TILE
    nt = S // tm
    cos_t, sin_t = tabs
    weights = [w["feat"], w["tok"], w["gq"], w["gkv"], w["gkvc"], w["uqT"], w["uk"], w["uvT"]]
    tok = lambda b, i: (b, i, 0)
    feat = lambda b, i: (b, i, 0, 0)
    in_specs = [pl.BlockSpec((None, tm, D), tok),
                pl.BlockSpec((None, tm, LANES), tok), pl.BlockSpec((None, tm, LANES), tok)]
    in_specs += [pl.BlockSpec(a.shape, lambda b, i: (0, 0)) for a in weights]
    mq_w = MLA_HEADS * MLA_QK_PAD
    dv_w, mv_w = DIFF_HEADS * VAL_ROWS, MLA_HEADS * VAL_ROWS
    out_specs = [pl.BlockSpec((None, None, 512, tm), feat), pl.BlockSpec((None, tm, 512), tok),
                 pl.BlockSpec((None, None, dv_w, tm), feat), pl.BlockSpec((None, None, mq_w, tm), feat),
                 pl.BlockSpec((None, tm, mq_w), tok), pl.BlockSpec((None, None, mv_w, tm), feat)]
    out_shape = [jax.ShapeDtypeStruct((B, nt, 512, tm), BF16), jax.ShapeDtypeStruct((B, S, 512), BF16),
                 jax.ShapeDtypeStruct((B, nt, dv_w, tm), BF16), jax.ShapeDtypeStruct((B, nt, mq_w, tm), BF16),
                 jax.ShapeDtypeStruct((B, S, mq_w), BF16), jax.ShapeDtypeStruct((B, nt, mv_w, tm), BF16)]
    return pl.pallas_call(
        _proj_kernel,
        grid=(B, nt),
        in_specs=in_specs,
        out_specs=out_specs,
        out_shape=out_shape,
        compiler_params=pltpu.CompilerParams(dimension_semantics=("arbitrary", "arbitrary"),
                                             vmem_limit_bytes=VMEM_LIMIT),
        name="proj",
    )(x3, cos_t, sin_t, *weights)


def _flash_body(qTs, keys_of, values_of, acc_ref, m_ref, s_refs, *, t):
    i = pl.program_id(2)
    m_ref[...] = jnp.full(m_ref.shape, -jnp.inf, F32)
    acc_ref[...] = jnp.zeros(acc_ref.shape, F32)

    def scores(j, u):
        s_refs[u][...] = _dot(keys_of(u, j), qTs[u])

    def finish(j, u, masked):
        s = s_refs[u][...]
        if masked:
            key = lax.broadcasted_iota(jnp.int32, s.shape, 0)
            qry = lax.broadcasted_iota(jnp.int32, s.shape, 1)
            s = jnp.where(key <= qry, s, -jnp.inf)
        m_prev = m_ref[u]
        m_new = jnp.maximum(m_prev, jnp.max(s, axis=0, keepdims=True))
        alpha = jnp.exp2(m_prev - m_new)
        p = jnp.exp2((s - m_new).astype(BF16))
        acc_ref[u] = alpha * acc_ref[u] + _dot(values_of(u, j), p)
        m_ref[u] = m_new

    def full_step(j, carry):
        scores(j, 1)
        finish(j, 0, False)
        scores(j + 1, 0)
        finish(j, 1, False)
        return carry

    def two_steps(jj, carry):
        return full_step(2 * jj + 1, full_step(2 * jj, carry))

    scores(0, 0)
    lax.fori_loop(0, lax.shift_right_logical(i, 1), two_steps, 0)

    @pl.when(i % 2 == 1)
    def _():
        full_step(i - 1, 0)

    scores(i, 1)
    finish(i, 0, True)
    finish(i, 1, True)


def _key_tile(k_ref, j, t, lanes=slice(None)):
    return k_ref[pl.ds(pl.multiple_of(j * t, t), t), lanes]


def _normalized(acc_ref, u):
    return acc_ref[u, :LANES, :] / acc_ref[u, LANES:LANES + 1, :]


def _diff_attn_kernel(lq1_ref, lk1_ref, lq2_ref, lk2_ref, g_ref, qT_ref, k_ref, vT_ref, o_ref,
                      acc_ref, m_ref, sa_ref, sb_ref, *, t):
    qT = qT_ref[...]
    row = lax.broadcasted_iota(jnp.int32, qT.shape, 0)
    first = (row % 64) < 32
    zero = jnp.zeros_like(qT)
    qTs = (jnp.where(first, qT, zero), jnp.where(first, zero, qT))
    _flash_body(qTs, lambda u, j: _key_tile(k_ref, j, t), lambda u, j: vT_ref[j],
                acc_ref, m_ref, (sa_ref, sb_ref), t=t)

    lam = (jnp.exp(jnp.sum(lq1_ref[...] * lk1_ref[...], axis=-1, keepdims=True))
           - jnp.exp(jnp.sum(lq2_ref[...] * lk2_ref[...], axis=-1, keepdims=True)) + LAMBDA_INIT)
    oT = _normalized(acc_ref, 0) - lam * _normalized(acc_ref, 1)
    oT = _rms_cols(oT, g_ref[...], SUBLN_EPS) * (1.0 - LAMBDA_INIT)
    o_ref[...] = oT.T.astype(o_ref.dtype)


def _mla_attn_kernel(qT_ref, k_ref, vT_ref, o_ref, acc_ref, m_ref, sa_ref, sb_ref, *, t):
    dk = MLA_QK_PAD
    qTs = tuple(qT_ref[u * dk:(u + 1) * dk, :] for u in range(2))
    _flash_body(qTs, lambda u, j: _key_tile(k_ref, j, t, slice(u * dk, (u + 1) * dk)),
                lambda u, j: vT_ref[j, u * VAL_ROWS:(u + 1) * VAL_ROWS, :],
                acc_ref, m_ref, (sa_ref, sb_ref), t=t)
    for u in range(2):
        o_ref[:, u * LANES:(u + 1) * LANES] = _normalized(acc_ref, u).T.astype(o_ref.dtype)


def _attn_call(kernel, extra, qT, k, vT, groups, heads_per_group, dk, name):
    B, nt, _, t = qT.shape
    S = nt * t
    hp = heads_per_group
    in_specs = [pl.BlockSpec(a.shape, lambda b, h, i: (0, 0)) for a in extra]
    in_specs += [
        pl.BlockSpec((None, None, hp * dk, t), lambda b, h, i: (b, i, h, 0)),
        pl.BlockSpec((None, S, hp * dk), lambda b, h, i: (b, 0, h)),
        pl.BlockSpec((None, nt, hp * VAL_ROWS, t), lambda b, h, i: (b, 0, h, 0)),
    ]
    return pl.pallas_call(
        functools.partial(kernel, t=t),
        grid=(B, groups, nt),
        in_specs=in_specs,
        out_specs=pl.BlockSpec((None, t, hp * LANES), lambda b, h, i: (b, i, h)),
        out_shape=jax.ShapeDtypeStruct((B, S, groups * hp * LANES), BF16),
        scratch_shapes=[
            pltpu.VMEM((2, VAL_ROWS, t), F32),
            pltpu.VMEM((2, 1, t), F32),
            pltpu.VMEM((t, t), F32),
            pltpu.VMEM((t, t), F32),
        ],
        compiler_params=pltpu.CompilerParams(
            dimension_semantics=("arbitrary", "arbitrary", "arbitrary"), vmem_limit_bytes=VMEM_LIMIT),
        name=name,
    )(*extra, qT, k, vT)


def _layer_norm(y, g, b):
    mu = jnp.mean(y, axis=-1, keepdims=True)
    d = y - mu
    var = jnp.mean(d * d, axis=-1, keepdims=True)
    return d * lax.rsqrt(var + LN_EPS) * g + b


def _split_bf16(a):
    hi = a.astype(BF16)
    return hi, (a - hi.astype(F32)).astype(BF16)


def _post_kernel(oa_ref, ob_ref, x_ref, wo_ref, g_ref, b_ref, wrh_ref, wrl_ref, br_ref,
                 x1_ref, idx_ref, gate_ref):
    half = oa_ref.shape[1]
    mixed = _dot(oa_ref[...], wo_ref[:half, :]) + _dot(ob_ref[...], wo_ref[half:, :])
    x1 = _layer_norm(DN_ALPHA * x_ref[...] + mixed, g_ref[...], b_ref[...])
    x1_ref[...] = x1

    x_hi, x_lo = _split_bf16(x1)
    logits = (_dot(x_hi, wrh_ref[...]) + _dot(x_lo, wrh_ref[...]) + _dot(x_hi, wrl_ref[...])) + br_ref[...]
    lane = lax.broadcasted_iota(jnp.int32, logits.shape, 1).astype(F32)
    work = logits
    vals, idxs = [], []
    for _ in range(TOP_K):
        m = jnp.max(work, axis=-1, keepdims=True)
        idx = jnp.min(jnp.where(work == m, lane, float(LANES)), axis=-1, keepdims=True)
        vals.append(m)
        idxs.append(idx)
        work = jnp.where(lane == idx, NEG_BIG, work)
    es = [jnp.exp(v - vals[0]) for v in vals]
    den = es[0] + es[1] + es[2] + es[3]
    idx_out = jnp.zeros(logits.shape, F32)
    gate_out = jnp.zeros(logits.shape, F32)
    for k in range(TOP_K):
        idx_out = jnp.where(lane == float(k), idxs[k], idx_out)
        gate_out = jnp.where(lane == float(k), es[k] / den, gate_out)
    idx_ref[...] = idx_out.astype(jnp.int32)
    gate_ref[...] = gate_out


def _post_call(oa, ob, x2, wo, g, b, wr, br):
    T = x2.shape[0]
    tm = POST_TM
    row = lambda i: (i, 0)
    full = lambda i: (0, 0)
    return pl.pallas_call(
        _post_kernel,
        grid=(T // tm,),
        in_specs=[
            pl.BlockSpec((tm, oa.shape[1]), row), pl.BlockSpec((tm, ob.shape[1]), row),
            pl.BlockSpec((tm, D_MODEL), row), pl.BlockSpec(wo.shape, full),
            pl.BlockSpec(g.shape, full), pl.BlockSpec(b.shape, full),
            pl.BlockSpec(wr.shape, full), pl.BlockSpec(wr.shape, full), pl.BlockSpec(br.shape, full),
        ],
        out_specs=[
            pl.BlockSpec((tm, D_MODEL), row), pl.BlockSpec((tm, LANES), row), pl.BlockSpec((tm, LANES), row),
        ],
        out_shape=[
            jax.ShapeDtypeStruct((T, D_MODEL), F32),
            jax.ShapeDtypeStruct((T, LANES), jnp.int32), jax.ShapeDtypeStruct((T, LANES), F32),
        ],
        compiler_params=pltpu.CompilerParams(dimension_semantics=("arbitrary",), vmem_limit_bytes=VMEM_LIMIT),
        name="post",
    )(oa, ob, x2, wo, g, b, *_split_bf16(wr), br)


def _lane_cumsum(x, n):
    lane = lax.broadcasted_iota(jnp.int32, x.shape, 1)
    s = 1
    while s < n:
        x = x + jnp.where(lane >= s, pltpu.roll(x, s, axis=1), 0.0)
        s *= 2
    return x


def _route_kernel(e_ref, lpos_ref, nch_ref, gbase_ref, meta_ref, *, bm, rows_per_tile):
    e = e_ref[...]
    R = e.shape[0]
    nt = R // rows_per_tile
    r_i = lax.broadcasted_iota(jnp.int32, (LANES, LANES), 0)
    c_i = lax.broadcasted_iota(jnp.int32, (LANES, LANES), 1)
    lane_incl = (r_i <= c_i).astype(BF16)
    rr = lax.broadcasted_iota(jnp.int32, (R, R), 0)
    rc = lax.broadcasted_iota(jnp.int32, (R, R), 1)
    same_tile = (rr // rows_per_tile) == (rc // rows_per_tile)
    rows_before = jnp.logical_and(rc < rr, same_tile).astype(BF16)
    tr = lax.broadcasted_iota(jnp.int32, (nt, R), 0)
    tc = lax.broadcasted_iota(jnp.int32, (nt, R), 1)
    tile_rows = (tc // rows_per_tile == tr).astype(BF16)
    er = lax.broadcasted_iota(jnp.int32, (R, nt), 0)
    ec = lax.broadcasted_iota(jnp.int32, (R, nt), 1)
    row_tile = (er // rows_per_tile == ec).astype(F32)
    lt_r = lax.broadcasted_iota(jnp.int32, (nt, nt), 0)
    lt_c = lax.broadcasted_iota(jnp.int32, (nt, nt), 1)
    tiles_before = (lt_c < lt_r).astype(BF16)
    lane_t = lax.broadcasted_iota(jnp.int32, (nt, LANES), 1)

    rank = jnp.zeros(e.shape, F32)
    cnt = jnp.zeros((nt, LANES), F32)
    for ex in range(N_EXPERTS):
        hit = e == ex
        m = jnp.where(hit, 1.0, 0.0).astype(BF16)
        incl = _dot(m, lane_incl)
        row_tot = jnp.broadcast_to(incl[:, LANES - 1:LANES], e.shape).astype(BF16)
        before = _dot(rows_before, row_tot)
        rank = jnp.where(hit, incl - 1.0 + before, rank)
        cnt = jnp.where(lane_t == ex, _dot(tile_rows, row_tot), cnt)

    chunk = float(ROW_CHUNK)
    cnt8 = jnp.floor((cnt + (chunk - 1.0)) * (1.0 / chunk)) * chunk
    lstart = _lane_cumsum(cnt8, N_EXPERTS) - cnt8
    tile_pre = _dot(tiles_before, cnt8.astype(BF16))
    tot8 = tile_pre[nt - 1:nt] + cnt8[nt - 1:nt]
    tot_bm = jnp.floor((tot8 + (bm - 1.0)) * (1.0 / bm)) * bm
    end_incl = _lane_cumsum(tot_bm, N_EXPERTS)
    ebase = end_incl - tot_bm

    lrow = jnp.dot(row_tile, lstart, preferred_element_type=F32,
                   precision=lax.Precision.HIGHEST)
    lstart_a = jnp.zeros(e.shape, F32)
    for ex in range(N_EXPERTS):
        lstart_a = jnp.where(e == ex, lrow[:, ex:ex + 1], lstart_a)
    lpos_ref[...] = (lstart_a + rank).astype(jnp.int32)
    nch_ref[...] = (cnt8 * (1.0 / chunk)).astype(jnp.int32)
    gbase_ref[...] = (ebase + tile_pre).astype(jnp.int32)

    sub = lax.broadcasted_iota(jnp.int32, meta_ref.shape, 0)
    meta = jnp.where(sub == 0, ebase + tot8, 0.0)
    meta = jnp.where(sub == 1, (tot_bm - tot8) * (1.0 / chunk), meta)
    n_used = end_incl[:, N_EXPERTS - 1:N_EXPERTS] * (1.0 / bm)
    meta = jnp.where(sub == 2, n_used, meta)
    lane_m = lax.broadcasted_iota(jnp.int32, meta_ref.shape, 1)
    meta = jnp.where(sub == 3, jnp.where(lane_m < N_EXPERTS, ebase * (1.0 / bm), n_used), meta)
    meta_ref[...] = meta.astype(jnp.int32)


def _route_call(flat_e, bm):
    A = flat_e.shape[0]
    R = A // LANES
    rows_per_tile = DISP_TILE * TOP_K // LANES
    nt = R // rows_per_tile
    full = lambda i: (0, 0)
    shapes = [(R, LANES), (nt, LANES), (nt, LANES), (8, LANES)]
    lpos, nch, gbase, meta = pl.pallas_call(
        functools.partial(_route_kernel, bm=bm, rows_per_tile=rows_per_tile),
        grid=(1,),
        in_specs=[pl.BlockSpec((R, LANES), full)],
        out_specs=[pl.BlockSpec(s, full) for s in shapes],
        out_shape=[jax.ShapeDtypeStruct(s, jnp.int32) for s in shapes],
        compiler_params=pltpu.CompilerParams(dimension_semantics=("arbitrary",), vmem_limit_bytes=VMEM_LIMIT),
        name="route",
    )(flat_e.reshape(R, LANES))
    E = N_EXPERTS
    return dict(lpos=lpos.reshape(A // TOP_K, TOP_K), nch=nch[:, :E].reshape(-1), gbase=gbase[:, :E].reshape(-1),
                tail_start=meta[0, :E], tail_n=meta[1, :E], n_used=meta[2, :1], first_block=meta[3, :E + 1])


def _chunk_rows(ref, first, rows=ROW_CHUNK):
    return ref.at[pl.ds(pl.multiple_of(first, ROW_CHUNK), rows), :]


def _group_copies(nch_ref, gbase_ref, tile, buf, hbm, sem, to_hbm):
    per_big = BIG_CHUNK // ROW_CHUNK

    def copy(local, remote):
        if to_hbm:
            pltpu.make_async_copy(local, remote, sem).start()
        else:
            pltpu.make_async_copy(remote, local, sem).start()

    def per_expert(ex, done):
        n = nch_ref[tile * N_EXPERTS + ex]
        g0 = gbase_ref[tile * N_EXPERTS + ex]
        l0 = done * ROW_CHUNK
        n_big = lax.shift_right_logical(n, per_big.bit_length() - 1)

        def big(c, carry):
            copy(_chunk_rows(buf, l0 + c * BIG_CHUNK, BIG_CHUNK), _chunk_rows(hbm, g0 + c * BIG_CHUNK, BIG_CHUNK))
            return carry

        def small(c, carry):
            copy(_chunk_rows(buf, l0 + c * ROW_CHUNK), _chunk_rows(hbm, g0 + c * ROW_CHUNK))
            return carry

        lax.fori_loop(0, n_big, big, 0)
        lax.fori_loop(n_big * per_big, n, small, 0)
        return done + n

    return lax.fori_loop(0, N_EXPERTS, per_expert, 0)


def _wait_chunks(n, hbm, sem):
    rows = pl.multiple_of(n * ROW_CHUNK, ROW_CHUNK)

    @pl.when(n > 0)
    def _():
        pltpu.make_async_copy(hbm.at[pl.ds(0, rows), :], hbm.at[pl.ds(0, rows), :], sem).wait()


def _dispatch_kernel(nch_ref, gbase_ref, tstart_ref, tn_ref, nu_ref, x_ref, lposT_ref, xs_hbm,
                     buf_ref, zero_ref, sem, zsem, cnt_ref, *, bm, n_blocks):
    i = pl.program_id(0)
    nt = pl.num_programs(0)
    slot = i % 2

    @pl.when(i >= 2)
    def _():
        _wait_chunks(cnt_ref[slot], xs_hbm, sem.at[slot])

    xb = x_ref[...].astype(BF16)
    lposT = lposT_ref[...]
    rows = DISP_PCHUNK
    for rb in range(DISP_LCAP // rows):
        r = lax.broadcasted_iota(jnp.int32, (rows, xb.shape[0]), 0) + rb * rows
        hit = r == lposT[0:1]
        for k in range(1, TOP_K):
            hit = jnp.logical_or(hit, r == lposT[k:k + 1])
        perm = jnp.where(hit, 1.0, 0.0).astype(BF16)
        buf_ref[slot, rb * rows:(rb + 1) * rows, :] = _dot(perm, xb).astype(BF16)
    cnt_ref[slot] = _group_copies(nch_ref, gbase_ref, i, buf_ref.at[slot], xs_hbm, sem.at[slot], True)

    @pl.when(i == nt - 1)
    def _():
        zero_ref[...] = jnp.zeros(zero_ref.shape, BF16)

        def per_expert(ex, done):
            first = tstart_ref[ex]

            def per_chunk(c, carry):
                pltpu.make_async_copy(_chunk_rows(zero_ref, 0), _chunk_rows(xs_hbm, first + c * ROW_CHUNK),
                                      zsem.at[0]).start()
                return carry

            lax.fori_loop(0, tn_ref[ex], per_chunk, 0)
            return done + tn_ref[ex]

        n_tail = lax.fori_loop(0, N_EXPERTS, per_expert, 0)

        def per_block(b, carry):
            pltpu.make_async_copy(zero_ref, xs_hbm.at[pl.ds(pl.multiple_of(b * bm, bm), bm), :], zsem.at[1]).start()
            return carry

        lax.fori_loop(nu_ref[0], n_blocks, per_block, 0)
        _wait_chunks(n_tail, xs_hbm, zsem.at[0])

        def wait_block(b, carry):
            pltpu.make_async_copy(zero_ref, xs_hbm.at[pl.ds(0, bm), :], zsem.at[1]).wait()
            return carry

        lax.fori_loop(nu_ref[0], n_blocks, wait_block, 0)
        _wait_chunks(cnt_ref[slot], xs_hbm, sem.at[slot])

        @pl.when(nt >= 2)
        def _():
            _wait_chunks(cnt_ref[1 - slot], xs_hbm, sem.at[1 - slot])


def _dispatch_call(rt, x1, lposT, bm, n_blocks):
    T = x1.shape[0]
    tile = DISP_TILE
    grid_spec = pltpu.PrefetchScalarGridSpec(
        num_scalar_prefetch=5,
        grid=(T // tile,),
        in_specs=[
            pl.BlockSpec((tile, D_MODEL), lambda i, *_: (i, 0)),
            pl.BlockSpec((None, TOP_K, tile), lambda i, *_: (i, 0, 0)),
        ],
        out_specs=pl.BlockSpec(memory_space=pl.ANY),
        scratch_shapes=[
            pltpu.VMEM((2, DISP_LCAP, D_MODEL), BF16),
            pltpu.VMEM((bm, D_MODEL), BF16),
            pltpu.SemaphoreType.DMA((2,)),
            pltpu.SemaphoreType.DMA((2,)),
            pltpu.SMEM((2,), jnp.int32),
        ],
    )
    return pl.pallas_call(
        functools.partial(_dispatch_kernel, bm=bm, n_blocks=n_blocks),
        grid_spec=grid_spec,
        out_shape=jax.ShapeDtypeStruct((n_blocks * bm, D_MODEL), BF16),
        compiler_params=pltpu.CompilerParams(dimension_semantics=("arbitrary",), vmem_limit_bytes=VMEM_LIMIT),
        name="dispatch",
    )(rt["nch"], rt["gbase"], rt["tail_start"], rt["tail_n"], rt["n_used"], x1, lposT)


def _experts_kernel(fb_ref, xs_hbm, wgu_ref, bgu_ref, wd_ref, bd_ref, y_hbm,
                    xs_ref, ys_ref, wgu_b, wd_b, xsem, ysem, *, bm, n_blocks):
    e = pl.program_id(0)
    n_used = fb_ref[N_EXPERTS]

    def x_copy(blk, slot):
        return pltpu.make_async_copy(xs_hbm.at[pl.ds(pl.multiple_of(blk * bm, bm), bm), :], xs_ref.at[slot],
                                     xsem.at[slot])

    def y_copy(blk, slot):
        return pltpu.make_async_copy(ys_ref.at[slot], y_hbm.at[pl.ds(pl.multiple_of(blk * bm, bm), bm), :],
                                     ysem.at[slot])

    n_buf = X_AHEAD + 1

    @pl.when(e == 0)
    def _():
        for b in range(X_AHEAD):
            @pl.when(b < n_used)
            def _():
                x_copy(b, b).start()

    wgu_b[...] = wgu_ref[...].astype(BF16)
    wd_b[...] = wd_ref[...].astype(BF16)

    def block(blk, carry):
        slot = blk % 2
        xslot = blk % n_buf
        x_copy(blk, xslot).wait()

        @pl.when(blk + X_AHEAD < n_used)
        def _():
            x_copy(blk + X_AHEAD, (blk + X_AHEAD) % n_buf).start()

        h = _dot(xs_ref[xslot], wgu_b[...]) + bgu_ref[...]
        gate = jnp.minimum(h[:, :D_FF], SWIGLU_LIMIT)
        up = jnp.clip(h[:, D_FF:], -SWIGLU_LIMIT, SWIGLU_LIMIT)
        act = (up + 1.0) * (gate * jax.nn.sigmoid(gate * SWIGLU_ALPHA))
        y = _dot(act.astype(BF16), wd_b[...]) + bd_ref[...]

        @pl.when(blk >= 2)
        def _():
            y_copy(blk - 2, slot).wait()

        ys_ref[slot] = y.astype(BF16)
        y_copy(blk, slot).start()
        return carry

    lax.fori_loop(fb_ref[e], fb_ref[e + 1], block, 0)

    @pl.when(e == N_EXPERTS - 1)
    def _():
        y_copy(n_used - 1, (n_used - 1) % 2).wait()

        @pl.when(n_used >= 2)
        def _():
            y_copy(n_used - 2, n_used % 2).wait()

        ys_ref[0] = jnp.zeros(ys_ref.shape[1:], BF16)

        def fill(blk, carry):
            y_copy(blk, 0).start()
            y_copy(blk, 0).wait()
            return carry

        lax.fori_loop(n_used, n_blocks, fill, 0)


def _experts_call(rt, xs, wgu, bgu, wd, bd, bm, n_blocks):
    E = N_EXPERTS
    by_expert = lambda e, fb: (e, 0, 0)
    grid_spec = pltpu.PrefetchScalarGridSpec(
        num_scalar_prefetch=1,
        grid=(E,),
        in_specs=[
            pl.BlockSpec(memory_space=pl.ANY),
            pl.BlockSpec((None, D_MODEL, 2 * D_FF), by_expert),
            pl.BlockSpec((None, 1, 2 * D_FF), by_expert),
            pl.BlockSpec((None, D_FF, D_MODEL), by_expert),
            pl.BlockSpec((None, 1, D_MODEL), by_expert),
        ],
        out_specs=pl.BlockSpec(memory_space=pl.ANY),
        scratch_shapes=[
            pltpu.VMEM((X_AHEAD + 1, bm, D_MODEL), BF16),
            pltpu.VMEM((2, bm, D_MODEL), BF16),
            pltpu.VMEM((D_MODEL, 2 * D_FF), BF16),
            pltpu.VMEM((D_FF, D_MODEL), BF16),
            pltpu.SemaphoreType.DMA((X_AHEAD + 1,)),
            pltpu.SemaphoreType.DMA((2,)),
        ],
    )
    return pl.pallas_call(
        functools.partial(_experts_kernel, bm=bm, n_blocks=n_blocks),
        grid_spec=grid_spec,
        out_shape=jax.ShapeDtypeStruct((n_blocks * bm, D_MODEL), BF16),
        compiler_params=pltpu.CompilerParams(dimension_semantics=("arbitrary",), vmem_limit_bytes=VMEM_LIMIT),
        name="experts",
    )(rt["first_block"], xs, wgu, bgu.reshape(E, 1, 2 * D_FF), wd, bd.reshape(E, 1, D_MODEL))


def _combine_kernel(nch_ref, gbase_ref, y_hbm, x1_ref, lpos_ref, gate_ref, g_ref, b_ref, o_ref,
                    buf_ref, sem, cnt_ref):
    i = pl.program_id(0)
    nt = pl.num_programs(0)
    slot = i % 2

    @pl.when(i == 0)
    def _():
        buf_ref[...] = jnp.zeros(buf_ref.shape, BF16)
        cnt_ref[0] = _group_copies(nch_ref, gbase_ref, 0, buf_ref.at[0], y_hbm, sem.at[0], False)

    @pl.when(i + 1 < nt)
    def _():
        cnt_ref[1 - slot] = _group_copies(nch_ref, gbase_ref, i + 1, buf_ref.at[1 - slot], y_hbm,
                                          sem.at[1 - slot], False)

    _wait_chunks(cnt_ref[slot], y_hbm, sem.at[slot])
    lpos = lpos_ref[...]
    gates = gate_ref[...]
    cols = DISP_PCHUNK
    y = jnp.zeros(o_ref.shape, F32)
    for cb in range(DISP_LCAP // cols):
        c = lax.broadcasted_iota(jnp.int32, (lpos.shape[0], cols), 1) + cb * cols
        w = jnp.zeros(c.shape, F32)
        for k in range(TOP_K):
            w = jnp.where(c == lpos[:, k:k + 1], gates[:, k:k + 1], w)
        y = y + _dot(w.astype(BF16), buf_ref[slot, cb * cols:(cb + 1) * cols, :])
    o_ref[...] = _layer_norm(DN_ALPHA * x1_ref[...] + y, g_ref[...], b_ref[...])


def _combine_call(rt, y, x1, gates, g, b):
    T = x1.shape[0]
    tile = DISP_TILE
    grid_spec = pltpu.PrefetchScalarGridSpec(
        num_scalar_prefetch=2,
        grid=(T // tile,),
        in_specs=[
            pl.BlockSpec(memory_space=pl.ANY),
            pl.BlockSpec((tile, D_MODEL), lambda i, *_: (i, 0)),
            pl.BlockSpec((tile, TOP_K), lambda i, *_: (i, 0)),
            pl.BlockSpec((tile, LANES), lambda i, *_: (i, 0)),
            pl.BlockSpec(g.shape, lambda i, *_: (0, 0)),
            pl.BlockSpec(b.shape, lambda i, *_: (0, 0)),
        ],
        out_specs=pl.BlockSpec((tile, D_MODEL), lambda i, *_: (i, 0)),
        scratch_shapes=[
            pltpu.VMEM((2, DISP_LCAP, D_MODEL), BF16),
            pltpu.SemaphoreType.DMA((2,)),
            pltpu.SMEM((2,), jnp.int32),
        ],
    )
    return pl.pallas_call(
        _combine_kernel,
        grid_spec=grid_spec,
        out_shape=jax.ShapeDtypeStruct((T, D_MODEL), F32),
        compiler_params=pltpu.CompilerParams(dimension_semantics=("arbitrary",), vmem_limit_bytes=VMEM_LIMIT),
        name="combine",
    )(rt["nch"], rt["gbase"], y, x1, rt["lpos"], gates, g, b)


def _rope_lane_order(n_sub):
    half = DIFF_HEAD_DIM // 2
    per = LANES // 2 // n_sub
    assert per == half or n_sub == 1
    cols = []
    for part in range(2):
        for sub in range(n_sub):
            cols.extend(sub * 64 + part * half + d for d in range(half))
    return np.asarray(cols)


def _prep_weights(w_in, mla_q_norm_g, w_uq, mla_kv_norm_g, w_ukv):
    o_dq, o_dk, o_dv, o_cq, o_ckv, o_kr = 0, 512, 1024, 1536, 1792, 1920
    head_order = _rope_lane_order(2)
    diff_cols = np.concatenate([h * LANES + head_order for h in range(DIFF_HEADS)])
    w = {}
    ckv = w_in[:, o_ckv:o_kr]
    w["feat"] = jnp.concatenate([w_in[:, o_dq + diff_cols], w_in[:, o_dv:o_cq], w_in[:, o_cq:o_ckv], ckv],
                                axis=1).T.astype(BF16)

    def spread_rope(cols64):
        z = jnp.zeros((cols64.shape[0], 32), cols64.dtype)
        return jnp.concatenate([cols64[:, :32], z, cols64[:, 32:], z], axis=1)

    w["tok"] = jnp.concatenate([w_in[:, o_dk + diff_cols], ckv, spread_rope(w_in[:, o_kr:o_kr + MLA_ROPE_DIM])],
                               axis=1).astype(BF16)
    uq = []
    for h in range(MLA_HEADS):
        base = h * MLA_QK_DIM
        uq.append(w_uq[:, base:base + MLA_NOPE_DIM])
        uq.append(spread_rope(w_uq[:, base + MLA_NOPE_DIM:base + MLA_QK_DIM]))
    w["uqT"] = jnp.concatenate(uq, axis=1).T.astype(BF16)
    per = MLA_NOPE_DIM + MLA_V_DIM
    w["uk"] = jnp.concatenate([w_ukv[:, h * per:h * per + MLA_NOPE_DIM] for h in range(MLA_HEADS)], axis=1).astype(BF16)
    w["uvT"] = jnp.concatenate(
        [w_ukv[:, h * per + MLA_NOPE_DIM:(h + 1) * per] for h in range(MLA_HEADS)], axis=1).T.astype(BF16)
    w["gq"] = mla_q_norm_g.reshape(MLA_Q_RANK, 1)
    w["gkv"] = mla_kv_norm_g.reshape(1, MLA_KV_RANK)
    w["gkvc"] = mla_kv_norm_g.reshape(MLA_KV_RANK, 1)
    return w


def _rope_tables(positions):
    half = MLA_ROPE_DIM // 2
    inv_freq = 1.0 / (ROPE_THETA ** (jnp.arange(0, MLA_ROPE_DIM, 2, dtype=F32) / MLA_ROPE_DIM))
    ang = positions.astype(F32)[..., None] * inv_freq
    ang = jnp.tile(ang, (1, 1, LANES // half))
    sign = jnp.where(jnp.arange(LANES) < LANES // 2, -1.0, 1.0).astype(F32)
    return jnp.cos(ang), jnp.sin(ang) * sign


def kernel(x, positions, w_in, lambda_q1, lambda_k1, lambda_q2, lambda_k2, subln_g, mla_q_norm_g, w_uq,
           mla_kv_norm_g, w_ukv, w_o, ln1_g, ln1_b, w_router, b_router, w_gate_up, b_gate_up, w_down, b_down,
           ln2_g, ln2_b):
    B, S, D = x.shape
    T = B * S
    l = 0
    x2 = x.reshape(T, D)
    w = _prep_weights(w_in[l], mla_q_norm_g[l], w_uq[l], mla_kv_norm_g[l], w_ukv[l])

    dqT, dk, dvT, mqT, mk, mvT = _proj_call(x, _rope_tables(positions), w)
    lam_vecs = [v[l].reshape(1, DIFF_HEAD_DIM) for v in (lambda_q1, lambda_k1, lambda_q2, lambda_k2)]
    o_a = _attn_call(_diff_attn_kernel, lam_vecs + [subln_g[l].reshape(DIFF_V_DIM, 1)],
                     dqT, dk, dvT, DIFF_HEADS, 1, LANES, "diff_attn")
    o_b = _attn_call(_mla_attn_kernel, [], mqT, mk, mvT, MLA_HEADS // 2, 2, MLA_QK_PAD, "mla_attn")

    wr = jnp.pad(w_router[l], ((0, 0), (0, LANES - N_EXPERTS)))
    br = jnp.pad(b_router[l], (0, LANES - N_EXPERTS), constant_values=NEG_BIG).reshape(1, LANES)
    x1, idx, gates = _post_call(
        o_a.reshape(T, -1), o_b.reshape(T, -1), x2, w_o[l].astype(BF16),
        ln1_g[l].reshape(1, D), ln1_b[l].reshape(1, D), wr, br)

    bm = MOE_BM
    A = T * TOP_K
    n_tiles = T // DISP_TILE
    n_blocks = pl.cdiv(A + n_tiles * N_EXPERTS * (ROW_CHUNK - 1) + N_EXPERTS * (bm - ROW_CHUNK), bm)
    rt = _route_call(idx[:, :TOP_K].reshape(A), bm)
    lposT = rt["lpos"].reshape(n_tiles, DISP_TILE, TOP_K).transpose(0, 2, 1)
    xs = _dispatch_call(rt, x1, lposT, bm, n_blocks)
    y = _experts_call(rt, xs, w_gate_up[l], b_gate_up[l], w_down[l], b_down[l], bm, n_blocks)
    out = _combine_call(rt, y, x1, gates, ln2_g[l].reshape(1, D), ln2_b[l].reshape(1, D))
    return out.reshape(B, S, D)
```

```python
import functools
import math

import numpy as np
import jax
import jax.numpy as jnp
from jax import lax
from jax.experimental import pallas as pl
from jax.experimental.pallas import tpu as pltpu

D_MODEL = 1024
DIFF_HEADS = 4
DIFF_HEAD_DIM = 64
DIFF_V_DIM = 128
MLA_HEADS = 4
MLA_V_DIM = 128
MLA_NOPE_DIM = 128
MLA_ROPE_DIM = 64
MLA_QK_DIM = MLA_NOPE_DIM + MLA_ROPE_DIM
MLA_Q_RANK = 256
MLA_KV_RANK = 128
DIFF_Q_COLS = DIFF_K_COLS = DIFF_V_COLS = 512
ROPE_THETA = 10000.0
N_EXPERTS = 32
TOP_K = 4
D_FF = 1024
SWIGLU_LIMIT = 7.0
SWIGLU_ALPHA = 1.702
LN_EPS = 1e-5
SUBLN_EPS = 1e-5
MLA_RMS_EPS = 1e-6
DEPTH = 1
DN_ALPHA = (2.0 * DEPTH) ** 0.25
LAMBDA_INIT = 0.8 - 0.6 * math.exp(-0.3 * 0)

LANES = 128
MLA_QK_PAD = 2 * LANES
BF16_ROWS = 16
VAL_ROWS = LANES + BF16_ROWS
VMEM_LIMIT = 56 * 1024 * 1024

ATTN_TILE = 512
POST_TM = 512
MOE_BM = 256
X_AHEAD = 3
ROW_CHUNK = 8
BIG_CHUNK = 32
DISP_TILE = 512
DISP_PCHUNK = 256
DISP_LCAP = -(-(DISP_TILE * TOP_K + N_EXPERTS * (ROW_CHUNK - 1)) // DISP_PCHUNK) * DISP_PCHUNK

NEG_BIG = -1e30
LOG2E = math.log2(math.e)
F32 = jnp.float32
BF16 = jnp.bfloat16


def _dot(a, b):
    return jnp.dot(a, b, preferred_element_type=F32)


def _dot_nt(a, b):
    return lax.dot_general(a, b, (((1,), (1,)), ((), ())), preferred_element_type=F32)


def _rope128(blk, cos, sin):
    return blk * cos + pltpu.roll(blk, 64, axis=1) * sin


def _rope128_t(blk, cos, sin):
    half = LANES // 2
    rolled = jnp.concatenate([blk[half:], blk[:half]], axis=0)
    return blk * cos + rolled * sin


def _rms_rows(t, g, eps):
    return t * lax.rsqrt(jnp.mean(t * t, axis=-1, keepdims=True) + eps) * g


def _rms_cols(t, g, eps):
    return t * lax.rsqrt(jnp.mean(t * t, axis=0, keepdims=True) + eps) * g


def _store_values(vT_ref, vT, heads):
    ones = jnp.ones((BF16_ROWS, vT.shape[1]), BF16)
    for h in range(heads):
        vT_ref[h * VAL_ROWS:h * VAL_ROWS + LANES, :] = vT[h * LANES:(h + 1) * LANES].astype(BF16)
        vT_ref[h * VAL_ROWS + LANES:(h + 1) * VAL_ROWS, :] = ones


def _proj_kernel(x_ref, cos_ref, sin_ref, wfeat_ref, wtok_ref, gq_ref, gkv_ref, gkvc_ref,
                 wuqT_ref, wuk_ref, wuvT_ref, dqT_ref, dk_ref, dvT_ref, mqT_ref, mk_ref, mvT_ref):
    xb = x_ref[...].astype(BF16)
    cos, sin = cos_ref[...], sin_ref[...]
    cosT, sinT = cos.T, sin.T

    dq_scale = DIFF_HEAD_DIM ** -0.5 * LOG2E
    mq_scale = MLA_QK_DIM ** -0.5 * LOG2E
    o_dv, o_cq, o_ckv = DIFF_Q_COLS, DIFF_Q_COLS + DIFF_V_COLS, DIFF_Q_COLS + DIFF_V_COLS + MLA_Q_RANK

    feat = _dot_nt(wfeat_ref[...], xb)
    for h in range(DIFF_HEADS):
        sl = slice(h * LANES, (h + 1) * LANES)
        dqT_ref[sl, :] = (_rope128_t(feat[sl], cosT, sinT) * dq_scale).astype(BF16)
    _store_values(dvT_ref, feat[o_dv:o_cq], DIFF_HEADS)

    cqT = _rms_cols(feat[o_cq:o_ckv], gq_ref[...], MLA_RMS_EPS)
    qT = _dot(wuqT_ref[...], cqT.astype(BF16))
    for h in range(MLA_HEADS):
        nope = slice(h * MLA_QK_PAD, h * MLA_QK_PAD + LANES)
        ropes = slice(h * MLA_QK_PAD + LANES, (h + 1) * MLA_QK_PAD)
        mqT_ref[nope, :] = (qT[nope] * mq_scale).astype(BF16)
        mqT_ref[ropes, :] = (_rope128_t(qT[ropes], cosT, sinT) * mq_scale).astype(BF16)

    ckvT = _rms_cols(feat[o_ckv:], gkvc_ref[...], MLA_RMS_EPS)
    _store_values(mvT_ref, _dot(wuvT_ref[...], ckvT.astype(BF16)), MLA_HEADS)

    tokm = _dot(xb, wtok_ref[...])
    for h in range(DIFF_HEADS):
        sl = slice(h * LANES, (h + 1) * LANES)
        dk_ref[:, sl] = _rope128(tokm[:, sl], cos, sin).astype(BF16)
    ckv = _rms_rows(tokm[:, DIFF_K_COLS:DIFF_K_COLS + MLA_KV_RANK], gkv_ref[...], MLA_RMS_EPS)
    k_nope = _dot(ckv.astype(BF16), wuk_ref[...])
    k_pe = _rope128(tokm[:, DIFF_K_COLS + MLA_KV_RANK:], cos, sin).astype(BF16)
    for h in range(MLA_HEADS):
        mk_ref[:, h * MLA_QK_PAD:h * MLA_QK_PAD + LANES] = k_nope[:, h * LANES:(h + 1) * LANES].astype(BF16)
        mk_ref[:, h * MLA_QK_PAD + LANES:(h + 1) * MLA_QK_PAD] = k_pe


def _proj_call(x3, tabs, w):
    B, S, D = x3.shape
    tm = ATTN_TILE
    nt = S // tm
    cos_t, sin_t = tabs
    weights = [w["feat"], w["tok"], w["gq"], w["gkv"], w["gkvc"], w["uqT"], w["uk"], w["uvT"]]
    tok = lambda b, i: (b, i, 0)
    feat = lambda b, i: (b, i, 0, 0)
    in_specs = [pl.BlockSpec((None, tm, D), tok),
                pl.BlockSpec((None, tm, LANES), tok), pl.BlockSpec((None, tm, LANES), tok)]
    in_specs += [pl.BlockSpec(a.shape, lambda b, i: (0, 0)) for a in weights]
    mq_w = MLA_HEADS * MLA_QK_PAD
    dv_w, mv_w = DIFF_HEADS * VAL_ROWS, MLA_HEADS * VAL_ROWS
    out_specs = [pl.BlockSpec((None, None, 512, tm), feat), pl.BlockSpec((None, tm, 512), tok),
                 pl.BlockSpec((None, None, dv_w, tm), feat), pl.BlockSpec((None, None, mq_w, tm), feat),
                 pl.BlockSpec((None, tm, mq_w), tok), pl.BlockSpec((None, None, mv_w, tm), feat)]
    out_shape = [jax.ShapeDtypeStruct((B, nt, 512, tm), BF16), jax.ShapeDtypeStruct((B, S, 512), BF16),
                 jax.ShapeDtypeStruct((B, nt, dv_w, tm), BF16), jax.ShapeDtypeStruct((B, nt, mq_w, tm), BF16),
                 jax.ShapeDtypeStruct((B, S, mq_w), BF16), jax.ShapeDtypeStruct((B, nt, mv_w, tm), BF16)]
    return pl.pallas_call(
        _proj_kernel,
        grid=(B, nt),
        in_specs=in_specs,
        out_specs=out_specs,
        out_shape=out_shape,
        compiler_params=pltpu.CompilerParams(dimension_semantics=("arbitrary", "arbitrary"),
                                             vmem_limit_bytes=VMEM_LIMIT),
        name="proj",
    )(x3, cos_t, sin_t, *weights)


def _flash_body(qTs, keys_of, values_of, acc_ref, m_ref, s_refs, *, t):
    i = pl.program_id(2)
    m_ref[...] = jnp.full(m_ref.shape, -jnp.inf, F32)
    acc_ref[...] = jnp.zeros(acc_ref.shape, F32)

    def scores(j, u):
        s_refs[u][...] = _dot(keys_of(u, j), qTs[u])

    def finish(j, u, masked):
        s = s_refs[u][...]
        if masked:
            key = lax.broadcasted_iota(jnp.int32, s.shape, 0)
            qry = lax.broadcasted_iota(jnp.int32, s.shape, 1)
            s = jnp.where(key <= qry, s, -jnp.inf)
        m_prev = m_ref[u]
        m_new = jnp.maximum(m_prev, jnp.max(s, axis=0, keepdims=True))
        alpha = jnp.exp2(m_prev - m_new)
        p = jnp.exp2((s - m_new).astype(BF16))
        acc_ref[u] = alpha * acc_ref[u] + _dot(values_of(u, j), p)
        m_ref[u] = m_new

    def full_step(j, carry):
        scores(j, 1)
        finish(j, 0, False)
        scores(j + 1, 0)
        finish(j, 1, False)
        return carry

    def two_steps(jj, carry):
        return full_step(2 * jj + 1, full_step(2 * jj, carry))

    scores(0, 0)
    lax.fori_loop(0, lax.shift_right_logical(i, 1), two_steps, 0)

    @pl.when(i % 2 == 1)
    def _():
        full_step(i - 1, 0)

    scores(i, 1)
    finish(i, 0, True)
    finish(i, 1, True)


def _key_tile(k_ref, j, t, lanes=slice(None)):
    return k_ref[pl.ds(pl.multiple_of(j * t, t), t), lanes]


def _normalized(acc_ref, u):
    return acc_ref[u, :LANES, :] / acc_ref[u, LANES:LANES + 1, :]


def _diff_attn_kernel(lq1_ref, lk1_ref, lq2_ref, lk2_ref, g_ref, qT_ref, k_ref, vT_ref, o_ref,
                      acc_ref, m_ref, sa_ref, sb_ref, *, t):
    qT = qT_ref[...]
    row = lax.broadcasted_iota(jnp.int32, qT.shape, 0)
    first = (row % 64) < 32
    zero = jnp.zeros_like(qT)
    qTs = (jnp.where(first, qT, zero), jnp.where(first, zero, qT))
    _flash_body(qTs, lambda u, j: _key_tile(k_ref, j, t), lambda u, j: vT_ref[j],
                acc_ref, m_ref, (sa_ref, sb_ref), t=t)

    lam = (jnp.exp(jnp.sum(lq1_ref[...] * lk1_ref[...], axis=-1, keepdims=True))
           - jnp.exp(jnp.sum(lq2_ref[...] * lk2_ref[...], axis=-1, keepdims=True)) + LAMBDA_INIT)
    oT = _normalized(acc_ref, 0) - lam * _normalized(acc_ref, 1)
    oT = _rms_cols(oT, g_ref[...], SUBLN_EPS) * (1.0 - LAMBDA_INIT)
    o_ref[...] = oT.T.astype(o_ref.dtype)


def _mla_attn_kernel(qT_ref, k_ref, vT_ref, o_ref, acc_ref, m_ref, sa_ref, sb_ref, *, t):
    dk = MLA_QK_PAD
    qTs = tuple(qT_ref[u * dk:(u + 1) * dk, :] for u in range(2))
    _flash_body(qTs, lambda u, j: _key_tile(k_ref, j, t, slice(u * dk, (u + 1) * dk)),
                lambda u, j: vT_ref[j, u * VAL_ROWS:(u + 1) * VAL_ROWS, :],
                acc_ref, m_ref, (sa_ref, sb_ref), t=t)
    for u in range(2):
        o_ref[:, u * LANES:(u + 1) * LANES] = _normalized(acc_ref, u).T.astype(o_ref.dtype)


def _attn_call(kernel, extra, qT, k, vT, groups, heads_per_group, dk, name):
    B, nt, _, t = qT.shape
    S = nt * t
    hp = heads_per_group
    in_specs = [pl.BlockSpec(a.shape, lambda b, h, i: (0, 0)) for a in extra]
    in_specs += [
        pl.BlockSpec((None, None, hp * dk, t), lambda b, h, i: (b, i, h, 0)),
        pl.BlockSpec((None, S, hp * dk), lambda b, h, i: (b, 0, h)),
        pl.BlockSpec((None, nt, hp * VAL_ROWS, t), lambda b, h, i: (b, 0, h, 0)),
    ]
    return pl.pallas_call(
        functools.partial(kernel, t=t),
        grid=(B, groups, nt),
        in_specs=in_specs,
        out_specs=pl.BlockSpec((None, t, hp * LANES), lambda b, h, i: (b, i, h)),
        out_shape=jax.ShapeDtypeStruct((B, S, groups * hp * LANES), BF16),
        scratch_shapes=[
            pltpu.VMEM((2, VAL_ROWS, t), F32),
            pltpu.VMEM((2, 1, t), F32),
            pltpu.VMEM((t, t), F32),
            pltpu.VMEM((t, t), F32),
        ],
        compiler_params=pltpu.CompilerParams(
            dimension_semantics=("arbitrary", "arbitrary", "arbitrary"), vmem_limit_bytes=VMEM_LIMIT),
        name=name,
    )(*extra, qT, k, vT)


def _layer_norm(y, g, b):
    mu = jnp.mean(y, axis=-1, keepdims=True)
    d = y - mu
    var = jnp.mean(d * d, axis=-1, keepdims=True)
    return d * lax.rsqrt(var + LN_EPS) * g + b


def _split_bf16(a):
    hi = a.astype(BF16)
    return hi, (a - hi.astype(F32)).astype(BF16)


def _post_kernel(oa_ref, ob_ref, x_ref, wo_ref, g_ref, b_ref, wrh_ref, wrl_ref, br_ref,
                 x1_ref, idx_ref, gate_ref):
    half = oa_ref.shape[1]
    mixed = _dot(oa_ref[...], wo_ref[:half, :]) + _dot(ob_ref[...], wo_ref[half:, :])
    x1 = _layer_norm(DN_ALPHA * x_ref[...] + mixed, g_ref[...], b_ref[...])
    x1_ref[...] = x1

    x_hi, x_lo = _split_bf16(x1)
    logits = (_dot(x_hi, wrh_ref[...]) + _dot(x_lo, wrh_ref[...]) + _dot(x_hi, wrl_ref[...])) + br_ref[...]
    lane = lax.broadcasted_iota(jnp.int32, logits.shape, 1).astype(F32)
    work = logits
    vals, idxs = [], []
    for _ in range(TOP_K):
        m = jnp.max(work, axis=-1, keepdims=True)
        idx = jnp.min(jnp.where(work == m, lane, float(LANES)), axis=-1, keepdims=True)
        vals.append(m)
        idxs.append(idx)
        work = jnp.where(lane == idx, NEG_BIG, work)
    es = [jnp.exp(v - vals[0]) for v in vals]
    den = es[0] + es[1] + es[2] + es[3]
    idx_out = jnp.zeros(logits.shape, F32)
    gate_out = jnp.zeros(logits.shape, F32)
    for k in range(TOP_K):
        idx_out = jnp.where(lane == float(k), idxs[k], idx_out)
        gate_out = jnp.where(lane == float(k), es[k] / den, gate_out)
    idx_ref[...] = idx_out.astype(jnp.int32)
    gate_ref[...] = gate_out


def _post_call(oa, ob, x2, wo, g, b, wr, br):
    T = x2.shape[0]
    tm = POST_TM
    row = lambda i: (i, 0)
    full = lambda i: (0, 0)
    return pl.pallas_call(
        _post_kernel,
        grid=(T // tm,),
        in_specs=[
            pl.BlockSpec((tm, oa.shape[1]), row), pl.BlockSpec((tm, ob.shape[1]), row),
            pl.BlockSpec((tm, D_MODEL), row), pl.BlockSpec(wo.shape, full),
            pl.BlockSpec(g.shape, full), pl.BlockSpec(b.shape, full),
            pl.BlockSpec(wr.shape, full), pl.BlockSpec(wr.shape, full), pl.BlockSpec(br.shape, full),
        ],
        out_specs=[
            pl.BlockSpec((tm, D_MODEL), row), pl.BlockSpec((tm, LANES), row), pl.BlockSpec((tm, LANES), row),
        ],
        out_shape=[
            jax.ShapeDtypeStruct((T, D_MODEL), F32),
            jax.ShapeDtypeStruct((T, LANES), jnp.int32), jax.ShapeDtypeStruct((T, LANES), F32),
        ],
        compiler_params=pltpu.CompilerParams(dimension_semantics=("arbitrary",), vmem_limit_bytes=VMEM_LIMIT),
        name="post",
    )(oa, ob, x2, wo, g, b, *_split_bf16(wr), br)


def _lane_cumsum(x, n):
    lane = lax.broadcasted_iota(jnp.int32, x.shape, 1)
    s = 1
    while s < n:
        x = x + jnp.where(lane >= s, pltpu.roll(x, s, axis=1), 0.0)
        s *= 2
    return x


def _route_kernel(e_ref, lpos_ref, nch_ref, gbase_ref, meta_ref, *, bm, rows_per_tile):
    e = e_ref[...]
    R = e.shape[0]
    nt = R // rows_per_tile
    r_i = lax.broadcasted_iota(jnp.int32, (LANES, LANES), 0)
    c_i = lax.broadcasted_iota(jnp.int32, (LANES, LANES), 1)
    lane_incl = (r_i <= c_i).astype(BF16)
    rr = lax.broadcasted_iota(jnp.int32, (R, R), 0)
    rc = lax.broadcasted_iota(jnp.int32, (R, R), 1)
    same_tile = (rr // rows_per_tile) == (rc // rows_per_tile)
    rows_before = jnp.logical_and(rc < rr, same_tile).astype(BF16)
    tr = lax.broadcasted_iota(jnp.int32, (nt, R), 0)
    tc = lax.broadcasted_iota(jnp.int32, (nt, R), 1)
    tile_rows = (tc // rows_per_tile == tr).astype(BF16)
    er = lax.broadcasted_iota(jnp.int32, (R, nt), 0)
    ec = lax.broadcasted_iota(jnp.int32, (R, nt), 1)
    row_tile = (er // rows_per_tile == ec).astype(F32)
    lt_r = lax.broadcasted_iota(jnp.int32, (nt, nt), 0)
    lt_c = lax.broadcasted_iota(jnp.int32, (nt, nt), 1)
    tiles_before = (lt_c < lt_r).astype(BF16)
    lane_t = lax.broadcasted_iota(jnp.int32, (nt, LANES), 1)

    rank = jnp.zeros(e.shape, F32)
    cnt = jnp.zeros((nt, LANES), F32)
    for ex in range(N_EXPERTS):
        hit = e == ex
        m = jnp.where(hit, 1.0, 0.0).astype(BF16)
        incl = _dot(m, lane_incl)
        row_tot = jnp.broadcast_to(incl[:, LANES - 1:LANES], e.shape).astype(BF16)
        before = _dot(rows_before, row_tot)
        rank = jnp.where(hit, incl - 1.0 + before, rank)
        cnt = jnp.where(lane_t == ex, _dot(tile_rows, row_tot), cnt)

    chunk = float(ROW_CHUNK)
    cnt8 = jnp.floor((cnt + (chunk - 1.0)) * (1.0 / chunk)) * chunk
    lstart = _lane_cumsum(cnt8, N_EXPERTS) - cnt8
    tile_pre = _dot(tiles_before, cnt8.astype(BF16))
    tot8 = tile_pre[nt - 1:nt] + cnt8[nt - 1:nt]
    tot_bm = jnp.floor((tot8 + (bm - 1.0)) * (1.0 / bm)) * bm
    end_incl = _lane_cumsum(tot_bm, N_EXPERTS)
    ebase = end_incl - tot_bm

    lrow = jnp.dot(row_tile, lstart, preferred_element_type=F32,
                   precision=lax.Precision.HIGHEST)
    lstart_a = jnp.zeros(e.shape, F32)
    for ex in range(N_EXPERTS):
        lstart_a = jnp.where(e == ex, lrow[:, ex:ex + 1], lstart_a)
    lpos_ref[...] = (lstart_a + rank).astype(jnp.int32)
    nch_ref[...] = (cnt8 * (1.0 / chunk)).astype(jnp.int32)
    gbase_ref[...] = (ebase + tile_pre).astype(jnp.int32)

    sub = lax.broadcasted_iota(jnp.int32, meta_ref.shape, 0)
    meta = jnp.where(sub == 0, ebase + tot8, 0.0)
    meta = jnp.where(sub == 1, (tot_bm - tot8) * (1.0 / chunk), meta)
    n_used = end_incl[:, N_EXPERTS - 1:N_EXPERTS] * (1.0 / bm)
    meta = jnp.where(sub == 2, n_used, meta)
    lane_m = lax.broadcasted_iota(jnp.int32, meta_ref.shape, 1)
    meta = jnp.where(sub == 3, jnp.where(lane_m < N_EXPERTS, ebase * (1.0 / bm), n_used), meta)
    meta_ref[...] = meta.astype(jnp.int32)


def _route_call(flat_e, bm):
    A = flat_e.shape[0]
    R = A // LANES
    rows_per_tile = DISP_TILE * TOP_K // LANES
    nt = R // rows_per_tile
    full = lambda i: (0, 0)
    shapes = [(R, LANES), (nt, LANES), (nt, LANES), (8, LANES)]
    lpos, nch, gbase, meta = pl.pallas_call(
        functools.partial(_route_kernel, bm=bm, rows_per_tile=rows_per_tile),
        grid=(1,),
        in_specs=[pl.BlockSpec((R, LANES), full)],
        out_specs=[pl.BlockSpec(s, full) for s in shapes],
        out_shape=[jax.ShapeDtypeStruct(s, jnp.int32) for s in shapes],
        compiler_params=pltpu.CompilerParams(dimension_semantics=("arbitrary",), vmem_limit_bytes=VMEM_LIMIT),
        name="route",
    )(flat_e.reshape(R, LANES))
    E = N_EXPERTS
    return dict(lpos=lpos.reshape(A // TOP_K, TOP_K), nch=nch[:, :E].reshape(-1), gbase=gbase[:, :E].reshape(-1),
                tail_start=meta[0, :E], tail_n=meta[1, :E], n_used=meta[2, :1], first_block=meta[3, :E + 1])


def _chunk_rows(ref, first, rows=ROW_CHUNK):
    return ref.at[pl.ds(pl.multiple_of(first, ROW_CHUNK), rows), :]


def _group_copies(nch_ref, gbase_ref, tile, buf, hbm, sem, to_hbm):
    per_big = BIG_CHUNK // ROW_CHUNK

    def copy(local, remote):
        if to_hbm:
            pltpu.make_async_copy(local, remote, sem).start()
        else:
            pltpu.make_async_copy(remote, local, sem).start()

    def per_expert(ex, done):
        n = nch_ref[tile * N_EXPERTS + ex]
        g0 = gbase_ref[tile * N_EXPERTS + ex]
        l0 = done * ROW_CHUNK
        n_big = lax.shift_right_logical(n, per_big.bit_length() - 1)

        def big(c, carry):
            copy(_chunk_rows(buf, l0 + c * BIG_CHUNK, BIG_CHUNK), _chunk_rows(hbm, g0 + c * BIG_CHUNK, BIG_CHUNK))
            return carry

        def small(c, carry):
            copy(_chunk_rows(buf, l0 + c * ROW_CHUNK), _chunk_rows(hbm, g0 + c * ROW_CHUNK))
            return carry

        lax.fori_loop(0, n_big, big, 0)
        lax.fori_loop(n_big * per_big, n, small, 0)
        return done + n

    return lax.fori_loop(0, N_EXPERTS, per_expert, 0)


def _wait_chunks(n, hbm, sem):
    rows = pl.multiple_of(n * ROW_CHUNK, ROW_CHUNK)

    @pl.when(n > 0)
    def _():
        pltpu.make_async_copy(hbm.at[pl.ds(0, rows), :], hbm.at[pl.ds(0, rows), :], sem).wait()


def _dispatch_kernel(nch_ref, gbase_ref, tstart_ref, tn_ref, nu_ref, x_ref, lposT_ref, xs_hbm,
                     buf_ref, zero_ref, sem, zsem, cnt_ref, *, bm, n_blocks):
    i = pl.program_id(0)
    nt = pl.num_programs(0)
    slot = i % 2

    @pl.when(i >= 2)
    def _():
        _wait_chunks(cnt_ref[slot], xs_hbm, sem.at[slot])

    xb = x_ref[...].astype(BF16)
    lposT = lposT_ref[...]
    rows = DISP_PCHUNK
    for rb in range(DISP_LCAP // rows):
        r = lax.broadcasted_iota(jnp.int32, (rows, xb.shape[0]), 0) + rb * rows
        hit = r == lposT[0:1]
        for k in range(1, TOP_K):
            hit = jnp.logical_or(hit, r == lposT[k:k + 1])
        perm = jnp.where(hit, 1.0, 0.0).astype(BF16)
        buf_ref[slot, rb * rows:(rb + 1) * rows, :] = _dot(perm, xb)
    cnt_ref[slot] = _group_copies(nch_ref, gbase_ref, i, buf_ref.at[slot], xs_hbm, sem.at[slot], True)

    @pl.when(i == nt - 1)
    def _():
        zero_ref[...] = jnp.zeros(zero_ref.shape, F32)

        def per_expert(ex, done):
            first = tstart_ref[ex]

            def per_chunk(c, carry):
                pltpu.make_async_copy(_chunk_rows(zero_ref, 0), _chunk_rows(xs_hbm, first + c * ROW_CHUNK),
                                      zsem.at[0]).start()
                return carry

            lax.fori_loop(0, tn_ref[ex], per_chunk, 0)
            return done + tn_ref[ex]

        n_tail = lax.fori_loop(0, N_EXPERTS, per_expert, 0)

        def per_block(b, carry):
            pltpu.make_async_copy(zero_ref, xs_hbm.at[pl.ds(pl.multiple_of(b * bm, bm), bm), :], zsem.at[1]).start()
            return carry

        lax.fori_loop(nu_ref[0], n_blocks, per_block, 0)
        _wait_chunks(n_tail, xs_hbm, zsem.at[0])

        def wait_block(b, carry):
            pltpu.make_async_copy(zero_ref, xs_hbm.at[pl.ds(0, bm), :], zsem.at[1]).wait()
            return carry

        lax.fori_loop(nu_ref[0], n_blocks, wait_block, 0)
        _wait_chunks(cnt_ref[slot], xs_hbm, sem.at[slot])

        @pl.when(nt >= 2)
        def _():
            _wait_chunks(cnt_ref[1 - slot], xs_hbm, sem.at[1 - slot])


def _dispatch_call(rt, x1, lposT, bm, n_blocks):
    T = x1.shape[0]
    tile = DISP_TILE
    grid_spec = pltpu.PrefetchScalarGridSpec(
        num_scalar_prefetch=5,
        grid=(T // tile,),
        in_specs=[
            pl.BlockSpec((tile, D_MODEL), lambda i, *_: (i, 0)),
            pl.BlockSpec((None, TOP_K, tile), lambda i, *_: (i, 0, 0)),
        ],
        out_specs=pl.BlockSpec(memory_space=pl.ANY),
        scratch_shapes=[
            pltpu.VMEM((2, DISP_LCAP, D_MODEL), F32),
            pltpu.VMEM((bm, D_MODEL), F32),
            pltpu.SemaphoreType.DMA((2,)),
            pltpu.SemaphoreType.DMA((2,)),
            pltpu.SMEM((2,), jnp.int32),
        ],
    )
    return pl.pallas_call(
        functools.partial(_dispatch_kernel, bm=bm, n_blocks=n_blocks),
        grid_spec=grid_spec,
        out_shape=jax.ShapeDtypeStruct((n_blocks * bm, D_MODEL), F32),
        compiler_params=pltpu.CompilerParams(dimension_semantics=("arbitrary",), vmem_limit_bytes=VMEM_LIMIT),
        name="dispatch",
    )(rt["nch"], rt["gbase"], rt["tail_start"], rt["tail_n"], rt["n_used"], x1, lposT)


def _experts_kernel(fb_ref, xs_hbm, wgu_ref, bgu_ref, wd_ref, bd_ref, y_hbm,
                    xs_ref, ys_ref, wgu_b, wd_b, xsem, ysem, *, bm, n_blocks):
    e = pl.program_id(0)
    n_used = fb_ref[N_EXPERTS]

    def x_copy(blk, slot):
        return pltpu.make_async_copy(xs_hbm.at[pl.ds(pl.multiple_of(blk * bm, bm), bm), :], xs_ref.at[slot],
                                     xsem.at[slot])

    def y_copy(blk, slot):
        return pltpu.make_async_copy(ys_ref.at[slot], y_hbm.at[pl.ds(pl.multiple_of(blk * bm, bm), bm), :],
                                     ysem.at[slot])

    n_buf = X_AHEAD + 1

    @pl.when(e == 0)
    def _():
        for b in range(X_AHEAD):
            @pl.when(b < n_used)
            def _():
                x_copy(b, b).start()

    wgu_b[...] = wgu_ref[...].astype(BF16)
    wd_b[...] = wd_ref[...].astype(BF16)

    def block(blk, carry):
        slot = blk % 2
        xslot = blk % n_buf
        x_copy(blk, xslot).wait()

        @pl.when(blk + X_AHEAD < n_used)
        def _():
            x_copy(blk + X_AHEAD, (blk + X_AHEAD) % n_buf).start()

        h = _dot(xs_ref[xslot].astype(BF16), wgu_b[...]) + bgu_ref[...]
        gate = jnp.minimum(h[:, :D_FF], SWIGLU_LIMIT)
        up = jnp.clip(h[:, D_FF:], -SWIGLU_LIMIT, SWIGLU_LIMIT)
        act = (up + 1.0) * (gate * jax.nn.sigmoid(gate * SWIGLU_ALPHA))
        y = _dot(act.astype(BF16), wd_b[...]) + bd_ref[...]

        @pl.when(blk >= 2)
        def _():
            y_copy(blk - 2, slot).wait()

        ys_ref[slot] = y
        y_copy(blk, slot).start()
        return carry

    lax.fori_loop(fb_ref[e], fb_ref[e + 1], block, 0)

    @pl.when(e == N_EXPERTS - 1)
    def _():
        y_copy(n_used - 1, (n_used - 1) % 2).wait()

        @pl.when(n_used >= 2)
        def _():
            y_copy(n_used - 2, n_used % 2).wait()

        ys_ref[0] = jnp.zeros(ys_ref.shape[1:], F32)

        def fill(blk, carry):
            y_copy(blk, 0).start()
            y_copy(blk, 0).wait()
            return carry

        lax.fori_loop(n_used, n_blocks, fill, 0)


def _experts_call(rt, xs, wgu, bgu, wd, bd, bm, n_blocks):
    E = N_EXPERTS
    by_expert = lambda e, fb: (e, 0, 0)
    grid_spec = pltpu.PrefetchScalarGridSpec(
        num_scalar_prefetch=1,
        grid=(E,),
        in_specs=[
            pl.BlockSpec(memory_space=pl.ANY),
            pl.BlockSpec((None, D_MODEL, 2 * D_FF), by_expert),
            pl.BlockSpec((None, 1, 2 * D_FF), by_expert),
            pl.BlockSpec((None, D_FF, D_MODEL), by_expert),
            pl.BlockSpec((None, 1, D_MODEL), by_expert),
        ],
        out_specs=pl.BlockSpec(memory_space=pl.ANY),
        scratch_shapes=[
            pltpu.VMEM((X_AHEAD + 1, bm, D_MODEL), F32),
            pltpu.VMEM((2, bm, D_MODEL), F32),
            pltpu.VMEM((D_MODEL, 2 * D_FF), BF16),
            pltpu.VMEM((D_FF, D_MODEL), BF16),
            pltpu.SemaphoreType.DMA((X_AHEAD + 1,)),
            pltpu.SemaphoreType.DMA((2,)),
        ],
    )
    return pl.pallas_call(
        functools.partial(_experts_kernel, bm=bm, n_blocks=n_blocks),
        grid_spec=grid_spec,
        out_shape=jax.ShapeDtypeStruct((n_blocks * bm, D_MODEL), F32),
        compiler_params=pltpu.CompilerParams(dimension_semantics=("arbitrary",), vmem_limit_bytes=VMEM_LIMIT),
        name="experts",
    )(rt["first_block"], xs, wgu, bgu.reshape(E, 1, 2 * D_FF), wd, bd.reshape(E, 1, D_MODEL))


def _combine_kernel(nch_ref, gbase_ref, y_hbm, x1_ref, lpos_ref, gate_ref, g_ref, b_ref, o_ref,
                    buf_ref, sem, cnt_ref):
    i = pl.program_id(0)
    nt = pl.num_programs(0)
    slot = i % 2

    @pl.when(i == 0)
    def _():
        buf_ref[...] = jnp.zeros(buf_ref.shape, F32)
        cnt_ref[0] = _group_copies(nch_ref, gbase_ref, 0, buf_ref.at[0], y_hbm, sem.at[0], False)

    @pl.when(i + 1 < nt)
    def _():
        cnt_ref[1 - slot] = _group_copies(nch_ref, gbase_ref, i + 1, buf_ref.at[1 - slot], y_hbm,
                                          sem.at[1 - slot], False)

    _wait_chunks(cnt_ref[slot], y_hbm, sem.at[slot])
    lpos = lpos_ref[...]
    gates = gate_ref[...]
    cols = DISP_PCHUNK
    y = jnp.zeros(o_ref.shape, F32)
    for cb in range(DISP_LCAP // cols):
        c = lax.broadcasted_iota(jnp.int32, (lpos.shape[0], cols), 1) + cb * cols
        w = jnp.zeros(c.shape, F32)
        for k in range(TOP_K):
            w = jnp.where(c == lpos[:, k:k + 1], gates[:, k:k + 1], w)
        y = y + _dot(w.astype(BF16), buf_ref[slot, cb * cols:(cb + 1) * cols, :].astype(BF16))
    o_ref[...] = _layer_norm(DN_ALPHA * x1_ref[...] + y, g_ref[...], b_ref[...])


def _combine_call(rt, y, x1, gates, g, b):
    T = x1.shape[0]
    tile = DISP_TILE
    grid_spec = pltpu.PrefetchScalarGridSpec(
        num_scalar_prefetch=2,
        grid=(T // tile,),
        in_specs=[
            pl.BlockSpec(memory_space=pl.ANY),
            pl.BlockSpec((tile, D_MODEL), lambda i, *_: (i, 0)),
            pl.BlockSpec((tile, TOP_K), lambda i, *_: (i, 0)),
            pl.BlockSpec((tile, LANES), lambda i, *_: (i, 0)),
            pl.BlockSpec(g.shape, lambda i, *_: (0, 0)),
            pl.BlockSpec(b.shape, lambda i, *_: (0, 0)),
        ],
        out_specs=pl.BlockSpec((tile, D_MODEL), lambda i, *_: (i, 0)),
        scratch_shapes=[
            pltpu.VMEM((2, DISP_LCAP, D_MODEL), F32),
            pltpu.SemaphoreType.DMA((2,)),
            pltpu.SMEM((2,), jnp.int32),
        ],
    )
    return pl.pallas_call(
        _combine_kernel,
        grid_spec=grid_spec,
        out_shape=jax.ShapeDtypeStruct((T, D_MODEL), F32),
        compiler_params=pltpu.CompilerParams(dimension_semantics=("arbitrary",), vmem_limit_bytes=VMEM_LIMIT),
        name="combine",
    )(rt["nch"], rt["gbase"], y, x1, rt["lpos"], gates, g, b)


def _rope_lane_order(n_sub):
    half = DIFF_HEAD_DIM // 2
    per = LANES // 2 // n_sub
    assert per == half or n_sub == 1
    cols = []
    for part in range(2):
        for sub in range(n_sub):
            cols.extend(sub * 64 + part * half + d for d in range(half))
    return np.asarray(cols)


def _prep_weights(w_in, mla_q_norm_g, w_uq, mla_kv_norm_g, w_ukv):
    o_dq, o_dk, o_dv, o_cq, o_ckv, o_kr = 0, 512, 1024, 1536, 1792, 1920
    head_order = _rope_lane_order(2)
    diff_cols = np.concatenate([h * LANES + head_order for h in range(DIFF_HEADS)])
    w = {}
    ckv = w_in[:, o_ckv:o_kr]
    w["feat"] = jnp.concatenate([w_in[:, o_dq + diff_cols], w_in[:, o_dv:o_cq], w_in[:, o_cq:o_ckv], ckv],
                                axis=1).T.astype(BF16)

    def spread_rope(cols64):
        z = jnp.zeros((cols64.shape[0], 32), cols64.dtype)
        return jnp.concatenate([cols64[:, :32], z, cols64[:, 32:], z], axis=1)

    w["tok"] = jnp.concatenate([w_in[:, o_dk + diff_cols], ckv, spread_rope(w_in[:, o_kr:o_kr + MLA_ROPE_DIM])],
                               axis=1).astype(BF16)
    uq = []
    for h in range(MLA_HEADS):
        base = h * MLA_QK_DIM
        uq.append(w_uq[:, base:base + MLA_NOPE_DIM])
        uq.append(spread_rope(w_uq[:, base + MLA_NOPE_DIM:base + MLA_QK_DIM]))
    w["uqT"] = jnp.concatenate(uq, axis=1).T.astype(BF16)
    per = MLA_NOPE_DIM + MLA_V_DIM
    w["uk"] = jnp.concatenate([w_ukv[:, h * per:h * per + MLA_NOPE_DIM] for h in range(MLA_HEADS)], axis=1).astype(BF16)
    w["uvT"] = jnp.concatenate(
        [w_ukv[:, h * per + MLA_NOPE_DIM:(h + 1) * per] for h in range(MLA_HEADS)], axis=1).T.astype(BF16)
    w["gq"] = mla_q_norm_g.reshape(MLA_Q_RANK, 1)
    w["gkv"] = mla_kv_norm_g.reshape(1, MLA_KV_RANK)
    w["gkvc"] = mla_kv_norm_g.reshape(MLA_KV_RANK, 1)
    return w


def _rope_tables(positions):
    half = MLA_ROPE_DIM // 2
    inv_freq = 1.0 / (ROPE_THETA ** (jnp.arange(0, MLA_ROPE_DIM, 2, dtype=F32) / MLA_ROPE_DIM))
    ang = positions.astype(F32)[..., None] * inv_freq
    ang = jnp.tile(ang, (1, 1, LANES // half))
    sign = jnp.where(jnp.arange(LANES) < LANES // 2, -1.0, 1.0).astype(F32)
    return jnp.cos(ang), jnp.sin(ang) * sign


def kernel(x, positions, w_in, lambda_q1, lambda_k1, lambda_q2, lambda_k2, subln_g, mla_q_norm_g, w_uq,
           mla_kv_norm_g, w_ukv, w_o, ln1_g, ln1_b, w_router, b_router, w_gate_up, b_gate_up, w_down, b_down,
           ln2_g, ln2_b):
    B, S, D = x.shape
    T = B * S
    l = 0
    x2 = x.reshape(T, D)
    w = _prep_weights(w_in[l], mla_q_norm_g[l], w_uq[l], mla_kv_norm_g[l], w_ukv[l])

    dqT, dk, dvT, mqT, mk, mvT = _proj_call(x, _rope_tables(positions), w)
    lam_vecs = [v[l].reshape(1, DIFF_HEAD_DIM) for v in (lambda_q1, lambda_k1, lambda_q2, lambda_k2)]
    o_a = _attn_call(_diff_attn_kernel, lam_vecs + [subln_g[l].reshape(DIFF_V_DIM, 1)],
                     dqT, dk, dvT, DIFF_HEADS, 1, LANES, "diff_attn")
    o_b = _attn_call(_mla_attn_kernel, [], mqT, mk, mvT, MLA_HEADS // 2, 2, MLA_QK_PAD, "mla_attn")

    wr = jnp.pad(w_router[l], ((0, 0), (0, LANES - N_EXPERTS)))
    br = jnp.pad(b_router[l], (0, LANES - N_EXPERTS), constant_values=NEG_BIG).reshape(1, LANES)
    x1, idx, gates = _post_call(
        o_a.reshape(T, -1), o_b.reshape(T, -1), x2, w_o[l].astype(BF16),
        ln1_g[l].reshape(1, D), ln1_b[l].reshape(1, D), wr, br)

    bm = MOE_BM
    A = T * TOP_K
    n_tiles = T // DISP_TILE
    n_blocks = pl.cdiv(A + n_tiles * N_EXPERTS * (ROW_CHUNK - 1) + N_EXPERTS * (bm - ROW_CHUNK), bm)
    rt = _route_call(idx[:, :TOP_K].reshape(A), bm)
    lposT = rt["lpos"].reshape(n_tiles, DISP_TILE, TOP_K).transpose(0, 2, 1)
    xs = _dispatch_call(rt, x1, lposT, bm, n_blocks)
    y = _experts_call(rt, xs, w_gate_up[l], b_gate_up[l], w_down[l], b_down[l], bm, n_blocks)
    out = _combine_call(rt, y, x1, gates, ln2_g[l].reshape(1, D), ln2_b[l].reshape(1, D))
    return out.reshape(B, S, D)
```

```python
import functools
import math

import numpy as np
import jax
import jax.numpy as jnp
from jax import lax
from jax.experimental import pallas as pl
from jax.experimental.pallas import tpu as pltpu

D_MODEL = 1024
DIFF_HEADS = 4
DIFF_HEAD_DIM = 64
DIFF_V_DIM = 128
MLA_HEADS = 4
MLA_V_DIM = 128
MLA_NOPE_DIM = 128
MLA_ROPE_DIM = 64
MLA_QK_DIM = MLA_NOPE_DIM + MLA_ROPE_DIM
MLA_Q_RANK = 256
MLA_KV_RANK = 128
DIFF_Q_COLS = DIFF_K_COLS = DIFF_V_COLS = 512
ROPE_THETA = 10000.0
N_EXPERTS = 32
TOP_K = 4
D_FF = 1024
SWIGLU_LIMIT = 7.0
SWIGLU_ALPHA = 1.702
LN_EPS = 1e-5
SUBLN_EPS = 1e-5
MLA_RMS_EPS = 1e-6
DEPTH = 1
DN_ALPHA = (2.0 * DEPTH) ** 0.25
LAMBDA_INIT = 0.8 - 0.6 * math.exp(-0.3 * 0)

LANES = 128
MLA_QK_PAD = 2 * LANES
BF16_ROWS = 16
PLANES = 8
VAL_ROWS = LANES + BF16_ROWS
VMEM_LIMIT = 56 * 1024 * 1024

ATTN_TILE = 512
POST_TM = 1024
MOE_BM = 256
X_AHEAD = 3
ROW_CHUNK = 8
BIG_CHUNK = 32
DISP_TILE = 512
DISP_PCHUNK = 256
DISP_LCAP = -(-(DISP_TILE * TOP_K + N_EXPERTS * (ROW_CHUNK - 1)) // DISP_PCHUNK) * DISP_PCHUNK

NEG_BIG = -1e30
LOG2E = math.log2(math.e)
F32 = jnp.float32
BF16 = jnp.bfloat16


def _dot(a, b):
    return jnp.dot(a, b, preferred_element_type=F32)


def _dot_nt(a, b):
    return lax.dot_general(a, b, (((1,), (1,)), ((), ())), preferred_element_type=F32)


def _rope128(blk, cos, sin):
    return blk * cos + pltpu.roll(blk, 64, axis=1) * sin


def _rope128_t(blk, cos, sin):
    half = LANES // 2
    rolled = jnp.concatenate([blk[half:], blk[:half]], axis=0)
    return blk * cos + rolled * sin


def _rms_rows(t, g, eps):
    return t * lax.rsqrt(jnp.mean(t * t, axis=-1, keepdims=True) + eps) * g


def _rms_cols(t, g, eps):
    return t * lax.rsqrt(jnp.mean(t * t, axis=0, keepdims=True) + eps) * g


def _store_values(vT_ref, vT, heads):
    ones = jnp.ones((BF16_ROWS, vT.shape[1]), BF16)
    for h in range(heads):
        vT_ref[h * VAL_ROWS:h * VAL_ROWS + LANES, :] = vT[h * LANES:(h + 1) * LANES].astype(BF16)
        vT_ref[h * VAL_ROWS + LANES:(h + 1) * VAL_ROWS, :] = ones


def _proj_kernel(x_ref, cos_ref, sin_ref, wfeat_ref, wtok_ref, gq_ref, gkv_ref, gkvc_ref,
                 wuqT_ref, wuk_ref, wuvT_ref, dqT_ref, dk_ref, dvT_ref, mqT_ref, mk_ref, mvT_ref):
    xb = x_ref[...].astype(BF16)
    cos, sin = cos_ref[...], sin_ref[...]
    cosT, sinT = cos.T, sin.T

    dq_scale = DIFF_HEAD_DIM ** -0.5 * LOG2E
    mq_scale = MLA_QK_DIM ** -0.5 * LOG2E
    o_dv, o_cq, o_ckv = DIFF_Q_COLS, DIFF_Q_COLS + DIFF_V_COLS, DIFF_Q_COLS + DIFF_V_COLS + MLA_Q_RANK

    feat = _dot_nt(wfeat_ref[...], xb)
    for h in range(DIFF_HEADS):
        sl = slice(h * LANES, (h + 1) * LANES)
        dqT_ref[sl, :] = (_rope128_t(feat[sl], cosT, sinT) * dq_scale).astype(BF16)
    _store_values(dvT_ref, feat[o_dv:o_cq], DIFF_HEADS)

    cqT = _rms_cols(feat[o_cq:o_ckv], gq_ref[...], MLA_RMS_EPS)
    qT = _dot(wuqT_ref[...], cqT.astype(BF16))
    for h in range(MLA_HEADS):
        nope = slice(h * MLA_QK_PAD, h * MLA_QK_PAD + LANES)
        ropes = slice(h * MLA_QK_PAD + LANES, (h + 1) * MLA_QK_PAD)
        mqT_ref[nope, :] = (qT[nope] * mq_scale).astype(BF16)
        mqT_ref[ropes, :] = (_rope128_t(qT[ropes], cosT, sinT) * mq_scale).astype(BF16)

    ckvT = _rms_cols(feat[o_ckv:], gkvc_ref[...], MLA_RMS_EPS)
    _store_values(mvT_ref, _dot(wuvT_ref[...], ckvT.astype(BF16)), MLA_HEADS)

    tokm = _dot(xb, wtok_ref[...])
    for h in range(DIFF_HEADS):
        sl = slice(h * LANES, (h + 1) * LANES)
        dk_ref[:, sl] = _rope128(tokm[:, sl], cos, sin).astype(BF16)
    ckv = _rms_rows(tokm[:, DIFF_K_COLS:DIFF_K_COLS + MLA_KV_RANK], gkv_ref[...], MLA_RMS_EPS)
    k_nope = _dot(ckv.astype(BF16), wuk_ref[...])
    k_pe = _rope128(tokm[:, DIFF_K_COLS + MLA_KV_RANK:], cos, sin).astype(BF16)
    for h in range(MLA_HEADS):
        mk_ref[:, h * MLA_QK_PAD:h * MLA_QK_PAD + LANES] = k_nope[:, h * LANES:(h + 1) * LANES].astype(BF16)
        mk_ref[:, h * MLA_QK_PAD + LANES:(h + 1) * MLA_QK_PAD] = k_pe


def _proj_call(x3, tabs, w):
    B, S, D = x3.shape
    tm = ATTN_TILE
    nt = S // tm
    cos_t, sin_t = tabs
    weights = [w["feat"], w["tok"], w["gq"], w["gkv"], w["gkvc"], w["uqT"], w["uk"], w["uvT"]]
    tok = lambda b, i: (b, i, 0)
    feat = lambda b, i: (b, i, 0, 0)
    in_specs = [pl.BlockSpec((None, tm, D), tok),
                pl.BlockSpec((None, tm, LANES), tok), pl.BlockSpec((None, tm, LANES), tok)]
    in_specs += [pl.BlockSpec(a.shape, lambda b, i: (0, 0)) for a in weights]
    mq_w = MLA_HEADS * MLA_QK_PAD
    dv_w, mv_w = DIFF_HEADS * VAL_ROWS, MLA_HEADS * VAL_ROWS
    out_specs = [pl.BlockSpec((None, None, 512, tm), feat), pl.BlockSpec((None, tm, 512), tok),
                 pl.BlockSpec((None, None, dv_w, tm), feat), pl.BlockSpec((None, None, mq_w, tm), feat),
                 pl.BlockSpec((None, tm, mq_w), tok), pl.BlockSpec((None, None, mv_w, tm), feat)]
    out_shape = [jax.ShapeDtypeStruct((B, nt, 512, tm), BF16), jax.ShapeDtypeStruct((B, S, 512), BF16),
                 jax.ShapeDtypeStruct((B, nt, dv_w, tm), BF16), jax.ShapeDtypeStruct((B, nt, mq_w, tm), BF16),
                 jax.ShapeDtypeStruct((B, S, mq_w), BF16), jax.ShapeDtypeStruct((B, nt, mv_w, tm), BF16)]
    return pl.pallas_call(
        _proj_kernel,
        grid=(B, nt),
        in_specs=in_specs,
        out_specs=out_specs,
        out_shape=out_shape,
        compiler_params=pltpu.CompilerParams(dimension_semantics=("arbitrary", "arbitrary"),
                                             vmem_limit_bytes=VMEM_LIMIT),
        name="proj",
    )(x3, cos_t, sin_t, *weights)


def _flash_body(qTs, keys_of, values_of, acc_ref, m_ref, s_refs, *, t):
    i = pl.program_id(2)
    m_ref[...] = jnp.full(m_ref.shape, -jnp.inf, F32)
    acc_ref[...] = jnp.zeros(acc_ref.shape, F32)

    def scores(j, u):
        s_refs[u][...] = _dot(keys_of(u, j), qTs[u])

    def finish(j, u, masked):
        s = s_refs[u][...]
        if masked:
            key = lax.broadcasted_iota(jnp.int32, s.shape, 0)
            qry = lax.broadcasted_iota(jnp.int32, s.shape, 1)
            s = jnp.where(key <= qry, s, -jnp.inf)
        m_prev = m_ref[u]
        m_new = jnp.maximum(m_prev, jnp.max(s, axis=0, keepdims=True))
        alpha = jnp.exp2(m_prev - m_new)
        p = jnp.exp2((s - m_new).astype(BF16))
        acc_ref[u] = alpha * acc_ref[u] + _dot(values_of(u, j), p)
        m_ref[u] = m_new

    def full_step(j, carry):
        scores(j, 1)
        finish(j, 0, False)
        scores(j + 1, 0)
        finish(j, 1, False)
        return carry

    def two_steps(jj, carry):
        return full_step(2 * jj + 1, full_step(2 * jj, carry))

    scores(0, 0)
    lax.fori_loop(0, lax.shift_right_logical(i, 1), two_steps, 0)

    @pl.when(i % 2 == 1)
    def _():
        full_step(i - 1, 0)

    scores(i, 1)
    finish(i, 0, True)
    finish(i, 1, True)


def _key_tile(k_ref, j, t, lanes=slice(None)):
    return k_ref[pl.ds(pl.multiple_of(j * t, t), t), lanes]


def _normalized(acc_ref, u):
    return acc_ref[u, :LANES, :] / acc_ref[u, LANES:LANES + 1, :]


def _diff_attn_kernel(lq1_ref, lk1_ref, lq2_ref, lk2_ref, g_ref, qT_ref, k_ref, vT_ref, o_ref,
                      acc_ref, m_ref, sa_ref, sb_ref, *, t):
    qT = qT_ref[...]
    row = lax.broadcasted_iota(jnp.int32, qT.shape, 0)
    first = (row % 64) < 32
    zero = jnp.zeros_like(qT)
    qTs = (jnp.where(first, qT, zero), jnp.where(first, zero, qT))
    _flash_body(qTs, lambda u, j: _key_tile(k_ref, j, t), lambda u, j: vT_ref[j],
                acc_ref, m_ref, (sa_ref, sb_ref), t=t)

    lam = (jnp.exp(jnp.sum(lq1_ref[...] * lk1_ref[...], axis=-1, keepdims=True))
           - jnp.exp(jnp.sum(lq2_ref[...] * lk2_ref[...], axis=-1, keepdims=True)) + LAMBDA_INIT)
    oT = _normalized(acc_ref, 0) - lam * _normalized(acc_ref, 1)
    oT = _rms_cols(oT, g_ref[...], SUBLN_EPS) * (1.0 - LAMBDA_INIT)
    o_ref[...] = oT.T.astype(o_ref.dtype)


def _mla_attn_kernel(qT_ref, k_ref, vT_ref, o_ref, acc_ref, m_ref, sa_ref, sb_ref, *, t):
    dk = MLA_QK_PAD
    qTs = tuple(qT_ref[u * dk:(u + 1) * dk, :] for u in range(2))
    _flash_body(qTs, lambda u, j: _key_tile(k_ref, j, t, slice(u * dk, (u + 1) * dk)),
                lambda u, j: vT_ref[j, u * VAL_ROWS:(u + 1) * VAL_ROWS, :],
                acc_ref, m_ref, (sa_ref, sb_ref), t=t)
    for u in range(2):
        o_ref[:, u * LANES:(u + 1) * LANES] = _normalized(acc_ref, u).T.astype(o_ref.dtype)


def _attn_call(kernel, extra, qT, k, vT, groups, heads_per_group, dk, name):
    B, nt, _, t = qT.shape
    S = nt * t
    hp = heads_per_group
    in_specs = [pl.BlockSpec(a.shape, lambda b, h, i: (0, 0)) for a in extra]
    in_specs += [
        pl.BlockSpec((None, None, hp * dk, t), lambda b, h, i: (b, i, h, 0)),
        pl.BlockSpec((None, S, hp * dk), lambda b, h, i: (b, 0, h)),
        pl.BlockSpec((None, nt, hp * VAL_ROWS, t), lambda b, h, i: (b, 0, h, 0)),
    ]
    return pl.pallas_call(
        functools.partial(kernel, t=t),
        grid=(B, groups, nt),
        in_specs=in_specs,
        out_specs=pl.BlockSpec((None, t, hp * LANES), lambda b, h, i: (b, i, h)),
        out_shape=jax.ShapeDtypeStruct((B, S, groups * hp * LANES), BF16),
        scratch_shapes=[
            pltpu.VMEM((2, VAL_ROWS, t), F32),
            pltpu.VMEM((2, 1, t), F32),
            pltpu.VMEM((t, t), F32),
            pltpu.VMEM((t, t), F32),
        ],
        compiler_params=pltpu.CompilerParams(
            dimension_semantics=("arbitrary", "arbitrary", "arbitrary"), vmem_limit_bytes=VMEM_LIMIT),
        name=name,
    )(*extra, qT, k, vT)


def _layer_norm(y, g, b):
    mu = jnp.mean(y, axis=-1, keepdims=True)
    d = y - mu
    var = jnp.mean(d * d, axis=-1, keepdims=True)
    return d * lax.rsqrt(var + LN_EPS) * g + b


def _split_bf16(a):
    hi = a.astype(BF16)
    return hi, (a - hi.astype(F32)).astype(BF16)


def _post_kernel(oa_ref, ob_ref, x_ref, wo_ref, g_ref, b_ref, wrh_ref, wrl_ref, br_ref,
                 x1_ref, eplane_ref, gate_ref):
    half = oa_ref.shape[1]
    mixed = _dot(oa_ref[...], wo_ref[:half, :]) + _dot(ob_ref[...], wo_ref[half:, :])
    x1 = _layer_norm(DN_ALPHA * x_ref[...] + mixed, g_ref[...], b_ref[...])
    x1_ref[...] = x1

    x_hi, x_lo = _split_bf16(x1)
    logits = (_dot(x_hi, wrh_ref[...]) + _dot(x_lo, wrh_ref[...]) + _dot(x_hi, wrl_ref[...])) + br_ref[...]
    lane = lax.broadcasted_iota(jnp.int32, logits.shape, 1).astype(F32)
    work = logits
    vals, idxs = [], []
    for _ in range(TOP_K):
        m = jnp.max(work, axis=-1, keepdims=True)
        idx = jnp.min(jnp.where(work == m, lane, float(LANES)), axis=-1, keepdims=True)
        vals.append(m)
        idxs.append(idx)
        work = jnp.where(lane == idx, NEG_BIG, work)
    es = [jnp.exp(v - vals[0]) for v in vals]
    den = es[0] + es[1] + es[2] + es[3]
    idx_out = jnp.zeros(logits.shape, F32)
    gate_out = jnp.zeros(logits.shape, F32)
    for k in range(TOP_K):
        idx_out = jnp.where(lane == float(k), idxs[k], idx_out)
        gate_out = jnp.where(lane == float(k), es[k] / den, gate_out)
    gate_ref[...] = gate_out
    idx_t = idx_out.T
    for j in range(eplane_ref.shape[1]):
        eplane_ref[:, j, :] = idx_t[:PLANES, j * LANES:(j + 1) * LANES].astype(jnp.int32)


def _post_call(oa, ob, x2, wo, g, b, wr, br):
    T = x2.shape[0]
    tm = POST_TM
    row = lambda i: (i, 0)
    full = lambda i: (0, 0)
    return pl.pallas_call(
        _post_kernel,
        grid=(T // tm,),
        in_specs=[
            pl.BlockSpec((tm, oa.shape[1]), row), pl.BlockSpec((tm, ob.shape[1]), row),
            pl.BlockSpec((tm, D_MODEL), row), pl.BlockSpec(wo.shape, full),
            pl.BlockSpec(g.shape, full), pl.BlockSpec(b.shape, full),
            pl.BlockSpec(wr.shape, full), pl.BlockSpec(wr.shape, full), pl.BlockSpec(br.shape, full),
        ],
        out_specs=[
            pl.BlockSpec((tm, D_MODEL), row), pl.BlockSpec((PLANES, tm // LANES, LANES), lambda i: (0, i, 0)),
            pl.BlockSpec((tm, LANES), row),
        ],
        out_shape=[
            jax.ShapeDtypeStruct((T, D_MODEL), F32),
            jax.ShapeDtypeStruct((PLANES, T // LANES, LANES), jnp.int32), jax.ShapeDtypeStruct((T, LANES), F32),
        ],
        compiler_params=pltpu.CompilerParams(dimension_semantics=("arbitrary",), vmem_limit_bytes=VMEM_LIMIT),
        name="post",
    )(oa, ob, x2, wo, g, b, *_split_bf16(wr), br)


def _lane_cumsum(x, n):
    lane = lax.broadcasted_iota(jnp.int32, x.shape, 1)
    s = 1
    while s < n:
        x = x + jnp.where(lane >= s, pltpu.roll(x, s, axis=1), 0.0)
        s *= 2
    return x


def _route_kernel(e_ref, lpos_ref, nch_ref, gbase_ref, meta_ref, *, bm, rows_per_tile):
    planes = [e_ref[k] for k in range(TOP_K)]
    R = planes[0].shape[0]
    shape = planes[0].shape
    nt = R // rows_per_tile
    r_i = lax.broadcasted_iota(jnp.int32, (LANES, LANES), 0)
    c_i = lax.broadcasted_iota(jnp.int32, (LANES, LANES), 1)
    lane_incl = (r_i <= c_i).astype(BF16)
    rr = lax.broadcasted_iota(jnp.int32, (R, R), 0)
    rc = lax.broadcasted_iota(jnp.int32, (R, R), 1)
    same_tile = (rr // rows_per_tile) == (rc // rows_per_tile)
    rows_before = jnp.logical_and(rc < rr, same_tile).astype(BF16)
    tr = lax.broadcasted_iota(jnp.int32, (nt, R), 0)
    tc = lax.broadcasted_iota(jnp.int32, (nt, R), 1)
    tile_rows = (tc // rows_per_tile == tr).astype(BF16)
    er = lax.broadcasted_iota(jnp.int32, (R, nt), 0)
    ec = lax.broadcasted_iota(jnp.int32, (R, nt), 1)
    row_tile = (er // rows_per_tile == ec).astype(F32)
    lt_r = lax.broadcasted_iota(jnp.int32, (nt, nt), 0)
    lt_c = lax.broadcasted_iota(jnp.int32, (nt, nt), 1)
    tiles_before = (lt_c < lt_r).astype(BF16)
    lane_t = lax.broadcasted_iota(jnp.int32, (nt, LANES), 1)

    ranks = [jnp.zeros(shape, F32) for _ in range(TOP_K)]
    cnt = jnp.zeros((nt, LANES), F32)
    for ex in range(N_EXPERTS):
        hits = [pk == ex for pk in planes]
        any_hit = functools.reduce(jnp.logical_or, hits)
        m = jnp.where(any_hit, 1.0, 0.0).astype(BF16)
        incl = _dot(m, lane_incl)
        row_tot = jnp.broadcast_to(incl[:, LANES - 1:LANES], shape).astype(BF16)
        before = _dot(rows_before, row_tot)
        rank_e = incl - 1.0 + before
        ranks = [jnp.where(h, rank_e, r) for h, r in zip(hits, ranks)]
        cnt = jnp.where(lane_t == ex, _dot(tile_rows, row_tot), cnt)

    chunk = float(ROW_CHUNK)
    cnt8 = jnp.floor((cnt + (chunk - 1.0)) * (1.0 / chunk)) * chunk
    lstart = _lane_cumsum(cnt8, N_EXPERTS) - cnt8
    tile_pre = _dot(tiles_before, cnt8.astype(BF16))
    tot8 = tile_pre[nt - 1:nt] + cnt8[nt - 1:nt]
    tot_bm = jnp.floor((tot8 + (bm - 1.0)) * (1.0 / bm)) * bm
    end_incl = _lane_cumsum(tot_bm, N_EXPERTS)
    ebase = end_incl - tot_bm

    lrow = jnp.dot(row_tile, lstart, preferred_element_type=F32,
                   precision=lax.Precision.HIGHEST)
    starts = [jnp.zeros(shape, F32) for _ in range(TOP_K)]
    for ex in range(N_EXPERTS):
        starts = [jnp.where(pk == ex, lrow[:, ex:ex + 1], st) for pk, st in zip(planes, starts)]
    lpos_ref[...] = jnp.zeros(lpos_ref.shape, jnp.int32)
    for k in range(TOP_K):
        lpos_ref[k] = (starts[k] + ranks[k]).astype(jnp.int32)
    nch_ref[...] = (cnt8 * (1.0 / chunk)).astype(jnp.int32)
    gbase_ref[...] = (ebase + tile_pre).astype(jnp.int32)

    sub = lax.broadcasted_iota(jnp.int32, meta_ref.shape, 0)
    meta = jnp.where(sub == 0, ebase + tot8, 0.0)
    meta = jnp.where(sub == 1, (tot_bm - tot8) * (1.0 / chunk), meta)
    n_used = end_incl[:, N_EXPERTS - 1:N_EXPERTS] * (1.0 / bm)
    meta = jnp.where(sub == 2, n_used, meta)
    lane_m = lax.broadcasted_iota(jnp.int32, meta_ref.shape, 1)
    meta = jnp.where(sub == 3, jnp.where(lane_m < N_EXPERTS, ebase * (1.0 / bm), n_used), meta)
    meta_ref[...] = meta.astype(jnp.int32)


def _route_call(eplanes, bm):
    R = eplanes.shape[1]
    rows_per_tile = DISP_TILE // LANES
    nt = R // rows_per_tile
    full = lambda i: (0,) * 2
    shapes = [(nt, LANES), (nt, LANES), (8, LANES)]
    lpos, nch, gbase, meta = pl.pallas_call(
        functools.partial(_route_kernel, bm=bm, rows_per_tile=rows_per_tile),
        grid=(1,),
        in_specs=[pl.BlockSpec(eplanes.shape, lambda i: (0, 0, 0))],
        out_specs=[pl.BlockSpec(eplanes.shape, lambda i: (0, 0, 0))] + [pl.BlockSpec(s, full) for s in shapes],
        out_shape=[jax.ShapeDtypeStruct(eplanes.shape, jnp.int32)]
        + [jax.ShapeDtypeStruct(s, jnp.int32) for s in shapes],
        compiler_params=pltpu.CompilerParams(dimension_semantics=("arbitrary",), vmem_limit_bytes=VMEM_LIMIT),
        name="route",
    )(eplanes)
    E = N_EXPERTS
    return dict(lpos=lpos, nch=nch[:, :E].reshape(-1), gbase=gbase[:, :E].reshape(-1),
                tail_start=meta[0, :E], tail_n=meta[1, :E], n_used=meta[2, :1], first_block=meta[3, :E + 1])


def _chunk_rows(ref, first, rows=ROW_CHUNK):
    return ref.at[pl.ds(pl.multiple_of(first, ROW_CHUNK), rows), :]


def _group_copies(nch_ref, gbase_ref, tile, buf, hbm, sem, to_hbm):
    per_big = BIG_CHUNK // ROW_CHUNK

    def copy(local, remote):
        if to_hbm:
            pltpu.make_async_copy(local, remote, sem).start()
        else:
            pltpu.make_async_copy(remote, local, sem).start()

    def per_expert(ex, done):
        n = nch_ref[tile * N_EXPERTS + ex]
        g0 = gbase_ref[tile * N_EXPERTS + ex]
        l0 = done * ROW_CHUNK
        n_big = lax.shift_right_logical(n, per_big.bit_length() - 1)

        def big(c, carry):
            copy(_chunk_rows(buf, l0 + c * BIG_CHUNK, BIG_CHUNK), _chunk_rows(hbm, g0 + c * BIG_CHUNK, BIG_CHUNK))
            return carry

        def small(c, carry):
            copy(_chunk_rows(buf, l0 + c * ROW_CHUNK), _chunk_rows(hbm, g0 + c * ROW_CHUNK))
            return carry

        lax.fori_loop(0, n_big, big, 0)
        lax.fori_loop(n_big * per_big, n, small, 0)
        return done + n

    return lax.fori_loop(0, N_EXPERTS, per_expert, 0)


def _wait_chunks(n, hbm, sem):
    rows = pl.multiple_of(n * ROW_CHUNK, ROW_CHUNK)

    @pl.when(n > 0)
    def _():
        pltpu.make_async_copy(hbm.at[pl.ds(0, rows), :], hbm.at[pl.ds(0, rows), :], sem).wait()


def _tile_planes(lpos_ref, i):
    rpt = DISP_TILE // LANES
    per8 = 8 // rpt
    eight = lpos_ref[:, pl.ds(pl.multiple_of((i // per8) * 8, 8), 8), :]
    out = eight[:, :rpt, :]
    for sft in range(1, per8):
        out = jnp.where(i % per8 == sft, eight[:, sft * rpt:(sft + 1) * rpt, :], out)
    return out


def _dispatch_kernel(nch_ref, gbase_ref, tstart_ref, tn_ref, nu_ref, x_ref, lpos_ref, xs_hbm,
                     buf_ref, zero_ref, sem, zsem, cnt_ref, *, bm, n_blocks):
    i = pl.program_id(0)
    nt = pl.num_programs(0)
    slot = i % 2

    @pl.when(i >= 2)
    def _():
        _wait_chunks(cnt_ref[slot], xs_hbm, sem.at[slot])

    xb = x_ref[...].astype(BF16)
    lpos = _tile_planes(lpos_ref, i)
    rows = DISP_PCHUNK
    for rb in range(DISP_LCAP // rows):
        r = lax.broadcasted_iota(jnp.int32, (rows, LANES), 0) + rb * rows
        pieces = []
        for j in range(lpos.shape[1]):
            hit = r == lpos[0, j:j + 1, :]
            for k in range(1, TOP_K):
                hit = jnp.logical_or(hit, r == lpos[k, j:j + 1, :])
            pieces.append(jnp.where(hit, 1.0, 0.0).astype(BF16))
        perm = jnp.concatenate(pieces, axis=1)
        buf_ref[slot, rb * rows:(rb + 1) * rows, :] = _dot(perm, xb)
    cnt_ref[slot] = _group_copies(nch_ref, gbase_ref, i, buf_ref.at[slot], xs_hbm, sem.at[slot], True)

    @pl.when(i == nt - 1)
    def _():
        zero_ref[...] = jnp.zeros(zero_ref.shape, F32)

        def per_expert(ex, done):
            first = tstart_ref[ex]

            def per_chunk(c, carry):
                pltpu.make_async_copy(_chunk_rows(zero_ref, 0), _chunk_rows(xs_hbm, first + c * ROW_CHUNK),
                                      zsem.at[0]).start()
                return carry

            lax.fori_loop(0, tn_ref[ex], per_chunk, 0)
            return done + tn_ref[ex]

        n_tail = lax.fori_loop(0, N_EXPERTS, per_expert, 0)

        def per_block(b, carry):
            pltpu.make_async_copy(zero_ref, xs_hbm.at[pl.ds(pl.multiple_of(b * bm, bm), bm), :], zsem.at[1]).start()
            return carry

        lax.fori_loop(nu_ref[0], n_blocks, per_block, 0)
        _wait_chunks(n_tail, xs_hbm, zsem.at[0])

        def wait_block(b, carry):
            pltpu.make_async_copy(zero_ref, xs_hbm.at[pl.ds(0, bm), :], zsem.at[1]).wait()
            return carry

        lax.fori_loop(nu_ref[0], n_blocks, wait_block, 0)
        _wait_chunks(cnt_ref[slot], xs_hbm, sem.at[slot])

        @pl.when(nt >= 2)
        def _():
            _wait_chunks(cnt_ref[1 - slot], xs_hbm, sem.at[1 - slot])


def _dispatch_call(rt, x1, bm, n_blocks):
    T = x1.shape[0]
    tile = DISP_TILE
    grid_spec = pltpu.PrefetchScalarGridSpec(
        num_scalar_prefetch=5,
        grid=(T // tile,),
        in_specs=[
            pl.BlockSpec((tile, D_MODEL), lambda i, *_: (i, 0)),
            pl.BlockSpec(rt["lpos"].shape, lambda i, *_: (0, 0, 0)),
        ],
        out_specs=pl.BlockSpec(memory_space=pl.ANY),
        scratch_shapes=[
            pltpu.VMEM((2, DISP_LCAP, D_MODEL), F32),
            pltpu.VMEM((bm, D_MODEL), F32),
            pltpu.SemaphoreType.DMA((2,)),
            pltpu.SemaphoreType.DMA((2,)),
            pltpu.SMEM((2,), jnp.int32),
        ],
    )
    return pl.pallas_call(
        functools.partial(_dispatch_kernel, bm=bm, n_blocks=n_blocks),
        grid_spec=grid_spec,
        out_shape=jax.ShapeDtypeStruct((n_blocks * bm, D_MODEL), F32),
        compiler_params=pltpu.CompilerParams(dimension_semantics=("arbitrary",), vmem_limit_bytes=VMEM_LIMIT),
        name="dispatch",
    )(rt["nch"], rt["gbase"], rt["tail_start"], rt["tail_n"], rt["n_used"], x1, rt["lpos"])


def _experts_kernel(fb_ref, xs_hbm, wgu_ref, bgu_ref, wd_ref, bd_ref, y_hbm,
                    xs_ref, ys_ref, wgu_b, wd_b, xsem, ysem, *, bm, n_blocks):
    e = pl.program_id(0)
    n_used = fb_ref[N_EXPERTS]

    def x_copy(blk, slot):
        return pltpu.make_async_copy(xs_hbm.at[pl.ds(pl.multiple_of(blk * bm, bm), bm), :], xs_ref.at[slot],
                                     xsem.at[slot])

    def y_copy(blk, slot):
        return pltpu.make_async_copy(ys_ref.at[slot], y_hbm.at[pl.ds(pl.multiple_of(blk * bm, bm), bm), :],
                                     ysem.at[slot])

    n_buf = X_AHEAD + 1

    @pl.when(e == 0)
    def _():
        for b in range(X_AHEAD):
            @pl.when(b < n_used)
            def _():
                x_copy(b, b).start()

    wgu_b[...] = wgu_ref[...].astype(BF16)
    wd_b[...] = wd_ref[...].astype(BF16)

    def block(blk, carry):
        slot = blk % 2
        xslot = blk % n_buf
        x_copy(blk, xslot).wait()

        @pl.when(blk + X_AHEAD < n_used)
        def _():
            x_copy(blk + X_AHEAD, (blk + X_AHEAD) % n_buf).start()

        h = _dot(xs_ref[xslot].astype(BF16), wgu_b[...]) + bgu_ref[...]
        gate = jnp.minimum(h[:, :D_FF], SWIGLU_LIMIT)
        up = jnp.clip(h[:, D_FF:], -SWIGLU_LIMIT, SWIGLU_LIMIT)
        act = (up + 1.0) * (gate * jax.nn.sigmoid(gate * SWIGLU_ALPHA))
        y = _dot(act.astype(BF16), wd_b[...]) + bd_ref[...]

        @pl.when(blk >= 2)
        def _():
            y_copy(blk - 2, slot).wait()

        ys_ref[slot] = y
        y_copy(blk, slot).start()
        return carry

    lax.fori_loop(fb_ref[e], fb_ref[e + 1], block, 0)

    @pl.when(e == N_EXPERTS - 1)
    def _():
        y_copy(n_used - 1, (n_used - 1) % 2).wait()

        @pl.when(n_used >= 2)
        def _():
            y_copy(n_used - 2, n_used % 2).wait()

        ys_ref[0] = jnp.zeros(ys_ref.shape[1:], F32)

        def fill(blk, carry):
            y_copy(blk, 0).start()
            y_copy(blk, 0).wait()
            return carry

        lax.fori_loop(n_used, n_blocks, fill, 0)


def _experts_call(rt, xs, wgu, bgu, wd, bd, bm, n_blocks):
    E = N_EXPERTS
    by_expert = lambda e, fb: (e, 0, 0)
    grid_spec = pltpu.PrefetchScalarGridSpec(
        num_scalar_prefetch=1,
        grid=(E,),
        in_specs=[
            pl.BlockSpec(memory_space=pl.ANY),
            pl.BlockSpec((None, D_MODEL, 2 * D_FF), by_expert),
            pl.BlockSpec((None, 1, 2 * D_FF), by_expert),
            pl.BlockSpec((None, D_FF, D_MODEL), by_expert),
            pl.BlockSpec((None, 1, D_MODEL), by_expert),
        ],
        out_specs=pl.BlockSpec(memory_space=pl.ANY),
        scratch_shapes=[
            pltpu.VMEM((X_AHEAD + 1, bm, D_MODEL), F32),
            pltpu.VMEM((2, bm, D_MODEL), F32),
            pltpu.VMEM((D_MODEL, 2 * D_FF), BF16),
            pltpu.VMEM((D_FF, D_MODEL), BF16),
            pltpu.SemaphoreType.DMA((X_AHEAD + 1,)),
            pltpu.SemaphoreType.DMA((2,)),
        ],
    )
    return pl.pallas_call(
        functools.partial(_experts_kernel, bm=bm, n_blocks=n_blocks),
        grid_spec=grid_spec,
        out_shape=jax.ShapeDtypeStruct((n_blocks * bm, D_MODEL), F32),
        compiler_params=pltpu.CompilerParams(dimension_semantics=("arbitrary",), vmem_limit_bytes=VMEM_LIMIT),
        name="experts",
    )(rt["first_block"], xs, wgu, bgu.reshape(E, 1, 2 * D_FF), wd, bd.reshape(E, 1, D_MODEL))


def _combine_kernel(nch_ref, gbase_ref, y_hbm, x1_ref, lpos_ref, gate_ref, g_ref, b_ref, o_ref,
                    buf_ref, sem, cnt_ref):
    i = pl.program_id(0)
    nt = pl.num_programs(0)
    slot = i % 2

    @pl.when(i == 0)
    def _():
        buf_ref[...] = jnp.zeros(buf_ref.shape, F32)
        cnt_ref[0] = _group_copies(nch_ref, gbase_ref, 0, buf_ref.at[0], y_hbm, sem.at[0], False)

    @pl.when(i + 1 < nt)
    def _():
        cnt_ref[1 - slot] = _group_copies(nch_ref, gbase_ref, i + 1, buf_ref.at[1 - slot], y_hbm,
                                          sem.at[1 - slot], False)

    _wait_chunks(cnt_ref[slot], y_hbm, sem.at[slot])
    planes = _tile_planes(lpos_ref, i)
    lpos = jnp.concatenate([planes[:, j, :] for j in range(planes.shape[1])], axis=1).astype(F32).T
    gates = gate_ref[...]
    cols = DISP_PCHUNK
    y = jnp.zeros(o_ref.shape, F32)
    for cb in range(DISP_LCAP // cols):
        c = (lax.broadcasted_iota(jnp.int32, (lpos.shape[0], cols), 1) + cb * cols).astype(F32)
        w = jnp.zeros(c.shape, F32)
        for k in range(TOP_K):
            w = jnp.where(c == lpos[:, k:k + 1], gates[:, k:k + 1], w)
        y = y + _dot(w.astype(BF16), buf_ref[slot, cb * cols:(cb + 1) * cols, :].astype(BF16))
    o_ref[...] = _layer_norm(DN_ALPHA * x1_ref[...] + y, g_ref[...], b_ref[...])


def _combine_call(rt, y, x1, gates, g, b):
    T = x1.shape[0]
    tile = DISP_TILE
    grid_spec = pltpu.PrefetchScalarGridSpec(
        num_scalar_prefetch=2,
        grid=(T // tile,),
        in_specs=[
            pl.BlockSpec(memory_space=pl.ANY),
            pl.BlockSpec((tile, D_MODEL), lambda i, *_: (i, 0)),
            pl.BlockSpec(rt["lpos"].shape, lambda i, *_: (0, 0, 0)),
            pl.BlockSpec((tile, LANES), lambda i, *_: (i, 0)),
            pl.BlockSpec(g.shape, lambda i, *_: (0, 0)),
            pl.BlockSpec(b.shape, lambda i, *_: (0, 0)),
        ],
        out_specs=pl.BlockSpec((tile, D_MODEL), lambda i, *_: (i, 0)),
        scratch_shapes=[
            pltpu.VMEM((2, DISP_LCAP, D_MODEL), F32),
            pltpu.SemaphoreType.DMA((2,)),
            pltpu.SMEM((2,), jnp.int32),
        ],
    )
    return pl.pallas_call(
        _combine_kernel,
        grid_spec=grid_spec,
        out_shape=jax.ShapeDtypeStruct((T, D_MODEL), F32),
        compiler_params=pltpu.CompilerParams(dimension_semantics=("arbitrary",), vmem_limit_bytes=VMEM_LIMIT),
        name="combine",
    )(rt["nch"], rt["gbase"], y, x1, rt["lpos"], gates, g, b)


def _rope_lane_order(n_sub):
    half = DIFF_HEAD_DIM // 2
    per = LANES // 2 // n_sub
    assert per == half or n_sub == 1
    cols = []
    for part in range(2):
        for sub in range(n_sub):
            cols.extend(sub * 64 + part * half + d for d in range(half))
    return np.asarray(cols)


def _prep_weights(w_in, mla_q_norm_g, w_uq, mla_kv_norm_g, w_ukv):
    o_dq, o_dk, o_dv, o_cq, o_ckv, o_kr = 0, 512, 1024, 1536, 1792, 1920
    head_order = _rope_lane_order(2)
    diff_cols = np.concatenate([h * LANES + head_order for h in range(DIFF_HEADS)])
    w = {}
    ckv = w_in[:, o_ckv:o_kr]
    w["feat"] = jnp.concatenate([w_in[:, o_dq + diff_cols], w_in[:, o_dv:o_cq], w_in[:, o_cq:o_ckv], ckv],
                                axis=1).T.astype(BF16)

    def spread_rope(cols64):
        z = jnp.zeros((cols64.shape[0], 32), cols64.dtype)
        return jnp.concatenate([cols64[:, :32], z, cols64[:, 32:], z], axis=1)

    w["tok"] = jnp.concatenate([w_in[:, o_dk + diff_cols], ckv, spread_rope(w_in[:, o_kr:o_kr + MLA_ROPE_DIM])],
                               axis=1).astype(BF16)
    uq = []
    for h in range(MLA_HEADS):
        base = h * MLA_QK_DIM
        uq.append(w_uq[:, base:base + MLA_NOPE_DIM])
        uq.append(spread_rope(w_uq[:, base + MLA_NOPE_DIM:base + MLA_QK_DIM]))
    w["uqT"] = jnp.concatenate(uq, axis=1).T.astype(BF16)
    per = MLA_NOPE_DIM + MLA_V_DIM
    w["uk"] = jnp.concatenate([w_ukv[:, h * per:h * per + MLA_NOPE_DIM] for h in range(MLA_HEADS)], axis=1).astype(BF16)
    w["uvT"] = jnp.concatenate(
        [w_ukv[:, h * per + MLA_NOPE_DIM:(h + 1) * per] for h in range(MLA_HEADS)], axis=1).T.astype(BF16)
    w["gq"] = mla_q_norm_g.reshape(MLA_Q_RANK, 1)
    w["gkv"] = mla_kv_norm_g.reshape(1, MLA_KV_RANK)
    w["gkvc"] = mla_kv_norm_g.reshape(MLA_KV_RANK, 1)
    return w


def _rope_tables(positions):
    half = MLA_ROPE_DIM // 2
    inv_freq = 1.0 / (ROPE_THETA ** (jnp.arange(0, MLA_ROPE_DIM, 2, dtype=F32) / MLA_ROPE_DIM))
    ang = positions.astype(F32)[..., None] * inv_freq
    ang = jnp.tile(ang, (1, 1, LANES // half))
    sign = jnp.where(jnp.arange(LANES) < LANES // 2, -1.0, 1.0).astype(F32)
    return jnp.cos(ang), jnp.sin(ang) * sign


def kernel(x, positions, w_in, lambda_q1, lambda_k1, lambda_q2, lambda_k2, subln_g, mla_q_norm_g, w_uq,
           mla_kv_norm_g, w_ukv, w_o, ln1_g, ln1_b, w_router, b_router, w_gate_up, b_gate_up, w_down, b_down,
           ln2_g, ln2_b):
    B, S, D = x.shape
    T = B * S
    l = 0
    x2 = x.reshape(T, D)
    w = _prep_weights(w_in[l], mla_q_norm_g[l], w_uq[l], mla_kv_norm_g[l], w_ukv[l])

    dqT, dk, dvT, mqT, mk, mvT = _proj_call(x, _rope_tables(positions), w)
    lam_vecs = [v[l].reshape(1, DIFF_HEAD_DIM) for v in (lambda_q1, lambda_k1, lambda_q2, lambda_k2)]
    o_a = _attn_call(_diff_attn_kernel, lam_vecs + [subln_g[l].reshape(DIFF_V_DIM, 1)],
                     dqT, dk, dvT, DIFF_HEADS, 1, LANES, "diff_attn")
    o_b = _attn_call(_mla_attn_kernel, [], mqT, mk, mvT, MLA_HEADS // 2, 2, MLA_QK_PAD, "mla_attn")

    wr = jnp.pad(w_router[l], ((0, 0), (0, LANES - N_EXPERTS)))
    br = jnp.pad(b_router[l], (0, LANES - N_EXPERTS), constant_values=NEG_BIG).reshape(1, LANES)
    x1, eplanes, gates = _post_call(
        o_a.reshape(T, -1), o_b.reshape(T, -1), x2, w_o[l].astype(BF16),
        ln1_g[l].reshape(1, D), ln1_b[l].reshape(1, D), wr, br)

    bm = MOE_BM
    A = T * TOP_K
    n_tiles = T // DISP_TILE
    n_blocks = pl.cdiv(A + n_tiles * N_EXPERTS * (ROW_CHUNK - 1) + N_EXPERTS * (bm - ROW_CHUNK), bm)
    rt = _route_call(eplanes, bm)
    xs = _dispatch_call(rt, x1, bm, n_blocks)
    y = _experts_call(rt, xs, w_gate_up[l], b_gate_up[l], w_down[l], b_down[l], bm, n_blocks)
    out = _combine_call(rt, y, x1, gates, ln2_g[l].reshape(1, D), ln2_b[l].reshape(1, D))
    return out.reshape(B, S, D)
```

```python
import functools
import math

import numpy as np
import jax
import jax.numpy as jnp
from jax import lax
from jax.experimental import pallas as pl
from jax.experimental.pallas import tpu as pltpu

D_MODEL = 1024
DIFF_HEADS = 4
DIFF_HEAD_DIM = 64
DIFF_V_DIM = 128
MLA_HEADS = 4
MLA_V_DIM = 128
MLA_NOPE_DIM = 128
MLA_ROPE_DIM = 64
MLA_QK_DIM = MLA_NOPE_DIM + MLA_ROPE_DIM
MLA_Q_RANK = 256
MLA_KV_RANK = 128
DIFF_Q_COLS = DIFF_K_COLS = DIFF_V_COLS = 512
ROPE_THETA = 10000.0
N_EXPERTS = 32
TOP_K = 4
D_FF = 1024
SWIGLU_LIMIT = 7.0
SWIGLU_ALPHA = 1.702
LN_EPS = 1e-5
SUBLN_EPS = 1e-5
MLA_RMS_EPS = 1e-6
DEPTH = 1
DN_ALPHA = (2.0 * DEPTH) ** 0.25
LAMBDA_INIT = 0.8 - 0.6 * math.exp(-0.3 * 0)

LANES = 128
MLA_QK_PAD = 2 * LANES
BF16_ROWS = 16
PLANES = 8
VAL_ROWS = LANES + BF16_ROWS
VMEM_LIMIT = 56 * 1024 * 1024

ATTN_TILE = 512
POST_TM = 1024
MOE_BM = 256
X_AHEAD = 3
ROW_CHUNK = 8
BIG_CHUNK = 32
DISP_TILE = 512
DISP_PCHUNK = 256
DISP_LCAP = -(-(DISP_TILE * TOP_K + N_EXPERTS * (ROW_CHUNK - 1)) // DISP_PCHUNK) * DISP_PCHUNK

NEG_BIG = -1e30
LOG2E = math.log2(math.e)
F32 = jnp.float32
BF16 = jnp.bfloat16


def _dot(a, b):
    return jnp.dot(a, b, preferred_element_type=F32)


def _dot_nt(a, b):
    return lax.dot_general(a, b, (((1,), (1,)), ((), ())), preferred_element_type=F32)


def _rope128(blk, cos, sin):
    return blk * cos + pltpu.roll(blk, 64, axis=1) * sin


def _rope128_t(blk, cos, sin):
    half = LANES // 2
    rolled = jnp.concatenate([blk[half:], blk[:half]], axis=0)
    return blk * cos + rolled * sin


def _rms_rows(t, g, eps):
    return t * lax.rsqrt(jnp.mean(t * t, axis=-1, keepdims=True) + eps) * g


def _rms_cols(t, g, eps):
    return t * lax.rsqrt(jnp.mean(t * t, axis=0, keepdims=True) + eps) * g


def _store_values(vT_ref, vT, heads):
    ones = jnp.ones((BF16_ROWS, vT.shape[1]), BF16)
    for h in range(heads):
        vT_ref[h * VAL_ROWS:h * VAL_ROWS + LANES, :] = vT[h * LANES:(h + 1) * LANES].astype(BF16)
        vT_ref[h * VAL_ROWS + LANES:(h + 1) * VAL_ROWS, :] = ones


def _proj_kernel(x_ref, cos_ref, sin_ref, wfeat_ref, wtok_ref, gq_ref, gkv_ref, gkvc_ref,
                 wuqT_ref, wuk_ref, wuvT_ref, dqT_ref, dk_ref, dvT_ref, mqT_ref, mk_ref, mvT_ref):
    xb = x_ref[...].astype(BF16)
    cos, sin = cos_ref[...], sin_ref[...]
    cosT, sinT = cos.T, sin.T

    dq_scale = DIFF_HEAD_DIM ** -0.5 * LOG2E
    mq_scale = MLA_QK_DIM ** -0.5 * LOG2E
    o_dv, o_cq, o_ckv = DIFF_Q_COLS, DIFF_Q_COLS + DIFF_V_COLS, DIFF_Q_COLS + DIFF_V_COLS + MLA_Q_RANK

    feat = _dot_nt(wfeat_ref[...], xb)
    for h in range(DIFF_HEADS):
        sl = slice(h * LANES, (h + 1) * LANES)
        dqT_ref[sl, :] = (_rope128_t(feat[sl], cosT, sinT) * dq_scale).astype(BF16)
    _store_values(dvT_ref, feat[o_dv:o_cq], DIFF_HEADS)

    cqT = _rms_cols(feat[o_cq:o_ckv], gq_ref[...], MLA_RMS_EPS)
    qT = _dot(wuqT_ref[...], cqT.astype(BF16))
    for h in range(MLA_HEADS):
        nope = slice(h * MLA_QK_PAD, h * MLA_QK_PAD + LANES)
        ropes = slice(h * MLA_QK_PAD + LANES, (h + 1) * MLA_QK_PAD)
        mqT_ref[nope, :] = (qT[nope] * mq_scale).astype(BF16)
        mqT_ref[ropes, :] = (_rope128_t(qT[ropes], cosT, sinT) * mq_scale).astype(BF16)

    ckvT = _rms_cols(feat[o_ckv:], gkvc_ref[...], MLA_RMS_EPS)
    _store_values(mvT_ref, _dot(wuvT_ref[...], ckvT.astype(BF16)), MLA_HEADS)

    tokm = _dot(xb, wtok_ref[...])
    for h in range(DIFF_HEADS):
        sl = slice(h * LANES, (h + 1) * LANES)
        dk_ref[:, sl] = _rope128(tokm[:, sl], cos, sin).astype(BF16)
    ckv = _rms_rows(tokm[:, DIFF_K_COLS:DIFF_K_COLS + MLA_KV_RANK], gkv_ref[...], MLA_RMS_EPS)
    k_nope = _dot(ckv.astype(BF16), wuk_ref[...])
    k_pe = _rope128(tokm[:, DIFF_K_COLS + MLA_KV_RANK:], cos, sin).astype(BF16)
    for h in range(MLA_HEADS):
        mk_ref[:, h * MLA_QK_PAD:h * MLA_QK_PAD + LANES] = k_nope[:, h * LANES:(h + 1) * LANES].astype(BF16)
        mk_ref[:, h * MLA_QK_PAD + LANES:(h + 1) * MLA_QK_PAD] = k_pe


def _proj_call(x3, tabs, w):
    B, S, D = x3.shape
    tm = ATTN_TILE
    nt = S // tm
    cos_t, sin_t = tabs
    weights = [w["feat"], w["tok"], w["gq"], w["gkv"], w["gkvc"], w["uqT"], w["uk"], w["uvT"]]
    tok = lambda b, i: (b, i, 0)
    feat = lambda b, i: (b, i, 0, 0)
    in_specs = [pl.BlockSpec((None, tm, D), tok),
                pl.BlockSpec((None, tm, LANES), tok), pl.BlockSpec((None, tm, LANES), tok)]
    in_specs += [pl.BlockSpec(a.shape, lambda b, i: (0, 0)) for a in weights]
    mq_w = MLA_HEADS * MLA_QK_PAD
    dv_w, mv_w = DIFF_HEADS * VAL_ROWS, MLA_HEADS * VAL_ROWS
    out_specs = [pl.BlockSpec((None, None, 512, tm), feat), pl.BlockSpec((None, tm, 512), tok),
                 pl.BlockSpec((None, None, dv_w, tm), feat), pl.BlockSpec((None, None, mq_w, tm), feat),
                 pl.BlockSpec((None, tm, mq_w), tok), pl.BlockSpec((None, None, mv_w, tm), feat)]
    out_shape = [jax.ShapeDtypeStruct((B, nt, 512, tm), BF16), jax.ShapeDtypeStruct((B, S, 512), BF16),
                 jax.ShapeDtypeStruct((B, nt, dv_w, tm), BF16), jax.ShapeDtypeStruct((B, nt, mq_w, tm), BF16),
                 jax.ShapeDtypeStruct((B, S, mq_w), BF16), jax.ShapeDtypeStruct((B, nt, mv_w, tm), BF16)]
    return pl.pallas_call(
        _proj_kernel,
        grid=(B, nt),
        in_specs=in_specs,
        out_specs=out_specs,
        out_shape=out_shape,
        compiler_params=pltpu.CompilerParams(dimension_semantics=("arbitrary", "arbitrary"),
                                             vmem_limit_bytes=VMEM_LIMIT),
        name="proj",
    )(x3, cos_t, sin_t, *weights)


def _flash_body(qTs, keys_of, values_of, acc_ref, m_ref, s_refs, *, t):
    i = pl.program_id(2)
    m_ref[...] = jnp.full(m_ref.shape, -jnp.inf, F32)
    acc_ref[...] = jnp.zeros(acc_ref.shape, F32)

    def scores(j, u):
        s_refs[u][...] = _dot(keys_of(u, j), qTs[u])

    def finish(j, u, masked):
        s = s_refs[u][...]
        if masked:
            key = lax.broadcasted_iota(jnp.int32, s.shape, 0)
            qry = lax.broadcasted_iota(jnp.int32, s.shape, 1)
            s = jnp.where(key <= qry, s, -jnp.inf)
        m_prev = m_ref[u]
        m_new = jnp.maximum(m_prev, jnp.max(s, axis=0, keepdims=True))
        alpha = jnp.exp2(m_prev - m_new)
        p = jnp.exp2((s - m_new).astype(BF16))
        acc_ref[u] = alpha * acc_ref[u] + _dot(values_of(u, j), p)
        m_ref[u] = m_new

    def full_step(j, carry):
        scores(j, 1)
        finish(j, 0, False)
        scores(j + 1, 0)
        finish(j, 1, False)
        return carry

    def two_steps(jj, carry):
        return full_step(2 * jj + 1, full_step(2 * jj, carry))

    scores(0, 0)
    lax.fori_loop(0, lax.shift_right_logical(i, 1), two_steps, 0)

    @pl.when(i % 2 == 1)
    def _():
        full_step(i - 1, 0)

    scores(i, 1)
    finish(i, 0, True)
    finish(i, 1, True)


def _key_tile(k_ref, j, t, lanes=slice(None)):
    return k_ref[pl.ds(pl.multiple_of(j * t, t), t), lanes]


def _normalized(acc_ref, u):
    return acc_ref[u, :LANES, :] / acc_ref[u, LANES:LANES + 1, :]


def _diff_attn_kernel(lq1_ref, lk1_ref, lq2_ref, lk2_ref, g_ref, qT_ref, k_ref, vT_ref, o_ref,
                      acc_ref, m_ref, sa_ref, sb_ref, *, t):
    qT = qT_ref[...]
    row = lax.broadcasted_iota(jnp.int32, qT.shape, 0)
    first = (row % 64) < 32
    zero = jnp.zeros_like(qT)
    qTs = (jnp.where(first, qT, zero), jnp.where(first, zero, qT))
    _flash_body(qTs, lambda u, j: _key_tile(k_ref, j, t), lambda u, j: vT_ref[j],
                acc_ref, m_ref, (sa_ref, sb_ref), t=t)

    lam = (jnp.exp(jnp.sum(lq1_ref[...] * lk1_ref[...], axis=-1, keepdims=True))
           - jnp.exp(jnp.sum(lq2_ref[...] * lk2_ref[...], axis=-1, keepdims=True)) + LAMBDA_INIT)
    oT = _normalized(acc_ref, 0) - lam * _normalized(acc_ref, 1)
    oT = _rms_cols(oT, g_ref[...], SUBLN_EPS) * (1.0 - LAMBDA_INIT)
    o_ref[...] = oT.T.astype(o_ref.dtype)


def _mla_attn_kernel(qT_ref, k_ref, vT_ref, o_ref, acc_ref, m_ref, sa_ref, sb_ref, *, t):
    dk = MLA_QK_PAD
    qTs = tuple(qT_ref[u * dk:(u + 1) * dk, :] for u in range(2))
    _flash_body(qTs, lambda u, j: _key_tile(k_ref, j, t, slice(u * dk, (u + 1) * dk)),
                lambda u, j: vT_ref[j, u * VAL_ROWS:(u + 1) * VAL_ROWS, :],
                acc_ref, m_ref, (sa_ref, sb_ref), t=t)
    for u in range(2):
        o_ref[:, u * LANES:(u + 1) * LANES] = _normalized(acc_ref, u).T.astype(o_ref.dtype)


def _attn_call(kernel, extra, qT, k, vT, groups, heads_per_group, dk, name):
    B, nt, _, t = qT.shape
    S = nt * t
    hp = heads_per_group
    in_specs = [pl.BlockSpec(a.shape, lambda b, h, i: (0, 0)) for a in extra]
    in_specs += [
        pl.BlockSpec((None, None, hp * dk, t), lambda b, h, i: (b, i, h, 0)),
        pl.BlockSpec((None, S, hp * dk), lambda b, h, i: (b, 0, h)),
        pl.BlockSpec((None, nt, hp * VAL_ROWS, t), lambda b, h, i: (b, 0, h, 0)),
    ]
    return pl.pallas_call(
        functools.partial(kernel, t=t),
        grid=(B, groups, nt),
        in_specs=in_specs,
        out_specs=pl.BlockSpec((None, t, hp * LANES), lambda b, h, i: (b, i, h)),
        out_shape=jax.ShapeDtypeStruct((B, S, groups * hp * LANES), BF16),
        scratch_shapes=[
            pltpu.VMEM((2, VAL_ROWS, t), F32),
            pltpu.VMEM((2, 1, t), F32),
            pltpu.VMEM((t, t), F32),
            pltpu.VMEM((t, t), F32),
        ],
        compiler_params=pltpu.CompilerParams(
            dimension_semantics=("arbitrary", "arbitrary", "arbitrary"), vmem_limit_bytes=VMEM_LIMIT),
        name=name,
    )(*extra, qT, k, vT)


def _layer_norm(y, g, b):
    mu = jnp.mean(y, axis=-1, keepdims=True)
    d = y - mu
    var = jnp.mean(d * d, axis=-1, keepdims=True)
    return d * lax.rsqrt(var + LN_EPS) * g + b


def _split_bf16(a):
    hi = a.astype(BF16)
    return hi, (a - hi.astype(F32)).astype(BF16)


def _post_kernel(oa_ref, ob_ref, x_ref, wo_ref, g_ref, b_ref, wrh_ref, wrl_ref, br_ref,
                 x1_ref, eplane_ref, gate_ref):
    half = oa_ref.shape[1]
    mixed = _dot(oa_ref[...], wo_ref[:half, :]) + _dot(ob_ref[...], wo_ref[half:, :])
    x1 = _layer_norm(DN_ALPHA * x_ref[...] + mixed, g_ref[...], b_ref[...])
    x1_ref[...] = x1

    x_hi, x_lo = _split_bf16(x1)
    logits = (_dot(x_hi, wrh_ref[...]) + _dot(x_lo, wrh_ref[...]) + _dot(x_hi, wrl_ref[...])) + br_ref[...]
    lane = lax.broadcasted_iota(jnp.int32, logits.shape, 1).astype(F32)
    work = logits
    vals, idxs = [], []
    for _ in range(TOP_K):
        m = jnp.max(work, axis=-1, keepdims=True)
        idx = jnp.min(jnp.where(work == m, lane, float(LANES)), axis=-1, keepdims=True)
        vals.append(m)
        idxs.append(idx)
        work = jnp.where(lane == idx, NEG_BIG, work)
    es = [jnp.exp(v - vals[0]) for v in vals]
    den = es[0] + es[1] + es[2] + es[3]
    idx_out = jnp.zeros(logits.shape, F32)
    gate_out = jnp.zeros(logits.shape, F32)
    for k in range(TOP_K):
        idx_out = jnp.where(lane == float(k), idxs[k], idx_out)
        gate_out = jnp.where(lane == float(k), es[k] / den, gate_out)
    gate_ref[...] = gate_out
    idx_t = idx_out.T
    for j in range(eplane_ref.shape[1]):
        eplane_ref[:, j, :] = idx_t[:PLANES, j * LANES:(j + 1) * LANES].astype(jnp.int32)


def _post_call(oa, ob, x2, wo, g, b, wr, br):
    T = x2.shape[0]
    tm = POST_TM
    row = lambda i: (i, 0)
    full = lambda i: (0, 0)
    return pl.pallas_call(
        _post_kernel,
        grid=(T // tm,),
        in_specs=[
            pl.BlockSpec((tm, oa.shape[1]), row), pl.BlockSpec((tm, ob.shape[1]), row),
            pl.BlockSpec((tm, D_MODEL), row), pl.BlockSpec(wo.shape, full),
            pl.BlockSpec(g.shape, full), pl.BlockSpec(b.shape, full),
            pl.BlockSpec(wr.shape, full), pl.BlockSpec(wr.shape, full), pl.BlockSpec(br.shape, full),
        ],
        out_specs=[
            pl.BlockSpec((tm, D_MODEL), row), pl.BlockSpec((PLANES, tm // LANES, LANES), lambda i: (0, i, 0)),
            pl.BlockSpec((tm, LANES), row),
        ],
        out_shape=[
            jax.ShapeDtypeStruct((T, D_MODEL), F32),
            jax.ShapeDtypeStruct((PLANES, T // LANES, LANES), jnp.int32), jax.ShapeDtypeStruct((T, LANES), F32),
        ],
        compiler_params=pltpu.CompilerParams(dimension_semantics=("arbitrary",), vmem_limit_bytes=VMEM_LIMIT),
        name="post",
    )(oa, ob, x2, wo, g, b, *_split_bf16(wr), br)


def _lane_cumsum(x, n):
    lane = lax.broadcasted_iota(jnp.int32, x.shape, 1)
    s = 1
    while s < n:
        x = x + jnp.where(lane >= s, pltpu.roll(x, s, axis=1), 0.0)
        s *= 2
    return x


def _route_kernel(e_ref, lpos_ref, nch_ref, gbase_ref, meta_ref, *, bm, rows_per_tile):
    planes = [e_ref[k] for k in range(TOP_K)]
    R = planes[0].shape[0]
    shape = planes[0].shape
    nt = R // rows_per_tile
    r_i = lax.broadcasted_iota(jnp.int32, (LANES, LANES), 0)
    c_i = lax.broadcasted_iota(jnp.int32, (LANES, LANES), 1)
    lane_incl = (r_i <= c_i).astype(BF16)
    rr = lax.broadcasted_iota(jnp.int32, (R, R), 0)
    rc = lax.broadcasted_iota(jnp.int32, (R, R), 1)
    same_tile = (rr // rows_per_tile) == (rc // rows_per_tile)
    rows_before = jnp.logical_and(rc < rr, same_tile).astype(BF16)
    tr = lax.broadcasted_iota(jnp.int32, (nt, R), 0)
    tc = lax.broadcasted_iota(jnp.int32, (nt, R), 1)
    tile_rows = (tc // rows_per_tile == tr).astype(BF16)
    er = lax.broadcasted_iota(jnp.int32, (R, nt), 0)
    ec = lax.broadcasted_iota(jnp.int32, (R, nt), 1)
    row_tile = (er // rows_per_tile == ec).astype(F32)
    lt_r = lax.broadcasted_iota(jnp.int32, (nt, nt), 0)
    lt_c = lax.broadcasted_iota(jnp.int32, (nt, nt), 1)
    tiles_before = (lt_c < lt_r).astype(BF16)
    lane_t = lax.broadcasted_iota(jnp.int32, (nt, LANES), 1)

    ranks = [jnp.zeros(shape, F32) for _ in range(TOP_K)]
    cnt = jnp.zeros((nt, LANES), F32)
    for ex in range(N_EXPERTS):
        hits = [pk == ex for pk in planes]
        any_hit = functools.reduce(jnp.logical_or, hits)
        m = jnp.where(any_hit, 1.0, 0.0).astype(BF16)
        incl = _dot(m, lane_incl)
        row_tot = jnp.broadcast_to(incl[:, LANES - 1:LANES], shape).astype(BF16)
        before = _dot(rows_before, row_tot)
        rank_e = incl - 1.0 + before
        ranks = [jnp.where(h, rank_e, r) for h, r in zip(hits, ranks)]
        cnt = jnp.where(lane_t == ex, _dot(tile_rows, row_tot), cnt)

    chunk = float(ROW_CHUNK)
    cnt8 = jnp.floor((cnt + (chunk - 1.0)) * (1.0 / chunk)) * chunk
    lstart = _lane_cumsum(cnt8, N_EXPERTS) - cnt8
    tile_pre = _dot(tiles_before, cnt8.astype(BF16))
    tot8 = tile_pre[nt - 1:nt] + cnt8[nt - 1:nt]
    tot_bm = jnp.floor((tot8 + (bm - 1.0)) * (1.0 / bm)) * bm
    end_incl = _lane_cumsum(tot_bm, N_EXPERTS)
    ebase = end_incl - tot_bm

    lrow = jnp.dot(row_tile, lstart, preferred_element_type=F32,
                   precision=lax.Precision.HIGHEST)
    starts = [jnp.zeros(shape, F32) for _ in range(TOP_K)]
    for ex in range(N_EXPERTS):
        starts = [jnp.where(pk == ex, lrow[:, ex:ex + 1], st) for pk, st in zip(planes, starts)]
    lpos_ref[...] = jnp.zeros(lpos_ref.shape, jnp.int32)
    for k in range(TOP_K):
        lpos_ref[k] = (starts[k] + ranks[k]).astype(jnp.int32)
    nch_ref[...] = (cnt8 * (1.0 / chunk)).astype(jnp.int32)
    gbase_ref[...] = (ebase + tile_pre).astype(jnp.int32)

    sub = lax.broadcasted_iota(jnp.int32, meta_ref.shape, 0)
    meta = jnp.where(sub == 0, ebase + tot8, 0.0)
    meta = jnp.where(sub == 1, (tot_bm - tot8) * (1.0 / chunk), meta)
    n_used = end_incl[:, N_EXPERTS - 1:N_EXPERTS] * (1.0 / bm)
    meta = jnp.where(sub == 2, n_used, meta)
    lane_m = lax.broadcasted_iota(jnp.int32, meta_ref.shape, 1)
    meta = jnp.where(sub == 3, jnp.where(lane_m < N_EXPERTS, ebase * (1.0 / bm), n_used), meta)
    meta_ref[...] = meta.astype(jnp.int32)


def _route_call(eplanes, bm):
    R = eplanes.shape[1]
    rows_per_tile = DISP_TILE // LANES
    nt = R // rows_per_tile
    full = lambda i: (0,) * 2
    shapes = [(nt, LANES), (nt, LANES), (8, LANES)]
    lpos, nch, gbase, meta = pl.pallas_call(
        functools.partial(_route_kernel, bm=bm, rows_per_tile=rows_per_tile),
        grid=(1,),
        in_specs=[pl.BlockSpec(eplanes.shape, lambda i: (0, 0, 0))],
        out_specs=[pl.BlockSpec(eplanes.shape, lambda i: (0, 0, 0))] + [pl.BlockSpec(s, full) for s in shapes],
        out_shape=[jax.ShapeDtypeStruct(eplanes.shape, jnp.int32)]
        + [jax.ShapeDtypeStruct(s, jnp.int32) for s in shapes],
        compiler_params=pltpu.CompilerParams(dimension_semantics=("arbitrary",), vmem_limit_bytes=VMEM_LIMIT),
        name="route",
    )(eplanes)
    E = N_EXPERTS
    return dict(lpos=lpos, nch=nch[:, :E].reshape(-1), gbase=gbase[:, :E].reshape(-1),
                tail_start=meta[0, :E], tail_n=meta[1, :E], n_used=meta[2, :1], first_block=meta[3, :E + 1])


def _chunk_rows(ref, first, rows=ROW_CHUNK):
    return ref.at[pl.ds(pl.multiple_of(first, ROW_CHUNK), rows), :]


def _group_copies(nch_ref, gbase_ref, tile, buf, hbm, sem, to_hbm):
    per_big = BIG_CHUNK // ROW_CHUNK

    def copy(local, remote):
        if to_hbm:
            pltpu.make_async_copy(local, remote, sem).start()
        else:
            pltpu.make_async_copy(remote, local, sem).start()

    def per_expert(ex, done):
        n = nch_ref[tile * N_EXPERTS + ex]
        g0 = gbase_ref[tile * N_EXPERTS + ex]
        l0 = done * ROW_CHUNK
        n_big = lax.shift_right_logical(n, per_big.bit_length() - 1)

        def big(c, carry):
            copy(_chunk_rows(buf, l0 + c * BIG_CHUNK, BIG_CHUNK), _chunk_rows(hbm, g0 + c * BIG_CHUNK, BIG_CHUNK))
            return carry

        def small(c, carry):
            copy(_chunk_rows(buf, l0 + c * ROW_CHUNK), _chunk_rows(hbm, g0 + c * ROW_CHUNK))
            return carry

        lax.fori_loop(0, n_big, big, 0)
        lax.fori_loop(n_big * per_big, n, small, 0)
        return done + n

    return lax.fori_loop(0, N_EXPERTS, per_expert, 0)


def _wait_chunks(n, hbm, sem):
    rows = pl.multiple_of(n * ROW_CHUNK, ROW_CHUNK)

    @pl.when(n > 0)
    def _():
        pltpu.make_async_copy(hbm.at[pl.ds(0, rows), :], hbm.at[pl.ds(0, rows), :], sem).wait()


def _tile_planes(lpos_ref, i):
    rpt = DISP_TILE // LANES
    per8 = 8 // rpt
    eight = lpos_ref[:, pl.ds(pl.multiple_of((i // per8) * 8, 8), 8), :]
    out = eight[:, :rpt, :]
    for sft in range(1, per8):
        out = jnp.where(i % per8 == sft, eight[:, sft * rpt:(sft + 1) * rpt, :], out)
    return out


def _dispatch_kernel(nch_ref, gbase_ref, tstart_ref, tn_ref, nu_ref, x_ref, lpos_ref, xs_hbm,
                     buf_ref, zero_ref, sem, zsem, cnt_ref, *, bm, n_blocks):
    i = pl.program_id(0)
    nt = pl.num_programs(0)
    slot = i % 2

    @pl.when(i >= 2)
    def _():
        _wait_chunks(cnt_ref[slot], xs_hbm, sem.at[slot])

    xb = x_ref[...].astype(BF16)
    lpos = _tile_planes(lpos_ref, i)
    rows = DISP_PCHUNK
    for rb in range(DISP_LCAP // rows):
        r = lax.broadcasted_iota(jnp.int32, (rows, LANES), 0) + rb * rows
        pieces = []
        for j in range(lpos.shape[1]):
            hit = r == lpos[0, j:j + 1, :]
            for k in range(1, TOP_K):
                hit = jnp.logical_or(hit, r == lpos[k, j:j + 1, :])
            pieces.append(jnp.where(hit, 1.0, 0.0).astype(BF16))
        perm = jnp.concatenate(pieces, axis=1)
        buf_ref[slot, rb * rows:(rb + 1) * rows, :] = _dot(perm, xb)
    cnt_ref[slot] = _group_copies(nch_ref, gbase_ref, i, buf_ref.at[slot], xs_hbm, sem.at[slot], True)

    @pl.when(i == nt - 1)
    def _():
        zero_ref[...] = jnp.zeros(zero_ref.shape, F32)

        def per_expert(ex, done):
            first = tstart_ref[ex]

            def per_chunk(c, carry):
                pltpu.make_async_copy(_chunk_rows(zero_ref, 0), _chunk_rows(xs_hbm, first + c * ROW_CHUNK),
                                      zsem.at[0]).start()
                return carry

            lax.fori_loop(0, tn_ref[ex], per_chunk, 0)
            return done + tn_ref[ex]

        n_tail = lax.fori_loop(0, N_EXPERTS, per_expert, 0)

        def per_block(b, carry):
            pltpu.make_async_copy(zero_ref, xs_hbm.at[pl.ds(pl.multiple_of(b * bm, bm), bm), :], zsem.at[1]).start()
            return carry

        lax.fori_loop(nu_ref[0], n_blocks, per_block, 0)
        _wait_chunks(n_tail, xs_hbm, zsem.at[0])

        def wait_block(b, carry):
            pltpu.make_async_copy(zero_ref, xs_hbm.at[pl.ds(0, bm), :], zsem.at[1]).wait()
            return carry

        lax.fori_loop(nu_ref[0], n_blocks, wait_block, 0)
        _wait_chunks(cnt_ref[slot], xs_hbm, sem.at[slot])

        @pl.when(nt >= 2)
        def _():
            _wait_chunks(cnt_ref[1 - slot], xs_hbm, sem.at[1 - slot])


def _dispatch_call(rt, x1, bm, n_blocks):
    T = x1.shape[0]
    tile = DISP_TILE
    grid_spec = pltpu.PrefetchScalarGridSpec(
        num_scalar_prefetch=5,
        grid=(T // tile,),
        in_specs=[
            pl.BlockSpec((tile, D_MODEL), lambda i, *_: (i, 0)),
            pl.BlockSpec(rt["lpos"].shape, lambda i, *_: (0, 0, 0)),
        ],
        out_specs=pl.BlockSpec(memory_space=pl.ANY),
        scratch_shapes=[
            pltpu.VMEM((2, DISP_LCAP, D_MODEL), F32),
            pltpu.VMEM((bm, D_MODEL), F32),
            pltpu.SemaphoreType.DMA((2,)),
            pltpu.SemaphoreType.DMA((2,)),
            pltpu.SMEM((2,), jnp.int32),
        ],
    )
    return pl.pallas_call(
        functools.partial(_dispatch_kernel, bm=bm, n_blocks=n_blocks),
        grid_spec=grid_spec,
        out_shape=jax.ShapeDtypeStruct((n_blocks * bm, D_MODEL), F32),
        compiler_params=pltpu.CompilerParams(dimension_semantics=("arbitrary",), vmem_limit_bytes=VMEM_LIMIT),
        name="dispatch",
    )(rt["nch"], rt["gbase"], rt["tail_start"], rt["tail_n"], rt["n_used"], x1, rt["lpos"])


def _experts_kernel(fb_ref, xs_hbm, wgu_ref, bgu_ref, wd_ref, bd_ref, y_hbm,
                    xs_ref, ys_ref, wgu_b, wd_b, xsem, ysem, *, bm, n_blocks):
    e = pl.program_id(0)
    n_used = fb_ref[N_EXPERTS]

    def x_copy(blk, slot):
        return pltpu.make_async_copy(xs_hbm.at[pl.ds(pl.multiple_of(blk * bm, bm), bm), :], xs_ref.at[slot],
                                     xsem.at[slot])

    def y_copy(blk, slot):
        return pltpu.make_async_copy(ys_ref.at[slot], y_hbm.at[pl.ds(pl.multiple_of(blk * bm, bm), bm), :],
                                     ysem.at[slot])

    n_buf = X_AHEAD + 1

    @pl.when(e == 0)
    def _():
        for b in range(X_AHEAD):
            @pl.when(b < n_used)
            def _():
                x_copy(b, b).start()

    wgu_b[...] = wgu_ref[...].astype(BF16)
    wd_b[...] = wd_ref[...].astype(BF16)

    def block(blk, carry):
        slot = blk % 2
        xslot = blk % n_buf
        x_copy(blk, xslot).wait()

        @pl.when(blk + X_AHEAD < n_used)
        def _():
            x_copy(blk + X_AHEAD, (blk + X_AHEAD) % n_buf).start()

        h = _dot(xs_ref[xslot].astype(BF16), wgu_b[...]) + bgu_ref[...]
        gate = jnp.minimum(h[:, :D_FF], SWIGLU_LIMIT)
        up = jnp.clip(h[:, D_FF:], -SWIGLU_LIMIT, SWIGLU_LIMIT)
        act = (up + 1.0) * (gate * jax.nn.sigmoid(gate * SWIGLU_ALPHA))
        y = _dot(act.astype(BF16), wd_b[...]) + bd_ref[...]

        @pl.when(blk >= 2)
        def _():
            y_copy(blk - 2, slot).wait()

        ys_ref[slot] = y
        y_copy(blk, slot).start()
        return carry

    lax.fori_loop(fb_ref[e], fb_ref[e + 1], block, 0)

    @pl.when(e == N_EXPERTS - 1)
    def _():
        y_copy(n_used - 1, (n_used - 1) % 2).wait()

        @pl.when(n_used >= 2)
        def _():
            y_copy(n_used - 2, n_used % 2).wait()

        ys_ref[0] = jnp.zeros(ys_ref.shape[1:], F32)

        def fill(blk, carry):
            y_copy(blk, 0).start()
            y_copy(blk, 0).wait()
            return carry

        lax.fori_loop(n_used, n_blocks, fill, 0)


def _experts_call(rt, xs, wgu, bgu, wd, bd, bm, n_blocks):
    E = N_EXPERTS
    by_expert = lambda e, fb: (e, 0, 0)
    grid_spec = pltpu.PrefetchScalarGridSpec(
        num_scalar_prefetch=1,
        grid=(E,),
        in_specs=[
            pl.BlockSpec(memory_space=pl.ANY),
            pl.BlockSpec((None, D_MODEL, 2 * D_FF), by_expert),
            pl.BlockSpec((None, 1, 2 * D_FF), by_expert),
            pl.BlockSpec((None, D_FF, D_MODEL), by_expert),
            pl.BlockSpec((None, 1, D_MODEL), by_expert),
        ],
        out_specs=pl.BlockSpec(memory_space=pl.ANY),
        scratch_shapes=[
            pltpu.VMEM((X_AHEAD + 1, bm, D_MODEL), F32),
            pltpu.VMEM((2, bm, D_MODEL), F32),
            pltpu.VMEM((D_MODEL, 2 * D_FF), BF16),
            pltpu.VMEM((D_FF, D_MODEL), BF16),
            pltpu.SemaphoreType.DMA((X_AHEAD + 1,)),
            pltpu.SemaphoreType.DMA((2,)),
        ],
    )
    return pl.pallas_call(
        functools.partial(_experts_kernel, bm=bm, n_blocks=n_blocks),
        grid_spec=grid_spec,
        out_shape=jax.ShapeDtypeStruct((n_blocks * bm, D_MODEL), F32),
        compiler_params=pltpu.CompilerParams(dimension_semantics=("arbitrary",), vmem_limit_bytes=VMEM_LIMIT),
        name="experts",
    )(rt["first_block"], xs, wgu, bgu.reshape(E, 1, 2 * D_FF), wd, bd.reshape(E, 1, D_MODEL))


def _combine_kernel(nch_ref, gbase_ref, y_hbm, x1_ref, lpos_ref, gate_ref, g_ref, b_ref, o_ref,
                    buf_ref, sem, cnt_ref):
    i = pl.program_id(0)
    nt = pl.num_programs(0)
    slot = i % 2

    @pl.when(i == 0)
    def _():
        buf_ref[...] = jnp.zeros(buf_ref.shape, F32)
        cnt_ref[0] = _group_copies(nch_ref, gbase_ref, 0, buf_ref.at[0], y_hbm, sem.at[0], False)

    @pl.when(i + 1 < nt)
    def _():
        cnt_ref[1 - slot] = _group_copies(nch_ref, gbase_ref, i + 1, buf_ref.at[1 - slot], y_hbm,
                                          sem.at[1 - slot], False)

    _wait_chunks(cnt_ref[slot], y_hbm, sem.at[slot])
    planes = _tile_planes(lpos_ref, i)
    lpos = jnp.concatenate([planes[:, j, :] for j in range(planes.shape[1])], axis=1).astype(F32).T
    gates = gate_ref[...]
    cols = DISP_PCHUNK
    y = jnp.zeros(o_ref.shape, F32)
    for cb in range(DISP_LCAP // cols):
        c = (lax.broadcasted_iota(jnp.int32, (lpos.shape[0], cols), 1) + cb * cols).astype(F32)
        w = jnp.zeros(c.shape, F32)
        for k in range(TOP_K):
            w = jnp.where(c == lpos[:, k:k + 1], gates[:, k:k + 1], w)
        y = y + _dot(w.astype(BF16), buf_ref[slot, cb * cols:(cb + 1) * cols, :].astype(BF16))
    o_ref[...] = _layer_norm(DN_ALPHA * x1_ref[...] + y, g_ref[...], b_ref[...])


def _combine_call(rt, y, x1, gates, g, b):
    T = x1.shape[0]
    tile = DISP_TILE
    grid_spec = pltpu.PrefetchScalarGridSpec(
        num_scalar_prefetch=2,
        grid=(T // tile,),
        in_specs=[
            pl.BlockSpec(memory_space=pl.ANY),
            pl.BlockSpec((tile, D_MODEL), lambda i, *_: (i, 0)),
            pl.BlockSpec(rt["lpos"].shape, lambda i, *_: (0, 0, 0)),
            pl.BlockSpec((tile, LANES), lambda i, *_: (i, 0)),
            pl.BlockSpec(g.shape, lambda i, *_: (0, 0)),
            pl.BlockSpec(b.shape, lambda i, *_: (0, 0)),
        ],
        out_specs=pl.BlockSpec((tile, D_MODEL), lambda i, *_: (i, 0)),
        scratch_shapes=[
            pltpu.VMEM((2, DISP_LCAP, D_MODEL), F32),
            pltpu.SemaphoreType.DMA((2,)),
            pltpu.SMEM((2,), jnp.int32),
        ],
    )
    return pl.pallas_call(
        _combine_kernel,
        grid_spec=grid_spec,
        out_shape=jax.ShapeDtypeStruct((T, D_MODEL), F32),
        compiler_params=pltpu.CompilerParams(dimension_semantics=("arbitrary",), vmem_limit_bytes=VMEM_LIMIT),
        name="combine",
    )(rt["nch"], rt["gbase"], y, x1, rt["lpos"], gates, g, b)


def _rope_lane_order(n_sub):
    half = DIFF_HEAD_DIM // 2
    per = LANES // 2 // n_sub
    assert per == half or n_sub == 1
    cols = []
    for part in range(2):
        for sub in range(n_sub):
            cols.extend(sub * 64 + part * half + d for d in range(half))
    return np.asarray(cols)


def _prep_weights(w_in, mla_q_norm_g, w_uq, mla_kv_norm_g, w_ukv):
    o_dq, o_dk, o_dv, o_cq, o_ckv, o_kr = 0, 512, 1024, 1536, 1792, 1920
    head_order = _rope_lane_order(2)
    diff_cols = np.concatenate([h * LANES + head_order for h in range(DIFF_HEADS)])
    w = {}
    ckv = w_in[:, o_ckv:o_kr]
    w["feat"] = jnp.concatenate([w_in[:, o_dq + diff_cols], w_in[:, o_dv:o_cq], w_in[:, o_cq:o_ckv], ckv],
                                axis=1).T.astype(BF16)

    def spread_rope(cols64):
        z = jnp.zeros((cols64.shape[0], 32), cols64.dtype)
        return jnp.concatenate([cols64[:, :32], z, cols64[:, 32:], z], axis=1)

    w["tok"] = jnp.concatenate([w_in[:, o_dk + diff_cols], ckv, spread_rope(w_in[:, o_kr:o_kr + MLA_ROPE_DIM])],
                               axis=1).astype(BF16)
    uq = []
    for h in range(MLA_HEADS):
        base = h * MLA_QK_DIM
        uq.append(w_uq[:, base:base + MLA_NOPE_DIM])
        uq.append(spread_rope(w_uq[:, base + MLA_NOPE_DIM:base + MLA_QK_DIM]))
    w["uqT"] = jnp.concatenate(uq, axis=1).T.astype(BF16)
    per = MLA_NOPE_DIM + MLA_V_DIM
    w["uk"] = jnp.concatenate([w_ukv[:, h * per:h * per + MLA_NOPE_DIM] for h in range(MLA_HEADS)], axis=1).astype(BF16)
    w["uvT"] = jnp.concatenate(
        [w_ukv[:, h * per + MLA_NOPE_DIM:(h + 1) * per] for h in range(MLA_HEADS)], axis=1).T.astype(BF16)
    w["gq"] = mla_q_norm_g.reshape(MLA_Q_RANK, 1)
    w["gkv"] = mla_kv_norm_g.reshape(1, MLA_KV_RANK)
    w["gkvc"] = mla_kv_norm_g.reshape(MLA_KV_RANK, 1)
    return w


def _rope_tables(positions):
    half = MLA_ROPE_DIM // 2
    inv_freq = 1.0 / (ROPE_THETA ** (jnp.arange(0, MLA_ROPE_DIM, 2, dtype=F32) / MLA_ROPE_DIM))
    ang = positions.astype(F32)[..., None] * inv_freq
    cos, sin = jnp.cos(ang), jnp.sin(ang)
    reps = LANES // half
    return jnp.tile(cos, (1, 1, reps)), jnp.concatenate([-sin] * (reps // 2) + [sin] * (reps // 2), axis=-1)


def kernel(x, positions, w_in, lambda_q1, lambda_k1, lambda_q2, lambda_k2, subln_g, mla_q_norm_g, w_uq,
           mla_kv_norm_g, w_ukv, w_o, ln1_g, ln1_b, w_router, b_router, w_gate_up, b_gate_up, w_down, b_down,
           ln2_g, ln2_b):
    B, S, D = x.shape
    T = B * S
    l = 0
    x2 = x.reshape(T, D)
    w = _prep_weights(w_in[l], mla_q_norm_g[l], w_uq[l], mla_kv_norm_g[l], w_ukv[l])

    dqT, dk, dvT, mqT, mk, mvT = _proj_call(x, _rope_tables(positions), w)
    lam_vecs = [v[l].reshape(1, DIFF_HEAD_DIM) for v in (lambda_q1, lambda_k1, lambda_q2, lambda_k2)]
    o_a = _attn_call(_diff_attn_kernel, lam_vecs + [subln_g[l].reshape(DIFF_V_DIM, 1)],
                     dqT, dk, dvT, DIFF_HEADS, 1, LANES, "diff_attn")
    o_b = _attn_call(_mla_attn_kernel, [], mqT, mk, mvT, MLA_HEADS // 2, 2, MLA_QK_PAD, "mla_attn")

    wr = jnp.pad(w_router[l], ((0, 0), (0, LANES - N_EXPERTS)))
    br = jnp.pad(b_router[l], (0, LANES - N_EXPERTS), constant_values=NEG_BIG).reshape(1, LANES)
    x1, eplanes, gates = _post_call(
        o_a.reshape(T, -1), o_b.reshape(T, -1), x2, w_o[l].astype(BF16),
        ln1_g[l].reshape(1, D), ln1_b[l].reshape(1, D), wr, br)

    bm = MOE_BM
    A = T * TOP_K
    n_tiles = T // DISP_TILE
    n_blocks = pl.cdiv(A + n_tiles * N_EXPERTS * (ROW_CHUNK - 1) + N_EXPERTS * (bm - ROW_CHUNK), bm)
    rt = _route_call(eplanes, bm)
    xs = _dispatch_call(rt, x1, bm, n_blocks)
    y = _experts_call(rt, xs, w_gate_up[l], b_gate_up[l], w_down[l], b_down[l], bm, n_blocks)
    out = _combine_call(rt, y, x1, gates, ln2_g[l].reshape(1, D), ln2_b[l].reshape(1, D))
    return out.reshape(B, S, D)
```

```python
import functools
import math

import numpy as np
import jax
import jax.numpy as jnp
from jax import lax
from jax.experimental import pallas as pl
from jax.experimental.pallas import tpu as pltpu

D_MODEL = 1024
DIFF_HEADS = 4
DIFF_HEAD_DIM = 64
DIFF_V_DIM = 128
MLA_HEADS = 4
MLA_V_DIM = 128
MLA_NOPE_DIM = 128
MLA_ROPE_DIM = 64
MLA_QK_DIM = MLA_NOPE_DIM + MLA_ROPE_DIM
MLA_Q_RANK = 256
MLA_KV_RANK = 128
DIFF_Q_COLS = DIFF_K_COLS = DIFF_V_COLS = 512
ROPE_THETA = 10000.0
N_EXPERTS = 32
TOP_K = 4
D_FF = 1024
SWIGLU_LIMIT = 7.0
SWIGLU_ALPHA = 1.702
LN_EPS = 1e-5
SUBLN_EPS = 1e-5
MLA_RMS_EPS = 1e-6
DEPTH = 1
DN_ALPHA = (2.0 * DEPTH) ** 0.25
LAMBDA_INIT = 0.8 - 0.6 * math.exp(-0.3 * 0)

LANES = 128
MLA_QK_PAD = 2 * LANES
BF16_ROWS = 16
PLANES = 8
VAL_ROWS = LANES + BF16_ROWS
VMEM_LIMIT = 56 * 1024 * 1024

ATTN_TILE = 512
POST_TM = 1024
MOE_BM = 256
X_AHEAD = 3
ROW_CHUNK = 8
BIG_CHUNK = 32
DISP_TILE = 512
DISP_PCHUNK = 256
DISP_LCAP = -(-(DISP_TILE * TOP_K + N_EXPERTS * (ROW_CHUNK - 1)) // DISP_PCHUNK) * DISP_PCHUNK

NEG_BIG = -1e30
LOG2E = math.log2(math.e)
F32 = jnp.float32
BF16 = jnp.bfloat16


def _dot(a, b):
    return jnp.dot(a, b, preferred_element_type=F32)


def _dot_nt(a, b):
    return lax.dot_general(a, b, (((1,), (1,)), ((), ())), preferred_element_type=F32)


def _rope128(blk, cos, sin):
    return blk * cos + pltpu.roll(blk, 64, axis=1) * sin


def _rope128_t(blk, cos, sin):
    half = LANES // 2
    rolled = jnp.concatenate([blk[half:], blk[:half]], axis=0)
    return blk * cos + rolled * sin


def _rms_rows(t, g, eps):
    return t * lax.rsqrt(jnp.mean(t * t, axis=-1, keepdims=True) + eps) * g


def _rms_cols(t, g, eps):
    return t * lax.rsqrt(jnp.mean(t * t, axis=0, keepdims=True) + eps) * g


def _store_values(vT_ref, vT, heads):
    ones = jnp.ones((BF16_ROWS, vT.shape[1]), BF16)
    for h in range(heads):
        vT_ref[h * VAL_ROWS:h * VAL_ROWS + LANES, :] = vT[h * LANES:(h + 1) * LANES].astype(BF16)
        vT_ref[h * VAL_ROWS + LANES:(h + 1) * VAL_ROWS, :] = ones


def _proj_kernel(x_ref, cos_ref, sin_ref, wfeat_ref, wtok_ref, gq_ref, gkv_ref, gkvc_ref,
                 wuqT_ref, wuk_ref, wuvT_ref, dqT_ref, dk_ref, dvT_ref, mqT_ref, mk_ref, mvT_ref):
    xb = x_ref[...].astype(BF16)
    cos, sin = cos_ref[...], sin_ref[...]
    cosT, sinT = cos.T, sin.T

    dq_scale = DIFF_HEAD_DIM ** -0.5 * LOG2E
    mq_scale = MLA_QK_DIM ** -0.5 * LOG2E
    o_dv, o_cq, o_ckv = DIFF_Q_COLS, DIFF_Q_COLS + DIFF_V_COLS, DIFF_Q_COLS + DIFF_V_COLS + MLA_Q_RANK

    feat = _dot_nt(wfeat_ref[...], xb)
    for h in range(DIFF_HEADS):
        sl = slice(h * LANES, (h + 1) * LANES)
        dqT_ref[sl, :] = (_rope128_t(feat[sl], cosT, sinT) * dq_scale).astype(BF16)
    _store_values(dvT_ref, feat[o_dv:o_cq], DIFF_HEADS)

    cqT = _rms_cols(feat[o_cq:o_ckv], gq_ref[...], MLA_RMS_EPS)
    qT = _dot(wuqT_ref[...], cqT.astype(BF16))
    for h in range(MLA_HEADS):
        nope = slice(h * MLA_QK_PAD, h * MLA_QK_PAD + LANES)
        ropes = slice(h * MLA_QK_PAD + LANES, (h + 1) * MLA_QK_PAD)
        mqT_ref[nope, :] = (qT[nope] * mq_scale).astype(BF16)
        mqT_ref[ropes, :] = (_rope128_t(qT[ropes], cosT, sinT) * mq_scale).astype(BF16)

    ckvT = _rms_cols(feat[o_ckv:], gkvc_ref[...], MLA_RMS_EPS)
    _store_values(mvT_ref, _dot(wuvT_ref[...], ckvT.astype(BF16)), MLA_HEADS)

    tokm = _dot(xb, wtok_ref[...])
    for h in range(DIFF_HEADS):
        sl = slice(h * LANES, (h + 1) * LANES)
        dk_ref[:, sl] = _rope128(tokm[:, sl], cos, sin).astype(BF16)
    ckv = _rms_rows(tokm[:, DIFF_K_COLS:DIFF_K_COLS + MLA_KV_RANK], gkv_ref[...], MLA_RMS_EPS)
    k_nope = _dot(ckv.astype(BF16), wuk_ref[...])
    k_pe = _rope128(tokm[:, DIFF_K_COLS + MLA_KV_RANK:], cos, sin).astype(BF16)
    for h in range(MLA_HEADS):
        mk_ref[:, h * MLA_QK_PAD:h * MLA_QK_PAD + LANES] = k_nope[:, h * LANES:(h + 1) * LANES].astype(BF16)
        mk_ref[:, h * MLA_QK_PAD + LANES:(h + 1) * MLA_QK_PAD] = k_pe


def _proj_call(x3, tabs, w):
    B, S, D = x3.shape
    tm = ATTN_TILE
    nt = S // tm
    cos_t, sin_t = tabs
    weights = [w["feat"], w["tok"], w["gq"], w["gkv"], w["gkvc"], w["uqT"], w["uk"], w["uvT"]]
    tok = lambda b, i: (b, i, 0)
    feat = lambda b, i: (b, i, 0, 0)
    in_specs = [pl.BlockSpec((None, tm, D), tok),
                pl.BlockSpec((None, tm, LANES), tok), pl.BlockSpec((None, tm, LANES), tok)]
    in_specs += [pl.BlockSpec(a.shape, lambda b, i: (0, 0)) for a in weights]
    mq_w = MLA_HEADS * MLA_QK_PAD
    dv_w, mv_w = DIFF_HEADS * VAL_ROWS, MLA_HEADS * VAL_ROWS
    out_specs = [pl.BlockSpec((None, None, 512, tm), feat), pl.BlockSpec((None, tm, 512), tok),
                 pl.BlockSpec((None, None, dv_w, tm), feat), pl.BlockSpec((None, None, mq_w, tm), feat),
                 pl.BlockSpec((None, tm, mq_w), tok), pl.BlockSpec((None, None, mv_w, tm), feat)]
    out_shape = [jax.ShapeDtypeStruct((B, nt, 512, tm), BF16), jax.ShapeDtypeStruct((B, S, 512), BF16),
                 jax.ShapeDtypeStruct((B, nt, dv_w, tm), BF16), jax.ShapeDtypeStruct((B, nt, mq_w, tm), BF16),
                 jax.ShapeDtypeStruct((B, S, mq_w), BF16), jax.ShapeDtypeStruct((B, nt, mv_w, tm), BF16)]
    return pl.pallas_call(
        _proj_kernel,
        grid=(B, nt),
        in_specs=in_specs,
        out_specs=out_specs,
        out_shape=out_shape,
        compiler_params=pltpu.CompilerParams(dimension_semantics=("arbitrary", "arbitrary"),
                                             vmem_limit_bytes=VMEM_LIMIT),
        name="proj",
    )(x3, cos_t, sin_t, *weights)


def _flash_body(qTs, keys_of, values_of, acc_ref, m_ref, s_refs, *, t):
    i = pl.program_id(2)
    m_ref[...] = jnp.full(m_ref.shape, -jnp.inf, F32)
    acc_ref[...] = jnp.zeros(acc_ref.shape, F32)

    def scores(j, u):
        s_refs[u][...] = _dot(keys_of(u, j), qTs[u])

    def finish(j, u, masked):
        s = s_refs[u][...]
        if masked:
            key = lax.broadcasted_iota(jnp.int32, s.shape, 0)
            qry = lax.broadcasted_iota(jnp.int32, s.shape, 1)
            s = jnp.where(key <= qry, s, -jnp.inf)
        m_prev = m_ref[u]
        m_new = jnp.maximum(m_prev, jnp.max(s, axis=0, keepdims=True))
        alpha = jnp.exp2(m_prev - m_new)
        p = jnp.exp2((s - m_new).astype(BF16))
        acc_ref[u] = alpha * acc_ref[u] + _dot(values_of(u, j), p)
        m_ref[u] = m_new

    def full_step(j, carry):
        scores(j, 1)
        finish(j, 0, False)
        scores(j + 1, 0)
        finish(j, 1, False)
        return carry

    def two_steps(jj, carry):
        return full_step(2 * jj + 1, full_step(2 * jj, carry))

    scores(0, 0)
    lax.fori_loop(0, lax.shift_right_logical(i, 1), two_steps, 0)

    @pl.when(i % 2 == 1)
    def _():
        full_step(i - 1, 0)

    scores(i, 1)
    finish(i, 0, True)
    finish(i, 1, True)


def _key_tile(k_ref, j, t, lanes=slice(None)):
    return k_ref[pl.ds(pl.multiple_of(j * t, t), t), lanes]


def _normalized(acc_ref, u):
    return acc_ref[u, :LANES, :] / acc_ref[u, LANES:LANES + 1, :]


def _diff_attn_kernel(lq1_ref, lk1_ref, lq2_ref, lk2_ref, g_ref, qT_ref, k_ref, vT_ref, o_ref,
                      acc_ref, m_ref, sa_ref, sb_ref, *, t):
    qT = qT_ref[...]
    row = lax.broadcasted_iota(jnp.int32, qT.shape, 0)
    first = (row % 64) < 32
    zero = jnp.zeros_like(qT)
    qTs = (jnp.where(first, qT, zero), jnp.where(first, zero, qT))
    _flash_body(qTs, lambda u, j: _key_tile(k_ref, j, t), lambda u, j: vT_ref[j],
                acc_ref, m_ref, (sa_ref, sb_ref), t=t)

    lam = (jnp.exp(jnp.sum(lq1_ref[...] * lk1_ref[...], axis=-1, keepdims=True))
           - jnp.exp(jnp.sum(lq2_ref[...] * lk2_ref[...], axis=-1, keepdims=True)) + LAMBDA_INIT)
    oT = _normalized(acc_ref, 0) - lam * _normalized(acc_ref, 1)
    oT = _rms_cols(oT, g_ref[...], SUBLN_EPS) * (1.0 - LAMBDA_INIT)
    o_ref[...] = oT.T.astype(o_ref.dtype)


def _mla_attn_kernel(qT_ref, k_ref, vT_ref, o_ref, acc_ref, m_ref, sa_ref, sb_ref, *, t):
    dk = MLA_QK_PAD
    qTs = tuple(qT_ref[u * dk:(u + 1) * dk, :] for u in range(2))
    _flash_body(qTs, lambda u, j: _key_tile(k_ref, j, t, slice(u * dk, (u + 1) * dk)),
                lambda u, j: vT_ref[j, u * VAL_ROWS:(u + 1) * VAL_ROWS, :],
                acc_ref, m_ref, (sa_ref, sb_ref), t=t)
    for u in range(2):
        o_ref[:, u * LANES:(u + 1) * LANES] = _normalized(acc_ref, u).T.astype(o_ref.dtype)


def _attn_call(kernel, extra, qT, k, vT, groups, heads_per_group, dk, name):
    B, nt, _, t = qT.shape
    S = nt * t
    hp = heads_per_group
    in_specs = [pl.BlockSpec(a.shape, lambda b, h, i: (0, 0)) for a in extra]
    in_specs += [
        pl.BlockSpec((None, None, hp * dk, t), lambda b, h, i: (b, i, h, 0)),
        pl.BlockSpec((None, S, hp * dk), lambda b, h, i: (b, 0, h)),
        pl.BlockSpec((None, nt, hp * VAL_ROWS, t), lambda b, h, i: (b, 0, h, 0)),
    ]
    return pl.pallas_call(
        functools.partial(kernel, t=t),
        grid=(B, groups, nt),
        in_specs=in_specs,
        out_specs=pl.BlockSpec((None, t, hp * LANES), lambda b, h, i: (b, i, h)),
        out_shape=jax.ShapeDtypeStruct((B, S, groups * hp * LANES), BF16),
        scratch_shapes=[
            pltpu.VMEM((2, VAL_ROWS, t), F32),
            pltpu.VMEM((2, 1, t), F32),
            pltpu.VMEM((t, t), F32),
            pltpu.VMEM((t, t), F32),
        ],
        compiler_params=pltpu.CompilerParams(
            dimension_semantics=("arbitrary", "arbitrary", "arbitrary"), vmem_limit_bytes=VMEM_LIMIT),
        name=name,
    )(*extra, qT, k, vT)


def _layer_norm(y, g, b):
    mu = jnp.mean(y, axis=-1, keepdims=True)
    d = y - mu
    var = jnp.mean(d * d, axis=-1, keepdims=True)
    return d * lax.rsqrt(var + LN_EPS) * g + b


def _split_bf16(a):
    hi = a.astype(BF16)
    return hi, (a - hi.astype(F32)).astype(BF16)


def _post_kernel(oa_ref, ob_ref, x_ref, wo_ref, g_ref, b_ref, wrh_ref, wrl_ref, br_ref,
                 x1_ref, eplane_ref, gate_ref):
    half = oa_ref.shape[1]
    mixed = _dot(oa_ref[...], wo_ref[:half, :]) + _dot(ob_ref[...], wo_ref[half:, :])
    x1 = _layer_norm(DN_ALPHA * x_ref[...] + mixed, g_ref[...], b_ref[...])
    x1_ref[...] = x1

    x_hi, x_lo = _split_bf16(x1)
    logits = (_dot(x_hi, wrh_ref[...]) + _dot(x_lo, wrh_ref[...]) + _dot(x_hi, wrl_ref[...])) + br_ref[...]
    lane = lax.broadcasted_iota(jnp.int32, logits.shape, 1).astype(F32)
    work = logits
    vals, idxs = [], []
    for _ in range(TOP_K):
        m = jnp.max(work, axis=-1, keepdims=True)
        idx = jnp.min(jnp.where(work == m, lane, float(LANES)), axis=-1, keepdims=True)
        vals.append(m)
        idxs.append(idx)
        work = jnp.where(lane == idx, NEG_BIG, work)
    es = [jnp.exp(v - vals[0]) for v in vals]
    den = es[0] + es[1] + es[2] + es[3]
    idx_out = jnp.zeros(logits.shape, F32)
    gate_out = jnp.zeros(logits.shape, F32)
    for k in range(TOP_K):
        idx_out = jnp.where(lane == float(k), idxs[k], idx_out)
        gate_out = jnp.where(lane == float(k), es[k] / den, gate_out)
    gate_ref[...] = gate_out
    idx_t = idx_out.T
    for j in range(eplane_ref.shape[1]):
        eplane_ref[:, j, :] = idx_t[:PLANES, j * LANES:(j + 1) * LANES].astype(jnp.int32)


def _post_call(oa, ob, x2, wo, g, b, wr, br):
    T = x2.shape[0]
    tm = POST_TM
    row = lambda i: (i, 0)
    full = lambda i: (0, 0)
    return pl.pallas_call(
        _post_kernel,
        grid=(T // tm,),
        in_specs=[
            pl.BlockSpec((tm, oa.shape[1]), row), pl.BlockSpec((tm, ob.shape[1]), row),
            pl.BlockSpec((tm, D_MODEL), row), pl.BlockSpec(wo.shape, full),
            pl.BlockSpec(g.shape, full), pl.BlockSpec(b.shape, full),
            pl.BlockSpec(wr.shape, full), pl.BlockSpec(wr.shape, full), pl.BlockSpec(br.shape, full),
        ],
        out_specs=[
            pl.BlockSpec((tm, D_MODEL), row), pl.BlockSpec((PLANES, tm // LANES, LANES), lambda i: (0, i, 0)),
            pl.BlockSpec((tm, LANES), row),
        ],
        out_shape=[
            jax.ShapeDtypeStruct((T, D_MODEL), F32),
            jax.ShapeDtypeStruct((PLANES, T // LANES, LANES), jnp.int32), jax.ShapeDtypeStruct((T, LANES), F32),
        ],
        compiler_params=pltpu.CompilerParams(dimension_semantics=("arbitrary",), vmem_limit_bytes=VMEM_LIMIT),
        name="post",
    )(oa, ob, x2, wo, g, b, *_split_bf16(wr), br)


def _lane_cumsum(x, n):
    lane = lax.broadcasted_iota(jnp.int32, x.shape, 1)
    s = 1
    while s < n:
        x = x + jnp.where(lane >= s, pltpu.roll(x, s, axis=1), 0.0)
        s *= 2
    return x


def _route_kernel(e_ref, lpos_ref, nch_ref, gbase_ref, meta_ref, *, bm, rows_per_tile):
    planes = [e_ref[k] for k in range(TOP_K)]
    R = planes[0].shape[0]
    shape = planes[0].shape
    nt = R // rows_per_tile
    r_i = lax.broadcasted_iota(jnp.int32, (LANES, LANES), 0)
    c_i = lax.broadcasted_iota(jnp.int32, (LANES, LANES), 1)
    lane_incl = (r_i <= c_i).astype(BF16)
    rr = lax.broadcasted_iota(jnp.int32, (R, R), 0)
    rc = lax.broadcasted_iota(jnp.int32, (R, R), 1)
    same_tile = (rr // rows_per_tile) == (rc // rows_per_tile)
    rows_before = jnp.logical_and(rc < rr, same_tile).astype(BF16)
    tr = lax.broadcasted_iota(jnp.int32, (nt, R), 0)
    tc = lax.broadcasted_iota(jnp.int32, (nt, R), 1)
    tile_rows = (tc // rows_per_tile == tr).astype(BF16)
    er = lax.broadcasted_iota(jnp.int32, (R, nt), 0)
    ec = lax.broadcasted_iota(jnp.int32, (R, nt), 1)
    row_tile = (er // rows_per_tile == ec).astype(F32)
    lt_r = lax.broadcasted_iota(jnp.int32, (nt, nt), 0)
    lt_c = lax.broadcasted_iota(jnp.int32, (nt, nt), 1)
    tiles_before = (lt_c < lt_r).astype(BF16)
    lane_t = lax.broadcasted_iota(jnp.int32, (nt, LANES), 1)

    ranks = [jnp.zeros(shape, F32) for _ in range(TOP_K)]
    cnt = jnp.zeros((nt, LANES), F32)
    for ex in range(N_EXPERTS):
        hits = [pk == ex for pk in planes]
        any_hit = functools.reduce(jnp.logical_or, hits)
        m = jnp.where(any_hit, 1.0, 0.0).astype(BF16)
        incl = _dot(m, lane_incl)
        row_tot = jnp.broadcast_to(incl[:, LANES - 1:LANES], shape).astype(BF16)
        before = _dot(rows_before, row_tot)
        rank_e = incl - 1.0 + before
        ranks = [jnp.where(h, rank_e, r) for h, r in zip(hits, ranks)]
        cnt = jnp.where(lane_t == ex, _dot(tile_rows, row_tot), cnt)

    chunk = float(ROW_CHUNK)
    cnt8 = jnp.floor((cnt + (chunk - 1.0)) * (1.0 / chunk)) * chunk
    lstart = _lane_cumsum(cnt8, N_EXPERTS) - cnt8
    tile_pre = _dot(tiles_before, cnt8.astype(BF16))
    tot8 = tile_pre[nt - 1:nt] + cnt8[nt - 1:nt]
    tot_bm = jnp.floor((tot8 + (bm - 1.0)) * (1.0 / bm)) * bm
    end_incl = _lane_cumsum(tot_bm, N_EXPERTS)
    ebase = end_incl - tot_bm

    lrow = jnp.dot(row_tile, lstart, preferred_element_type=F32,
                   precision=lax.Precision.HIGHEST)
    starts = [jnp.zeros(shape, F32) for _ in range(TOP_K)]
    for ex in range(N_EXPERTS):
        starts = [jnp.where(pk == ex, lrow[:, ex:ex + 1], st) for pk, st in zip(planes, starts)]
    lpos_ref[...] = jnp.zeros(lpos_ref.shape, jnp.int32)
    for k in range(TOP_K):
        lpos_ref[k] = (starts[k] + ranks[k]).astype(jnp.int32)
    nch_ref[...] = (cnt8 * (1.0 / chunk)).astype(jnp.int32)
    gbase_ref[...] = (ebase + tile_pre).astype(jnp.int32)

    sub = lax.broadcasted_iota(jnp.int32, meta_ref.shape, 0)
    meta = jnp.where(sub == 0, ebase + tot8, 0.0)
    meta = jnp.where(sub == 1, (tot_bm - tot8) * (1.0 / chunk), meta)
    n_used = end_incl[:, N_EXPERTS - 1:N_EXPERTS] * (1.0 / bm)
    meta = jnp.where(sub == 2, n_used, meta)
    lane_m = lax.broadcasted_iota(jnp.int32, meta_ref.shape, 1)
    meta = jnp.where(sub == 3, jnp.where(lane_m < N_EXPERTS, ebase * (1.0 / bm), n_used), meta)
    meta_ref[...] = meta.astype(jnp.int32)


def _route_call(eplanes, bm):
    R = eplanes.shape[1]
    rows_per_tile = DISP_TILE // LANES
    nt = R // rows_per_tile
    full = lambda i: (0,) * 2
    shapes = [(nt, LANES), (nt, LANES), (8, LANES)]
    lpos, nch, gbase, meta = pl.pallas_call(
        functools.partial(_route_kernel, bm=bm, rows_per_tile=rows_per_tile),
        grid=(1,),
        in_specs=[pl.BlockSpec(eplanes.shape, lambda i: (0, 0, 0))],
        out_specs=[pl.BlockSpec(eplanes.shape, lambda i: (0, 0, 0))] + [pl.BlockSpec(s, full) for s in shapes],
        out_shape=[jax.ShapeDtypeStruct(eplanes.shape, jnp.int32)]
        + [jax.ShapeDtypeStruct(s, jnp.int32) for s in shapes],
        compiler_params=pltpu.CompilerParams(dimension_semantics=("arbitrary",), vmem_limit_bytes=VMEM_LIMIT),
        name="route",
    )(eplanes)
    E = N_EXPERTS
    return dict(lpos=lpos, nch=nch[:, :E].reshape(-1), gbase=gbase[:, :E].reshape(-1),
                tail_start=meta[0, :E], tail_n=meta[1, :E], n_used=meta[2, :1], first_block=meta[3, :E + 1])


def _chunk_rows(ref, first, rows=ROW_CHUNK):
    return ref.at[pl.ds(pl.multiple_of(first, ROW_CHUNK), rows), :]


def _group_copies(nch_ref, gbase_ref, tile, buf, hbm, sem, to_hbm):
    per_big = BIG_CHUNK // ROW_CHUNK

    def copy(local, remote):
        if to_hbm:
            pltpu.make_async_copy(local, remote, sem).start()
        else:
            pltpu.make_async_copy(remote, local, sem).start()

    def per_expert(ex, done):
        n = nch_ref[tile * N_EXPERTS + ex]
        g0 = gbase_ref[tile * N_EXPERTS + ex]
        l0 = done * ROW_CHUNK
        n_big = lax.shift_right_logical(n, per_big.bit_length() - 1)

        def big(c, carry):
            copy(_chunk_rows(buf, l0 + c * BIG_CHUNK, BIG_CHUNK), _chunk_rows(hbm, g0 + c * BIG_CHUNK, BIG_CHUNK))
            return carry

        def small(c, carry):
            copy(_chunk_rows(buf, l0 + c * ROW_CHUNK), _chunk_rows(hbm, g0 + c * ROW_CHUNK))
            return carry

        lax.fori_loop(0, n_big, big, 0)
        lax.fori_loop(n_big * per_big, n, small, 0)
        return done + n

    return lax.fori_loop(0, N_EXPERTS, per_expert, 0)


def _wait_chunks(n, hbm, sem):
    rows = pl.multiple_of(n * ROW_CHUNK, ROW_CHUNK)

    @pl.when(n > 0)
    def _():
        pltpu.make_async_copy(hbm.at[pl.ds(0, rows), :], hbm.at[pl.ds(0, rows), :], sem).wait()


def _tile_planes(lpos_ref, i):
    rpt = DISP_TILE // LANES
    per8 = 8 // rpt
    eight = lpos_ref[:, pl.ds(pl.multiple_of((i // per8) * 8, 8), 8), :]
    out = eight[:, :rpt, :]
    for sft in range(1, per8):
        out = jnp.where(i % per8 == sft, eight[:, sft * rpt:(sft + 1) * rpt, :], out)
    return out


def _dispatch_kernel(nch_ref, gbase_ref, tstart_ref, tn_ref, nu_ref, x_ref, lpos_ref, xs_hbm,
                     buf_ref, zero_ref, sem, zsem, cnt_ref, *, bm, n_blocks):
    i = pl.program_id(0)
    nt = pl.num_programs(0)
    slot = i % 2

    @pl.when(i >= 2)
    def _():
        _wait_chunks(cnt_ref[slot], xs_hbm, sem.at[slot])

    xb = x_ref[...].astype(BF16)
    lpos = _tile_planes(lpos_ref, i)
    rows = DISP_PCHUNK
    for rb in range(DISP_LCAP // rows):
        r = lax.broadcasted_iota(jnp.int32, (rows, LANES), 0) + rb * rows
        pieces = []
        for j in range(lpos.shape[1]):
            hit = r == lpos[0, j:j + 1, :]
            for k in range(1, TOP_K):
                hit = jnp.logical_or(hit, r == lpos[k, j:j + 1, :])
            pieces.append(jnp.where(hit, 1.0, 0.0).astype(BF16))
        perm = jnp.concatenate(pieces, axis=1)
        buf_ref[slot, rb * rows:(rb + 1) * rows, :] = _dot(perm, xb)
    cnt_ref[slot] = _group_copies(nch_ref, gbase_ref, i, buf_ref.at[slot], xs_hbm, sem.at[slot], True)

    @pl.when(i == nt - 1)
    def _():
        zero_ref[...] = jnp.zeros(zero_ref.shape, F32)

        def per_expert(ex, done):
            first = tstart_ref[ex]

            def per_chunk(c, carry):
                pltpu.make_async_copy(_chunk_rows(zero_ref, 0), _chunk_rows(xs_hbm, first + c * ROW_CHUNK),
                                      zsem.at[0]).start()
                return carry

            lax.fori_loop(0, tn_ref[ex], per_chunk, 0)
            return done + tn_ref[ex]

        n_tail = lax.fori_loop(0, N_EXPERTS, per_expert, 0)

        def per_block(b, carry):
            pltpu.make_async_copy(zero_ref, xs_hbm.at[pl.ds(pl.multiple_of(b * bm, bm), bm), :], zsem.at[1]).start()
            return carry

        lax.fori_loop(nu_ref[0], n_blocks, per_block, 0)
        _wait_chunks(n_tail, xs_hbm, zsem.at[0])

        def wait_block(b, carry):
            pltpu.make_async_copy(zero_ref, xs_hbm.at[pl.ds(0, bm), :], zsem.at[1]).wait()
            return carry

        lax.fori_loop(nu_ref[0], n_blocks, wait_block, 0)
        _wait_chunks(cnt_ref[slot], xs_hbm, sem.at[slot])

        @pl.when(nt >= 2)
        def _():
            _wait_chunks(cnt_ref[1 - slot], xs_hbm, sem.at[1 - slot])


def _dispatch_call(rt, x1, bm, n_blocks):
    T = x1.shape[0]
    tile = DISP_TILE
    grid_spec = pltpu.PrefetchScalarGridSpec(
        num_scalar_prefetch=5,
        grid=(T // tile,),
        in_specs=[
            pl.BlockSpec((tile, D_MODEL), lambda i, *_: (i, 0)),
            pl.BlockSpec(rt["lpos"].shape, lambda i, *_: (0, 0, 0)),
        ],
        out_specs=pl.BlockSpec(memory_space=pl.ANY),
        scratch_shapes=[
            pltpu.VMEM((2, DISP_LCAP, D_MODEL), F32),
            pltpu.VMEM((bm, D_MODEL), F32),
            pltpu.SemaphoreType.DMA((2,)),
            pltpu.SemaphoreType.DMA((2,)),
            pltpu.SMEM((2,), jnp.int32),
        ],
    )
    return pl.pallas_call(
        functools.partial(_dispatch_kernel, bm=bm, n_blocks=n_blocks),
        grid_spec=grid_spec,
        out_shape=jax.ShapeDtypeStruct((n_blocks * bm, D_MODEL), F32),
        compiler_params=pltpu.CompilerParams(dimension_semantics=("arbitrary",), vmem_limit_bytes=VMEM_LIMIT),
        name="dispatch",
    )(rt["nch"], rt["gbase"], rt["tail_start"], rt["tail_n"], rt["n_used"], x1, rt["lpos"])


def _experts_kernel(fb_ref, xs_hbm, wgu_ref, bgu_ref, wd_ref, bd_ref, y_hbm,
                    xs_ref, ys_ref, wgu_b, wd_b, xsem, ysem, *, bm, n_blocks):
    e = pl.program_id(0)
    n_used = fb_ref[N_EXPERTS]

    def x_copy(blk, slot):
        return pltpu.make_async_copy(xs_hbm.at[pl.ds(pl.multiple_of(blk * bm, bm), bm), :], xs_ref.at[slot],
                                     xsem.at[slot])

    def y_copy(blk, slot):
        return pltpu.make_async_copy(ys_ref.at[slot], y_hbm.at[pl.ds(pl.multiple_of(blk * bm, bm), bm), :],
                                     ysem.at[slot])

    n_buf = X_AHEAD + 1

    @pl.when(e == 0)
    def _():
        for b in range(X_AHEAD):
            @pl.when(b < n_used)
            def _():
                x_copy(b, b).start()

    wgu_b[...] = wgu_ref[...].astype(BF16)
    wd_b[...] = wd_ref[...].astype(BF16)

    def block(blk, carry):
        slot = blk % 2
        xslot = blk % n_buf
        x_copy(blk, xslot).wait()

        @pl.when(blk + X_AHEAD < n_used)
        def _():
            x_copy(blk + X_AHEAD, (blk + X_AHEAD) % n_buf).start()

        h = _dot(xs_ref[xslot].astype(BF16), wgu_b[...]) + bgu_ref[...]
        gate = jnp.minimum(h[:, :D_FF], SWIGLU_LIMIT)
        up = jnp.clip(h[:, D_FF:], -SWIGLU_LIMIT, SWIGLU_LIMIT)
        act = (up + 1.0) * (gate * jax.nn.sigmoid(gate * SWIGLU_ALPHA))
        y = _dot(act.astype(BF16), wd_b[...]) + bd_ref[...]

        @pl.when(blk >= 2)
        def _():
            y_copy(blk - 2, slot).wait()

        ys_ref[slot] = y
        y_copy(blk, slot).start()
        return carry

    lax.fori_loop(fb_ref[e], fb_ref[e + 1], block, 0)

    @pl.when(e == N_EXPERTS - 1)
    def _():
        y_copy(n_used - 1, (n_used - 1) % 2).wait()

        @pl.when(n_used >= 2)
        def _():
            y_copy(n_used - 2, n_used % 2).wait()

        ys_ref[0] = jnp.zeros(ys_ref.shape[1:], F32)

        def fill(blk, carry):
            y_copy(blk, 0).start()
            return carry

        def drain(blk, carry):
            y_copy(blk, 0).wait()
            return carry

        lax.fori_loop(n_used, n_blocks, fill, 0)
        lax.fori_loop(n_used, n_blocks, drain, 0)


def _experts_call(rt, xs, wgu, bgu, wd, bd, bm, n_blocks):
    E = N_EXPERTS
    by_expert = lambda e, fb: (e, 0, 0)
    grid_spec = pltpu.PrefetchScalarGridSpec(
        num_scalar_prefetch=1,
        grid=(E,),
        in_specs=[
            pl.BlockSpec(memory_space=pl.ANY),
            pl.BlockSpec((None, D_MODEL, 2 * D_FF), by_expert),
            pl.BlockSpec((None, 1, 2 * D_FF), by_expert),
            pl.BlockSpec((None, D_FF, D_MODEL), by_expert),
            pl.BlockSpec((None, 1, D_MODEL), by_expert),
        ],
        out_specs=pl.BlockSpec(memory_space=pl.ANY),
        scratch_shapes=[
            pltpu.VMEM((X_AHEAD + 1, bm, D_MODEL), F32),
            pltpu.VMEM((2, bm, D_MODEL), F32),
            pltpu.VMEM((D_MODEL, 2 * D_FF), BF16),
            pltpu.VMEM((D_FF, D_MODEL), BF16),
            pltpu.SemaphoreType.DMA((X_AHEAD + 1,)),
            pltpu.SemaphoreType.DMA((2,)),
        ],
    )
    return pl.pallas_call(
        functools.partial(_experts_kernel, bm=bm, n_blocks=n_blocks),
        grid_spec=grid_spec,
        out_shape=jax.ShapeDtypeStruct((n_blocks * bm, D_MODEL), F32),
        compiler_params=pltpu.CompilerParams(dimension_semantics=("arbitrary",), vmem_limit_bytes=VMEM_LIMIT),
        name="experts",
    )(rt["first_block"], xs, wgu, bgu.reshape(E, 1, 2 * D_FF), wd, bd.reshape(E, 1, D_MODEL))


def _combine_kernel(nch_ref, gbase_ref, y_hbm, x1_ref, lpos_ref, gate_ref, g_ref, b_ref, o_ref,
                    buf_ref, sem, cnt_ref):
    i = pl.program_id(0)
    nt = pl.num_programs(0)
    slot = i % 2

    @pl.when(i == 0)
    def _():
        buf_ref[...] = jnp.zeros(buf_ref.shape, F32)
        cnt_ref[0] = _group_copies(nch_ref, gbase_ref, 0, buf_ref.at[0], y_hbm, sem.at[0], False)

    @pl.when(i + 1 < nt)
    def _():
        cnt_ref[1 - slot] = _group_copies(nch_ref, gbase_ref, i + 1, buf_ref.at[1 - slot], y_hbm,
                                          sem.at[1 - slot], False)

    _wait_chunks(cnt_ref[slot], y_hbm, sem.at[slot])
    planes = _tile_planes(lpos_ref, i)
    lpos = jnp.concatenate([planes[:, j, :] for j in range(planes.shape[1])], axis=1).astype(F32).T
    gates = gate_ref[...]
    cols = DISP_PCHUNK
    y = jnp.zeros(o_ref.shape, F32)
    for cb in range(DISP_LCAP // cols):
        c = (lax.broadcasted_iota(jnp.int32, (lpos.shape[0], cols), 1) + cb * cols).astype(F32)
        w = jnp.zeros(c.shape, F32)
        for k in range(TOP_K):
            w = jnp.where(c == lpos[:, k:k + 1], gates[:, k:k + 1], w)
        y = y + _dot(w.astype(BF16), buf_ref[slot, cb * cols:(cb + 1) * cols, :].astype(BF16))
    o_ref[...] = _layer_norm(DN_ALPHA * x1_ref[...] + y, g_ref[...], b_ref[...])


def _combine_call(rt, y, x1, gates, g, b):
    T = x1.shape[0]
    tile = DISP_TILE
    grid_spec = pltpu.PrefetchScalarGridSpec(
        num_scalar_prefetch=2,
        grid=(T // tile,),
        in_specs=[
            pl.BlockSpec(memory_space=pl.ANY),
            pl.BlockSpec((tile, D_MODEL), lambda i, *_: (i, 0)),
            pl.BlockSpec(rt["lpos"].shape, lambda i, *_: (0, 0, 0)),
            pl.BlockSpec((tile, LANES), lambda i, *_: (i, 0)),
            pl.BlockSpec(g.shape, lambda i, *_: (0, 0)),
            pl.BlockSpec(b.shape, lambda i, *_: (0, 0)),
        ],
        out_specs=pl.BlockSpec((tile, D_MODEL), lambda i, *_: (i, 0)),
        scratch_shapes=[
            pltpu.VMEM((2, DISP_LCAP, D_MODEL), F32),
            pltpu.SemaphoreType.DMA((2,)),
            pltpu.SMEM((2,), jnp.int32),
        ],
    )
    return pl.pallas_call(
        _combine_kernel,
        grid_spec=grid_spec,
        out_shape=jax.ShapeDtypeStruct((T, D_MODEL), F32),
        compiler_params=pltpu.CompilerParams(dimension_semantics=("arbitrary",), vmem_limit_bytes=VMEM_LIMIT),
        name="combine",
    )(rt["nch"], rt["gbase"], y, x1, rt["lpos"], gates, g, b)


def _rope_lane_order(n_sub):
    half = DIFF_HEAD_DIM // 2
    per = LANES // 2 // n_sub
    assert per == half or n_sub == 1
    cols = []
    for part in range(2):
        for sub in range(n_sub):
            cols.extend(sub * 64 + part * half + d for d in range(half))
    return np.asarray(cols)


def _prep_weights(w_in, mla_q_norm_g, w_uq, mla_kv_norm_g, w_ukv):
    o_dq, o_dk, o_dv, o_cq, o_ckv, o_kr = 0, 512, 1024, 1536, 1792, 1920
    head_order = _rope_lane_order(2)
    diff_cols = np.concatenate([h * LANES + head_order for h in range(DIFF_HEADS)])
    w = {}
    ckv = w_in[:, o_ckv:o_kr]
    w["feat"] = jnp.concatenate([w_in[:, o_dq + diff_cols], w_in[:, o_dv:o_cq], w_in[:, o_cq:o_ckv], ckv],
                                axis=1).T.astype(BF16)

    def spread_rope(cols64):
        z = jnp.zeros((cols64.shape[0], 32), cols64.dtype)
        return jnp.concatenate([cols64[:, :32], z, cols64[:, 32:], z], axis=1)

    w["tok"] = jnp.concatenate([w_in[:, o_dk + diff_cols], ckv, spread_rope(w_in[:, o_kr:o_kr + MLA_ROPE_DIM])],
                               axis=1).astype(BF16)
    uq = []
    for h in range(MLA_HEADS):
        base = h * MLA_QK_DIM
        uq.append(w_uq[:, base:base + MLA_NOPE_DIM])
        uq.append(spread_rope(w_uq[:, base + MLA_NOPE_DIM:base + MLA_QK_DIM]))
    w["uqT"] = jnp.concatenate(uq, axis=1).T.astype(BF16)
    per = MLA_NOPE_DIM + MLA_V_DIM
    w["uk"] = jnp.concatenate([w_ukv[:, h * per:h * per + MLA_NOPE_DIM] for h in range(MLA_HEADS)], axis=1).astype(BF16)
    w["uvT"] = jnp.concatenate(
        [w_ukv[:, h * per + MLA_NOPE_DIM:(h + 1) * per] for h in range(MLA_HEADS)], axis=1).T.astype(BF16)
    w["gq"] = mla_q_norm_g.reshape(MLA_Q_RANK, 1)
    w["gkv"] = mla_kv_norm_g.reshape(1, MLA_KV_RANK)
    w["gkvc"] = mla_kv_norm_g.reshape(MLA_KV_RANK, 1)
    return w


def _rope_tables(positions):
    half = MLA_ROPE_DIM // 2
    inv_freq = 1.0 / (ROPE_THETA ** (jnp.arange(0, MLA_ROPE_DIM, 2, dtype=F32) / MLA_ROPE_DIM))
    ang = positions.astype(F32)[..., None] * inv_freq
    ang = jnp.tile(ang, (1, 1, LANES // half))
    sign = jnp.where(jnp.arange(LANES) < LANES // 2, -1.0, 1.0).astype(F32)
    return jnp.cos(ang), jnp.sin(ang) * sign


def kernel(x, positions, w_in, lambda_q1, lambda_k1, lambda_q2, lambda_k2, subln_g, mla_q_norm_g, w_uq,
           mla_kv_norm_g, w_ukv, w_o, ln1_g, ln1_b, w_router, b_router, w_gate_up, b_gate_up, w_down, b_down,
           ln2_g, ln2_b):
    B, S, D = x.shape
    T = B * S
    l = 0
    x2 = x.reshape(T, D)
    w = _prep_weights(w_in[l], mla_q_norm_g[l], w_uq[l], mla_kv_norm_g[l], w_ukv[l])

    dqT, dk, dvT, mqT, mk, mvT = _proj_call(x, _rope_tables(positions), w)
    lam_vecs = [v[l].reshape(1, DIFF_HEAD_DIM) for v in (lambda_q1, lambda_k1, lambda_q2, lambda_k2)]
    o_a = _attn_call(_diff_attn_kernel, lam_vecs + [subln_g[l].reshape(DIFF_V_DIM, 1)],
                     dqT, dk, dvT, DIFF_HEADS, 1, LANES, "diff_attn")
    o_b = _attn_call(_mla_attn_kernel, [], mqT, mk, mvT, MLA_HEADS // 2, 2, MLA_QK_PAD, "mla_attn")

    wr = jnp.pad(w_router[l], ((0, 0), (0, LANES - N_EXPERTS)))
    br = jnp.pad(b_router[l], (0, LANES - N_EXPERTS), constant_values=NEG_BIG).reshape(1, LANES)
    x1, eplanes, gates = _post_call(
        o_a.reshape(T, -1), o_b.reshape(T, -1), x2, w_o[l].astype(BF16),
        ln1_g[l].reshape(1, D), ln1_b[l].reshape(1, D), wr, br)

    bm = MOE_BM
    A = T * TOP_K
    n_tiles = T // DISP_TILE
    n_blocks = pl.cdiv(A + n_tiles * N_EXPERTS * (ROW_CHUNK - 1) + N_EXPERTS * (bm - ROW_CHUNK), bm)
    rt = _route_call(eplanes, bm)
    xs = _dispatch_call(rt, x1, bm, n_blocks)
    y = _experts_call(rt, xs, w_gate_up[l], b_gate_up[l], w_down[l], b_down[l], bm, n_blocks)
    out = _combine_call(rt, y, x1, gates, ln2_g[l].reshape(1, D), ln2_b[l].reshape(1, D))
    return out.reshape(B, S, D)
```

```python
import functools
import math

import numpy as np
import jax
import jax.numpy as jnp
from jax import lax
from jax.experimental import pallas as pl
from jax.experimental.pallas import tpu as pltpu

D_MODEL = 1024
DIFF_HEADS = 4
DIFF_HEAD_DIM = 64
DIFF_V_DIM = 128
MLA_HEADS = 4
MLA_V_DIM = 128
MLA_NOPE_DIM = 128
MLA_ROPE_DIM = 64
MLA_QK_DIM = MLA_NOPE_DIM + MLA_ROPE_DIM
MLA_Q_RANK = 256
MLA_KV_RANK = 128
DIFF_Q_COLS = DIFF_K_COLS = DIFF_V_COLS = 512
ROPE_THETA = 10000.0
N_EXPERTS = 32
TOP_K = 4
D_FF = 1024
SWIGLU_LIMIT = 7.0
SWIGLU_ALPHA = 1.702
LN_EPS = 1e-5
SUBLN_EPS = 1e-5
MLA_RMS_EPS = 1e-6
DEPTH = 1
DN_ALPHA = (2.0 * DEPTH) ** 0.25
LAMBDA_INIT = 0.8 - 0.6 * math.exp(-0.3 * 0)

LANES = 128
MLA_QK_PAD = 2 * LANES
BF16_ROWS = 16
PLANES = 8
VAL_ROWS = LANES + BF16_ROWS
VMEM_LIMIT = 56 * 1024 * 1024

ATTN_TILE = 1024
POST_TM = 1024
MOE_BM = 256
X_AHEAD = 3
ROW_CHUNK = 8
BIG_CHUNK = 32
DISP_TILE = 512
DISP_PCHUNK = 256
DISP_LCAP = -(-(DISP_TILE * TOP_K + N_EXPERTS * (ROW_CHUNK - 1)) // DISP_PCHUNK) * DISP_PCHUNK

NEG_BIG = -1e30
LOG2E = math.log2(math.e)
F32 = jnp.float32
BF16 = jnp.bfloat16


def _dot(a, b):
    return jnp.dot(a, b, preferred_element_type=F32)


def _dot_nt(a, b):
    return lax.dot_general(a, b, (((1,), (1,)), ((), ())), preferred_element_type=F32)


def _rope128(blk, cos, sin):
    return blk * cos + pltpu.roll(blk, 64, axis=1) * sin


def _rope128_t(blk, cos, sin):
    half = LANES // 2
    rolled = jnp.concatenate([blk[half:], blk[:half]], axis=0)
    return blk * cos + rolled * sin


def _rms_rows(t, g, eps):
    return t * lax.rsqrt(jnp.mean(t * t, axis=-1, keepdims=True) + eps) * g


def _rms_cols(t, g, eps):
    return t * lax.rsqrt(jnp.mean(t * t, axis=0, keepdims=True) + eps) * g


def _store_values(vT_ref, vT, heads):
    ones = jnp.ones((BF16_ROWS, vT.shape[1]), BF16)
    for h in range(heads):
        vT_ref[h * VAL_ROWS:h * VAL_ROWS + LANES, :] = vT[h * LANES:(h + 1) * LANES].astype(BF16)
        vT_ref[h * VAL_ROWS + LANES:(h + 1) * VAL_ROWS, :] = ones


def _proj_kernel(x_ref, cos_ref, sin_ref, wfeat_ref, wtok_ref, gq_ref, gkv_ref, gkvc_ref,
                 wuqT_ref, wuk_ref, wuvT_ref, dqT_ref, dk_ref, dvT_ref, mqT_ref, mk_ref, mvT_ref):
    xb = x_ref[...].astype(BF16)
    cos, sin = cos_ref[...], sin_ref[...]
    cosT, sinT = cos.T, sin.T

    dq_scale = DIFF_HEAD_DIM ** -0.5 * LOG2E
    mq_scale = MLA_QK_DIM ** -0.5 * LOG2E
    o_dv, o_cq, o_ckv = DIFF_Q_COLS, DIFF_Q_COLS + DIFF_V_COLS, DIFF_Q_COLS + DIFF_V_COLS + MLA_Q_RANK

    feat = _dot_nt(wfeat_ref[...], xb)
    for h in range(DIFF_HEADS):
        sl = slice(h * LANES, (h + 1) * LANES)
        dqT_ref[sl, :] = (_rope128_t(feat[sl], cosT, sinT) * dq_scale).astype(BF16)
    _store_values(dvT_ref, feat[o_dv:o_cq], DIFF_HEADS)

    cqT = _rms_cols(feat[o_cq:o_ckv], gq_ref[...], MLA_RMS_EPS)
    qT = _dot(wuqT_ref[...], cqT.astype(BF16))
    for h in range(MLA_HEADS):
        nope = slice(h * MLA_QK_PAD, h * MLA_QK_PAD + LANES)
        ropes = slice(h * MLA_QK_PAD + LANES, (h + 1) * MLA_QK_PAD)
        mqT_ref[nope, :] = (qT[nope] * mq_scale).astype(BF16)
        mqT_ref[ropes, :] = (_rope128_t(qT[ropes], cosT, sinT) * mq_scale).astype(BF16)

    ckvT = _rms_cols(feat[o_ckv:], gkvc_ref[...], MLA_RMS_EPS)
    _store_values(mvT_ref, _dot(wuvT_ref[...], ckvT.astype(BF16)), MLA_HEADS)

    tokm = _dot(xb, wtok_ref[...])
    for h in range(DIFF_HEADS):
        sl = slice(h * LANES, (h + 1) * LANES)
        dk_ref[:, sl] = _rope128(tokm[:, sl], cos, sin).astype(BF16)
    ckv = _rms_rows(tokm[:, DIFF_K_COLS:DIFF_K_COLS + MLA_KV_RANK], gkv_ref[...], MLA_RMS_EPS)
    k_nope = _dot(ckv.astype(BF16), wuk_ref[...])
    k_pe = _rope128(tokm[:, DIFF_K_COLS + MLA_KV_RANK:], cos, sin).astype(BF16)
    for h in range(MLA_HEADS):
        mk_ref[:, h * MLA_QK_PAD:h * MLA_QK_PAD + LANES] = k_nope[:, h * LANES:(h + 1) * LANES].astype(BF16)
        mk_ref[:, h * MLA_QK_PAD + LANES:(h + 1) * MLA_QK_PAD] = k_pe


def _proj_call(x3, tabs, w):
    B, S, D = x3.shape
    tm = ATTN_TILE
    nt = S // tm
    cos_t, sin_t = tabs
    weights = [w["feat"], w["tok"], w["gq"], w["gkv"], w["gkvc"], w["uqT"], w["uk"], w["uvT"]]
    tok = lambda b, i: (b, i, 0)
    feat = lambda b, i: (b, i, 0, 0)
    in_specs = [pl.BlockSpec((None, tm, D), tok),
                pl.BlockSpec((None, tm, LANES), tok), pl.BlockSpec((None, tm, LANES), tok)]
    in_specs += [pl.BlockSpec(a.shape, lambda b, i: (0, 0)) for a in weights]
    mq_w = MLA_HEADS * MLA_QK_PAD
    dv_w, mv_w = DIFF_HEADS * VAL_ROWS, MLA_HEADS * VAL_ROWS
    out_specs = [pl.BlockSpec((None, None, 512, tm), feat), pl.BlockSpec((None, tm, 512), tok),
                 pl.BlockSpec((None, None, dv_w, tm), feat), pl.BlockSpec((None, None, mq_w, tm), feat),
                 pl.BlockSpec((None, tm, mq_w), tok), pl.BlockSpec((None, None, mv_w, tm), feat)]
    out_shape = [jax.ShapeDtypeStruct((B, nt, 512, tm), BF16), jax.ShapeDtypeStruct((B, S, 512), BF16),
                 jax.ShapeDtypeStruct((B, nt, dv_w, tm), BF16), jax.ShapeDtypeStruct((B, nt, mq_w, tm), BF16),
                 jax.ShapeDtypeStruct((B, S, mq_w), BF16), jax.ShapeDtypeStruct((B, nt, mv_w, tm), BF16)]
    return pl.pallas_call(
        _proj_kernel,
        grid=(B, nt),
        in_specs=in_specs,
        out_specs=out_specs,
        out_shape=out_shape,
        compiler_params=pltpu.CompilerParams(dimension_semantics=("arbitrary", "arbitrary"),
                                             vmem_limit_bytes=VMEM_LIMIT),
        name="proj",
    )(x3, cos_t, sin_t, *weights)


def _flash_body(qTs, keys_of, values_of, acc_ref, m_ref, s_refs, *, t):
    i = pl.program_id(2)
    m_ref[...] = jnp.full(m_ref.shape, -jnp.inf, F32)
    acc_ref[...] = jnp.zeros(acc_ref.shape, F32)

    def scores(j, u):
        s_refs[u][...] = _dot(keys_of(u, j), qTs[u])

    def finish(j, u, masked):
        s = s_refs[u][...]
        if masked:
            key = lax.broadcasted_iota(jnp.int32, s.shape, 0)
            qry = lax.broadcasted_iota(jnp.int32, s.shape, 1)
            s = jnp.where(key <= qry, s, -jnp.inf)
        m_prev = m_ref[u]
        m_new = jnp.maximum(m_prev, jnp.max(s, axis=0, keepdims=True))
        alpha = jnp.exp2(m_prev - m_new)
        p = jnp.exp2((s - m_new).astype(BF16))
        acc_ref[u] = alpha * acc_ref[u] + _dot(values_of(u, j), p)
        m_ref[u] = m_new

    def full_step(j, carry):
        scores(j, 1)
        finish(j, 0, False)
        scores(j + 1, 0)
        finish(j, 1, False)
        return carry

    def two_steps(jj, carry):
        return full_step(2 * jj + 1, full_step(2 * jj, carry))

    scores(0, 0)
    lax.fori_loop(0, lax.shift_right_logical(i, 1), two_steps, 0)

    @pl.when(i % 2 == 1)
    def _():
        full_step(i - 1, 0)

    scores(i, 1)
    finish(i, 0, True)
    finish(i, 1, True)


def _key_tile(k_ref, j, t, lanes=slice(None)):
    return k_ref[pl.ds(pl.multiple_of(j * t, t), t), lanes]


def _normalized(acc_ref, u):
    return acc_ref[u, :LANES, :] / acc_ref[u, LANES:LANES + 1, :]


def _diff_attn_kernel(lq1_ref, lk1_ref, lq2_ref, lk2_ref, g_ref, qT_ref, k_ref, vT_ref, o_ref,
                      acc_ref, m_ref, sa_ref, sb_ref, *, t):
    qT = qT_ref[...]
    row = lax.broadcasted_iota(jnp.int32, qT.shape, 0)
    first = (row % 64) < 32
    zero = jnp.zeros_like(qT)
    qTs = (jnp.where(first, qT, zero), jnp.where(first, zero, qT))
    _flash_body(qTs, lambda u, j: _key_tile(k_ref, j, t), lambda u, j: vT_ref[j],
                acc_ref, m_ref, (sa_ref, sb_ref), t=t)

    lam = (jnp.exp(jnp.sum(lq1_ref[...] * lk1_ref[...], axis=-1, keepdims=True))
           - jnp.exp(jnp.sum(lq2_ref[...] * lk2_ref[...], axis=-1, keepdims=True)) + LAMBDA_INIT)
    oT = _normalized(acc_ref, 0) - lam * _normalized(acc_ref, 1)
    oT = _rms_cols(oT, g_ref[...], SUBLN_EPS) * (1.0 - LAMBDA_INIT)
    o_ref[...] = oT.T.astype(o_ref.dtype)


def _mla_attn_kernel(qT_ref, k_ref, vT_ref, o_ref, acc_ref, m_ref, sa_ref, sb_ref, *, t):
    dk = MLA_QK_PAD
    qTs = tuple(qT_ref[u * dk:(u + 1) * dk, :] for u in range(2))
    _flash_body(qTs, lambda u, j: _key_tile(k_ref, j, t, slice(u * dk, (u + 1) * dk)),
                lambda u, j: vT_ref[j, u * VAL_ROWS:(u + 1) * VAL_ROWS, :],
                acc_ref, m_ref, (sa_ref, sb_ref), t=t)
    for u in range(2):
        o_ref[:, u * LANES:(u + 1) * LANES] = _normalized(acc_ref, u).T.astype(o_ref.dtype)


def _attn_call(kernel, extra, qT, k, vT, groups, heads_per_group, dk, name):
    B, nt, _, t = qT.shape
    S = nt * t
    hp = heads_per_group
    in_specs = [pl.BlockSpec(a.shape, lambda b, h, i: (0, 0)) for a in extra]
    in_specs += [
        pl.BlockSpec((None, None, hp * dk, t), lambda b, h, i: (b, i, h, 0)),
        pl.BlockSpec((None, S, hp * dk), lambda b, h, i: (b, 0, h)),
        pl.BlockSpec((None, nt, hp * VAL_ROWS, t), lambda b, h, i: (b, 0, h, 0)),
    ]
    return pl.pallas_call(
        functools.partial(kernel, t=t),
        grid=(B, groups, nt),
        in_specs=in_specs,
        out_specs=pl.BlockSpec((None, t, hp * LANES), lambda b, h, i: (b, i, h)),
        out_shape=jax.ShapeDtypeStruct((B, S, groups * hp * LANES), BF16),
        scratch_shapes=[
            pltpu.VMEM((2, VAL_ROWS, t), F32),
            pltpu.VMEM((2, 1, t), F32),
            pltpu.VMEM((t, t), F32),
            pltpu.VMEM((t, t), F32),
        ],
        compiler_params=pltpu.CompilerParams(
            dimension_semantics=("arbitrary", "arbitrary", "arbitrary"), vmem_limit_bytes=VMEM_LIMIT),
        name=name,
    )(*extra, qT, k, vT)


def _layer_norm(y, g, b):
    mu = jnp.mean(y, axis=-1, keepdims=True)
    d = y - mu
    var = jnp.mean(d * d, axis=-1, keepdims=True)
    return d * lax.rsqrt(var + LN_EPS) * g + b


def _split_bf16(a):
    hi = a.astype(BF16)
    return hi, (a - hi.astype(F32)).astype(BF16)


def _post_kernel(oa_ref, ob_ref, x_ref, wo_ref, g_ref, b_ref, wrh_ref, wrl_ref, br_ref,
                 x1_ref, eplane_ref, gate_ref):
    half = oa_ref.shape[1]
    mixed = _dot(oa_ref[...], wo_ref[:half, :]) + _dot(ob_ref[...], wo_ref[half:, :])
    x1 = _layer_norm(DN_ALPHA * x_ref[...] + mixed, g_ref[...], b_ref[...])
    x1_ref[...] = x1

    x_hi, x_lo = _split_bf16(x1)
    logits = (_dot(x_hi, wrh_ref[...]) + _dot(x_lo, wrh_ref[...]) + _dot(x_hi, wrl_ref[...])) + br_ref[...]
    lane = lax.broadcasted_iota(jnp.int32, logits.shape, 1).astype(F32)
    work = logits
    vals, idxs = [], []
    for _ in range(TOP_K):
        m = jnp.max(work, axis=-1, keepdims=True)
        idx = jnp.min(jnp.where(work == m, lane, float(LANES)), axis=-1, keepdims=True)
        vals.append(m)
        idxs.append(idx)
        work = jnp.where(lane == idx, NEG_BIG, work)
    es = [jnp.exp(v - vals[0]) for v in vals]
    den = es[0] + es[1] + es[2] + es[3]
    idx_out = jnp.zeros(logits.shape, F32)
    gate_out = jnp.zeros(logits.shape, F32)
    for k in range(TOP_K):
        idx_out = jnp.where(lane == float(k), idxs[k], idx_out)
        gate_out = jnp.where(lane == float(k), es[k] / den, gate_out)
    gate_ref[...] = gate_out
    idx_t = idx_out.T
    for j in range(eplane_ref.shape[1]):
        eplane_ref[:, j, :] = idx_t[:PLANES, j * LANES:(j + 1) * LANES].astype(jnp.int32)


def _post_call(oa, ob, x2, wo, g, b, wr, br):
    T = x2.shape[0]
    tm = POST_TM
    row = lambda i: (i, 0)
    full = lambda i: (0, 0)
    return pl.pallas_call(
        _post_kernel,
        grid=(T // tm,),
        in_specs=[
            pl.BlockSpec((tm, oa.shape[1]), row), pl.BlockSpec((tm, ob.shape[1]), row),
            pl.BlockSpec((tm, D_MODEL), row), pl.BlockSpec(wo.shape, full),
            pl.BlockSpec(g.shape, full), pl.BlockSpec(b.shape, full),
            pl.BlockSpec(wr.shape, full), pl.BlockSpec(wr.shape, full), pl.BlockSpec(br.shape, full),
        ],
        out_specs=[
            pl.BlockSpec((tm, D_MODEL), row), pl.BlockSpec((PLANES, tm // LANES, LANES), lambda i: (0, i, 0)),
            pl.BlockSpec((tm, LANES), row),
        ],
        out_shape=[
            jax.ShapeDtypeStruct((T, D_MODEL), F32),
            jax.ShapeDtypeStruct((PLANES, T // LANES, LANES), jnp.int32), jax.ShapeDtypeStruct((T, LANES), F32),
        ],
        compiler_params=pltpu.CompilerParams(dimension_semantics=("arbitrary",), vmem_limit_bytes=VMEM_LIMIT),
        name="post",
    )(oa, ob, x2, wo, g, b, *_split_bf16(wr), br)


def _lane_cumsum(x, n):
    lane = lax.broadcasted_iota(jnp.int32, x.shape, 1)
    s = 1
    while s < n:
        x = x + jnp.where(lane >= s, pltpu.roll(x, s, axis=1), 0.0)
        s *= 2
    return x


def _route_kernel(e_ref, lpos_ref, nch_ref, gbase_ref, meta_ref, *, bm, rows_per_tile):
    planes = [e_ref[k] for k in range(TOP_K)]
    R = planes[0].shape[0]
    shape = planes[0].shape
    nt = R // rows_per_tile
    r_i = lax.broadcasted_iota(jnp.int32, (LANES, LANES), 0)
    c_i = lax.broadcasted_iota(jnp.int32, (LANES, LANES), 1)
    lane_incl = (r_i <= c_i).astype(BF16)
    rr = lax.broadcasted_iota(jnp.int32, (R, R), 0)
    rc = lax.broadcasted_iota(jnp.int32, (R, R), 1)
    same_tile = (rr // rows_per_tile) == (rc // rows_per_tile)
    rows_before = jnp.logical_and(rc < rr, same_tile).astype(BF16)
    tr = lax.broadcasted_iota(jnp.int32, (nt, R), 0)
    tc = lax.broadcasted_iota(jnp.int32, (nt, R), 1)
    tile_rows = (tc // rows_per_tile == tr).astype(BF16)
    er = lax.broadcasted_iota(jnp.int32, (R, nt), 0)
    ec = lax.broadcasted_iota(jnp.int32, (R, nt), 1)
    row_tile = (er // rows_per_tile == ec).astype(F32)
    lt_r = lax.broadcasted_iota(jnp.int32, (nt, nt), 0)
    lt_c = lax.broadcasted_iota(jnp.int32, (nt, nt), 1)
    tiles_before = (lt_c < lt_r).astype(BF16)
    lane_t = lax.broadcasted_iota(jnp.int32, (nt, LANES), 1)

    ranks = [jnp.zeros(shape, F32) for _ in range(TOP_K)]
    cnt = jnp.zeros((nt, LANES), F32)
    for ex in range(N_EXPERTS):
        hits = [pk == ex for pk in planes]
        any_hit = functools.reduce(jnp.logical_or, hits)
        m = jnp.where(any_hit, 1.0, 0.0).astype(BF16)
        incl = _dot(m, lane_incl)
        row_tot = jnp.broadcast_to(incl[:, LANES - 1:LANES], shape).astype(BF16)
        before = _dot(rows_before, row_tot)
        rank_e = incl - 1.0 + before
        ranks = [jnp.where(h, rank_e, r) for h, r in zip(hits, ranks)]
        cnt = jnp.where(lane_t == ex, _dot(tile_rows, row_tot), cnt)

    chunk = float(ROW_CHUNK)
    cnt8 = jnp.floor((cnt + (chunk - 1.0)) * (1.0 / chunk)) * chunk
    lstart = _lane_cumsum(cnt8, N_EXPERTS) - cnt8
    tile_pre = _dot(tiles_before, cnt8.astype(BF16))
    tot8 = tile_pre[nt - 1:nt] + cnt8[nt - 1:nt]
    tot_bm = jnp.floor((tot8 + (bm - 1.0)) * (1.0 / bm)) * bm
    end_incl = _lane_cumsum(tot_bm, N_EXPERTS)
    ebase = end_incl - tot_bm

    lrow = jnp.dot(row_tile, lstart, preferred_element_type=F32,
                   precision=lax.Precision.HIGHEST)
    starts = [jnp.zeros(shape, F32) for _ in range(TOP_K)]
    for ex in range(N_EXPERTS):
        starts = [jnp.where(pk == ex, lrow[:, ex:ex + 1], st) for pk, st in zip(planes, starts)]
    lpos_ref[...] = jnp.zeros(lpos_ref.shape, jnp.int32)
    for k in range(TOP_K):
        lpos_ref[k] = (starts[k] + ranks[k]).astype(jnp.int32)
    nch_ref[...] = (cnt8 * (1.0 / chunk)).astype(jnp.int32)
    gbase_ref[...] = (ebase + tile_pre).astype(jnp.int32)

    sub = lax.broadcasted_iota(jnp.int32, meta_ref.shape, 0)
    meta = jnp.where(sub == 0, ebase + tot8, 0.0)
    meta = jnp.where(sub == 1, (tot_bm - tot8) * (1.0 / chunk), meta)
    n_used = end_incl[:, N_EXPERTS - 1:N_EXPERTS] * (1.0 / bm)
    meta = jnp.where(sub == 2, n_used, meta)
    lane_m = lax.broadcasted_iota(jnp.int32, meta_ref.shape, 1)
    meta = jnp.where(sub == 3, jnp.where(lane_m < N_EXPERTS, ebase * (1.0 / bm), n_used), meta)
    meta_ref[...] = meta.astype(jnp.int32)


def _route_call(eplanes, bm):
    R = eplanes.shape[1]
    rows_per_tile = DISP_TILE // LANES
    nt = R // rows_per_tile
    full = lambda i: (0,) * 2
    shapes = [(nt, LANES), (nt, LANES), (8, LANES)]
    lpos, nch, gbase, meta = pl.pallas_call(
        functools.partial(_route_kernel, bm=bm, rows_per_tile=rows_per_tile),
        grid=(1,),
        in_specs=[pl.BlockSpec(eplanes.shape, lambda i: (0, 0, 0))],
        out_specs=[pl.BlockSpec(eplanes.shape, lambda i: (0, 0, 0))] + [pl.BlockSpec(s, full) for s in shapes],
        out_shape=[jax.ShapeDtypeStruct(eplanes.shape, jnp.int32)]
        + [jax.ShapeDtypeStruct(s, jnp.int32) for s in shapes],
        compiler_params=pltpu.CompilerParams(dimension_semantics=("arbitrary",), vmem_limit_bytes=VMEM_LIMIT),
        name="route",
    )(eplanes)
    E = N_EXPERTS
    return dict(lpos=lpos, nch=nch[:, :E].reshape(-1), gbase=gbase[:, :E].reshape(-1),
                tail_start=meta[0, :E], tail_n=meta[1, :E], n_used=meta[2, :1], first_block=meta[3, :E + 1])


def _chunk_rows(ref, first, rows=ROW_CHUNK):
    return ref.at[pl.ds(pl.multiple_of(first, ROW_CHUNK), rows), :]


def _group_copies(nch_ref, gbase_ref, tile, buf, hbm, sem, to_hbm):
    per_big = BIG_CHUNK // ROW_CHUNK

    def copy(local, remote):
        if to_hbm:
            pltpu.make_async_copy(local, remote, sem).start()
        else:
            pltpu.make_async_copy(remote, local, sem).start()

    def per_expert(ex, done):
        n = nch_ref[tile * N_EXPERTS + ex]
        g0 = gbase_ref[tile * N_EXPERTS + ex]
        l0 = done * ROW_CHUNK
        n_big = lax.shift_right_logical(n, per_big.bit_length() - 1)

        def big(c, carry):
            copy(_chunk_rows(buf, l0 + c * BIG_CHUNK, BIG_CHUNK), _chunk_rows(hbm, g0 + c * BIG_CHUNK, BIG_CHUNK))
            return carry

        def small(c, carry):
            copy(_chunk_rows(buf, l0 + c * ROW_CHUNK), _chunk_rows(hbm, g0 + c * ROW_CHUNK))
            return carry

        lax.fori_loop(0, n_big, big, 0)
        lax.fori_loop(n_big * per_big, n, small, 0)
        return done + n

    return lax.fori_loop(0, N_EXPERTS, per_expert, 0)


def _wait_chunks(n, hbm, sem):
    rows = pl.multiple_of(n * ROW_CHUNK, ROW_CHUNK)

    @pl.when(n > 0)
    def _():
        pltpu.make_async_copy(hbm.at[pl.ds(0, rows), :], hbm.at[pl.ds(0, rows), :], sem).wait()


def _tile_planes(lpos_ref, i):
    rpt = DISP_TILE // LANES
    per8 = 8 // rpt
    eight = lpos_ref[:, pl.ds(pl.multiple_of((i // per8) * 8, 8), 8), :]
    out = eight[:, :rpt, :]
    for sft in range(1, per8):
        out = jnp.where(i % per8 == sft, eight[:, sft * rpt:(sft + 1) * rpt, :], out)
    return out


def _dispatch_kernel(nch_ref, gbase_ref, tstart_ref, tn_ref, nu_ref, x_ref, lpos_ref, xs_hbm,
                     buf_ref, zero_ref, sem, zsem, cnt_ref, *, bm, n_blocks):
    i = pl.program_id(0)
    nt = pl.num_programs(0)
    slot = i % 2

    @pl.when(i >= 2)
    def _():
        _wait_chunks(cnt_ref[slot], xs_hbm, sem.at[slot])

    xb = x_ref[...].astype(BF16)
    lpos = _tile_planes(lpos_ref, i)
    rows = DISP_PCHUNK
    for rb in range(DISP_LCAP // rows):
        r = lax.broadcasted_iota(jnp.int32, (rows, LANES), 0) + rb * rows
        pieces = []
        for j in range(lpos.shape[1]):
            hit = r == lpos[0, j:j + 1, :]
            for k in range(1, TOP_K):
                hit = jnp.logical_or(hit, r == lpos[k, j:j + 1, :])
            pieces.append(jnp.where(hit, 1.0, 0.0).astype(BF16))
        perm = jnp.concatenate(pieces, axis=1)
        buf_ref[slot, rb * rows:(rb + 1) * rows, :] = _dot(perm, xb)
    cnt_ref[slot] = _group_copies(nch_ref, gbase_ref, i, buf_ref.at[slot], xs_hbm, sem.at[slot], True)

    @pl.when(i == nt - 1)
    def _():
        zero_ref[...] = jnp.zeros(zero_ref.shape, F32)

        def per_expert(ex, done):
            first = tstart_ref[ex]

            def per_chunk(c, carry):
                pltpu.make_async_copy(_chunk_rows(zero_ref, 0), _chunk_rows(xs_hbm, first + c * ROW_CHUNK),
                                      zsem.at[0]).start()
                return carry

            lax.fori_loop(0, tn_ref[ex], per_chunk, 0)
            return done + tn_ref[ex]

        n_tail = lax.fori_loop(0, N_EXPERTS, per_expert, 0)

        def per_block(b, carry):
            pltpu.make_async_copy(zero_ref, xs_hbm.at[pl.ds(pl.multiple_of(b * bm, bm), bm), :], zsem.at[1]).start()
            return carry

        lax.fori_loop(nu_ref[0], n_blocks, per_block, 0)
        _wait_chunks(n_tail, xs_hbm, zsem.at[0])

        def wait_block(b, carry):
            pltpu.make_async_copy(zero_ref, xs_hbm.at[pl.ds(0, bm), :], zsem.at[1]).wait()
            return carry

        lax.fori_loop(nu_ref[0], n_blocks, wait_block, 0)
        _wait_chunks(cnt_ref[slot], xs_hbm, sem.at[slot])

        @pl.when(nt >= 2)
        def _():
            _wait_chunks(cnt_ref[1 - slot], xs_hbm, sem.at[1 - slot])


def _dispatch_call(rt, x1, bm, n_blocks):
    T = x1.shape[0]
    tile = DISP_TILE
    grid_spec = pltpu.PrefetchScalarGridSpec(
        num_scalar_prefetch=5,
        grid=(T // tile,),
        in_specs=[
            pl.BlockSpec((tile, D_MODEL), lambda i, *_: (i, 0)),
            pl.BlockSpec(rt["lpos"].shape, lambda i, *_: (0, 0, 0)),
        ],
        out_specs=pl.BlockSpec(memory_space=pl.ANY),
        scratch_shapes=[
            pltpu.VMEM((2, DISP_LCAP, D_MODEL), F32),
            pltpu.VMEM((bm, D_MODEL), F32),
            pltpu.SemaphoreType.DMA((2,)),
            pltpu.SemaphoreType.DMA((2,)),
            pltpu.SMEM((2,), jnp.int32),
        ],
    )
    return pl.pallas_call(
        functools.partial(_dispatch_kernel, bm=bm, n_blocks=n_blocks),
        grid_spec=grid_spec,
        out_shape=jax.ShapeDtypeStruct((n_blocks * bm, D_MODEL), F32),
        compiler_params=pltpu.CompilerParams(dimension_semantics=("arbitrary",), vmem_limit_bytes=VMEM_LIMIT),
        name="dispatch",
    )(rt["nch"], rt["gbase"], rt["tail_start"], rt["tail_n"], rt["n_used"], x1, rt["lpos"])


def _experts_kernel(fb_ref, xs_hbm, wgu_ref, bgu_ref, wd_ref, bd_ref, y_hbm,
                    xs_ref, ys_ref, wgu_b, wd_b, xsem, ysem, *, bm, n_blocks):
    e = pl.program_id(0)
    n_used = fb_ref[N_EXPERTS]

    def x_copy(blk, slot):
        return pltpu.make_async_copy(xs_hbm.at[pl.ds(pl.multiple_of(blk * bm, bm), bm), :], xs_ref.at[slot],
                                     xsem.at[slot])

    def y_copy(blk, slot):
        return pltpu.make_async_copy(ys_ref.at[slot], y_hbm.at[pl.ds(pl.multiple_of(blk * bm, bm), bm), :],
                                     ysem.at[slot])

    n_buf = X_AHEAD + 1

    @pl.when(e == 0)
    def _():
        for b in range(X_AHEAD):
            @pl.when(b < n_used)
            def _():
                x_copy(b, b).start()

    wgu_b[...] = wgu_ref[...].astype(BF16)
    wd_b[...] = wd_ref[...].astype(BF16)

    def block(blk, carry):
        slot = blk % 2
        xslot = blk % n_buf
        x_copy(blk, xslot).wait()

        @pl.when(blk + X_AHEAD < n_used)
        def _():
            x_copy(blk + X_AHEAD, (blk + X_AHEAD) % n_buf).start()

        h = _dot(xs_ref[xslot].astype(BF16), wgu_b[...]) + bgu_ref[...]
        gate = jnp.minimum(h[:, :D_FF], SWIGLU_LIMIT)
        up = jnp.clip(h[:, D_FF:], -SWIGLU_LIMIT, SWIGLU_LIMIT)
        act = (up + 1.0) * (gate * jax.nn.sigmoid(gate * SWIGLU_ALPHA))
        y = _dot(act.astype(BF16), wd_b[...]) + bd_ref[...]

        @pl.when(blk >= 2)
        def _():
            y_copy(blk - 2, slot).wait()

        ys_ref[slot] = y
        y_copy(blk, slot).start()
        return carry

    lax.fori_loop(fb_ref[e], fb_ref[e + 1], block, 0)

    @pl.when(e == N_EXPERTS - 1)
    def _():
        y_copy(n_used - 1, (n_used - 1) % 2).wait()

        @pl.when(n_used >= 2)
        def _():
            y_copy(n_used - 2, n_used % 2).wait()

        ys_ref[0] = jnp.zeros(ys_ref.shape[1:], F32)

        def fill(blk, carry):
            y_copy(blk, 0).start()
            return carry

        def drain(blk, carry):
            y_copy(blk, 0).wait()
            return carry

        lax.fori_loop(n_used, n_blocks, fill, 0)
        lax.fori_loop(n_used, n_blocks, drain, 0)


def _experts_call(rt, xs, wgu, bgu, wd, bd, bm, n_blocks):
    E = N_EXPERTS
    by_expert = lambda e, fb: (e, 0, 0)
    grid_spec = pltpu.PrefetchScalarGridSpec(
        num_scalar_prefetch=1,
        grid=(E,),
        in_specs=[
            pl.BlockSpec(memory_space=pl.ANY),
            pl.BlockSpec((None, D_MODEL, 2 * D_FF), by_expert),
            pl.BlockSpec((None, 1, 2 * D_FF), by_expert),
            pl.BlockSpec((None, D_FF, D_MODEL), by_expert),
            pl.BlockSpec((None, 1, D_MODEL), by_expert),
        ],
        out_specs=pl.BlockSpec(memory_space=pl.ANY),
        scratch_shapes=[
            pltpu.VMEM((X_AHEAD + 1, bm, D_MODEL), F32),
            pltpu.VMEM((2, bm, D_MODEL), F32),
            pltpu.VMEM((D_MODEL, 2 * D_FF), BF16),
            pltpu.VMEM((D_FF, D_MODEL), BF16),
            pltpu.SemaphoreType.DMA((X_AHEAD + 1,)),
            pltpu.SemaphoreType.DMA((2,)),
        ],
    )
    return pl.pallas_call(
        functools.partial(_experts_kernel, bm=bm, n_blocks=n_blocks),
        grid_spec=grid_spec,
        out_shape=jax.ShapeDtypeStruct((n_blocks * bm, D_MODEL), F32),
        compiler_params=pltpu.CompilerParams(dimension_semantics=("arbitrary",), vmem_limit_bytes=VMEM_LIMIT),
        name="experts",
    )(rt["first_block"], xs, wgu, bgu.reshape(E, 1, 2 * D_FF), wd, bd.reshape(E, 1, D_MODEL))


def _combine_kernel(nch_ref, gbase_ref, y_hbm, x1_ref, lpos_ref, gate_ref, g_ref, b_ref, o_ref,
                    buf_ref, sem, cnt_ref):
    i = pl.program_id(0)
    nt = pl.num_programs(0)
    slot = i % 2

    @pl.when(i == 0)
    def _():
        buf_ref[...] = jnp.zeros(buf_ref.shape, F32)
        cnt_ref[0] = _group_copies(nch_ref, gbase_ref, 0, buf_ref.at[0], y_hbm, sem.at[0], False)

    @pl.when(i + 1 < nt)
    def _():
        cnt_ref[1 - slot] = _group_copies(nch_ref, gbase_ref, i + 1, buf_ref.at[1 - slot], y_hbm,
                                          sem.at[1 - slot], False)

    _wait_chunks(cnt_ref[slot], y_hbm, sem.at[slot])
    planes = _tile_planes(lpos_ref, i)
    lpos = jnp.concatenate([planes[:, j, :] for j in range(planes.shape[1])], axis=1).astype(F32).T
    gates = gate_ref[...]
    cols = DISP_PCHUNK
    y = jnp.zeros(o_ref.shape, F32)
    for cb in range(DISP_LCAP // cols):
        c = (lax.broadcasted_iota(jnp.int32, (lpos.shape[0], cols), 1) + cb * cols).astype(F32)
        w = jnp.zeros(c.shape, F32)
        for k in range(TOP_K):
            w = jnp.where(c == lpos[:, k:k + 1], gates[:, k:k + 1], w)
        y = y + _dot(w.astype(BF16), buf_ref[slot, cb * cols:(cb + 1) * cols, :].astype(BF16))
    o_ref[...] = _layer_norm(DN_ALPHA * x1_ref[...] + y, g_ref[...], b_ref[...])


def _combine_call(rt, y, x1, gates, g, b):
    T = x1.shape[0]
    tile = DISP_TILE
    grid_spec = pltpu.PrefetchScalarGridSpec(
        num_scalar_prefetch=2,
        grid=(T // tile,),
        in_specs=[
            pl.BlockSpec(memory_space=pl.ANY),
            pl.BlockSpec((tile, D_MODEL), lambda i, *_: (i, 0)),
            pl.BlockSpec(rt["lpos"].shape, lambda i, *_: (0, 0, 0)),
            pl.BlockSpec((tile, LANES), lambda i, *_: (i, 0)),
            pl.BlockSpec(g.shape, lambda i, *_: (0, 0)),
            pl.BlockSpec(b.shape, lambda i, *_: (0, 0)),
        ],
        out_specs=pl.BlockSpec((tile, D_MODEL), lambda i, *_: (i, 0)),
        scratch_shapes=[
            pltpu.VMEM((2, DISP_LCAP, D_MODEL), F32),
            pltpu.SemaphoreType.DMA((2,)),
            pltpu.SMEM((2,), jnp.int32),
        ],
    )
    return pl.pallas_call(
        _combine_kernel,
        grid_spec=grid_spec,
        out_shape=jax.ShapeDtypeStruct((T, D_MODEL), F32),
        compiler_params=pltpu.CompilerParams(dimension_semantics=("arbitrary",), vmem_limit_bytes=VMEM_LIMIT),
        name="combine",
    )(rt["nch"], rt["gbase"], y, x1, rt["lpos"], gates, g, b)


def _rope_lane_order(n_sub):
    half = DIFF_HEAD_DIM // 2
    per = LANES // 2 // n_sub
    assert per == half or n_sub == 1
    cols = []
    for part in range(2):
        for sub in range(n_sub):
            cols.extend(sub * 64 + part * half + d for d in range(half))
    return np.asarray(cols)


def _prep_weights(w_in, mla_q_norm_g, w_uq, mla_kv_norm_g, w_ukv):
    o_dq, o_dk, o_dv, o_cq, o_ckv, o_kr = 0, 512, 1024, 1536, 1792, 1920
    head_order = _rope_lane_order(2)
    diff_cols = np.concatenate([h * LANES + head_order for h in range(DIFF_HEADS)])
    w = {}
    ckv = w_in[:, o_ckv:o_kr]
    w["feat"] = jnp.concatenate([w_in[:, o_dq + diff_cols], w_in[:, o_dv:o_cq], w_in[:, o_cq:o_ckv], ckv],
                                axis=1).T.astype(BF16)

    def spread_rope(cols64):
        z = jnp.zeros((cols64.shape[0], 32), cols64.dtype)
        return jnp.concatenate([cols64[:, :32], z, cols64[:, 32:], z], axis=1)

    w["tok"] = jnp.concatenate([w_in[:, o_dk + diff_cols], ckv, spread_rope(w_in[:, o_kr:o_kr + MLA_ROPE_DIM])],
                               axis=1).astype(BF16)
    uq = []
    for h in range(MLA_HEADS):
        base = h * MLA_QK_DIM
        uq.append(w_uq[:, base:base + MLA_NOPE_DIM])
        uq.append(spread_rope(w_uq[:, base + MLA_NOPE_DIM:base + MLA_QK_DIM]))
    w["uqT"] = jnp.concatenate(uq, axis=1).T.astype(BF16)
    per = MLA_NOPE_DIM + MLA_V_DIM
    w["uk"] = jnp.concatenate([w_ukv[:, h * per:h * per + MLA_NOPE_DIM] for h in range(MLA_HEADS)], axis=1).astype(BF16)
    w["uvT"] = jnp.concatenate(
        [w_ukv[:, h * per + MLA_NOPE_DIM:(h + 1) * per] for h in range(MLA_HEADS)], axis=1).T.astype(BF16)
    w["gq"] = mla_q_norm_g.reshape(MLA_Q_RANK, 1)
    w["gkv"] = mla_kv_norm_g.reshape(1, MLA_KV_RANK)
    w["gkvc"] = mla_kv_norm_g.reshape(MLA_KV_RANK, 1)
    return w


def _rope_tables(positions):
    half = MLA_ROPE_DIM // 2
    inv_freq = 1.0 / (ROPE_THETA ** (jnp.arange(0, MLA_ROPE_DIM, 2, dtype=F32) / MLA_ROPE_DIM))
    ang = positions.astype(F32)[..., None] * inv_freq
    ang = jnp.tile(ang, (1, 1, LANES // half))
    sign = jnp.where(jnp.arange(LANES) < LANES // 2, -1.0, 1.0).astype(F32)
    return jnp.cos(ang), jnp.sin(ang) * sign


def kernel(x, positions, w_in, lambda_q1, lambda_k1, lambda_q2, lambda_k2, subln_g, mla_q_norm_g, w_uq,
           mla_kv_norm_g, w_ukv, w_o, ln1_g, ln1_b, w_router, b_router, w_gate_up, b_gate_up, w_down, b_down,
           ln2_g, ln2_b):
    B, S, D = x.shape
    T = B * S
    l = 0
    x2 = x.reshape(T, D)
    w = _prep_weights(w_in[l], mla_q_norm_g[l], w_uq[l], mla_kv_norm_g[l], w_ukv[l])

    dqT, dk, dvT, mqT, mk, mvT = _proj_call(x, _rope_tables(positions), w)
    lam_vecs = [v[l].reshape(1, DIFF_HEAD_DIM) for v in (lambda_q1, lambda_k1, lambda_q2, lambda_k2)]
    o_a = _attn_call(_diff_attn_kernel, lam_vecs + [subln_g[l].reshape(DIFF_V_DIM, 1)],
                     dqT, dk, dvT, DIFF_HEADS, 1, LANES, "diff_attn")
    o_b = _attn_call(_mla_attn_kernel, [], mqT, mk, mvT, MLA_HEADS // 2, 2, MLA_QK_PAD, "mla_attn")

    wr = jnp.pad(w_router[l], ((0, 0), (0, LANES - N_EXPERTS)))
    br = jnp.pad(b_router[l], (0, LANES - N_EXPERTS), constant_values=NEG_BIG).reshape(1, LANES)
    x1, eplanes, gates = _post_call(
        o_a.reshape(T, -1), o_b.reshape(T, -1), x2, w_o[l].astype(BF16),
        ln1_g[l].reshape(1, D), ln1_b[l].reshape(1, D), wr, br)

    bm = MOE_BM
    A = T * TOP_K
    n_tiles = T // DISP_TILE
    n_blocks = pl.cdiv(A + n_tiles * N_EXPERTS * (ROW_CHUNK - 1) + N_EXPERTS * (bm - ROW_CHUNK), bm)
    rt = _route_call(eplanes, bm)
    xs = _dispatch_call(rt, x1, bm, n_blocks)
    y = _experts_call(rt, xs, w_gate_up[l], b_gate_up[l], w_down[l], b_down[l], bm, n_blocks)
    out = _combine_call(rt, y, x1, gates, ln2_g[l].reshape(1, D), ln2_b[l].reshape(1, D))
    return out.reshape(B, S, D)
```
